```python
import jax, jax.numpy as jnp
from jax import lax
import numpy as np

D_MODEL = 1024
BATCH = 8
SEQ = 2048
DEPTH = 4

N_EVEN = (DEPTH + 1) // 2
N_ODD = DEPTH // 2

N_Q_HEADS = 8
N_KV_HEADS = 2
HEAD_DIM = 64
Q_GROUP = N_Q_HEADS // N_KV_HEADS
WINDOW = 128
ATTN_BLOCK = 128
ALIBI_MAX = 8.0
CONV_WIDTH_B = D_MODEL // 2
CONV_GROUPS_B = 8
CONV_K_B = 3
Q_DIM = N_Q_HEADS * HEAD_DIM
KV_DIM = N_KV_HEADS * HEAD_DIM
EVEN_IN = Q_DIM + 2 * KV_DIM + 3 * CONV_WIDTH_B
EVEN_MIX = Q_DIM + CONV_WIDTH_B
EVEN_SPLITS = (Q_DIM, Q_DIM + KV_DIM, Q_DIM + 2 * KV_DIM, Q_DIM + 2 * KV_DIM + CONV_WIDTH_B, Q_DIM + 2 * KV_DIM + 2 * CONV_WIDTH_B)
LRU_WIDTH = D_MODEL
LRU_HEADS = 8
LRU_HEAD_DIM = LRU_WIDTH // LRU_HEADS
CONV_K_C = 4
LRU_C = 8.0
D_FF = (D_MODEL * 7) // 2
N_EXPERTS = 8
TOP_K = 2
MOE_BLOCK = 128
NORM_EPS = 1e-6
NEG_INF = -1e30

kernel_name = 'hybrid_swa_shortconv_rglru_moe'


def rms_norm(x, g):
    xf = x.astype(jnp.float32)
    y = xf * lax.rsqrt(jnp.mean(xf * xf, axis=-1, keepdims=True) + NORM_EPS)
    return (y * g.astype(jnp.float32)).astype(x.dtype)


def causal_depthwise_conv(x, w):
    K = w.shape[0]
    S = x.shape[1]
    xp = jnp.pad(x, ((0, 0), (K - 1, 0), (0, 0)))
    return sum(xp[:, k:k + S] * w[k] for k in range(K))


def alibi_slopes():
    h = jnp.arange(1, N_Q_HEADS + 1, dtype=jnp.float32)
    return jnp.exp2(-ALIBI_MAX * h / N_Q_HEADS)


def sliding_window_attention(q, k, v, sinks):
    B, S = q.shape[0], q.shape[1]
    L = ATTN_BLOCK
    nb = S // L
    qb = q.reshape(B, nb, L, N_KV_HEADS, Q_GROUP, HEAD_DIM)

    def band(t):
        cur = t.reshape(B, nb, L, N_KV_HEADS, HEAD_DIM)
        prev = jnp.pad(cur, ((0, 0), (1, 0), (0, 0), (0, 0), (0, 0)))[:, :-1]
        return jnp.concatenate([prev, cur], axis=2)

    kb, vb = band(k), band(v)
    scores = jnp.einsum('bnqkgd,bnskd->bnkgqs', qb, kb, preferred_element_type=jnp.float32) * (HEAD_DIM ** -0.5)
    qi = jnp.arange(L)[:, None]
    sj = jnp.arange(2 * L)[None, :]
    diff = L + qi - sj
    s_abs = (jnp.arange(nb)[:, None] - 1) * L + jnp.arange(2 * L)[None, :]
    valid = (diff >= 0)[None] & (diff < WINDOW)[None] & (s_abs >= 0)[:, None, :]
    slopes = alibi_slopes().reshape(N_KV_HEADS, Q_GROUP)
    bias = -slopes[:, :, None, None] * diff.astype(jnp.float32)
    scores = jnp.where(valid[None, :, None, None], scores + bias, NEG_INF)
    sink = jnp.broadcast_to(sinks.astype(jnp.float32).reshape(N_KV_HEADS, Q_GROUP)[None, None, :, :, None, None], scores.shape[:-1] + (1,))
    probs = jax.nn.softmax(jnp.concatenate([scores, sink], axis=-1), axis=-1)[..., :-1]
    out = jnp.einsum('bnkgqs,bnskd->bnqkgd', probs.astype(v.dtype), vb)
    return out.reshape(B, S, Q_DIM)


def attn_conv_mixer(h, w_in, q_gain, k_gain, sinks, conv_w, w_out):
    B, S, _ = h.shape
    q, k, v, gate_b, gate_c, u = jnp.split(h @ w_in, EVEN_SPLITS, axis=-1)
    q = rms_norm(q.reshape(B, S, N_Q_HEADS, HEAD_DIM), q_gain)
    k = rms_norm(k.reshape(B, S, N_KV_HEADS, HEAD_DIM), k_gain)
    v = v.reshape(B, S, N_KV_HEADS, HEAD_DIM)
    attn = sliding_window_attention(q, k, v, sinks)
    conv = gate_b * causal_depthwise_conv(gate_c * u, conv_w)
    return jnp.concatenate([attn, conv], axis=-1) @ w_out


def recurrent_mixer(h, w_in, conv_w, conv_b, ga_w, ga_b, gx_w, gx_b, lam, w_out):
    B, S, _ = h.shape
    y_branch, x_branch = jnp.split(h @ w_in, 2, axis=-1)
    y_branch = jax.nn.gelu(y_branch)
    xc = causal_depthwise_conv(x_branch, conv_w) + conv_b
    xh = xc.reshape(B, S, LRU_HEADS, LRU_HEAD_DIM)
    r = jax.nn.sigmoid(jnp.einsum('bshi,hij->bshj', xh, ga_w).reshape(B, S, LRU_WIDTH) + ga_b)
    i = jax.nn.sigmoid(jnp.einsum('bshi,hij->bshj', xh, gx_w).reshape(B, S, LRU_WIDTH) + gx_b)
    log_a = -LRU_C * r.astype(jnp.float32) * jax.nn.softplus(-lam.astype(jnp.float32))
    a = jnp.exp(log_a)
    mult = jnp.sqrt(jnp.maximum(-jnp.expm1(2.0 * log_a), 0.0))
    mult = jnp.where(jnp.arange(S)[None, :, None] == 0, 1.0, mult)
    b = mult * (i * xc).astype(jnp.float32)

    def combine(c1, c2):
        a1, b1 = c1
        a2, b2 = c2
        return a1 * a2, a2 * b1 + b2

    _, hs = lax.associative_scan(combine, (a, b), axis=1)
    return (y_branch * hs.astype(h.dtype)) @ w_out


def swiglu(h, w_gate, w_up, w_down):
    return (jax.nn.silu(h @ w_gate) * (h @ w_up)) @ w_down


def moe_swiglu(h, w_router, w_gate, w_up, w_down):
    B, S, D = h.shape
    T = B * S
    xf = h.reshape(T, D)
    logits = (xf @ w_router).astype(jnp.float32)
    top_logits, top_idx = lax.top_k(logits, TOP_K)
    gates = jax.nn.softmax(top_logits, axis=-1)
    n_assign = T * TOP_K
    e_flat = top_idx.reshape(-1).astype(jnp.int32)
    tok_flat = jnp.repeat(jnp.arange(T, dtype=jnp.int32), TOP_K)
    g_flat = gates.reshape(-1)
    order = jnp.argsort(e_flat)
    e_sorted = e_flat[order]
    counts = jnp.zeros((N_EXPERTS,), jnp.int32).at[e_flat].add(1)
    padded = (counts + MOE_BLOCK - 1) // MOE_BLOCK * MOE_BLOCK
    pad_end = jnp.cumsum(padded)
    pad_start = pad_end - padded
    start = jnp.cumsum(counts) - counts
    dest = pad_start[e_sorted] + jnp.arange(n_assign, dtype=jnp.int32) - start[e_sorted]
    n_rows = n_assign + N_EXPERTS * MOE_BLOCK
    n_blocks = n_rows // MOE_BLOCK
    row_tok = jnp.zeros((n_rows,), jnp.int32).at[dest].set(tok_flat[order])
    row_gate = jnp.zeros((n_rows,), jnp.float32).at[dest].set(g_flat[order])
    blk_start = jnp.arange(n_blocks, dtype=jnp.int32) * MOE_BLOCK
    blk_expert = jnp.minimum(jnp.searchsorted(pad_end, blk_start, side='right'), N_EXPERTS - 1)

    def expert_block(args):
        tok, e = args
        xb = xf[tok]
        hb = jax.nn.silu(xb @ w_gate[e]) * (xb @ w_up[e])
        return hb @ w_down[e]

    yb = lax.map(expert_block, (row_tok.reshape(n_blocks, MOE_BLOCK), blk_expert))
    y = yb.reshape(n_rows, D).astype(jnp.float32) * row_gate[:, None]
    out = jnp.zeros((T, D), jnp.float32).at[row_tok].add(y)
    return out.astype(h.dtype).reshape(B, S, D)


def setup_inputs(seed: int = 0) -> dict:
    key = jax.random.key(seed)
    ks = iter(jax.random.split(key, 32))
    f32 = jnp.float32
    res_scale = (2 * DEPTH) ** -0.5

    def nrm(shape, scale):
        return jax.random.normal(next(ks), shape, f32) * scale

    x = nrm((BATCH, SEQ, D_MODEL), 1.0)
    norm_mix = 1.0 + nrm((DEPTH, D_MODEL), 0.02)
    norm_ffn = 1.0 + nrm((DEPTH, D_MODEL), 0.02)
    hy_w_in = nrm((N_EVEN, D_MODEL, EVEN_IN), D_MODEL ** -0.5)
    hy_q_gain = 1.0 + nrm((N_EVEN, HEAD_DIM), 0.02)
    hy_k_gain = 1.0 + nrm((N_EVEN, HEAD_DIM), 0.02)
    hy_sinks = nrm((N_EVEN, N_Q_HEADS), 0.5)
    hy_conv_w = nrm((N_EVEN, CONV_K_B, CONV_WIDTH_B), CONV_K_B ** -0.5)
    hy_w_out = nrm((N_EVEN, EVEN_MIX, D_MODEL), EVEN_MIX ** -0.5 * res_scale)
    rg_w_in = nrm((N_ODD, D_MODEL, 2 * LRU_WIDTH), D_MODEL ** -0.5)
    rg_conv_w = nrm((N_ODD, CONV_K_C, LRU_WIDTH), CONV_K_C ** -0.5)
    rg_conv_b = nrm((N_ODD, LRU_WIDTH), 0.01)
    rg_gate_a_w = nrm((N_ODD, LRU_HEADS, LRU_HEAD_DIM, LRU_HEAD_DIM), LRU_HEAD_DIM ** -0.5)
    rg_gate_a_b = nrm((N_ODD, LRU_WIDTH), 0.01)
    rg_gate_x_w = nrm((N_ODD, LRU_HEADS, LRU_HEAD_DIM, LRU_HEAD_DIM), LRU_HEAD_DIM ** -0.5)
    rg_gate_x_b = nrm((N_ODD, LRU_WIDTH), 0.01)
    a_pow_c = jax.random.uniform(next(ks), (N_ODD, LRU_WIDTH), f32, minval=0.9, maxval=0.999)
    a0 = a_pow_c ** (1.0 / LRU_C)
    rg_lambda = jnp.log(a0) - jnp.log1p(-a0)
    rg_w_out = nrm((N_ODD, LRU_WIDTH, D_MODEL), LRU_WIDTH ** -0.5 * res_scale)
    ffn_w_gate = nrm((N_EVEN, D_MODEL, D_FF), D_MODEL ** -0.5)
    ffn_w_up = nrm((N_EVEN, D_MODEL, D_FF), D_MODEL ** -0.5)
    ffn_w_down = nrm((N_EVEN, D_FF, D_MODEL), D_FF ** -0.5 * res_scale)
    moe_router = nrm((N_ODD, D_MODEL, N_EXPERTS), D_MODEL ** -0.5)
    moe_w_gate = nrm((N_ODD, N_EXPERTS, D_MODEL, D_FF), D_MODEL ** -0.5)
    moe_w_up = nrm((N_ODD, N_EXPERTS, D_MODEL, D_FF), D_MODEL ** -0.5)
    moe_w_down = nrm((N_ODD, N_EXPERTS, D_FF, D_MODEL), D_FF ** -0.5 * res_scale)
    return {'x': x, 'norm_mix': norm_mix, 'norm_ffn': norm_ffn,
            'hy_w_in': hy_w_in, 'hy_q_gain': hy_q_gain, 'hy_k_gain': hy_k_gain, 'hy_sinks': hy_sinks,
            'hy_conv_w': hy_conv_w, 'hy_w_out': hy_w_out,
            'rg_w_in': rg_w_in, 'rg_conv_w': rg_conv_w, 'rg_conv_b': rg_conv_b,
            'rg_gate_a_w': rg_gate_a_w, 'rg_gate_a_b': rg_gate_a_b, 'rg_gate_x_w': rg_gate_x_w,
            'rg_gate_x_b': rg_gate_x_b, 'rg_lambda': rg_lambda, 'rg_w_out': rg_w_out,
            'ffn_w_gate': ffn_w_gate, 'ffn_w_up': ffn_w_up, 'ffn_w_down': ffn_w_down,
            'moe_router': moe_router, 'moe_w_gate': moe_w_gate, 'moe_w_up': moe_w_up, 'moe_w_down': moe_w_down}


def reference(x, norm_mix, norm_ffn, hy_w_in, hy_q_gain, hy_k_gain, hy_sinks, hy_conv_w, hy_w_out,
              rg_w_in, rg_conv_w, rg_conv_b, rg_gate_a_w, rg_gate_a_b, rg_gate_x_w, rg_gate_x_b, rg_lambda, rg_w_out,
              ffn_w_gate, ffn_w_up, ffn_w_down, moe_router, moe_w_gate, moe_w_up, moe_w_down):
    for layer in range(DEPTH):
        j = layer // 2
        h = rms_norm(x, norm_mix[layer])
        if layer % 2 == 0:
            x = x + attn_conv_mixer(h, hy_w_in[j], hy_q_gain[j], hy_k_gain[j], hy_sinks[j], hy_conv_w[j], hy_w_out[j])
            h = rms_norm(x, norm_ffn[layer])
            x = x + swiglu(h, ffn_w_gate[j], ffn_w_up[j], ffn_w_down[j])
        else:
            x = x + recurrent_mixer(h, rg_w_in[j], rg_conv_w[j], rg_conv_b[j], rg_gate_a_w[j], rg_gate_a_b[j],
                                    rg_gate_x_w[j], rg_gate_x_b[j], rg_lambda[j], rg_w_out[j])
            h = rms_norm(x, norm_ffn[layer])
            x = x + moe_swiglu(h, moe_router[j], moe_w_gate[j], moe_w_up[j], moe_w_down[j])
    return x
```

```python
import functools

import jax
import jax.numpy as jnp
from jax import lax
from jax.experimental import pallas as pl
from jax.experimental.pallas import tpu as pltpu

F32 = jnp.float32
BF16 = jnp.bfloat16

LANES = 128
SUBLANES = 8
VMEM_LIMIT_BYTES = 56 * 1024 * 1024

N_Q_HEADS = 8
N_KV_HEADS = 2
HEAD_DIM = 64
Q_GROUP = N_Q_HEADS // N_KV_HEADS
WINDOW = 128
ATTN_BLOCK = 128
ALIBI_MAX = 8.0
CONV_K_B = 3
CONV_K_C = 4
LRU_HEADS = 8
LRU_C = 8.0
N_EXPERTS = 8
TOP_K = 2
NORM_EPS = 1e-6
NEG_INF = -1e30

SEQ_TILE = 512
FFN_TILE = 512
MOE_ROWS = 512
FF_CHUNK = 1792
ROUTE_LANES = LANES


def _rms(x, g):
    return x * lax.rsqrt(jnp.mean(x * x, axis=-1, keepdims=True) + NORM_EPS) * g


def _split_bf16(x):
    hi = x.astype(BF16)
    lo = (x - hi.astype(F32)).astype(BF16)
    return hi, lo


def _resident(shape, index_map):
    return pl.BlockSpec(shape, index_map, pipeline_mode=pl.Buffered(1))


def _even_mixer_kernel(sinks_ref, x_ref, g_ref, win_ref, qg_ref, kg_ref, cw_ref, wout_ref, o_ref,
                       kd_ref, vd_ref, cbuf_ref, attn_ref):
    ts = x_ref.shape[0]
    q_dim = N_Q_HEADS * HEAD_DIM
    kv_dim = N_KV_HEADS * HEAD_DIM
    cw_dim = cw_ref.shape[1]
    st = pl.program_id(1)

    @pl.when(st == 0)
    def _():
        kd_ref[:, 0:ATTN_BLOCK, :] = jnp.zeros((N_KV_HEADS, ATTN_BLOCK, LANES), BF16)
        vd_ref[:, 0:ATTN_BLOCK, :] = jnp.zeros((N_KV_HEADS, ATTN_BLOCK, LANES), BF16)
        cbuf_ref[0:SUBLANES, :] = jnp.zeros((SUBLANES, cw_dim), F32)

    x = x_ref[...]
    h = _rms(x, g_ref[...]).astype(BF16)
    proj = jnp.dot(h, win_ref[...], preferred_element_type=F32)
    q = proj[:, 0:q_dim]
    k = proj[:, q_dim:q_dim + kv_dim]
    v = proj[:, q_dim + kv_dim:q_dim + 2 * kv_dim]
    o0 = q_dim + 2 * kv_dim
    gate_b = proj[:, o0:o0 + cw_dim]
    gate_c = proj[:, o0 + cw_dim:o0 + 2 * cw_dim]
    u = proj[:, o0 + 2 * cw_dim:o0 + 3 * cw_dim]

    ri = lax.broadcasted_iota(jnp.int32, (LANES, LANES), 0)
    ci = lax.broadcasted_iota(jnp.int32, (LANES, LANES), 1)
    half_mean = jnp.where((ri < HEAD_DIM) == (ci < HEAD_DIM), 1.0 / HEAD_DIM, 0.0).astype(BF16)

    def half_mean_square(z):
        hi, lo = _split_bf16(z * z)
        cols = []
        for c in range(z.shape[1] // LANES):
            sl = slice(c * LANES, (c + 1) * LANES)
            cols.append(jnp.dot(hi[:, sl], half_mean, preferred_element_type=F32)
                        + jnp.dot(lo[:, sl], half_mean, preferred_element_type=F32))
        return cols[0] if len(cols) == 1 else jnp.concatenate(cols, axis=1)

    qn = q * lax.rsqrt(half_mean_square(q) + NORM_EPS) * qg_ref[...] * (HEAD_DIM ** -0.5)
    kn = k * lax.rsqrt(half_mean_square(k) + NORM_EPS) * kg_ref[...]

    low_half_t = lax.broadcasted_iota(jnp.int32, (ts, LANES), 1) < HEAD_DIM
    k_sw = pltpu.roll(kn, HEAD_DIM, axis=1)
    v_sw = pltpu.roll(v, HEAD_DIM, axis=1)
    kd_ref[0, ATTN_BLOCK:ATTN_BLOCK + ts, :] = jnp.where(low_half_t, kn, k_sw).astype(BF16)
    kd_ref[1, ATTN_BLOCK:ATTN_BLOCK + ts, :] = jnp.where(low_half_t, k_sw, kn).astype(BF16)
    vd_ref[0, ATTN_BLOCK:ATTN_BLOCK + ts, :] = jnp.where(low_half_t, v, v_sw).astype(BF16)
    vd_ref[1, ATTN_BLOCK:ATTN_BLOCK + ts, :] = jnp.where(low_half_t, v_sw, v).astype(BF16)

    low_half = lax.broadcasted_iota(jnp.int32, (ATTN_BLOCK, LANES), 1) < HEAD_DIM
    qi = lax.broadcasted_iota(jnp.int32, (ATTN_BLOCK, 2 * ATTN_BLOCK), 0)
    sj = lax.broadcasted_iota(jnp.int32, (ATTN_BLOCK, 2 * ATTN_BLOCK), 1)
    diff = ATTN_BLOCK + qi - sj
    in_window = (diff >= 0) & (diff < WINDOW)
    diff_f = diff.astype(F32)
    first_key = jnp.where(st == 0, ATTN_BLOCK, 0)

    for j in range(ts // ATTN_BLOCK):
        rows = slice(j * ATTN_BLOCK, (j + 1) * ATTN_BLOCK)
        band = slice(j * ATTN_BLOCK, (j + 2) * ATTN_BLOCK)
        valid = (in_window & (sj >= first_key)) if j == 0 else in_window
        pair_out = []
        for g in range(N_KV_HEADS):
            q_rows = []
            for p in range(Q_GROUP // 2):
                c = g * (Q_GROUP // 2) + p
                qp = qn[rows, c * LANES:(c + 1) * LANES]
                q_rows.append(jnp.where(low_half, qp, 0.0))
                q_rows.append(jnp.where(low_half, 0.0, qp))
            qs = jnp.concatenate(q_rows, axis=0).astype(BF16)
            s = lax.dot_general(qs, kd_ref[g, band, :], (((1,), (1,)), ((), ())),
                                preferred_element_type=F32)
            probs = []
            denom = []
            for hh in range(Q_GROUP):
                head = g * Q_GROUP + hh
                slope = 2.0 ** (-ALIBI_MAX * (head + 1) / N_Q_HEADS)
                sh = s[hh * ATTN_BLOCK:(hh + 1) * ATTN_BLOCK, :] - slope * diff_f
                sh = jnp.where(valid, sh, NEG_INF)
                sink = sinks_ref[0, head]
                m = jnp.maximum(jnp.max(sh, axis=-1, keepdims=True), sink)
                pr = jnp.exp(sh - m)
                probs.append(pr)
                denom.append(jnp.sum(pr, axis=-1, keepdims=True) + jnp.exp(sink - m))
            pm = jnp.concatenate(probs, axis=0).astype(BF16)
            o = jnp.dot(pm, vd_ref[g, band, :], preferred_element_type=F32)
            o = o / jnp.concatenate(denom, axis=0)
            for p in range(Q_GROUP // 2):
                lo_rows = o[(2 * p) * ATTN_BLOCK:(2 * p + 1) * ATTN_BLOCK, :]
                hi_rows = o[(2 * p + 1) * ATTN_BLOCK:(2 * p + 2) * ATTN_BLOCK, :]
                pair_out.append(jnp.where(low_half, lo_rows, hi_rows))
        attn_ref[rows, :] = jnp.concatenate(pair_out, axis=1)

    cu = gate_c * u
    cbuf_ref[SUBLANES:SUBLANES + ts, :] = cu
    conv = cw_ref[CONV_K_B - 1:CONV_K_B, :] * cu
    for kk in range(CONV_K_B - 1):
        back = CONV_K_B - 1 - kk
        conv = conv + cw_ref[kk:kk + 1, :] * cbuf_ref[SUBLANES - back:SUBLANES - back + ts, :]
    conv = gate_b * conv

    mix = jnp.concatenate([attn_ref[...], conv], axis=1).astype(BF16)
    o_ref[...] = x + jnp.dot(mix, wout_ref[...], preferred_element_type=F32)

    kd_ref[:, 0:ATTN_BLOCK, :] = kd_ref[:, ts:ts + ATTN_BLOCK, :]
    vd_ref[:, 0:ATTN_BLOCK, :] = vd_ref[:, ts:ts + ATTN_BLOCK, :]
    cbuf_ref[0:SUBLANES, :] = cbuf_ref[ts:ts + SUBLANES, :]


def _even_mixer(x, g, w_in, q_gain, k_gain, sinks, conv_w, w_out):
    b, s, d = x.shape
    ts = SEQ_TILE
    n_in = w_in.shape[1]
    cw_dim = conv_w.shape[1]
    mix_dim = w_out.shape[0]
    qg = jnp.tile(q_gain, N_Q_HEADS).reshape(1, N_Q_HEADS * HEAD_DIM)
    kg = jnp.tile(k_gain, N_KV_HEADS).reshape(1, N_KV_HEADS * HEAD_DIM)
    const = lambda bi, si: (0, 0)
    return pl.pallas_call(
        _even_mixer_kernel,
        out_shape=jax.ShapeDtypeStruct((b, s, d), F32),
        grid=(b, s // ts),
        in_specs=[
            pl.BlockSpec(memory_space=pltpu.SMEM),
            pl.BlockSpec((None, ts, d), lambda bi, si: (bi, si, 0)),
            _resident((1, d), const),
            _resident((d, n_in), const),
            _resident((1, N_Q_HEADS * HEAD_DIM), const),
            _resident((1, N_KV_HEADS * HEAD_DIM), const),
            _resident((CONV_K_B, cw_dim), const),
            _resident((mix_dim, d), const),
        ],
        out_specs=pl.BlockSpec((None, ts, d), lambda bi, si: (bi, si, 0)),
        scratch_shapes=[
            pltpu.VMEM((N_KV_HEADS, ATTN_BLOCK + ts, LANES), BF16),
            pltpu.VMEM((N_KV_HEADS, ATTN_BLOCK + ts, LANES), BF16),
            pltpu.VMEM((SUBLANES + ts, cw_dim), F32),
            pltpu.VMEM((ts, N_Q_HEADS * HEAD_DIM), F32),
        ],
        compiler_params=pltpu.CompilerParams(
            dimension_semantics=("arbitrary", "arbitrary"), vmem_limit_bytes=VMEM_LIMIT_BYTES),
        name="even_mixer",
    )(sinks.reshape(1, N_Q_HEADS), x, g.reshape(1, d), w_in.astype(BF16), qg, kg, conv_w,
      w_out.astype(BF16))


def _swiglu_rows(hb, wg_ref, wu_ref, wd_ref):
    d_ff = wg_ref.shape[1]
    acc = None
    for c in range(d_ff // FF_CHUNK):
        cols = slice(c * FF_CHUNK, (c + 1) * FF_CHUNK)
        gate = jnp.dot(hb, wg_ref[:, cols], preferred_element_type=F32)
        up = jnp.dot(hb, wu_ref[:, cols], preferred_element_type=F32)
        act = (gate * (1.0 / (1.0 + jnp.exp(-gate))) * up).astype(BF16)
        part = jnp.dot(act, wd_ref[cols, :], preferred_element_type=F32)
        acc = part if acc is None else acc + part
    return acc


def _ffn_kernel(x_ref, g_ref, wg_ref, wu_ref, wd_ref, o_ref):
    x = x_ref[...]
    hb = _rms(x, g_ref[...]).astype(BF16)
    o_ref[...] = x + _swiglu_rows(hb, wg_ref, wu_ref, wd_ref)


def _ffn(x, g, w_gate, w_up, w_down):
    t, d = x.shape
    d_ff = w_gate.shape[1]
    tm = FFN_TILE
    const = lambda i: (0, 0)
    return pl.pallas_call(
        _ffn_kernel,
        out_shape=jax.ShapeDtypeStruct((t, d), F32),
        grid=(t // tm,),
        in_specs=[
            pl.BlockSpec((tm, d), lambda i: (i, 0)),
            _resident((1, d), const),
            _resident((d, d_ff), const),
            _resident((d, d_ff), const),
            _resident((d_ff, d), const),
        ],
        out_specs=pl.BlockSpec((tm, d), lambda i: (i, 0)),
        compiler_params=pltpu.CompilerParams(
            dimension_semantics=("arbitrary",), vmem_limit_bytes=VMEM_LIMIT_BYTES),
        name="dense_ffn",
    )(x, g.reshape(1, d), w_gate.astype(BF16), w_up.astype(BF16), w_down.astype(BF16))


def _odd_mixer_kernel(x_ref, g_ref, win_ref, cw_ref, cb_ref, gw_ref, gab_ref, gxb_ref, lam_ref, wout_ref,
                      g2_ref, wr_ref, o_ref, route_ref, xbuf_ref, a_ref, b_ref, hs_ref, hcar_ref):
    ts = x_ref.shape[0]
    w = cw_ref.shape[1]
    hd = w // LRU_HEADS
    st = pl.program_id(1)

    @pl.when(st == 0)
    def _():
        xbuf_ref[0:SUBLANES, :] = jnp.zeros((SUBLANES, w), F32)
        hcar_ref[...] = jnp.zeros((1, w), F32)

    x = x_ref[...]
    h = _rms(x, g_ref[...]).astype(BF16)
    proj = jnp.dot(h, win_ref[...], preferred_element_type=F32)
    y = proj[:, 0:w]
    y = y * (0.5 * (1.0 + jnp.tanh(0.7978845608028654 * (y + 0.044715 * (y * y * y)))))
    xb = proj[:, w:2 * w]

    xbuf_ref[SUBLANES:SUBLANES + ts, :] = xb
    xc = cw_ref[CONV_K_C - 1:CONV_K_C, :] * xb + cb_ref[...]
    for kk in range(CONV_K_C - 1):
        back = CONV_K_C - 1 - kk
        xc = xc + cw_ref[kk:kk + 1, :] * xbuf_ref[SUBLANES - back:SUBLANES - back + ts, :]

    r_cols = []
    i_cols = []
    for hh in range(LRU_HEADS):
        ri = jnp.dot(xc[:, hh * hd:(hh + 1) * hd].astype(BF16), gw_ref[hh], preferred_element_type=F32)
        r_cols.append(ri[:, 0:hd])
        i_cols.append(ri[:, hd:2 * hd])
    r_pre = jnp.concatenate(r_cols, axis=1) + gab_ref[...]
    i_pre = jnp.concatenate(i_cols, axis=1) + gxb_ref[...]
    r = 1.0 / (1.0 + jnp.exp(-r_pre))
    ig = 1.0 / (1.0 + jnp.exp(-i_pre))
    neg_lam = -lam_ref[...]
    softplus = jnp.maximum(neg_lam, 0.0) + jnp.log1p(jnp.exp(-jnp.abs(neg_lam)))
    log_a = -LRU_C * r * softplus
    a = jnp.exp(log_a)
    th = jnp.tanh(log_a)
    mult = jnp.sqrt(jnp.maximum(-2.0 * th / (1.0 - th), 0.0))
    row = lax.broadcasted_iota(jnp.int32, (ts, w), 0)
    mult = jnp.where((row == 0) & (st == 0), 1.0, mult)
    a_ref[...] = a
    b_ref[...] = mult * (ig * xc)

    srow = lax.broadcasted_iota(jnp.int32, (SUBLANES, w), 0)

    def group(i, hprev):
        off = pl.multiple_of(i * SUBLANES, SUBLANES)
        av = a_ref[pl.ds(off, SUBLANES), :]
        bv = b_ref[pl.ds(off, SUBLANES), :]
        for sh in (1, 2, 4):
            a_sh = jnp.where(srow >= sh, pltpu.roll(av, sh, axis=0), 1.0)
            b_sh = jnp.where(srow >= sh, pltpu.roll(bv, sh, axis=0), 0.0)
            bv = bv + av * b_sh
            av = av * a_sh
        hv = bv + av * hprev
        hs_ref[pl.ds(off, SUBLANES), :] = hv
        return hv[SUBLANES - 1:SUBLANES, :]

    hcar_ref[...] = lax.fori_loop(0, ts // SUBLANES, group, hcar_ref[...])

    out = x + jnp.dot((y * hs_ref[...]).astype(BF16), wout_ref[...], preferred_element_type=F32)
    o_ref[...] = out
    xbuf_ref[0:SUBLANES, :] = xbuf_ref[ts:ts + SUBLANES, :]

    h2_hi, h2_lo = _split_bf16(_rms(out, g2_ref[...]))
    wr_hi, wr_lo = _split_bf16(wr_ref[...])
    logits = (jnp.dot(h2_hi, wr_hi, preferred_element_type=F32)
              + jnp.dot(h2_lo, wr_hi, preferred_element_type=F32)
              + jnp.dot(h2_hi, wr_lo, preferred_element_type=F32))
    lane = lax.broadcasted_iota(jnp.int32, (ts, ROUTE_LANES), 1)
    lg = jnp.where(lane < N_EXPERTS, logits, -jnp.inf)
    m1 = jnp.max(lg, axis=-1, keepdims=True)
    i1 = jnp.min(jnp.where(lg == m1, lane, ROUTE_LANES), axis=-1, keepdims=True)
    lg2 = jnp.where(lane == i1, -jnp.inf, lg)
    m2 = jnp.max(lg2, axis=-1, keepdims=True)
    i2 = jnp.min(jnp.where(lg2 == m2, lane, ROUTE_LANES), axis=-1, keepdims=True)
    e2 = jnp.exp(m2 - m1)
    gate1 = 1.0 / (1.0 + e2)
    gate2 = e2 / (1.0 + e2)
    route_ref[...] = jnp.where(lane == 0, i1.astype(F32),
                               jnp.where(lane == 1, i2.astype(F32),
                                         jnp.where(lane == 2, gate1, jnp.where(lane == 3, gate2, 0.0))))


def _odd_mixer(x, g, w_in, conv_w, conv_b, ga_w, ga_b, gx_w, gx_b, lam, w_out, g2, w_router):
    b, s, d = x.shape
    ts = SEQ_TILE
    w = conv_w.shape[1]
    hd = w // LRU_HEADS
    gw = jnp.concatenate([ga_w, gx_w], axis=-1).astype(BF16)
    wr = jnp.pad(w_router, ((0, 0), (0, ROUTE_LANES - N_EXPERTS)))
    const = lambda bi, si: (0, 0)
    row = lambda a: a.reshape(1, -1)
    return pl.pallas_call(
        _odd_mixer_kernel,
        out_shape=(jax.ShapeDtypeStruct((b, s, d), F32), jax.ShapeDtypeStruct((b, s, ROUTE_LANES), F32)),
        grid=(b, s // ts),
        in_specs=[
            pl.BlockSpec((None, ts, d), lambda bi, si: (bi, si, 0)),
            _resident((1, d), const),
            _resident((d, 2 * w), const),
            _resident((CONV_K_C, w), const),
            _resident((1, w), const),
            _resident((LRU_HEADS, hd, 2 * hd), lambda bi, si: (0, 0, 0)),
            _resident((1, w), const),
            _resident((1, w), const),
            _resident((1, w), const),
            _resident((w, d), const),
            _resident((1, d), const),
            _resident((d, ROUTE_LANES), const),
        ],
        out_specs=(pl.BlockSpec((None, ts, d), lambda bi, si: (bi, si, 0)),
                   pl.BlockSpec((None, ts, ROUTE_LANES), lambda bi, si: (bi, si, 0))),
        scratch_shapes=[
            pltpu.VMEM((SUBLANES + ts, w), F32),
            pltpu.VMEM((ts, w), F32),
            pltpu.VMEM((ts, w), F32),
            pltpu.VMEM((ts, w), F32),
            pltpu.VMEM((1, w), F32),
        ],
        compiler_params=pltpu.CompilerParams(
            dimension_semantics=("arbitrary", "arbitrary"), vmem_limit_bytes=VMEM_LIMIT_BYTES),
        name="odd_mixer",
    )(x, row(g), w_in.astype(BF16), conv_w, row(conv_b), gw, row(ga_b), row(gx_b), row(lam),
      w_out.astype(BF16), row(g2), wr)


def _routing_tables(route, t):
    r = MOE_ROWS
    n_rows = TOP_K * t + N_EXPERTS * r
    n_blocks = n_rows // r
    e = route[:, 0:TOP_K].astype(jnp.int32)
    ind = (e[:, :, None] == jnp.arange(N_EXPERTS, dtype=jnp.int32)[None, None, :]).astype(jnp.int32).sum(1)
    csum = jnp.cumsum(ind, axis=0)
    counts = csum[-1]
    padded = (counts + r - 1) // r * r
    pad_end = jnp.cumsum(padded)
    pad_start = pad_end - padded
    dest = jnp.take_along_axis(pad_start[None, :] + csum - ind, e, axis=1).reshape(-1)
    tok = jnp.repeat(jnp.arange(t, dtype=jnp.int32), TOP_K)
    slot = jnp.tile(jnp.arange(TOP_K, dtype=jnp.int32), t)
    row_tok = jnp.zeros((n_rows,), jnp.int32).at[dest].set(tok)
    is_real = jnp.zeros((n_rows,), jnp.bool_).at[dest].set(True)
    real_dst = jnp.zeros((n_rows,), jnp.int32).at[dest].set(slot * t + tok)
    pad_rank = jnp.cumsum(jnp.logical_not(is_real).astype(jnp.int32)) - 1
    row_dst = jnp.where(is_real, real_dst, TOP_K * t + pad_rank)
    blk_start = jnp.arange(n_blocks, dtype=jnp.int32) * r
    blk_expert = jnp.minimum(jnp.searchsorted(pad_end, blk_start, side="right"), N_EXPERTS - 1)
    return (row_tok.reshape(n_blocks, r), row_dst.reshape(n_blocks, r), blk_expert.astype(jnp.int32))


def _expert_kernel(be_ref, tok_hbm, dst_hbm, x_hbm, g_ref, wg_ref, wu_ref, wd_ref, o_hbm,
                   tok_s, dst_s, xbuf, ybuf, sem_idx, sem_in, sem_out):
    del be_ref
    i = pl.program_id(0)
    rows = xbuf.shape[0]

    idx_copies = (pltpu.make_async_copy(tok_hbm.at[i], tok_s, sem_idx.at[0]),
                  pltpu.make_async_copy(dst_hbm.at[i], dst_s, sem_idx.at[1]))
    for c in idx_copies:
        c.start()
    for c in idx_copies:
        c.wait()

    def row_in(r):
        return pltpu.make_async_copy(x_hbm.at[pl.ds(tok_s[r], 1), :], xbuf.at[pl.ds(r, 1), :], sem_in)

    def row_out(r):
        return pltpu.make_async_copy(ybuf.at[pl.ds(r, 1), :], o_hbm.at[pl.ds(dst_s[r], 1), :], sem_out)

    def start_in(r, c):
        row_in(r).start()
        return c

    def wait_in(r, c):
        row_in(r).wait()
        return c

    def start_out(r, c):
        row_out(r).start()
        return c

    def wait_out(r, c):
        row_out(r).wait()
        return c

    lax.fori_loop(0, rows, start_in, 0)
    lax.fori_loop(0, rows, wait_in, 0)
    hb = _rms(xbuf[...], g_ref[...]).astype(BF16)
    ybuf[...] = _swiglu_rows(hb, wg_ref, wu_ref, wd_ref)
    lax.fori_loop(0, rows, start_out, 0)
    lax.fori_loop(0, rows, wait_out, 0)


def _experts(x, g, route, w_gate, w_up, w_down):
    t, d = x.shape
    d_ff = w_gate.shape[2]
    r = MOE_ROWS
    row_tok, row_dst, blk_expert = _routing_tables(route, t)
    n_blocks = row_tok.shape[0]
    by_expert = lambda i, be: (be[i], 0, 0)
    grid_spec = pltpu.PrefetchScalarGridSpec(
        num_scalar_prefetch=1,
        grid=(n_blocks,),
        in_specs=[
            pl.BlockSpec(memory_space=pl.ANY),
            pl.BlockSpec(memory_space=pl.ANY),
            pl.BlockSpec(memory_space=pl.ANY),
            _resident((1, d), lambda i, be: (0, 0)),
            pl.BlockSpec((None, d, d_ff), by_expert, pipeline_mode=pl.Buffered(1)),
            pl.BlockSpec((None, d, d_ff), by_expert, pipeline_mode=pl.Buffered(1)),
            pl.BlockSpec((None, d_ff, d), by_expert, pipeline_mode=pl.Buffered(1)),
        ],
        out_specs=pl.BlockSpec(memory_space=pl.ANY),
        scratch_shapes=[
            pltpu.SMEM((r,), jnp.int32),
            pltpu.SMEM((r,), jnp.int32),
            pltpu.VMEM((r, d), F32),
            pltpu.VMEM((r, d), F32),
            pltpu.SemaphoreType.DMA((2,)),
            pltpu.SemaphoreType.DMA,
            pltpu.SemaphoreType.DMA,
        ],
    )
    return pl.pallas_call(
        _expert_kernel,
        out_shape=jax.ShapeDtypeStruct((n_blocks * r, d), F32),
        grid_spec=grid_spec,
        compiler_params=pltpu.CompilerParams(
            dimension_semantics=("arbitrary",), vmem_limit_bytes=VMEM_LIMIT_BYTES),
        name="moe_experts",
    )(blk_expert, row_tok, row_dst, x, g.reshape(1, d), w_gate.astype(BF16), w_up.astype(BF16),
      w_down.astype(BF16))


def _combine_kernel(x_ref, y0_ref, y1_ref, route_ref, o_ref):
    rt = route_ref[...]
    o_ref[...] = x_ref[...] + (y0_ref[...] * rt[:, 2:3] + y1_ref[...] * rt[:, 3:4])


def _combine(x, y, route):
    t, d = x.shape
    tm = FFN_TILE
    nb = t // tm
    return pl.pallas_call(
        _combine_kernel,
        out_shape=jax.ShapeDtypeStruct((t, d), F32),
        grid=(nb,),
        in_specs=[
            pl.BlockSpec((tm, d), lambda i: (i, 0)),
            pl.BlockSpec((tm, d), lambda i: (i, 0)),
            pl.BlockSpec((tm, d), lambda i: (i + nb, 0)),
            pl.BlockSpec((tm, ROUTE_LANES), lambda i: (i, 0)),
        ],
        out_specs=pl.BlockSpec((tm, d), lambda i: (i, 0)),
        compiler_params=pltpu.CompilerParams(dimension_semantics=("arbitrary",)),
        name="moe_combine",
    )(x, y, y, route)


def kernel(x, norm_mix, norm_ffn, hy_w_in, hy_q_gain, hy_k_gain, hy_sinks, hy_conv_w, hy_w_out, rg_w_in, rg_conv_w, rg_conv_b, rg_gate_a_w, rg_gate_a_b, rg_gate_x_w, rg_gate_x_b, rg_lambda, rg_w_out, ffn_w_gate, ffn_w_up, ffn_w_down, moe_router, moe_w_gate, moe_w_up, moe_w_down):
    b, s, d = x.shape
    t = b * s
    depth = norm_mix.shape[0]
    for layer in range(depth):
        j = layer // 2
        if layer % 2 == 0:
            x = _even_mixer(x, norm_mix[layer], hy_w_in[j], hy_q_gain[j], hy_k_gain[j], hy_sinks[j],
                            hy_conv_w[j], hy_w_out[j])
            x = _ffn(x.reshape(t, d), norm_ffn[layer], ffn_w_gate[j], ffn_w_up[j], ffn_w_down[j])
            x = x.reshape(b, s, d)
        else:
            x, route = _odd_mixer(x, norm_mix[layer], rg_w_in[j], rg_conv_w[j], rg_conv_b[j],
                                  rg_gate_a_w[j], rg_gate_a_b[j], rg_gate_x_w[j], rg_gate_x_b[j],
                                  rg_lambda[j], rg_w_out[j], norm_ffn[layer], moe_router[j])
            x = x.reshape(t, d)
            route = route.reshape(t, ROUTE_LANES)
            y = _experts(x, norm_ffn[layer], route, moe_w_gate[j], moe_w_up[j], moe_w_down[j])
            x = _combine(x, y, route).reshape(b, s, d)
    return x
```

```python
import jax
import jax.numpy as jnp
from jax import lax
from jax.experimental import pallas as pl
from jax.experimental.pallas import tpu as pltpu

F32 = jnp.float32
BF16 = jnp.bfloat16

LANES = 128
SUBLANES = 8
VMEM_LIMIT_BYTES = 56 * 1024 * 1024

N_Q_HEADS = 8
N_KV_HEADS = 2
HEAD_DIM = 64
Q_GROUP = N_Q_HEADS // N_KV_HEADS
WINDOW = 128
ATTN_BLOCK = 128
ALIBI_MAX = 8.0
CONV_K_B = 3
CONV_K_C = 4
LRU_HEADS = 8
LRU_C = 8.0
N_EXPERTS = 8
TOP_K = 2
NORM_EPS = 1e-6
NEG_INF = -1e30

SEQ_TILE = 512
FFN_TILE = 512
MOE_ROWS = 512
FF_CHUNK = 1792
ROUTE_LANES = LANES
ROUTE_E0 = 0
ROUTE_GATE0 = 2
ROUTE_RANK0 = 4
GELU_C0 = 0.7978845608028654
GELU_C1 = 0.044715


def _rms(x, g):
    return x * lax.rsqrt(jnp.mean(x * x, axis=-1, keepdims=True) + NORM_EPS) * g


def _split_bf16(x):
    hi = x.astype(BF16)
    lo = (x - hi.astype(F32)).astype(BF16)
    return hi, lo


def _resident(shape, index_map):
    return pl.BlockSpec(shape, index_map, pipeline_mode=pl.Buffered(1))


def _even_mixer_kernel(sinks_ref, x_ref, g_ref, win_ref, qg_ref, kg_ref, cw_ref, wout_ref, o_ref,
                       kd_ref, vd_ref, cbuf_ref, attn_ref):
    ts = x_ref.shape[0]
    q_dim = N_Q_HEADS * HEAD_DIM
    kv_dim = N_KV_HEADS * HEAD_DIM
    cw_dim = cw_ref.shape[1]
    st = pl.program_id(1)

    @pl.when(st == 0)
    def _():
        kd_ref[:, 0:ATTN_BLOCK, :] = jnp.zeros((N_KV_HEADS, ATTN_BLOCK, LANES), BF16)
        vd_ref[:, 0:ATTN_BLOCK, :] = jnp.zeros((N_KV_HEADS, ATTN_BLOCK, LANES), BF16)
        cbuf_ref[0:SUBLANES, :] = jnp.zeros((SUBLANES, cw_dim), F32)

    x = x_ref[...]
    h = _rms(x, g_ref[...]).astype(BF16)
    proj = jnp.dot(h, win_ref[...], preferred_element_type=F32)
    q = proj[:, 0:q_dim]
    k = proj[:, q_dim:q_dim + kv_dim]
    v = proj[:, q_dim + kv_dim:q_dim + 2 * kv_dim]
    o0 = q_dim + 2 * kv_dim
    gate_b = proj[:, o0:o0 + cw_dim]
    gate_c = proj[:, o0 + cw_dim:o0 + 2 * cw_dim]
    u = proj[:, o0 + 2 * cw_dim:o0 + 3 * cw_dim]

    ri = lax.broadcasted_iota(jnp.int32, (LANES, LANES), 0)
    ci = lax.broadcasted_iota(jnp.int32, (LANES, LANES), 1)
    half_mean = jnp.where((ri < HEAD_DIM) == (ci < HEAD_DIM), 1.0 / HEAD_DIM, 0.0).astype(BF16)

    def half_mean_square(z):
        hi, lo = _split_bf16(z * z)
        cols = []
        for c in range(z.shape[1] // LANES):
            sl = slice(c * LANES, (c + 1) * LANES)
            cols.append(jnp.dot(hi[:, sl], half_mean, preferred_element_type=F32)
                        + jnp.dot(lo[:, sl], half_mean, preferred_element_type=F32))
        return cols[0] if len(cols) == 1 else jnp.concatenate(cols, axis=1)

    qn = q * lax.rsqrt(half_mean_square(q) + NORM_EPS) * qg_ref[...] * (HEAD_DIM ** -0.5)
    kn = k * lax.rsqrt(half_mean_square(k) + NORM_EPS) * kg_ref[...]

    low_half_t = lax.broadcasted_iota(jnp.int32, (ts, LANES), 1) < HEAD_DIM
    k_sw = pltpu.roll(kn, HEAD_DIM, axis=1)
    v_sw = pltpu.roll(v, HEAD_DIM, axis=1)
    kd_ref[0, ATTN_BLOCK:ATTN_BLOCK + ts, :] = jnp.where(low_half_t, kn, k_sw).astype(BF16)
    kd_ref[1, ATTN_BLOCK:ATTN_BLOCK + ts, :] = jnp.where(low_half_t, k_sw, kn).astype(BF16)
    vd_ref[0, ATTN_BLOCK:ATTN_BLOCK + ts, :] = jnp.where(low_half_t, v, v_sw).astype(BF16)
    vd_ref[1, ATTN_BLOCK:ATTN_BLOCK + ts, :] = jnp.where(low_half_t, v_sw, v).astype(BF16)

    low_half = lax.broadcasted_iota(jnp.int32, (ATTN_BLOCK, LANES), 1) < HEAD_DIM
    qi = lax.broadcasted_iota(jnp.int32, (ATTN_BLOCK, 2 * ATTN_BLOCK), 0)
    sj = lax.broadcasted_iota(jnp.int32, (ATTN_BLOCK, 2 * ATTN_BLOCK), 1)
    diff = ATTN_BLOCK + qi - sj
    in_window = (diff >= 0) & (diff < WINDOW)
    diff_f = diff.astype(F32)
    first_key = jnp.where(st == 0, ATTN_BLOCK, 0)

    for j in range(ts // ATTN_BLOCK):
        rows = slice(j * ATTN_BLOCK, (j + 1) * ATTN_BLOCK)
        band = slice(j * ATTN_BLOCK, (j + 2) * ATTN_BLOCK)
        valid = (in_window & (sj >= first_key)) if j == 0 else in_window
        pair_out = []
        for g in range(N_KV_HEADS):
            q_rows = []
            for p in range(Q_GROUP // 2):
                c = g * (Q_GROUP // 2) + p
                qp = qn[rows, c * LANES:(c + 1) * LANES]
                q_rows.append(jnp.where(low_half, qp, 0.0))
                q_rows.append(jnp.where(low_half, 0.0, qp))
            qs = jnp.concatenate(q_rows, axis=0).astype(BF16)
            s = lax.dot_general(qs, kd_ref[g, band, :], (((1,), (1,)), ((), ())),
                                preferred_element_type=F32)
            probs = []
            denom = []
            for hh in range(Q_GROUP):
                head = g * Q_GROUP + hh
                slope = 2.0 ** (-ALIBI_MAX * (head + 1) / N_Q_HEADS)
                sh = s[hh * ATTN_BLOCK:(hh + 1) * ATTN_BLOCK, :] - slope * diff_f
                sh = jnp.where(valid, sh, NEG_INF)
                sink = sinks_ref[0, head]
                m = jnp.maximum(jnp.max(sh, axis=-1, keepdims=True), sink)
                pr = jnp.exp(sh - m)
                probs.append(pr)
                denom.append(jnp.sum(pr, axis=-1, keepdims=True) + jnp.exp(sink - m))
            pm = jnp.concatenate(probs, axis=0).astype(BF16)
            o = jnp.dot(pm, vd_ref[g, band, :], preferred_element_type=F32)
            o = o / jnp.concatenate(denom, axis=0)
            for p in range(Q_GROUP // 2):
                lo_rows = o[(2 * p) * ATTN_BLOCK:(2 * p + 1) * ATTN_BLOCK, :]
                hi_rows = o[(2 * p + 1) * ATTN_BLOCK:(2 * p + 2) * ATTN_BLOCK, :]
                pair_out.append(jnp.where(low_half, lo_rows, hi_rows))
        attn_ref[rows, :] = jnp.concatenate(pair_out, axis=1)

    cu = gate_c * u
    cbuf_ref[SUBLANES:SUBLANES + ts, :] = cu
    conv = cw_ref[CONV_K_B - 1:CONV_K_B, :] * cu
    for kk in range(CONV_K_B - 1):
        back = CONV_K_B - 1 - kk
        conv = conv + cw_ref[kk:kk + 1, :] * cbuf_ref[SUBLANES - back:SUBLANES - back + ts, :]
    conv = gate_b * conv

    mix = jnp.concatenate([attn_ref[...], conv], axis=1).astype(BF16)
    o_ref[...] = x + jnp.dot(mix, wout_ref[...], preferred_element_type=F32)

    kd_ref[:, 0:ATTN_BLOCK, :] = kd_ref[:, ts:ts + ATTN_BLOCK, :]
    vd_ref[:, 0:ATTN_BLOCK, :] = vd_ref[:, ts:ts + ATTN_BLOCK, :]
    cbuf_ref[0:SUBLANES, :] = cbuf_ref[ts:ts + SUBLANES, :]


def _even_mixer(x, g, w_in, q_gain, k_gain, sinks, conv_w, w_out):
    b, s, d = x.shape
    ts = SEQ_TILE
    n_in = w_in.shape[1]
    cw_dim = conv_w.shape[1]
    mix_dim = w_out.shape[0]
    qg = jnp.tile(q_gain, N_Q_HEADS).reshape(1, N_Q_HEADS * HEAD_DIM)
    kg = jnp.tile(k_gain, N_KV_HEADS).reshape(1, N_KV_HEADS * HEAD_DIM)
    const = lambda bi, si: (0, 0)
    return pl.pallas_call(
        _even_mixer_kernel,
        out_shape=jax.ShapeDtypeStruct((b, s, d), F32),
        grid=(b, s // ts),
        in_specs=[
            pl.BlockSpec(memory_space=pltpu.SMEM),
            pl.BlockSpec((None, ts, d), lambda bi, si: (bi, si, 0)),
            _resident((1, d), const),
            _resident((d, n_in), const),
            _resident((1, N_Q_HEADS * HEAD_DIM), const),
            _resident((1, N_KV_HEADS * HEAD_DIM), const),
            _resident((CONV_K_B, cw_dim), const),
            _resident((mix_dim, d), const),
        ],
        out_specs=pl.BlockSpec((None, ts, d), lambda bi, si: (bi, si, 0)),
        scratch_shapes=[
            pltpu.VMEM((N_KV_HEADS, ATTN_BLOCK + ts, LANES), BF16),
            pltpu.VMEM((N_KV_HEADS, ATTN_BLOCK + ts, LANES), BF16),
            pltpu.VMEM((SUBLANES + ts, cw_dim), F32),
            pltpu.VMEM((ts, N_Q_HEADS * HEAD_DIM), F32),
        ],
        compiler_params=pltpu.CompilerParams(
            dimension_semantics=("arbitrary", "arbitrary"), vmem_limit_bytes=VMEM_LIMIT_BYTES),
        name="even_mixer",
    )(sinks.reshape(1, N_Q_HEADS), x, g.reshape(1, d), w_in.astype(BF16), qg, kg, conv_w,
      w_out.astype(BF16))


def _swiglu_rows(hb, wg_ref, wu_ref, wd_ref):
    d_ff = wg_ref.shape[1]
    acc = None
    for c in range(d_ff // FF_CHUNK):
        cols = slice(c * FF_CHUNK, (c + 1) * FF_CHUNK)
        gate = jnp.dot(hb, wg_ref[:, cols], preferred_element_type=F32)
        up = jnp.dot(hb, wu_ref[:, cols], preferred_element_type=F32)
        act = (gate * (1.0 / (1.0 + jnp.exp(-gate))) * up).astype(BF16)
        part = jnp.dot(act, wd_ref[cols, :], preferred_element_type=F32)
        acc = part if acc is None else acc + part
    return acc


def _ffn_kernel(x_ref, g_ref, wg_ref, wu_ref, wd_ref, o_ref):
    x = x_ref[...]
    hb = _rms(x, g_ref[...]).astype(BF16)
    o_ref[...] = x + _swiglu_rows(hb, wg_ref, wu_ref, wd_ref)


def _ffn(x, g, w_gate, w_up, w_down):
    t, d = x.shape
    d_ff = w_gate.shape[1]
    tm = FFN_TILE
    const = lambda i: (0, 0)
    return pl.pallas_call(
        _ffn_kernel,
        out_shape=jax.ShapeDtypeStruct((t, d), F32),
        grid=(t // tm,),
        in_specs=[
            pl.BlockSpec((tm, d), lambda i: (i, 0)),
            _resident((1, d), const),
            _resident((d, d_ff), const),
            _resident((d, d_ff), const),
            _resident((d_ff, d), const),
        ],
        out_specs=pl.BlockSpec((tm, d), lambda i: (i, 0)),
        compiler_params=pltpu.CompilerParams(
            dimension_semantics=("arbitrary",), vmem_limit_bytes=VMEM_LIMIT_BYTES),
        name="dense_ffn",
    )(x, g.reshape(1, d), w_gate.astype(BF16), w_up.astype(BF16), w_down.astype(BF16))


def _odd_mixer_kernel(x_ref, g_ref, win_ref, cw_ref, cb_ref, gw_ref, gab_ref, gxb_ref, lam_ref, wout_ref,
                      g2_ref, wr_ref, tri_ref, o_ref, h2_ref, route_ref, rt_ref, cnt_out_ref,
                      xbuf_ref, a_ref, b_ref, hs_ref, hcar_ref, cnt_ref):
    ts = x_ref.shape[0]
    w = cw_ref.shape[1]
    hd = w // LRU_HEADS
    st = pl.program_id(1)

    @pl.when(st == 0)
    def _():
        xbuf_ref[0:SUBLANES, :] = jnp.zeros((SUBLANES, w), F32)
        hcar_ref[...] = jnp.zeros((1, w), F32)

    @pl.when((st == 0) & (pl.program_id(0) == 0))
    def _():
        cnt_ref[...] = jnp.zeros((1, ROUTE_LANES), F32)

    x = x_ref[...]
    h = _rms(x, g_ref[...]).astype(BF16)
    proj = jnp.dot(h, win_ref[...], preferred_element_type=F32)
    y = proj[:, 0:w]
    y = y * (0.5 * (1.0 + jnp.tanh(GELU_C0 * (y + GELU_C1 * (y * y * y)))))
    xb = proj[:, w:2 * w]

    xbuf_ref[SUBLANES:SUBLANES + ts, :] = xb
    xc = cw_ref[CONV_K_C - 1:CONV_K_C, :] * xb + cb_ref[...]
    for kk in range(CONV_K_C - 1):
        back = CONV_K_C - 1 - kk
        xc = xc + cw_ref[kk:kk + 1, :] * xbuf_ref[SUBLANES - back:SUBLANES - back + ts, :]

    r_cols = []
    i_cols = []
    for hh in range(LRU_HEADS):
        ri = jnp.dot(xc[:, hh * hd:(hh + 1) * hd].astype(BF16), gw_ref[hh], preferred_element_type=F32)
        r_cols.append(ri[:, 0:hd])
        i_cols.append(ri[:, hd:2 * hd])
    r_pre = jnp.concatenate(r_cols, axis=1) + gab_ref[...]
    i_pre = jnp.concatenate(i_cols, axis=1) + gxb_ref[...]
    r = 1.0 / (1.0 + jnp.exp(-r_pre))
    ig = 1.0 / (1.0 + jnp.exp(-i_pre))
    neg_lam = -lam_ref[...]
    softplus = jnp.maximum(neg_lam, 0.0) + jnp.log1p(jnp.exp(-jnp.abs(neg_lam)))
    log_a = -LRU_C * r * softplus
    a = jnp.exp(log_a)
    th = jnp.tanh(log_a)
    mult = jnp.sqrt(jnp.maximum(-2.0 * th / (1.0 - th), 0.0))
    row = lax.broadcasted_iota(jnp.int32, (ts, w), 0)
    mult = jnp.where((row == 0) & (st == 0), 1.0, mult)
    a_ref[...] = a
    b_ref[...] = mult * (ig * xc)

    srow = lax.broadcasted_iota(jnp.int32, (SUBLANES, w), 0)

    def group(i, hprev):
        off = pl.multiple_of(i * SUBLANES, SUBLANES)
        av = a_ref[pl.ds(off, SUBLANES), :]
        bv = b_ref[pl.ds(off, SUBLANES), :]
        for sh in (1, 2, 4):
            a_sh = jnp.where(srow >= sh, pltpu.roll(av, sh, axis=0), 1.0)
            b_sh = jnp.where(srow >= sh, pltpu.roll(bv, sh, axis=0), 0.0)
            bv = bv + av * b_sh
            av = av * a_sh
        hv = bv + av * hprev
        hs_ref[pl.ds(off, SUBLANES), :] = hv
        return hv[SUBLANES - 1:SUBLANES, :]

    hcar_ref[...] = lax.fori_loop(0, ts // SUBLANES, group, hcar_ref[...])

    out = x + jnp.dot((y * hs_ref[...]).astype(BF16), wout_ref[...], preferred_element_type=F32)
    o_ref[...] = out
    xbuf_ref[0:SUBLANES, :] = xbuf_ref[ts:ts + SUBLANES, :]

    h2 = _rms(out, g2_ref[...])
    h2_ref[...] = h2
    h2_hi, h2_lo = _split_bf16(h2)
    wr_hi, wr_lo = _split_bf16(wr_ref[...])
    logits = (jnp.dot(h2_hi, wr_hi, preferred_element_type=F32)
              + jnp.dot(h2_lo, wr_hi, preferred_element_type=F32)
              + jnp.dot(h2_hi, wr_lo, preferred_element_type=F32))
    lane = lax.broadcasted_iota(jnp.int32, (ts, ROUTE_LANES), 1)
    lg = jnp.where(lane < N_EXPERTS, logits, -jnp.inf)
    m1 = jnp.max(lg, axis=-1, keepdims=True)
    i1 = jnp.min(jnp.where(lg == m1, lane, ROUTE_LANES), axis=-1, keepdims=True)
    lg2 = jnp.where(lane == i1, -jnp.inf, lg)
    m2 = jnp.max(lg2, axis=-1, keepdims=True)
    i2 = jnp.min(jnp.where(lg2 == m2, lane, ROUTE_LANES), axis=-1, keepdims=True)
    e2 = jnp.exp(m2 - m1)
    gate1 = 1.0 / (1.0 + e2)
    gate2 = e2 / (1.0 + e2)

    first = lane == i1
    second = lane == i2
    ind = jnp.where(first | second, 1.0, 0.0)
    cums = jnp.dot(tri_ref[...], ind.astype(BF16), preferred_element_type=F32)
    rank = cnt_ref[...] + cums - ind
    cnt_ref[...] = cnt_ref[...] + cums[ts - 1:ts, :]
    rank1 = jnp.sum(jnp.where(first, rank, 0.0), axis=-1, keepdims=True)
    rank2 = jnp.sum(jnp.where(second, rank, 0.0), axis=-1, keepdims=True)
    cnt_out_ref[...] = jnp.broadcast_to(cnt_ref[...], (SUBLANES, ROUTE_LANES)).astype(jnp.int32)

    record = jnp.where(lane == ROUTE_E0, i1.astype(F32), 0.0)
    record = jnp.where(lane == ROUTE_E0 + 1, i2.astype(F32), record)
    record = jnp.where(lane == ROUTE_GATE0, gate1, record)
    record = jnp.where(lane == ROUTE_GATE0 + 1, gate2, record)
    record = jnp.where(lane == ROUTE_RANK0, rank1, record)
    record = jnp.where(lane == ROUTE_RANK0 + 1, rank2, record)
    route_ref[...] = record
    rt_ref[...] = record.T[0:SUBLANES, :].astype(jnp.int32)


def _odd_mixer(x, g, w_in, conv_w, conv_b, ga_w, ga_b, gx_w, gx_b, lam, w_out, g2, w_router):
    b, s, d = x.shape
    ts = SEQ_TILE
    w = conv_w.shape[1]
    hd = w // LRU_HEADS
    gw = jnp.concatenate([ga_w, gx_w], axis=-1).astype(BF16)
    wr = jnp.pad(w_router, ((0, 0), (0, ROUTE_LANES - N_EXPERTS)))
    tri = jnp.tril(jnp.ones((ts, ts), BF16))
    nst = s // ts
    const = lambda bi, si: (0, 0)
    row = lambda a: a.reshape(1, -1)
    tile = lambda bi, si: (bi, si, 0)
    return pl.pallas_call(
        _odd_mixer_kernel,
        out_shape=(jax.ShapeDtypeStruct((b, s, d), F32),
                   jax.ShapeDtypeStruct((b, s, d), F32),
                   jax.ShapeDtypeStruct((b, s, ROUTE_LANES), F32),
                   jax.ShapeDtypeStruct((b * nst, SUBLANES, ts), jnp.int32),
                   jax.ShapeDtypeStruct((SUBLANES, ROUTE_LANES), jnp.int32)),
        grid=(b, nst),
        in_specs=[
            pl.BlockSpec((None, ts, d), lambda bi, si: (bi, si, 0)),
            _resident((1, d), const),
            _resident((d, 2 * w), const),
            _resident((CONV_K_C, w), const),
            _resident((1, w), const),
            _resident((LRU_HEADS, hd, 2 * hd), lambda bi, si: (0, 0, 0)),
            _resident((1, w), const),
            _resident((1, w), const),
            _resident((1, w), const),
            _resident((w, d), const),
            _resident((1, d), const),
            _resident((d, ROUTE_LANES), const),
            _resident((ts, ts), const),
        ],
        out_specs=(pl.BlockSpec((None, ts, d), tile),
                   pl.BlockSpec((None, ts, d), tile),
                   pl.BlockSpec((None, ts, ROUTE_LANES), tile),
                   pl.BlockSpec((None, SUBLANES, ts), lambda bi, si: (bi * nst + si, 0, 0)),
                   pl.BlockSpec((SUBLANES, ROUTE_LANES), const)),
        scratch_shapes=[
            pltpu.VMEM((SUBLANES + ts, w), F32),
            pltpu.VMEM((ts, w), F32),
            pltpu.VMEM((ts, w), F32),
            pltpu.VMEM((ts, w), F32),
            pltpu.VMEM((1, w), F32),
            pltpu.VMEM((1, ROUTE_LANES), F32),
        ],
        compiler_params=pltpu.CompilerParams(
            dimension_semantics=("arbitrary", "arbitrary"), vmem_limit_bytes=VMEM_LIMIT_BYTES),
        name="odd_mixer",
    )(x, row(g), w_in.astype(BF16), conv_w, row(conv_b), gw, row(ga_b), row(gx_b), row(lam),
      w_out.astype(BF16), row(g2), wr, tri)


def _moe_tables(rt, counts, t):
    r = MOE_ROWS
    n_blocks = TOP_K * t // r + N_EXPERTS
    counts = counts[0, 0:N_EXPERTS]
    padded = (counts + r - 1) // r * r
    pad_end = jnp.cumsum(padded)
    pad_start = pad_end - padded
    e = rt[:, ROUTE_E0:ROUTE_E0 + TOP_K, :]
    rank = rt[:, ROUTE_RANK0:ROUTE_RANK0 + TOP_K, :]
    start = jnp.zeros_like(e)
    for ex in range(N_EXPERTS):
        start = jnp.where(e == ex, pad_start[ex], start)
    dest = start + rank
    blk = jnp.arange(n_blocks, dtype=jnp.int32)
    blk_expert = jnp.minimum(jnp.searchsorted(pad_end, blk * r, side="right"), N_EXPERTS - 1)
    n_used = pad_end[N_EXPERTS - 1] // r
    last_blk = jnp.where(padded > 0, pad_end // r - 1, -1)
    has_pad = jnp.any(blk[:, None] == last_blk[None, :], axis=1) | (blk >= n_used)
    return (dest.astype(jnp.int32), blk_expert.astype(jnp.int32), has_pad.astype(jnp.int32),
            n_used.astype(jnp.int32).reshape(1))


def _rows_wait(src_rows, dst_rows, sem, n):
    for _ in range(n):
        pltpu.make_async_copy(src_rows, dst_rows, sem).wait()


def _dispatch_kernel(pad_ref, dest_hbm, h_hbm, xg_hbm, dsm, zbuf, sem_idx, sem_row, sem_zero):
    i = pl.program_id(0)
    n = pl.num_programs(0)
    tm = dsm.shape[2]
    r = zbuf.shape[0]
    n_blocks = pad_ref.shape[0]
    slot = lax.rem(i, 2)

    def idx_copy(step, s):
        return pltpu.make_async_copy(dest_hbm.at[step], dsm.at[s], sem_idx.at[s])

    def zero_copy(blk):
        return pltpu.make_async_copy(zbuf, xg_hbm.at[pl.ds(blk * r, r), :], sem_zero)

    @pl.when(i == 0)
    def _():
        idx_copy(0, 0).start()
        zbuf[...] = jnp.zeros(zbuf.shape, zbuf.dtype)

        def zero_start(blk, c):
            @pl.when(pad_ref[blk] == 1)
            def _():
                zero_copy(blk).start()
            return c

        def zero_wait(blk, c):
            @pl.when(pad_ref[blk] == 1)
            def _():
                zero_copy(blk).wait()
            return c

        lax.fori_loop(0, n_blocks, zero_start, 0)
        lax.fori_loop(0, n_blocks, zero_wait, 0)

    @pl.when(i + 1 < n)
    def _():
        idx_copy(i + 1, 1 - slot).start()

    idx_copy(i, slot).wait()

    def issue(row, c):
        for k in range(TOP_K):
            pltpu.make_async_copy(h_hbm.at[pl.ds(i * tm + row, 1), :],
                                  xg_hbm.at[pl.ds(dsm[slot, k, row], 1), :], sem_row).start()
        return c

    lax.fori_loop(0, tm, issue, 0, unroll=8)

    tile_rows = (h_hbm.at[pl.ds(0, tm), :], xg_hbm.at[pl.ds(0, tm), :])

    @pl.when(i > 0)
    def _():
        _rows_wait(*tile_rows, sem_row, TOP_K)

    @pl.when(i == n - 1)
    def _():
        _rows_wait(*tile_rows, sem_row, TOP_K)


def _dispatch(h2, dest, has_pad):
    t, d = h2.shape
    tm = dest.shape[2]
    r = MOE_ROWS
    n_rows = TOP_K * t + N_EXPERTS * r
    grid_spec = pltpu.PrefetchScalarGridSpec(
        num_scalar_prefetch=1,
        grid=(t // tm,),
        in_specs=[pl.BlockSpec(memory_space=pl.ANY), pl.BlockSpec(memory_space=pl.ANY)],
        out_specs=pl.BlockSpec(memory_space=pl.ANY),
        scratch_shapes=[
            pltpu.SMEM((2, TOP_K, tm), jnp.int32),
            pltpu.VMEM((r, d), F32),
            pltpu.SemaphoreType.DMA((2,)),
            pltpu.SemaphoreType.DMA,
            pltpu.SemaphoreType.DMA,
        ],
    )
    return pl.pallas_call(
        _dispatch_kernel,
        out_shape=jax.ShapeDtypeStruct((n_rows, d), F32),
        grid_spec=grid_spec,
        compiler_params=pltpu.CompilerParams(dimension_semantics=("arbitrary",)),
        name="moe_dispatch",
    )(has_pad, dest, h2)


def _expert_kernel(be_ref, nu_ref, x_ref, wg_ref, wu_ref, wd_ref, y_ref):
    del be_ref
    i = pl.program_id(0)

    @pl.when(i < nu_ref[0])
    def _():
        y_ref[...] = _swiglu_rows(x_ref[...].astype(BF16), wg_ref, wu_ref, wd_ref)

    @pl.when(i >= nu_ref[0])
    def _():
        y_ref[...] = jnp.zeros(y_ref.shape, y_ref.dtype)


def _experts(xg, blk_expert, n_used, w_gate, w_up, w_down):
    n_rows, d = xg.shape
    d_ff = w_gate.shape[2]
    r = MOE_ROWS
    by_expert = lambda i, be, nu: (be[i], 0, 0)
    rows = lambda i, be, nu: (i, 0)
    grid_spec = pltpu.PrefetchScalarGridSpec(
        num_scalar_prefetch=2,
        grid=(n_rows // r,),
        in_specs=[
            pl.BlockSpec((r, d), rows),
            pl.BlockSpec((None, d, d_ff), by_expert, pipeline_mode=pl.Buffered(1)),
            pl.BlockSpec((None, d, d_ff), by_expert, pipeline_mode=pl.Buffered(1)),
            pl.BlockSpec((None, d_ff, d), by_expert, pipeline_mode=pl.Buffered(1)),
        ],
        out_specs=pl.BlockSpec((r, d), rows),
    )
    return pl.pallas_call(
        _expert_kernel,
        out_shape=jax.ShapeDtypeStruct((n_rows, d), F32),
        grid_spec=grid_spec,
        compiler_params=pltpu.CompilerParams(
            dimension_semantics=("arbitrary",), vmem_limit_bytes=VMEM_LIMIT_BYTES),
        name="moe_experts",
    )(blk_expert, n_used, xg, w_gate.astype(BF16), w_up.astype(BF16), w_down.astype(BF16))


def _combine_kernel(dest_hbm, y_hbm, x_ref, route_ref, o_ref, dsm, ybuf, sem_idx, sem_row):
    i = pl.program_id(0)
    n = pl.num_programs(0)
    tm = x_ref.shape[0]
    slot = lax.rem(i, 2)

    def fetch(step, s):
        idx = pltpu.make_async_copy(dest_hbm.at[step], dsm.at[s], sem_idx.at[s])
        idx.start()
        idx.wait()

        def issue(row, c):
            for k in range(TOP_K):
                pltpu.make_async_copy(y_hbm.at[pl.ds(dsm[s, k, row], 1), :],
                                      ybuf.at[s, k, pl.ds(row, 1), :], sem_row.at[s]).start()
            return c

        lax.fori_loop(0, tm, issue, 0, unroll=8)

    @pl.when(i == 0)
    def _():
        fetch(0, 0)

    @pl.when(i + 1 < n)
    def _():
        fetch(i + 1, 1 - slot)

    _rows_wait(y_hbm.at[pl.ds(0, tm), :], ybuf.at[slot, 0], sem_row.at[slot], TOP_K)
    rt = route_ref[...]
    moe = ybuf[slot, 0] * rt[:, ROUTE_GATE0:ROUTE_GATE0 + 1]
    for k in range(1, TOP_K):
        moe = moe + ybuf[slot, k] * rt[:, ROUTE_GATE0 + k:ROUTE_GATE0 + k + 1]
    o_ref[...] = x_ref[...] + moe


def _combine(x, y, route, dest):
    t, d = x.shape
    tm = dest.shape[2]
    return pl.pallas_call(
        _combine_kernel,
        out_shape=jax.ShapeDtypeStruct((t, d), F32),
        grid=(t // tm,),
        in_specs=[
            pl.BlockSpec(memory_space=pl.ANY),
            pl.BlockSpec(memory_space=pl.ANY),
            pl.BlockSpec((tm, d), lambda i: (i, 0)),
            pl.BlockSpec((tm, ROUTE_LANES), lambda i: (i, 0)),
        ],
        out_specs=pl.BlockSpec((tm, d), lambda i: (i, 0)),
        scratch_shapes=[
            pltpu.SMEM((2, TOP_K, tm), jnp.int32),
            pltpu.VMEM((2, TOP_K, tm, d), F32),
            pltpu.SemaphoreType.DMA((2,)),
            pltpu.SemaphoreType.DMA((2,)),
        ],
        compiler_params=pltpu.CompilerParams(
            dimension_semantics=("arbitrary",), vmem_limit_bytes=VMEM_LIMIT_BYTES),
        name="moe_combine",
    )(dest, y, x, route)


def kernel(x, norm_mix, norm_ffn, hy_w_in, hy_q_gain, hy_k_gain, hy_sinks, hy_conv_w, hy_w_out, rg_w_in, rg_conv_w, rg_conv_b, rg_gate_a_w, rg_gate_a_b, rg_gate_x_w, rg_gate_x_b, rg_lambda, rg_w_out, ffn_w_gate, ffn_w_up, ffn_w_down, moe_router, moe_w_gate, moe_w_up, moe_w_down):
    b, s, d = x.shape
    t = b * s
    depth = norm_mix.shape[0]
    for layer in range(depth):
        j = layer // 2
        if layer % 2 == 0:
            x = _even_mixer(x, norm_mix[layer], hy_w_in[j], hy_q_gain[j], hy_k_gain[j], hy_sinks[j],
                            hy_conv_w[j], hy_w_out[j])
            x = _ffn(x.reshape(t, d), norm_ffn[layer], ffn_w_gate[j], ffn_w_up[j], ffn_w_down[j])
            x = x.reshape(b, s, d)
        else:
            x, h2, route, rt, counts = _odd_mixer(
                x, norm_mix[layer], rg_w_in[j], rg_conv_w[j], rg_conv_b[j], rg_gate_a_w[j], rg_gate_a_b[j],
                rg_gate_x_w[j], rg_gate_x_b[j], rg_lambda[j], rg_w_out[j], norm_ffn[layer], moe_router[j])
            dest, blk_expert, has_pad, n_used = _moe_tables(rt, counts, t)
            xg = _dispatch(h2.reshape(t, d), dest, has_pad)
            y = _experts(xg, blk_expert, n_used, moe_w_gate[j], moe_w_up[j], moe_w_down[j])
            x = _combine(x.reshape(t, d), y, route.reshape(t, ROUTE_LANES), dest).reshape(b, s, d)
    return x
```

```python
import jax
import jax.numpy as jnp
from jax import lax
from jax.experimental import pallas as pl
from jax.experimental.pallas import tpu as pltpu

F32 = jnp.float32
BF16 = jnp.bfloat16

LANES = 128
SUBLANES = 8
VMEM_LIMIT_BYTES = 56 * 1024 * 1024

N_Q_HEADS = 8
N_KV_HEADS = 2
HEAD_DIM = 64
Q_GROUP = N_Q_HEADS // N_KV_HEADS
WINDOW = 128
ATTN_BLOCK = 128
ALIBI_MAX = 8.0
CONV_K_B = 3
CONV_K_C = 4
LRU_HEADS = 8
LRU_C = 8.0
N_EXPERTS = 8
TOP_K = 2
NORM_EPS = 1e-6
NEG_INF = -1e30

SEQ_TILE = 512
FFN_TILE = 512
MOE_ROWS = 512
FF_CHUNK = 1792
ROUTE_LANES = LANES
ROUTE_E0 = 0
ROUTE_GATE0 = 2
ROUTE_RANK0 = 4
GELU_C0 = 0.7978845608028654
GELU_C1 = 0.044715


def _rms(x, g):
    return x * lax.rsqrt(jnp.mean(x * x, axis=-1, keepdims=True) + NORM_EPS) * g


def _split_bf16(x):
    hi = x.astype(BF16)
    lo = (x - hi.astype(F32)).astype(BF16)
    return hi, lo


def _store_row_tiles(ref, start, x):
    n = x.shape[0]
    for c in range(x.shape[1] // LANES):
        ref[pl.ds(start + c, n, stride=SUBLANES), :] = x[:, c * LANES:(c + 1) * LANES]


def _load_row_tiles(ref, start, n):
    return jnp.concatenate([ref[pl.ds(start + c, n, stride=SUBLANES), :] for c in range(SUBLANES)], axis=1)


def _resident(shape, index_map):
    return pl.BlockSpec(shape, index_map, pipeline_mode=pl.Buffered(1))


def _even_mixer_kernel(sinks_ref, x_ref, g_ref, win_ref, qg_ref, kg_ref, cw_ref, wout_ref, o_ref,
                       kd_ref, vd_ref, cbuf_ref, attn_ref):
    ts = x_ref.shape[0]
    q_dim = N_Q_HEADS * HEAD_DIM
    kv_dim = N_KV_HEADS * HEAD_DIM
    cw_dim = cw_ref.shape[1]
    st = pl.program_id(1)

    @pl.when(st == 0)
    def _():
        kd_ref[:, 0:ATTN_BLOCK, :] = jnp.zeros((N_KV_HEADS, ATTN_BLOCK, LANES), BF16)
        vd_ref[:, 0:ATTN_BLOCK, :] = jnp.zeros((N_KV_HEADS, ATTN_BLOCK, LANES), BF16)
        cbuf_ref[0:SUBLANES, :] = jnp.zeros((SUBLANES, cw_dim), F32)

    x = x_ref[...]
    h = _rms(x, g_ref[...]).astype(BF16)
    proj = jnp.dot(h, win_ref[...], preferred_element_type=F32)
    q = proj[:, 0:q_dim]
    k = proj[:, q_dim:q_dim + kv_dim]
    v = proj[:, q_dim + kv_dim:q_dim + 2 * kv_dim]
    o0 = q_dim + 2 * kv_dim
    gate_b = proj[:, o0:o0 + cw_dim]
    gate_c = proj[:, o0 + cw_dim:o0 + 2 * cw_dim]
    u = proj[:, o0 + 2 * cw_dim:o0 + 3 * cw_dim]

    ri = lax.broadcasted_iota(jnp.int32, (LANES, LANES), 0)
    ci = lax.broadcasted_iota(jnp.int32, (LANES, LANES), 1)
    half_mean = jnp.where((ri < HEAD_DIM) == (ci < HEAD_DIM), 1.0 / HEAD_DIM, 0.0).astype(BF16)

    def half_mean_square(z):
        hi, lo = _split_bf16(z * z)
        cols = []
        for c in range(z.shape[1] // LANES):
            sl = slice(c * LANES, (c + 1) * LANES)
            cols.append(jnp.dot(hi[:, sl], half_mean, preferred_element_type=F32)
                        + jnp.dot(lo[:, sl], half_mean, preferred_element_type=F32))
        return cols[0] if len(cols) == 1 else jnp.concatenate(cols, axis=1)

    qn = q * lax.rsqrt(half_mean_square(q) + NORM_EPS) * qg_ref[...] * (HEAD_DIM ** -0.5)
    kn = k * lax.rsqrt(half_mean_square(k) + NORM_EPS) * kg_ref[...]

    low_half_t = lax.broadcasted_iota(jnp.int32, (ts, LANES), 1) < HEAD_DIM
    k_sw = pltpu.roll(kn, HEAD_DIM, axis=1)
    v_sw = pltpu.roll(v, HEAD_DIM, axis=1)
    kd_ref[0, ATTN_BLOCK:ATTN_BLOCK + ts, :] = jnp.where(low_half_t, kn, k_sw).astype(BF16)
    kd_ref[1, ATTN_BLOCK:ATTN_BLOCK + ts, :] = jnp.where(low_half_t, k_sw, kn).astype(BF16)
    vd_ref[0, ATTN_BLOCK:ATTN_BLOCK + ts, :] = jnp.where(low_half_t, v, v_sw).astype(BF16)
    vd_ref[1, ATTN_BLOCK:ATTN_BLOCK + ts, :] = jnp.where(low_half_t, v_sw, v).astype(BF16)

    low_half = lax.broadcasted_iota(jnp.int32, (ATTN_BLOCK, LANES), 1) < HEAD_DIM
    qi = lax.broadcasted_iota(jnp.int32, (ATTN_BLOCK, 2 * ATTN_BLOCK), 0)
    sj = lax.broadcasted_iota(jnp.int32, (ATTN_BLOCK, 2 * ATTN_BLOCK), 1)
    diff = ATTN_BLOCK + qi - sj
    in_window = (diff >= 0) & (diff < WINDOW)
    diff_f = diff.astype(F32)
    first_key = jnp.where(st == 0, ATTN_BLOCK, 0)

    for j in range(ts // ATTN_BLOCK):
        rows = slice(j * ATTN_BLOCK, (j + 1) * ATTN_BLOCK)
        band = slice(j * ATTN_BLOCK, (j + 2) * ATTN_BLOCK)
        valid = (in_window & (sj >= first_key)) if j == 0 else in_window
        pair_out = []
        for g in range(N_KV_HEADS):
            q_rows = []
            for p in range(Q_GROUP // 2):
                c = g * (Q_GROUP // 2) + p
                qp = qn[rows, c * LANES:(c + 1) * LANES]
                q_rows.append(jnp.where(low_half, qp, 0.0))
                q_rows.append(jnp.where(low_half, 0.0, qp))
            qs = jnp.concatenate(q_rows, axis=0).astype(BF16)
            s = lax.dot_general(qs, kd_ref[g, band, :], (((1,), (1,)), ((), ())),
                                preferred_element_type=F32)
            probs = []
            denom = []
            for hh in range(Q_GROUP):
                head = g * Q_GROUP + hh
                slope = 2.0 ** (-ALIBI_MAX * (head + 1) / N_Q_HEADS)
                sh = s[hh * ATTN_BLOCK:(hh + 1) * ATTN_BLOCK, :] - slope * diff_f
                sh = jnp.where(valid, sh, NEG_INF)
                sink = sinks_ref[0, head]
                m = jnp.maximum(jnp.max(sh, axis=-1, keepdims=True), sink)
                pr = jnp.exp(sh - m)
                probs.append(pr)
                denom.append(jnp.sum(pr, axis=-1, keepdims=True) + jnp.exp(sink - m))
            pm = jnp.concatenate(probs, axis=0).astype(BF16)
            o = jnp.dot(pm, vd_ref[g, band, :], preferred_element_type=F32)
            o = o / jnp.concatenate(denom, axis=0)
            for p in range(Q_GROUP // 2):
                lo_rows = o[(2 * p) * ATTN_BLOCK:(2 * p + 1) * ATTN_BLOCK, :]
                hi_rows = o[(2 * p + 1) * ATTN_BLOCK:(2 * p + 2) * ATTN_BLOCK, :]
                pair_out.append(jnp.where(low_half, lo_rows, hi_rows))
        attn_ref[rows, :] = jnp.concatenate(pair_out, axis=1)

    cu = gate_c * u
    cbuf_ref[SUBLANES:SUBLANES + ts, :] = cu
    conv = cw_ref[CONV_K_B - 1:CONV_K_B, :] * cu
    for kk in range(CONV_K_B - 1):
        back = CONV_K_B - 1 - kk
        conv = conv + cw_ref[kk:kk + 1, :] * cbuf_ref[SUBLANES - back:SUBLANES - back + ts, :]
    conv = gate_b * conv

    mix = jnp.concatenate([attn_ref[...], conv], axis=1).astype(BF16)
    o_ref[...] = x + jnp.dot(mix, wout_ref[...], preferred_element_type=F32)

    kd_ref[:, 0:ATTN_BLOCK, :] = kd_ref[:, ts:ts + ATTN_BLOCK, :]
    vd_ref[:, 0:ATTN_BLOCK, :] = vd_ref[:, ts:ts + ATTN_BLOCK, :]
    cbuf_ref[0:SUBLANES, :] = cbuf_ref[ts:ts + SUBLANES, :]


def _even_mixer(x, g, w_in, q_gain, k_gain, sinks, conv_w, w_out):
    b, s, d = x.shape
    ts = SEQ_TILE
    n_in = w_in.shape[1]
    cw_dim = conv_w.shape[1]
    mix_dim = w_out.shape[0]
    qg = jnp.tile(q_gain, N_Q_HEADS).reshape(1, N_Q_HEADS * HEAD_DIM)
    kg = jnp.tile(k_gain, N_KV_HEADS).reshape(1, N_KV_HEADS * HEAD_DIM)
    const = lambda bi, si: (0, 0)
    return pl.pallas_call(
        _even_mixer_kernel,
        out_shape=jax.ShapeDtypeStruct((b, s, d), F32),
        grid=(b, s // ts),
        in_specs=[
            pl.BlockSpec(memory_space=pltpu.SMEM),
            pl.BlockSpec((None, ts, d), lambda bi, si: (bi, si, 0)),
            _resident((1, d), const),
            _resident((d, n_in), const),
            _resident((1, N_Q_HEADS * HEAD_DIM), const),
            _resident((1, N_KV_HEADS * HEAD_DIM), const),
            _resident((CONV_K_B, cw_dim), const),
            _resident((mix_dim, d), const),
        ],
        out_specs=pl.BlockSpec((None, ts, d), lambda bi, si: (bi, si, 0)),
        scratch_shapes=[
            pltpu.VMEM((N_KV_HEADS, ATTN_BLOCK + ts, LANES), BF16),
            pltpu.VMEM((N_KV_HEADS, ATTN_BLOCK + ts, LANES), BF16),
            pltpu.VMEM((SUBLANES + ts, cw_dim), F32),
            pltpu.VMEM((ts, N_Q_HEADS * HEAD_DIM), F32),
        ],
        compiler_params=pltpu.CompilerParams(
            dimension_semantics=("arbitrary", "arbitrary"), vmem_limit_bytes=VMEM_LIMIT_BYTES),
        name="even_mixer",
    )(sinks.reshape(1, N_Q_HEADS), x, g.reshape(1, d), w_in.astype(BF16), qg, kg, conv_w,
      w_out.astype(BF16))


def _swiglu_rows(hb, wg_ref, wu_ref, wd_ref):
    d_ff = wg_ref.shape[1]
    acc = None
    for c in range(d_ff // FF_CHUNK):
        cols = slice(c * FF_CHUNK, (c + 1) * FF_CHUNK)
        gate = jnp.dot(hb, wg_ref[:, cols], preferred_element_type=F32)
        up = jnp.dot(hb, wu_ref[:, cols], preferred_element_type=F32)
        act = (gate * (1.0 / (1.0 + jnp.exp(-gate))) * up).astype(BF16)
        part = jnp.dot(act, wd_ref[cols, :], preferred_element_type=F32)
        acc = part if acc is None else acc + part
    return acc


def _ffn_kernel(x_ref, g_ref, wg_ref, wu_ref, wd_ref, o_ref):
    x = x_ref[...]
    hb = _rms(x, g_ref[...]).astype(BF16)
    o_ref[...] = x + _swiglu_rows(hb, wg_ref, wu_ref, wd_ref)


def _ffn(x, g, w_gate, w_up, w_down):
    t, d = x.shape
    d_ff = w_gate.shape[1]
    tm = FFN_TILE
    const = lambda i: (0, 0)
    return pl.pallas_call(
        _ffn_kernel,
        out_shape=jax.ShapeDtypeStruct((t, d), F32),
        grid=(t // tm,),
        in_specs=[
            pl.BlockSpec((tm, d), lambda i: (i, 0)),
            _resident((1, d), const),
            _resident((d, d_ff), const),
            _resident((d, d_ff), const),
            _resident((d_ff, d), const),
        ],
        out_specs=pl.BlockSpec((tm, d), lambda i: (i, 0)),
        compiler_params=pltpu.CompilerParams(
            dimension_semantics=("arbitrary",), vmem_limit_bytes=VMEM_LIMIT_BYTES),
        name="dense_ffn",
    )(x, g.reshape(1, d), w_gate.astype(BF16), w_up.astype(BF16), w_down.astype(BF16))


def _odd_mixer_kernel(x_ref, g_ref, win_ref, cw_ref, cb_ref, gw_ref, gab_ref, gxb_ref, lam_ref, wout_ref,
                      g2_ref, wr_ref, tri_ref, o_ref, h2_ref, route_ref, rt_ref, cnt_out_ref,
                      xbuf_ref, a_ref, b_ref, hs_ref, hcar_ref, cnt_ref):
    ts = x_ref.shape[0]
    w = cw_ref.shape[1]
    hd = w // LRU_HEADS
    st = pl.program_id(1)

    @pl.when(st == 0)
    def _():
        xbuf_ref[0:SUBLANES, :] = jnp.zeros((SUBLANES, w), F32)
        hcar_ref[...] = jnp.zeros((1, w), F32)

    @pl.when((st == 0) & (pl.program_id(0) == 0))
    def _():
        cnt_ref[...] = jnp.zeros((1, ROUTE_LANES), F32)

    x = x_ref[...]
    h = _rms(x, g_ref[...]).astype(BF16)
    proj = jnp.dot(h, win_ref[...], preferred_element_type=F32)
    y = proj[:, 0:w]
    y = y * (0.5 * (1.0 + jnp.tanh(GELU_C0 * (y + GELU_C1 * (y * y * y)))))
    xb = proj[:, w:2 * w]

    xbuf_ref[SUBLANES:SUBLANES + ts, :] = xb
    xc = cw_ref[CONV_K_C - 1:CONV_K_C, :] * xb + cb_ref[...]
    for kk in range(CONV_K_C - 1):
        back = CONV_K_C - 1 - kk
        xc = xc + cw_ref[kk:kk + 1, :] * xbuf_ref[SUBLANES - back:SUBLANES - back + ts, :]

    r_cols = []
    i_cols = []
    for hh in range(LRU_HEADS):
        ri = jnp.dot(xc[:, hh * hd:(hh + 1) * hd].astype(BF16), gw_ref[hh], preferred_element_type=F32)
        r_cols.append(ri[:, 0:hd])
        i_cols.append(ri[:, hd:2 * hd])
    r_pre = jnp.concatenate(r_cols, axis=1) + gab_ref[...]
    i_pre = jnp.concatenate(i_cols, axis=1) + gxb_ref[...]
    r = 1.0 / (1.0 + jnp.exp(-r_pre))
    ig = 1.0 / (1.0 + jnp.exp(-i_pre))
    neg_lam = -lam_ref[...]
    softplus = jnp.maximum(neg_lam, 0.0) + jnp.log1p(jnp.exp(-jnp.abs(neg_lam)))
    log_a = -LRU_C * r * softplus
    a = jnp.exp(log_a)
    th = jnp.tanh(log_a)
    mult = jnp.sqrt(jnp.maximum(-2.0 * th / (1.0 - th), 0.0))
    row = lax.broadcasted_iota(jnp.int32, (ts, w), 0)
    mult = jnp.where((row == 0) & (st == 0), 1.0, mult)
    a_ref[...] = a
    b_ref[...] = mult * (ig * xc)

    srow = lax.broadcasted_iota(jnp.int32, (SUBLANES, w), 0)

    def group(i, hprev):
        off = pl.multiple_of(i * SUBLANES, SUBLANES)
        av = a_ref[pl.ds(off, SUBLANES), :]
        bv = b_ref[pl.ds(off, SUBLANES), :]
        for sh in (1, 2, 4):
            a_sh = jnp.where(srow >= sh, pltpu.roll(av, sh, axis=0), 1.0)
            b_sh = jnp.where(srow >= sh, pltpu.roll(bv, sh, axis=0), 0.0)
            bv = bv + av * b_sh
            av = av * a_sh
        hv = bv + av * hprev
        hs_ref[pl.ds(off, SUBLANES), :] = hv
        return hv[SUBLANES - 1:SUBLANES, :]

    hcar_ref[...] = lax.fori_loop(0, ts // SUBLANES, group, hcar_ref[...])

    out = x + jnp.dot((y * hs_ref[...]).astype(BF16), wout_ref[...], preferred_element_type=F32)
    o_ref[...] = out
    xbuf_ref[0:SUBLANES, :] = xbuf_ref[ts:ts + SUBLANES, :]

    h2 = _rms(out, g2_ref[...])
    _store_row_tiles(h2_ref, 0, h2)
    h2_hi, h2_lo = _split_bf16(h2)
    wr_hi, wr_lo = _split_bf16(wr_ref[...])
    logits = (jnp.dot(h2_hi, wr_hi, preferred_element_type=F32)
              + jnp.dot(h2_lo, wr_hi, preferred_element_type=F32)
              + jnp.dot(h2_hi, wr_lo, preferred_element_type=F32))
    lane = lax.broadcasted_iota(jnp.int32, (ts, ROUTE_LANES), 1)
    lg = jnp.where(lane < N_EXPERTS, logits, -jnp.inf)
    m1 = jnp.max(lg, axis=-1, keepdims=True)
    i1 = jnp.min(jnp.where(lg == m1, lane, ROUTE_LANES), axis=-1, keepdims=True)
    lg2 = jnp.where(lane == i1, -jnp.inf, lg)
    m2 = jnp.max(lg2, axis=-1, keepdims=True)
    i2 = jnp.min(jnp.where(lg2 == m2, lane, ROUTE_LANES), axis=-1, keepdims=True)
    e2 = jnp.exp(m2 - m1)
    gate1 = 1.0 / (1.0 + e2)
    gate2 = e2 / (1.0 + e2)

    first = lane == i1
    second = lane == i2
    ind = jnp.where(first | second, 1.0, 0.0)
    cums = jnp.dot(tri_ref[...], ind.astype(BF16), preferred_element_type=F32)
    rank = cnt_ref[...] + cums - ind
    cnt_ref[...] = cnt_ref[...] + cums[ts - 1:ts, :]
    rank1 = jnp.sum(jnp.where(first, rank, 0.0), axis=-1, keepdims=True)
    rank2 = jnp.sum(jnp.where(second, rank, 0.0), axis=-1, keepdims=True)
    cnt_out_ref[...] = jnp.broadcast_to(cnt_ref[...], (SUBLANES, ROUTE_LANES)).astype(jnp.int32)

    record = jnp.where(lane == ROUTE_E0, i1.astype(F32), 0.0)
    record = jnp.where(lane == ROUTE_E0 + 1, i2.astype(F32), record)
    record = jnp.where(lane == ROUTE_GATE0, gate1, record)
    record = jnp.where(lane == ROUTE_GATE0 + 1, gate2, record)
    record = jnp.where(lane == ROUTE_RANK0, rank1, record)
    record = jnp.where(lane == ROUTE_RANK0 + 1, rank2, record)
    route_ref[...] = record
    rt_ref[...] = record.T[0:SUBLANES, :].astype(jnp.int32)


def _odd_mixer(x, g, w_in, conv_w, conv_b, ga_w, ga_b, gx_w, gx_b, lam, w_out, g2, w_router):
    b, s, d = x.shape
    ts = SEQ_TILE
    w = conv_w.shape[1]
    hd = w // LRU_HEADS
    gw = jnp.concatenate([ga_w, gx_w], axis=-1).astype(BF16)
    wr = jnp.pad(w_router, ((0, 0), (0, ROUTE_LANES - N_EXPERTS)))
    tri = jnp.tril(jnp.ones((ts, ts), BF16))
    nst = s // ts
    const = lambda bi, si: (0, 0)
    row = lambda a: a.reshape(1, -1)
    tile = lambda bi, si: (bi, si, 0)
    return pl.pallas_call(
        _odd_mixer_kernel,
        out_shape=(jax.ShapeDtypeStruct((b, s, d), F32),
                   jax.ShapeDtypeStruct((b * s * SUBLANES, LANES), F32),
                   jax.ShapeDtypeStruct((b, s, ROUTE_LANES), F32),
                   jax.ShapeDtypeStruct((b * nst, SUBLANES, ts), jnp.int32),
                   jax.ShapeDtypeStruct((SUBLANES, ROUTE_LANES), jnp.int32)),
        grid=(b, nst),
        in_specs=[
            pl.BlockSpec((None, ts, d), lambda bi, si: (bi, si, 0)),
            _resident((1, d), const),
            _resident((d, 2 * w), const),
            _resident((CONV_K_C, w), const),
            _resident((1, w), const),
            _resident((LRU_HEADS, hd, 2 * hd), lambda bi, si: (0, 0, 0)),
            _resident((1, w), const),
            _resident((1, w), const),
            _resident((1, w), const),
            _resident((w, d), const),
            _resident((1, d), const),
            _resident((d, ROUTE_LANES), const),
            _resident((ts, ts), const),
        ],
        out_specs=(pl.BlockSpec((None, ts, d), tile),
                   pl.BlockSpec((ts * SUBLANES, LANES), lambda bi, si: (bi * nst + si, 0)),
                   pl.BlockSpec((None, ts, ROUTE_LANES), tile),
                   pl.BlockSpec((None, SUBLANES, ts), lambda bi, si: (bi * nst + si, 0, 0)),
                   pl.BlockSpec((SUBLANES, ROUTE_LANES), const)),
        scratch_shapes=[
            pltpu.VMEM((SUBLANES + ts, w), F32),
            pltpu.VMEM((ts, w), F32),
            pltpu.VMEM((ts, w), F32),
            pltpu.VMEM((ts, w), F32),
            pltpu.VMEM((1, w), F32),
            pltpu.VMEM((1, ROUTE_LANES), F32),
        ],
        compiler_params=pltpu.CompilerParams(
            dimension_semantics=("arbitrary", "arbitrary"), vmem_limit_bytes=VMEM_LIMIT_BYTES),
        name="odd_mixer",
    )(x, row(g), w_in.astype(BF16), conv_w, row(conv_b), gw, row(ga_b), row(gx_b), row(lam),
      w_out.astype(BF16), row(g2), wr, tri)


def _moe_tables(rt, counts, t):
    r = MOE_ROWS
    n_blocks = TOP_K * t // r + N_EXPERTS
    counts = counts[0, 0:N_EXPERTS]
    padded = (counts + r - 1) // r * r
    pad_end = jnp.cumsum(padded)
    pad_start = pad_end - padded
    e = rt[:, ROUTE_E0:ROUTE_E0 + TOP_K, :]
    rank = rt[:, ROUTE_RANK0:ROUTE_RANK0 + TOP_K, :]
    start = jnp.zeros_like(e)
    for ex in range(N_EXPERTS):
        start = jnp.where(e == ex, pad_start[ex], start)
    dest = start + rank
    blk = jnp.arange(n_blocks, dtype=jnp.int32)
    blk_expert = jnp.minimum(jnp.searchsorted(pad_end, blk * r, side="right"), N_EXPERTS - 1)
    n_used = pad_end[N_EXPERTS - 1] // r
    last_blk = jnp.where(padded > 0, pad_end // r - 1, -1)
    has_pad = jnp.any(blk[:, None] == last_blk[None, :], axis=1) | (blk >= n_used)
    dest = dest.astype(jnp.int32).reshape(dest.shape[0], TOP_K * dest.shape[2])
    return dest, blk_expert.astype(jnp.int32), has_pad.astype(jnp.int32), n_used.astype(jnp.int32).reshape(1)


def _row_copy(src, src_row, dst, dst_row, sem):
    return pltpu.make_async_copy(src.at[pl.ds(src_row * SUBLANES, SUBLANES), :],
                                 dst.at[pl.ds(dst_row * SUBLANES, SUBLANES), :], sem)


def _rows_wait(src, dst, n_rows, sem):
    pltpu.make_async_copy(src.at[pl.ds(0, n_rows * SUBLANES), :],
                          dst.at[pl.ds(0, n_rows * SUBLANES), :], sem).wait()


def _dispatch_kernel(pad_ref, dest_hbm, h_ref, xg_hbm, dsm, stage, zbuf, sem_idx, sem_row, sem_zero):
    i = pl.program_id(0)
    n = pl.num_programs(0)
    tm = h_ref.shape[0] // SUBLANES
    r = zbuf.shape[0] // SUBLANES
    n_blocks = pad_ref.shape[0]
    slot = lax.rem(i, 2)

    def idx_copy(step, s):
        return pltpu.make_async_copy(dest_hbm.at[step], dsm.at[pl.ds(s * TOP_K * tm, TOP_K * tm)],
                                     sem_idx.at[s])

    def zero_copy(blk):
        return pltpu.make_async_copy(zbuf, xg_hbm.at[pl.ds(blk * r * SUBLANES, r * SUBLANES), :], sem_zero)

    @pl.when(i == 0)
    def _():
        idx_copy(0, 0).start()
        zbuf[...] = jnp.zeros(zbuf.shape, zbuf.dtype)

        def zero_start(blk, c):
            @pl.when(pad_ref[blk] == 1)
            def _():
                zero_copy(blk).start()
            return c

        def zero_wait(blk, c):
            @pl.when(pad_ref[blk] == 1)
            def _():
                zero_copy(blk).wait()
            return c

        lax.fori_loop(0, n_blocks, zero_start, 0)
        lax.fori_loop(0, n_blocks, zero_wait, 0)

    @pl.when(i + 1 < n)
    def _():
        idx_copy(i + 1, 1 - slot).start()

    idx_copy(i, slot).wait()

    stage_row0 = slot * tm
    stage[pl.ds(stage_row0 * SUBLANES, tm * SUBLANES), :] = h_ref[...]

    def issue(row, c):
        for k in range(TOP_K):
            _row_copy(stage, stage_row0 + row, xg_hbm, dsm[(slot * TOP_K + k) * tm + row],
                      sem_row.at[slot]).start()
        return c

    lax.fori_loop(0, tm, issue, 0, unroll=8)

    def wait_slot(s):
        for _ in range(TOP_K):
            _rows_wait(stage, xg_hbm, tm, sem_row.at[s])

    @pl.when(i > 0)
    def _():
        wait_slot(1 - slot)

    @pl.when(i == n - 1)
    def _():
        wait_slot(slot)


def _dispatch(h2, dest, has_pad):
    tm = dest.shape[1] // TOP_K
    t = h2.shape[0] // SUBLANES
    r = MOE_ROWS
    n_rows = TOP_K * t + N_EXPERTS * r
    grid_spec = pltpu.PrefetchScalarGridSpec(
        num_scalar_prefetch=1,
        grid=(t // tm,),
        in_specs=[pl.BlockSpec(memory_space=pl.ANY),
                  pl.BlockSpec((tm * SUBLANES, LANES), lambda i, pad: (i, 0))],
        out_specs=pl.BlockSpec(memory_space=pl.ANY),
        scratch_shapes=[
            pltpu.SMEM((2 * TOP_K * tm,), jnp.int32),
            pltpu.VMEM((2 * tm * SUBLANES, LANES), F32),
            pltpu.VMEM((r * SUBLANES, LANES), F32),
            pltpu.SemaphoreType.DMA((2,)),
            pltpu.SemaphoreType.DMA((2,)),
            pltpu.SemaphoreType.DMA,
        ],
    )
    return pl.pallas_call(
        _dispatch_kernel,
        out_shape=jax.ShapeDtypeStruct((n_rows * SUBLANES, LANES), F32),
        grid_spec=grid_spec,
        compiler_params=pltpu.CompilerParams(dimension_semantics=("arbitrary",)),
        name="moe_dispatch",
    )(has_pad, dest, h2)


def _expert_kernel(be_ref, nu_ref, x_ref, wg_ref, wu_ref, wd_ref, y_ref):
    del be_ref
    i = pl.program_id(0)
    r = x_ref.shape[0] // SUBLANES

    @pl.when(i < nu_ref[0])
    def _():
        xb = _load_row_tiles(x_ref, 0, r).astype(BF16)
        _store_row_tiles(y_ref, 0, _swiglu_rows(xb, wg_ref, wu_ref, wd_ref))

    @pl.when(i >= nu_ref[0])
    def _():
        y_ref[...] = jnp.zeros(y_ref.shape, y_ref.dtype)


def _experts(xg, blk_expert, n_used, w_gate, w_up, w_down):
    d, d_ff = w_gate.shape[1], w_gate.shape[2]
    r = MOE_ROWS
    by_expert = lambda i, be, nu: (be[i], 0, 0)
    rows = lambda i, be, nu: (i, 0)
    grid_spec = pltpu.PrefetchScalarGridSpec(
        num_scalar_prefetch=2,
        grid=(xg.shape[0] // (r * SUBLANES),),
        in_specs=[
            pl.BlockSpec((r * SUBLANES, LANES), rows),
            pl.BlockSpec((None, d, d_ff), by_expert, pipeline_mode=pl.Buffered(1)),
            pl.BlockSpec((None, d, d_ff), by_expert, pipeline_mode=pl.Buffered(1)),
            pl.BlockSpec((None, d_ff, d), by_expert, pipeline_mode=pl.Buffered(1)),
        ],
        out_specs=pl.BlockSpec((r * SUBLANES, LANES), rows),
    )
    return pl.pallas_call(
        _expert_kernel,
        out_shape=jax.ShapeDtypeStruct(xg.shape, F32),
        grid_spec=grid_spec,
        compiler_params=pltpu.CompilerParams(
            dimension_semantics=("arbitrary",), vmem_limit_bytes=VMEM_LIMIT_BYTES),
        name="moe_experts",
    )(blk_expert, n_used, xg, w_gate.astype(BF16), w_up.astype(BF16), w_down.astype(BF16))


def _combine_kernel(dest_hbm, y_hbm, x_ref, route_ref, o_ref, dsm, ybuf, sem_idx, sem_row):
    i = pl.program_id(0)
    n = pl.num_programs(0)
    tm = x_ref.shape[0]
    slot = lax.rem(i, 2)

    def fetch(step, s):
        idx = pltpu.make_async_copy(dest_hbm.at[step], dsm.at[pl.ds(s * TOP_K * tm, TOP_K * tm)],
                                    sem_idx.at[s])
        idx.start()
        idx.wait()

        def issue(row, c):
            for k in range(TOP_K):
                at = (s * TOP_K + k) * tm + row
                _row_copy(y_hbm, dsm[at], ybuf, at, sem_row.at[s]).start()
            return c

        lax.fori_loop(0, tm, issue, 0, unroll=8)

    @pl.when(i == 0)
    def _():
        fetch(0, 0)

    @pl.when(i + 1 < n)
    def _():
        fetch(i + 1, 1 - slot)

    for _ in range(TOP_K):
        _rows_wait(y_hbm, ybuf, tm, sem_row.at[slot])
    rt = route_ref[...]
    moe = None
    for k in range(TOP_K):
        yk = _load_row_tiles(ybuf, (slot * TOP_K + k) * tm * SUBLANES, tm)
        term = yk * rt[:, ROUTE_GATE0 + k:ROUTE_GATE0 + k + 1]
        moe = term if moe is None else moe + term
    o_ref[...] = x_ref[...] + moe


def _combine(x, y, route, dest):
    t, d = x.shape
    tm = dest.shape[1] // TOP_K
    return pl.pallas_call(
        _combine_kernel,
        out_shape=jax.ShapeDtypeStruct((t, d), F32),
        grid=(t // tm,),
        in_specs=[
            pl.BlockSpec(memory_space=pl.ANY),
            pl.BlockSpec(memory_space=pl.ANY),
            pl.BlockSpec((tm, d), lambda i: (i, 0)),
            pl.BlockSpec((tm, ROUTE_LANES), lambda i: (i, 0)),
        ],
        out_specs=pl.BlockSpec((tm, d), lambda i: (i, 0)),
        scratch_shapes=[
            pltpu.SMEM((2 * TOP_K * tm,), jnp.int32),
            pltpu.VMEM((2 * TOP_K * tm * SUBLANES, LANES), F32),
            pltpu.SemaphoreType.DMA((2,)),
            pltpu.SemaphoreType.DMA((2,)),
        ],
        compiler_params=pltpu.CompilerParams(
            dimension_semantics=("arbitrary",), vmem_limit_bytes=VMEM_LIMIT_BYTES),
        name="moe_combine",
    )(dest, y, x, route)


def kernel(x, norm_mix, norm_ffn, hy_w_in, hy_q_gain, hy_k_gain, hy_sinks, hy_conv_w, hy_w_out, rg_w_in, rg_conv_w, rg_conv_b, rg_gate_a_w, rg_gate_a_b, rg_gate_x_w, rg_gate_x_b, rg_lambda, rg_w_out, ffn_w_gate, ffn_w_up, ffn_w_down, moe_router, moe_w_gate, moe_w_up, moe_w_down):
    b, s, d = x.shape
    t = b * s
    depth = norm_mix.shape[0]
    for layer in range(depth):
        j = layer // 2
        if layer % 2 == 0:
            x = _even_mixer(x, norm_mix[layer], hy_w_in[j], hy_q_gain[j], hy_k_gain[j], hy_sinks[j],
                            hy_conv_w[j], hy_w_out[j])
            x = _ffn(x.reshape(t, d), norm_ffn[layer], ffn_w_gate[j], ffn_w_up[j], ffn_w_down[j])
            x = x.reshape(b, s, d)
        else:
            x, h2, route, rt, counts = _odd_mixer(
                x, norm_mix[layer], rg_w_in[j], rg_conv_w[j], rg_conv_b[j], rg_gate_a_w[j], rg_gate_a_b[j],
                rg_gate_x_w[j], rg_gate_x_b[j], rg_lambda[j], rg_w_out[j], norm_ffn[layer], moe_router[j])
            dest, blk_expert, has_pad, n_used = _moe_tables(rt, counts, t)
            xg = _dispatch(h2, dest, has_pad)
            y = _experts(xg, blk_expert, n_used, moe_w_gate[j], moe_w_up[j], moe_w_down[j])
            x = _combine(x.reshape(t, d), y, route.reshape(t, ROUTE_LANES), dest).reshape(b, s, d)
    return x
```

```python
import functools

import jax
import jax.numpy as jnp
from jax import lax
from jax.experimental import pallas as pl
from jax.experimental.pallas import tpu as pltpu

F32 = jnp.float32
BF16 = jnp.bfloat16

LANES = 128
SUBLANES = 8
VMEM_LIMIT_BYTES = 56 * 1024 * 1024
DMA_THREADS = 2

N_Q_HEADS = 8
N_KV_HEADS = 2
HEAD_DIM = 64
Q_GROUP = N_Q_HEADS // N_KV_HEADS
WINDOW = 128
ATTN_BLOCK = 128
ALIBI_MAX = 8.0
CONV_K_B = 3
CONV_K_C = 4
LRU_HEADS = 8
LRU_C = 8.0
N_EXPERTS = 8
TOP_K = 2
NORM_EPS = 1e-6
NEG_INF = -1e30

SEQ_TILE = 512
FFN_TILE = 512
MOE_ROWS = 512
FF_CHUNK = 1792
WEIGHT_CHUNKS = 8
ROUTE_LANES = LANES
ROUTE_E0 = 0
ROUTE_GATE0 = 2
ROUTE_RANK0 = 4
GELU_C0 = 0.7978845608028654
GELU_C1 = 0.044715


def _rms(x, g):
    return x * lax.rsqrt(jnp.mean(x * x, axis=-1, keepdims=True) + NORM_EPS) * g


def _split_bf16(x):
    hi = x.astype(BF16)
    lo = (x - hi.astype(F32)).astype(BF16)
    return hi, lo


def _store_row_tiles(ref, start, x):
    n = x.shape[0]
    for c in range(x.shape[1] // LANES):
        ref[pl.ds(start + c, n, stride=SUBLANES), :] = x[:, c * LANES:(c + 1) * LANES]


def _load_row_tiles(ref, start, n):
    return jnp.concatenate([ref[pl.ds(start + c, n, stride=SUBLANES), :] for c in range(SUBLANES)], axis=1)


def _resident(shape, index_map):
    return pl.BlockSpec(shape, index_map, pipeline_mode=pl.Buffered(1))


def _even_mixer_kernel(sinks_ref, x_ref, g_ref, win_ref, qg_ref, kg_ref, cw_ref, wout_ref, o_ref,
                       kd_ref, vd_ref, cbuf_ref, attn_ref):
    ts = x_ref.shape[0]
    q_dim = N_Q_HEADS * HEAD_DIM
    kv_dim = N_KV_HEADS * HEAD_DIM
    cw_dim = cw_ref.shape[1]
    st = pl.program_id(1)

    @pl.when(st == 0)
    def _():
        kd_ref[:, 0:ATTN_BLOCK, :] = jnp.zeros((N_KV_HEADS, ATTN_BLOCK, LANES), BF16)
        vd_ref[:, 0:ATTN_BLOCK, :] = jnp.zeros((N_KV_HEADS, ATTN_BLOCK, LANES), BF16)
        cbuf_ref[0:SUBLANES, :] = jnp.zeros((SUBLANES, cw_dim), F32)

    x = x_ref[...]
    h = _rms(x, g_ref[...]).astype(BF16)
    proj = jnp.dot(h, win_ref[...], preferred_element_type=F32)
    q = proj[:, 0:q_dim]
    k = proj[:, q_dim:q_dim + kv_dim]
    v = proj[:, q_dim + kv_dim:q_dim + 2 * kv_dim]
    o0 = q_dim + 2 * kv_dim
    gate_b = proj[:, o0:o0 + cw_dim]
    gate_c = proj[:, o0 + cw_dim:o0 + 2 * cw_dim]
    u = proj[:, o0 + 2 * cw_dim:o0 + 3 * cw_dim]

    ri = lax.broadcasted_iota(jnp.int32, (LANES, LANES), 0)
    ci = lax.broadcasted_iota(jnp.int32, (LANES, LANES), 1)
    half_mean = jnp.where((ri < HEAD_DIM) == (ci < HEAD_DIM), 1.0 / HEAD_DIM, 0.0).astype(BF16)

    def half_mean_square(z):
        hi, lo = _split_bf16(z * z)
        cols = []
        for c in range(z.shape[1] // LANES):
            sl = slice(c * LANES, (c + 1) * LANES)
            cols.append(jnp.dot(hi[:, sl], half_mean, preferred_element_type=F32)
                        + jnp.dot(lo[:, sl], half_mean, preferred_element_type=F32))
        return cols[0] if len(cols) == 1 else jnp.concatenate(cols, axis=1)

    qn = q * lax.rsqrt(half_mean_square(q) + NORM_EPS) * qg_ref[...] * (HEAD_DIM ** -0.5)
    kn = k * lax.rsqrt(half_mean_square(k) + NORM_EPS) * kg_ref[...]

    low_half_t = lax.broadcasted_iota(jnp.int32, (ts, LANES), 1) < HEAD_DIM
    k_sw = pltpu.roll(kn, HEAD_DIM, axis=1)
    v_sw = pltpu.roll(v, HEAD_DIM, axis=1)
    kd_ref[0, ATTN_BLOCK:ATTN_BLOCK + ts, :] = jnp.where(low_half_t, kn, k_sw).astype(BF16)
    kd_ref[1, ATTN_BLOCK:ATTN_BLOCK + ts, :] = jnp.where(low_half_t, k_sw, kn).astype(BF16)
    vd_ref[0, ATTN_BLOCK:ATTN_BLOCK + ts, :] = jnp.where(low_half_t, v, v_sw).astype(BF16)
    vd_ref[1, ATTN_BLOCK:ATTN_BLOCK + ts, :] = jnp.where(low_half_t, v_sw, v).astype(BF16)

    low_half = lax.broadcasted_iota(jnp.int32, (ATTN_BLOCK, LANES), 1) < HEAD_DIM
    qi = lax.broadcasted_iota(jnp.int32, (ATTN_BLOCK, 2 * ATTN_BLOCK), 0)
    sj = lax.broadcasted_iota(jnp.int32, (ATTN_BLOCK, 2 * ATTN_BLOCK), 1)
    diff = ATTN_BLOCK + qi - sj
    in_window = (diff >= 0) & (diff < WINDOW)
    diff_f = diff.astype(F32)
    first_key = jnp.where(st == 0, ATTN_BLOCK, 0)

    for j in range(ts // ATTN_BLOCK):
        rows = slice(j * ATTN_BLOCK, (j + 1) * ATTN_BLOCK)
        band = slice(j * ATTN_BLOCK, (j + 2) * ATTN_BLOCK)
        valid = (in_window & (sj >= first_key)) if j == 0 else in_window
        pair_out = []
        for g in range(N_KV_HEADS):
            q_rows = []
            for p in range(Q_GROUP // 2):
                c = g * (Q_GROUP // 2) + p
                qp = qn[rows, c * LANES:(c + 1) * LANES]
                q_rows.append(jnp.where(low_half, qp, 0.0))
                q_rows.append(jnp.where(low_half, 0.0, qp))
            qs = jnp.concatenate(q_rows, axis=0).astype(BF16)
            s = lax.dot_general(qs, kd_ref[g, band, :], (((1,), (1,)), ((), ())),
                                preferred_element_type=F32)
            probs = []
            denom = []
            for hh in range(Q_GROUP):
                head = g * Q_GROUP + hh
                slope = 2.0 ** (-ALIBI_MAX * (head + 1) / N_Q_HEADS)
                sh = s[hh * ATTN_BLOCK:(hh + 1) * ATTN_BLOCK, :] - slope * diff_f
                sh = jnp.where(valid, sh, NEG_INF)
                sink = sinks_ref[0, head]
                m = jnp.maximum(jnp.max(sh, axis=-1, keepdims=True), sink)
                pr = jnp.exp(sh - m)
                probs.append(pr)
                denom.append(jnp.sum(pr, axis=-1, keepdims=True) + jnp.exp(sink - m))
            pm = jnp.concatenate(probs, axis=0).astype(BF16)
            o = jnp.dot(pm, vd_ref[g, band, :], preferred_element_type=F32)
            o = o / jnp.concatenate(denom, axis=0)
            for p in range(Q_GROUP // 2):
                lo_rows = o[(2 * p) * ATTN_BLOCK:(2 * p + 1) * ATTN_BLOCK, :]
                hi_rows = o[(2 * p + 1) * ATTN_BLOCK:(2 * p + 2) * ATTN_BLOCK, :]
                pair_out.append(jnp.where(low_half, lo_rows, hi_rows))
        attn_ref[rows, :] = jnp.concatenate(pair_out, axis=1)

    cu = gate_c * u
    cbuf_ref[SUBLANES:SUBLANES + ts, :] = cu
    conv = cw_ref[CONV_K_B - 1:CONV_K_B, :] * cu
    for kk in range(CONV_K_B - 1):
        back = CONV_K_B - 1 - kk
        conv = conv + cw_ref[kk:kk + 1, :] * cbuf_ref[SUBLANES - back:SUBLANES - back + ts, :]
    conv = gate_b * conv

    mix = jnp.concatenate([attn_ref[...], conv], axis=1).astype(BF16)
    o_ref[...] = x + jnp.dot(mix, wout_ref[...], preferred_element_type=F32)

    kd_ref[:, 0:ATTN_BLOCK, :] = kd_ref[:, ts:ts + ATTN_BLOCK, :]
    vd_ref[:, 0:ATTN_BLOCK, :] = vd_ref[:, ts:ts + ATTN_BLOCK, :]
    cbuf_ref[0:SUBLANES, :] = cbuf_ref[ts:ts + SUBLANES, :]


def _even_mixer(x, g, w_in, q_gain, k_gain, sinks, conv_w, w_out):
    b, s, d = x.shape
    ts = SEQ_TILE
    n_in = w_in.shape[1]
    cw_dim = conv_w.shape[1]
    mix_dim = w_out.shape[0]
    qg = jnp.tile(q_gain, N_Q_HEADS).reshape(1, N_Q_HEADS * HEAD_DIM)
    kg = jnp.tile(k_gain, N_KV_HEADS).reshape(1, N_KV_HEADS * HEAD_DIM)
    const = lambda bi, si: (0, 0)
    return pl.pallas_call(
        _even_mixer_kernel,
        out_shape=jax.ShapeDtypeStruct((b, s, d), F32),
        grid=(b, s // ts),
        in_specs=[
            pl.BlockSpec(memory_space=pltpu.SMEM),
            pl.BlockSpec((None, ts, d), lambda bi, si: (bi, si, 0)),
            _resident((1, d), const),
            _resident((d, n_in), const),
            _resident((1, N_Q_HEADS * HEAD_DIM), const),
            _resident((1, N_KV_HEADS * HEAD_DIM), const),
            _resident((CONV_K_B, cw_dim), const),
            _resident((mix_dim, d), const),
        ],
        out_specs=pl.BlockSpec((None, ts, d), lambda bi, si: (bi, si, 0)),
        scratch_shapes=[
            pltpu.VMEM((N_KV_HEADS, ATTN_BLOCK + ts, LANES), BF16),
            pltpu.VMEM((N_KV_HEADS, ATTN_BLOCK + ts, LANES), BF16),
            pltpu.VMEM((SUBLANES + ts, cw_dim), F32),
            pltpu.VMEM((ts, N_Q_HEADS * HEAD_DIM), F32),
        ],
        compiler_params=pltpu.CompilerParams(
            dimension_semantics=("arbitrary", "arbitrary"), vmem_limit_bytes=VMEM_LIMIT_BYTES),
        name="even_mixer",
    )(sinks.reshape(1, N_Q_HEADS), x, g.reshape(1, d), w_in.astype(BF16), qg, kg, conv_w,
      w_out.astype(BF16))


def _swiglu_rows(hb, wg_ref, wu_ref, wd_ref):
    d_ff = wg_ref.shape[1]
    acc = None
    for c in range(d_ff // FF_CHUNK):
        cols = slice(c * FF_CHUNK, (c + 1) * FF_CHUNK)
        gate = jnp.dot(hb, wg_ref[:, cols], preferred_element_type=F32)
        up = jnp.dot(hb, wu_ref[:, cols], preferred_element_type=F32)
        act = (gate * (1.0 / (1.0 + jnp.exp(-gate))) * up).astype(BF16)
        part = jnp.dot(act, wd_ref[cols, :], preferred_element_type=F32)
        acc = part if acc is None else acc + part
    return acc


def _load_cast(jobs, sem):
    chunks = []
    for src, dst, stage in jobs:
        chunk_rows = stage.shape[1]
        for c in range(dst.shape[0] // chunk_rows):
            chunks.append((src, dst, stage, pl.ds(c * chunk_rows, chunk_rows)))

    def copy(n):
        src, _, stage, rows = chunks[n]
        return pltpu.make_async_copy(src.at[rows, :], stage.at[n % 2], sem.at[n % 2])

    copy(0).start()
    for n, (_, dst, stage, rows) in enumerate(chunks):
        if n + 1 < len(chunks):
            copy(n + 1).start()
        copy(n).wait()
        dst[rows, :] = stage[n % 2].astype(BF16)


def _swiglu_weight_scratch(d, d_ff):
    in_rows = d // WEIGHT_CHUNKS
    out_rows = d_ff // WEIGHT_CHUNKS
    return [
        pltpu.VMEM((d, d_ff), BF16),
        pltpu.VMEM((d, d_ff), BF16),
        pltpu.VMEM((d_ff, d), BF16),
        pltpu.VMEM((2, in_rows, d_ff), F32),
        pltpu.VMEM((2, out_rows, d), F32),
        pltpu.SemaphoreType.DMA((2,)),
    ]


def _ffn_kernel(j, x_ref, g_ref, wg_hbm, wu_hbm, wd_hbm, o_ref,
                wg_ref, wu_ref, wd_ref, stage_in, stage_out, sem):
    @pl.when(pl.program_id(0) == 0)
    def _():
        _load_cast([(wg_hbm.at[j], wg_ref, stage_in), (wu_hbm.at[j], wu_ref, stage_in),
                    (wd_hbm.at[j], wd_ref, stage_out)], sem)

    x = x_ref[...]
    hb = _rms(x, g_ref[...]).astype(BF16)
    o_ref[...] = x + _swiglu_rows(hb, wg_ref, wu_ref, wd_ref)


def _ffn(x, g, j, w_gate, w_up, w_down):
    t, d = x.shape
    d_ff = w_gate.shape[2]
    tm = FFN_TILE
    return pl.pallas_call(
        functools.partial(_ffn_kernel, j),
        out_shape=jax.ShapeDtypeStruct((t, d), F32),
        grid=(t // tm,),
        in_specs=[
            pl.BlockSpec((tm, d), lambda i: (i, 0)),
            _resident((1, d), lambda i: (0, 0)),
            pl.BlockSpec(memory_space=pl.ANY),
            pl.BlockSpec(memory_space=pl.ANY),
            pl.BlockSpec(memory_space=pl.ANY),
        ],
        out_specs=pl.BlockSpec((tm, d), lambda i: (i, 0)),
        scratch_shapes=_swiglu_weight_scratch(d, d_ff),
        compiler_params=pltpu.CompilerParams(
            dimension_semantics=("arbitrary",), vmem_limit_bytes=VMEM_LIMIT_BYTES),
        name="dense_ffn",
    )(x, g.reshape(1, d), w_gate, w_up, w_down)


def _odd_mixer_kernel(x_ref, g_ref, win_ref, cw_ref, cb_ref, gw_ref, gab_ref, gxb_ref, lam_ref, wout_ref,
                      g2_ref, wr_ref, tri_ref, o_ref, h2_ref, route_ref, rt_ref, cnt_out_ref,
                      xbuf_ref, a_ref, b_ref, hs_ref, hcar_ref, cnt_ref):
    ts = x_ref.shape[0]
    w = cw_ref.shape[1]
    hd = w // LRU_HEADS
    st = pl.program_id(1)

    @pl.when(st == 0)
    def _():
        xbuf_ref[0:SUBLANES, :] = jnp.zeros((SUBLANES, w), F32)
        hcar_ref[...] = jnp.zeros((1, w), F32)

    @pl.when((st == 0) & (pl.program_id(0) == 0))
    def _():
        cnt_ref[...] = jnp.zeros((1, ROUTE_LANES), F32)

    x = x_ref[...]
    h = _rms(x, g_ref[...]).astype(BF16)
    proj = jnp.dot(h, win_ref[...], preferred_element_type=F32)
    y = proj[:, 0:w]
    y = y * (0.5 * (1.0 + jnp.tanh(GELU_C0 * (y + GELU_C1 * (y * y * y)))))
    xb = proj[:, w:2 * w]

    xbuf_ref[SUBLANES:SUBLANES + ts, :] = xb
    xc = cw_ref[CONV_K_C - 1:CONV_K_C, :] * xb + cb_ref[...]
    for kk in range(CONV_K_C - 1):
        back = CONV_K_C - 1 - kk
        xc = xc + cw_ref[kk:kk + 1, :] * xbuf_ref[SUBLANES - back:SUBLANES - back + ts, :]

    r_cols = []
    i_cols = []
    for hh in range(LRU_HEADS):
        ri = jnp.dot(xc[:, hh * hd:(hh + 1) * hd].astype(BF16), gw_ref[hh], preferred_element_type=F32)
        r_cols.append(ri[:, 0:hd])
        i_cols.append(ri[:, hd:2 * hd])
    r_pre = jnp.concatenate(r_cols, axis=1) + gab_ref[...]
    i_pre = jnp.concatenate(i_cols, axis=1) + gxb_ref[...]
    r = 1.0 / (1.0 + jnp.exp(-r_pre))
    ig = 1.0 / (1.0 + jnp.exp(-i_pre))
    neg_lam = -lam_ref[...]
    softplus = jnp.maximum(neg_lam, 0.0) + jnp.log1p(jnp.exp(-jnp.abs(neg_lam)))
    log_a = -LRU_C * r * softplus
    a = jnp.exp(log_a)
    th = jnp.tanh(log_a)
    mult = jnp.sqrt(jnp.maximum(-2.0 * th / (1.0 - th), 0.0))
    row = lax.broadcasted_iota(jnp.int32, (ts, w), 0)
    mult = jnp.where((row == 0) & (st == 0), 1.0, mult)
    a_ref[...] = a
    b_ref[...] = mult * (ig * xc)

    srow = lax.broadcasted_iota(jnp.int32, (SUBLANES, w), 0)

    def group(i, hprev):
        off = pl.multiple_of(i * SUBLANES, SUBLANES)
        av = a_ref[pl.ds(off, SUBLANES), :]
        bv = b_ref[pl.ds(off, SUBLANES), :]
        for sh in (1, 2, 4):
            a_sh = jnp.where(srow >= sh, pltpu.roll(av, sh, axis=0), 1.0)
            b_sh = jnp.where(srow >= sh, pltpu.roll(bv, sh, axis=0), 0.0)
            bv = bv + av * b_sh
            av = av * a_sh
        hv = bv + av * hprev
        hs_ref[pl.ds(off, SUBLANES), :] = hv
        return hv[SUBLANES - 1:SUBLANES, :]

    hcar_ref[...] = lax.fori_loop(0, ts // SUBLANES, group, hcar_ref[...])

    out = x + jnp.dot((y * hs_ref[...]).astype(BF16), wout_ref[...], preferred_element_type=F32)
    o_ref[...] = out
    xbuf_ref[0:SUBLANES, :] = xbuf_ref[ts:ts + SUBLANES, :]

    h2 = _rms(out, g2_ref[...])
    _store_row_tiles(h2_ref, 0, h2)
    h2_hi, h2_lo = _split_bf16(h2)
    wr_hi, wr_lo = _split_bf16(wr_ref[...])
    logits = (jnp.dot(h2_hi, wr_hi, preferred_element_type=F32)
              + jnp.dot(h2_lo, wr_hi, preferred_element_type=F32)
              + jnp.dot(h2_hi, wr_lo, preferred_element_type=F32))
    lane = lax.broadcasted_iota(jnp.int32, (ts, ROUTE_LANES), 1)
    lg = jnp.where(lane < N_EXPERTS, logits, -jnp.inf)
    m1 = jnp.max(lg, axis=-1, keepdims=True)
    i1 = jnp.min(jnp.where(lg == m1, lane, ROUTE_LANES), axis=-1, keepdims=True)
    lg2 = jnp.where(lane == i1, -jnp.inf, lg)
    m2 = jnp.max(lg2, axis=-1, keepdims=True)
    i2 = jnp.min(jnp.where(lg2 == m2, lane, ROUTE_LANES), axis=-1, keepdims=True)
    e2 = jnp.exp(m2 - m1)
    gate1 = 1.0 / (1.0 + e2)
    gate2 = e2 / (1.0 + e2)

    first = lane == i1
    second = lane == i2
    ind = jnp.where(first | second, 1.0, 0.0)
    cums = jnp.dot(tri_ref[...], ind.astype(BF16), preferred_element_type=F32)
    rank = cnt_ref[...] + cums - ind
    cnt_ref[...] = cnt_ref[...] + cums[ts - 1:ts, :]
    rank1 = jnp.sum(jnp.where(first, rank, 0.0), axis=-1, keepdims=True)
    rank2 = jnp.sum(jnp.where(second, rank, 0.0), axis=-1, keepdims=True)
    cnt_out_ref[...] = jnp.broadcast_to(cnt_ref[...], (SUBLANES, ROUTE_LANES)).astype(jnp.int32)

    record = jnp.where(lane == ROUTE_E0, i1.astype(F32), 0.0)
    record = jnp.where(lane == ROUTE_E0 + 1, i2.astype(F32), record)
    record = jnp.where(lane == ROUTE_GATE0, gate1, record)
    record = jnp.where(lane == ROUTE_GATE0 + 1, gate2, record)
    record = jnp.where(lane == ROUTE_RANK0, rank1, record)
    record = jnp.where(lane == ROUTE_RANK0 + 1, rank2, record)
    route_ref[...] = record
    rt_ref[...] = record.T[0:SUBLANES, :].astype(jnp.int32)


def _odd_mixer(x, g, w_in, conv_w, conv_b, ga_w, ga_b, gx_w, gx_b, lam, w_out, g2, w_router):
    b, s, d = x.shape
    ts = SEQ_TILE
    w = conv_w.shape[1]
    hd = w // LRU_HEADS
    gw = jnp.concatenate([ga_w, gx_w], axis=-1).astype(BF16)
    wr = jnp.pad(w_router, ((0, 0), (0, ROUTE_LANES - N_EXPERTS)))
    tri = jnp.tril(jnp.ones((ts, ts), BF16))
    nst = s // ts
    const = lambda bi, si: (0, 0)
    row = lambda a: a.reshape(1, -1)
    tile = lambda bi, si: (bi, si, 0)
    return pl.pallas_call(
        _odd_mixer_kernel,
        out_shape=(jax.ShapeDtypeStruct((b, s, d), F32),
                   jax.ShapeDtypeStruct((b * s * SUBLANES, LANES), F32),
                   jax.ShapeDtypeStruct((b, s, ROUTE_LANES), F32),
                   jax.ShapeDtypeStruct((b * nst, SUBLANES, ts), jnp.int32),
                   jax.ShapeDtypeStruct((SUBLANES, ROUTE_LANES), jnp.int32)),
        grid=(b, nst),
        in_specs=[
            pl.BlockSpec((None, ts, d), lambda bi, si: (bi, si, 0)),
            _resident((1, d), const),
            _resident((d, 2 * w), const),
            _resident((CONV_K_C, w), const),
            _resident((1, w), const),
            _resident((LRU_HEADS, hd, 2 * hd), lambda bi, si: (0, 0, 0)),
            _resident((1, w), const),
            _resident((1, w), const),
            _resident((1, w), const),
            _resident((w, d), const),
            _resident((1, d), const),
            _resident((d, ROUTE_LANES), const),
            _resident((ts, ts), const),
        ],
        out_specs=(pl.BlockSpec((None, ts, d), tile),
                   pl.BlockSpec((ts * SUBLANES, LANES), lambda bi, si: (bi * nst + si, 0)),
                   pl.BlockSpec((None, ts, ROUTE_LANES), tile),
                   pl.BlockSpec((None, SUBLANES, ts), lambda bi, si: (bi * nst + si, 0, 0)),
                   pl.BlockSpec((SUBLANES, ROUTE_LANES), const)),
        scratch_shapes=[
            pltpu.VMEM((SUBLANES + ts, w), F32),
            pltpu.VMEM((ts, w), F32),
            pltpu.VMEM((ts, w), F32),
            pltpu.VMEM((ts, w), F32),
            pltpu.VMEM((1, w), F32),
            pltpu.VMEM((1, ROUTE_LANES), F32),
        ],
        compiler_params=pltpu.CompilerParams(
            dimension_semantics=("arbitrary", "arbitrary"), vmem_limit_bytes=VMEM_LIMIT_BYTES),
        name="odd_mixer",
    )(x, row(g), w_in.astype(BF16), conv_w, row(conv_b), gw, row(ga_b), row(gx_b), row(lam),
      w_out.astype(BF16), row(g2), wr, tri)


def _moe_tables(rt, counts, t):
    r = MOE_ROWS
    n_blocks = TOP_K * t // r + N_EXPERTS
    counts = counts[0, 0:N_EXPERTS]
    padded = (counts + r - 1) // r * r
    pad_end = jnp.cumsum(padded)
    pad_start = pad_end - padded
    e = rt[:, ROUTE_E0:ROUTE_E0 + TOP_K, :]
    rank = rt[:, ROUTE_RANK0:ROUTE_RANK0 + TOP_K, :]
    start = jnp.zeros_like(e)
    for ex in range(N_EXPERTS):
        start = jnp.where(e == ex, pad_start[ex], start)
    dest = start + rank
    blk = jnp.arange(n_blocks, dtype=jnp.int32)
    blk_expert = jnp.minimum(jnp.sum(blk[:, None] * r >= pad_end[None, :], axis=1), N_EXPERTS - 1)
    n_used = pad_end[N_EXPERTS - 1] // r
    last_blk = jnp.where(padded > 0, pad_end // r - 1, -1)
    has_pad = jnp.any(blk[:, None] == last_blk[None, :], axis=1) | (blk >= n_used)
    dest = dest.astype(jnp.int32).reshape(dest.shape[0], TOP_K * dest.shape[2])
    return dest, blk_expert.astype(jnp.int32), has_pad.astype(jnp.int32), n_used.astype(jnp.int32).reshape(1)


def _row_copy(src, src_row, dst, dst_row, sem):
    return pltpu.make_async_copy(src.at[pl.ds(src_row * SUBLANES, SUBLANES), :],
                                 dst.at[pl.ds(dst_row * SUBLANES, SUBLANES), :], sem)


def _rows_wait(src, dst, n_rows, sem):
    pltpu.make_async_copy(src.at[pl.ds(0, n_rows * SUBLANES), :],
                          dst.at[pl.ds(0, n_rows * SUBLANES), :], sem).wait()


def _dispatch_kernel(pad_ref, dest_hbm, h_ref, xg_hbm, dsm, stage, zbuf, sem_idx, sem_row, sem_zero):
    i = pl.program_id(0)
    n = pl.num_programs(0)
    tm = h_ref.shape[0] // SUBLANES
    r = zbuf.shape[0] // SUBLANES
    n_blocks = pad_ref.shape[0]
    slot = lax.rem(i, 2)

    def idx_copy(step, s):
        return pltpu.make_async_copy(dest_hbm.at[step], dsm.at[pl.ds(s * TOP_K * tm, TOP_K * tm)],
                                     sem_idx.at[s])

    def zero_copy(blk):
        return pltpu.make_async_copy(zbuf, xg_hbm.at[pl.ds(blk * r * SUBLANES, r * SUBLANES), :], sem_zero)

    @pl.when(i == 0)
    def _():
        idx_copy(0, 0).start()
        zbuf[...] = jnp.zeros(zbuf.shape, zbuf.dtype)

        def zero_start(blk, c):
            @pl.when(pad_ref[blk] == 1)
            def _():
                zero_copy(blk).start()
            return c

        def zero_wait(blk, c):
            @pl.when(pad_ref[blk] == 1)
            def _():
                zero_copy(blk).wait()
            return c

        lax.fori_loop(0, n_blocks, zero_start, 0)
        lax.fori_loop(0, n_blocks, zero_wait, 0)

    @pl.when(i + 1 < n)
    def _():
        idx_copy(i + 1, 1 - slot).start()

    idx_copy(i, slot).wait()

    stage_row0 = slot * tm
    stage[pl.ds(stage_row0 * SUBLANES, tm * SUBLANES), :] = h_ref[...]

    def issue(row, c):
        for k in range(TOP_K):
            _row_copy(stage, stage_row0 + row, xg_hbm, dsm[(slot * TOP_K + k) * tm + row],
                      sem_row.at[slot]).start(priority=k % DMA_THREADS)
        return c

    lax.fori_loop(0, tm, issue, 0, unroll=8)

    def wait_slot(s):
        for _ in range(TOP_K):
            _rows_wait(stage, xg_hbm, tm, sem_row.at[s])

    @pl.when(i > 0)
    def _():
        wait_slot(1 - slot)

    @pl.when(i == n - 1)
    def _():
        wait_slot(slot)


def _dispatch(h2, dest, has_pad):
    tm = dest.shape[1] // TOP_K
    t = h2.shape[0] // SUBLANES
    r = MOE_ROWS
    n_rows = TOP_K * t + N_EXPERTS * r
    grid_spec = pltpu.PrefetchScalarGridSpec(
        num_scalar_prefetch=1,
        grid=(t // tm,),
        in_specs=[pl.BlockSpec(memory_space=pl.ANY),
                  pl.BlockSpec((tm * SUBLANES, LANES), lambda i, pad: (i, 0))],
        out_specs=pl.BlockSpec(memory_space=pl.ANY),
        scratch_shapes=[
            pltpu.SMEM((2 * TOP_K * tm,), jnp.int32),
            pltpu.VMEM((2 * tm * SUBLANES, LANES), F32),
            pltpu.VMEM((r * SUBLANES, LANES), F32),
            pltpu.SemaphoreType.DMA((2,)),
            pltpu.SemaphoreType.DMA((2,)),
            pltpu.SemaphoreType.DMA,
        ],
    )
    return pl.pallas_call(
        _dispatch_kernel,
        out_shape=jax.ShapeDtypeStruct((n_rows * SUBLANES, LANES), F32),
        grid_spec=grid_spec,
        compiler_params=pltpu.CompilerParams(dimension_semantics=("arbitrary",)),
        name="moe_dispatch",
    )(has_pad, dest, h2)


def _expert_kernel(j, be_ref, nu_ref, x_ref, wg_hbm, wu_hbm, wd_hbm, y_ref,
                   wg_ref, wu_ref, wd_ref, stage_in, stage_out, sem):
    i = pl.program_id(0)
    r = x_ref.shape[0] // SUBLANES
    e = be_ref[i]
    new_expert = (i == 0) | (e != be_ref[jnp.maximum(i - 1, 0)])

    @pl.when((i < nu_ref[0]) & new_expert)
    def _():
        _load_cast([(wg_hbm.at[j, e], wg_ref, stage_in), (wu_hbm.at[j, e], wu_ref, stage_in),
                    (wd_hbm.at[j, e], wd_ref, stage_out)], sem)

    @pl.when(i < nu_ref[0])
    def _():
        xb = _load_row_tiles(x_ref, 0, r).astype(BF16)
        _store_row_tiles(y_ref, 0, _swiglu_rows(xb, wg_ref, wu_ref, wd_ref))

    @pl.when(i >= nu_ref[0])
    def _():
        y_ref[...] = jnp.zeros(y_ref.shape, y_ref.dtype)


def _experts(xg, blk_expert, n_used, j, w_gate, w_up, w_down):
    d, d_ff = w_gate.shape[2], w_gate.shape[3]
    r = MOE_ROWS
    rows = lambda i, be, nu: (i, 0)
    grid_spec = pltpu.PrefetchScalarGridSpec(
        num_scalar_prefetch=2,
        grid=(xg.shape[0] // (r * SUBLANES),),
        in_specs=[
            pl.BlockSpec((r * SUBLANES, LANES), rows),
            pl.BlockSpec(memory_space=pl.ANY),
            pl.BlockSpec(memory_space=pl.ANY),
            pl.BlockSpec(memory_space=pl.ANY),
        ],
        out_specs=pl.BlockSpec((r * SUBLANES, LANES), rows),
        scratch_shapes=_swiglu_weight_scratch(d, d_ff),
    )
    return pl.pallas_call(
        functools.partial(_expert_kernel, j),
        out_shape=jax.ShapeDtypeStruct(xg.shape, F32),
        grid_spec=grid_spec,
        compiler_params=pltpu.CompilerParams(
            dimension_semantics=("arbitrary",), vmem_limit_bytes=VMEM_LIMIT_BYTES),
        name="moe_experts",
    )(blk_expert, n_used, xg, w_gate, w_up, w_down)


def _combine_kernel(dest_hbm, y_hbm, x_ref, route_ref, o_ref, dsm, ybuf, sem_idx, sem_row):
    i = pl.program_id(0)
    n = pl.num_programs(0)
    tm = x_ref.shape[0]
    slot = lax.rem(i, 2)

    def fetch(step, s):
        idx = pltpu.make_async_copy(dest_hbm.at[step], dsm.at[pl.ds(s * TOP_K * tm, TOP_K * tm)],
                                    sem_idx.at[s])
        idx.start()
        idx.wait()

        def issue(row, c):
            for k in range(TOP_K):
                at = (s * TOP_K + k) * tm + row
                _row_copy(y_hbm, dsm[at], ybuf, at, sem_row.at[s]).start(priority=k % DMA_THREADS)
            return c

        lax.fori_loop(0, tm, issue, 0, unroll=8)

    @pl.when(i == 0)
    def _():
        fetch(0, 0)

    @pl.when(i + 1 < n)
    def _():
        fetch(i + 1, 1 - slot)

    for _ in range(TOP_K):
        _rows_wait(y_hbm, ybuf, tm, sem_row.at[slot])
    rt = route_ref[...]
    moe = None
    for k in range(TOP_K):
        yk = _load_row_tiles(ybuf, (slot * TOP_K + k) * tm * SUBLANES, tm)
        term = yk * rt[:, ROUTE_GATE0 + k:ROUTE_GATE0 + k + 1]
        moe = term if moe is None else moe + term
    o_ref[...] = x_ref[...] + moe


def _combine(x, y, route, dest):
    t, d = x.shape
    tm = dest.shape[1] // TOP_K
    return pl.pallas_call(
        _combine_kernel,
        out_shape=jax.ShapeDtypeStruct((t, d), F32),
        grid=(t // tm,),
        in_specs=[
            pl.BlockSpec(memory_space=pl.ANY),
            pl.BlockSpec(memory_space=pl.ANY),
            pl.BlockSpec((tm, d), lambda i: (i, 0)),
            pl.BlockSpec((tm, ROUTE_LANES), lambda i: (i, 0)),
        ],
        out_specs=pl.BlockSpec((tm, d), lambda i: (i, 0)),
        scratch_shapes=[
            pltpu.SMEM((2 * TOP_K * tm,), jnp.int32),
            pltpu.VMEM((2 * TOP_K * tm * SUBLANES, LANES), F32),
            pltpu.SemaphoreType.DMA((2,)),
            pltpu.SemaphoreType.DMA((2,)),
        ],
        compiler_params=pltpu.CompilerParams(
            dimension_semantics=("arbitrary",), vmem_limit_bytes=VMEM_LIMIT_BYTES),
        name="moe_combine",
    )(dest, y, x, route)


def kernel(x, norm_mix, norm_ffn, hy_w_in, hy_q_gain, hy_k_gain, hy_sinks, hy_conv_w, hy_w_out, rg_w_in, rg_conv_w, rg_conv_b, rg_gate_a_w, rg_gate_a_b, rg_gate_x_w, rg_gate_x_b, rg_lambda, rg_w_out, ffn_w_gate, ffn_w_up, ffn_w_down, moe_router, moe_w_gate, moe_w_up, moe_w_down):
    b, s, d = x.shape
    t = b * s
    depth = norm_mix.shape[0]
    for layer in range(depth):
        j = layer // 2
        if layer % 2 == 0:
            x = _even_mixer(x, norm_mix[layer], hy_w_in[j], hy_q_gain[j], hy_k_gain[j], hy_sinks[j],
                            hy_conv_w[j], hy_w_out[j])
            x = _ffn(x.reshape(t, d), norm_ffn[layer], j, ffn_w_gate, ffn_w_up, ffn_w_down)
            x = x.reshape(b, s, d)
        else:
            x, h2, route, rt, counts = _odd_mixer(
                x, norm_mix[layer], rg_w_in[j], rg_conv_w[j], rg_conv_b[j], rg_gate_a_w[j], rg_gate_a_b[j],
                rg_gate_x_w[j], rg_gate_x_b[j], rg_lambda[j], rg_w_out[j], norm_ffn[layer], moe_router[j])
            dest, blk_expert, has_pad, n_used = _moe_tables(rt, counts, t)
            xg = _dispatch(h2, dest, has_pad)
            y = _experts(xg, blk_expert, n_used, j, moe_w_gate, moe_w_up, moe_w_down)
            x = _combine(x.reshape(t, d), y, route.reshape(t, ROUTE_LANES), dest).reshape(b, s, d)
    return x
```

```python
import functools

import jax
import jax.numpy as jnp
from jax import lax
from jax.experimental import pallas as pl
from jax.experimental.pallas import tpu as pltpu

F32 = jnp.float32
BF16 = jnp.bfloat16

LANES = 128
SUBLANES = 8
VMEM_LIMIT_BYTES = 56 * 1024 * 1024
DMA_THREADS = 2

N_Q_HEADS = 8
N_KV_HEADS = 2
HEAD_DIM = 64
Q_GROUP = N_Q_HEADS // N_KV_HEADS
WINDOW = 128
ATTN_BLOCK = 128
ALIBI_MAX = 8.0
CONV_K_B = 3
CONV_K_C = 4
LRU_HEADS = 8
LRU_C = 8.0
N_EXPERTS = 8
TOP_K = 2
NORM_EPS = 1e-6
NEG_INF = -1e30

SEQ_TILE = 512
FFN_TILE = 512
MOE_ROWS = 512
FF_CHUNK = 1792
WEIGHT_CHUNKS = 8
ROUTE_LANES = LANES
ROUTE_E0 = 0
ROUTE_GATE0 = 2
ROUTE_RANK0 = 4
GELU_C0 = 0.7978845608028654
GELU_C1 = 0.044715


def _rms(x, g):
    return x * lax.rsqrt(jnp.mean(x * x, axis=-1, keepdims=True) + NORM_EPS) * g


def _split_bf16(x):
    hi = x.astype(BF16)
    lo = (x - hi.astype(F32)).astype(BF16)
    return hi, lo


def _store_row_tiles(ref, start, x):
    n = x.shape[0]
    for c in range(x.shape[1] // LANES):
        ref[pl.ds(start + c, n, stride=SUBLANES), :] = x[:, c * LANES:(c + 1) * LANES]


def _load_row_tiles(ref, start, n):
    return jnp.concatenate([ref[pl.ds(start + c, n, stride=SUBLANES), :] for c in range(SUBLANES)], axis=1)


def _resident(shape, index_map):
    return pl.BlockSpec(shape, index_map, pipeline_mode=pl.Buffered(1))


def _even_mixer_kernel(sinks_ref, x_ref, g_ref, win_ref, qg_ref, kg_ref, cw_ref, wout_ref, o_ref,
                       kd_ref, vd_ref, cbuf_ref, attn_ref):
    ts = x_ref.shape[0]
    q_dim = N_Q_HEADS * HEAD_DIM
    kv_dim = N_KV_HEADS * HEAD_DIM
    cw_dim = cw_ref.shape[1]
    st = pl.program_id(1)

    @pl.when(st == 0)
    def _():
        kd_ref[:, 0:ATTN_BLOCK, :] = jnp.zeros((N_KV_HEADS, ATTN_BLOCK, LANES), BF16)
        vd_ref[:, 0:ATTN_BLOCK, :] = jnp.zeros((N_KV_HEADS, ATTN_BLOCK, LANES), BF16)
        cbuf_ref[0:SUBLANES, :] = jnp.zeros((SUBLANES, cw_dim), F32)

    x = x_ref[...]
    h = _rms(x, g_ref[...]).astype(BF16)
    proj = jnp.dot(h, win_ref[...], preferred_element_type=F32)
    q = proj[:, 0:q_dim]
    k = proj[:, q_dim:q_dim + kv_dim]
    v = proj[:, q_dim + kv_dim:q_dim + 2 * kv_dim]
    o0 = q_dim + 2 * kv_dim
    gate_b = proj[:, o0:o0 + cw_dim]
    gate_c = proj[:, o0 + cw_dim:o0 + 2 * cw_dim]
    u = proj[:, o0 + 2 * cw_dim:o0 + 3 * cw_dim]

    ri = lax.broadcasted_iota(jnp.int32, (LANES, LANES), 0)
    ci = lax.broadcasted_iota(jnp.int32, (LANES, LANES), 1)
    half_mean = jnp.where((ri < HEAD_DIM) == (ci < HEAD_DIM), 1.0 / HEAD_DIM, 0.0).astype(BF16)

    def half_mean_square(z):
        hi, lo = _split_bf16(z * z)
        cols = []
        for c in range(z.shape[1] // LANES):
            sl = slice(c * LANES, (c + 1) * LANES)
            cols.append(jnp.dot(hi[:, sl], half_mean, preferred_element_type=F32)
                        + jnp.dot(lo[:, sl], half_mean, preferred_element_type=F32))
        return cols[0] if len(cols) == 1 else jnp.concatenate(cols, axis=1)

    qn = q * lax.rsqrt(half_mean_square(q) + NORM_EPS) * qg_ref[...] * (HEAD_DIM ** -0.5)
    kn = k * lax.rsqrt(half_mean_square(k) + NORM_EPS) * kg_ref[...]

    low_half_t = lax.broadcasted_iota(jnp.int32, (ts, LANES), 1) < HEAD_DIM
    k_sw = pltpu.roll(kn, HEAD_DIM, axis=1)
    v_sw = pltpu.roll(v, HEAD_DIM, axis=1)
    kd_ref[0, ATTN_BLOCK:ATTN_BLOCK + ts, :] = jnp.where(low_half_t, kn, k_sw).astype(BF16)
    kd_ref[1, ATTN_BLOCK:ATTN_BLOCK + ts, :] = jnp.where(low_half_t, k_sw, kn).astype(BF16)
    vd_ref[0, ATTN_BLOCK:ATTN_BLOCK + ts, :] = jnp.where(low_half_t, v, v_sw).astype(BF16)
    vd_ref[1, ATTN_BLOCK:ATTN_BLOCK + ts, :] = jnp.where(low_half_t, v_sw, v).astype(BF16)

    low_half = lax.broadcasted_iota(jnp.int32, (ATTN_BLOCK, LANES), 1) < HEAD_DIM
    qi = lax.broadcasted_iota(jnp.int32, (ATTN_BLOCK, 2 * ATTN_BLOCK), 0)
    sj = lax.broadcasted_iota(jnp.int32, (ATTN_BLOCK, 2 * ATTN_BLOCK), 1)
    diff = ATTN_BLOCK + qi - sj
    in_window = (diff >= 0) & (diff < WINDOW)
    diff_f = diff.astype(F32)
    first_key = jnp.where(st == 0, ATTN_BLOCK, 0)

    for j in range(ts // ATTN_BLOCK):
        rows = slice(j * ATTN_BLOCK, (j + 1) * ATTN_BLOCK)
        band = slice(j * ATTN_BLOCK, (j + 2) * ATTN_BLOCK)
        valid = (in_window & (sj >= first_key)) if j == 0 else in_window
        pair_out = []
        for g in range(N_KV_HEADS):
            q_rows = []
            for p in range(Q_GROUP // 2):
                c = g * (Q_GROUP // 2) + p
                qp = qn[rows, c * LANES:(c + 1) * LANES]
                q_rows.append(jnp.where(low_half, qp, 0.0))
                q_rows.append(jnp.where(low_half, 0.0, qp))
            qs = jnp.concatenate(q_rows, axis=0).astype(BF16)
            s = lax.dot_general(qs, kd_ref[g, band, :], (((1,), (1,)), ((), ())),
                                preferred_element_type=F32)
            probs = []
            denom = []
            for hh in range(Q_GROUP):
                head = g * Q_GROUP + hh
                slope = 2.0 ** (-ALIBI_MAX * (head + 1) / N_Q_HEADS)
                sh = s[hh * ATTN_BLOCK:(hh + 1) * ATTN_BLOCK, :] - slope * diff_f
                sh = jnp.where(valid, sh, NEG_INF)
                sink = sinks_ref[0, head]
                m = jnp.maximum(jnp.max(sh, axis=-1, keepdims=True), sink)
                pr = jnp.exp(sh - m)
                probs.append(pr)
                denom.append(jnp.sum(pr, axis=-1, keepdims=True) + jnp.exp(sink - m))
            pm = jnp.concatenate(probs, axis=0).astype(BF16)
            o = jnp.dot(pm, vd_ref[g, band, :], preferred_element_type=F32)
            o = o / jnp.concatenate(denom, axis=0)
            for p in range(Q_GROUP // 2):
                lo_rows = o[(2 * p) * ATTN_BLOCK:(2 * p + 1) * ATTN_BLOCK, :]
                hi_rows = o[(2 * p + 1) * ATTN_BLOCK:(2 * p + 2) * ATTN_BLOCK, :]
                pair_out.append(jnp.where(low_half, lo_rows, hi_rows))
        attn_ref[rows, :] = jnp.concatenate(pair_out, axis=1)

    cu = gate_c * u
    cbuf_ref[SUBLANES:SUBLANES + ts, :] = cu
    conv = cw_ref[CONV_K_B - 1:CONV_K_B, :] * cu
    for kk in range(CONV_K_B - 1):
        back = CONV_K_B - 1 - kk
        conv = conv + cw_ref[kk:kk + 1, :] * cbuf_ref[SUBLANES - back:SUBLANES - back + ts, :]
    conv = gate_b * conv

    mix = jnp.concatenate([attn_ref[...], conv], axis=1).astype(BF16)
    o_ref[...] = x + jnp.dot(mix, wout_ref[...], preferred_element_type=F32)

    kd_ref[:, 0:ATTN_BLOCK, :] = kd_ref[:, ts:ts + ATTN_BLOCK, :]
    vd_ref[:, 0:ATTN_BLOCK, :] = vd_ref[:, ts:ts + ATTN_BLOCK, :]
    cbuf_ref[0:SUBLANES, :] = cbuf_ref[ts:ts + SUBLANES, :]


def _even_mixer(x, g, w_in, q_gain, k_gain, sinks, conv_w, w_out):
    b, s, d = x.shape
    ts = SEQ_TILE
    n_in = w_in.shape[1]
    cw_dim = conv_w.shape[1]
    mix_dim = w_out.shape[0]
    qg = jnp.tile(q_gain, N_Q_HEADS).reshape(1, N_Q_HEADS * HEAD_DIM)
    kg = jnp.tile(k_gain, N_KV_HEADS).reshape(1, N_KV_HEADS * HEAD_DIM)
    const = lambda bi, si: (0, 0)
    return pl.pallas_call(
        _even_mixer_kernel,
        out_shape=jax.ShapeDtypeStruct((b, s, d), F32),
        grid=(b, s // ts),
        in_specs=[
            pl.BlockSpec(memory_space=pltpu.SMEM),
            pl.BlockSpec((None, ts, d), lambda bi, si: (bi, si, 0)),
            _resident((1, d), const),
            _resident((d, n_in), const),
            _resident((1, N_Q_HEADS * HEAD_DIM), const),
            _resident((1, N_KV_HEADS * HEAD_DIM), const),
            _resident((CONV_K_B, cw_dim), const),
            _resident((mix_dim, d), const),
        ],
        out_specs=pl.BlockSpec((None, ts, d), lambda bi, si: (bi, si, 0)),
        scratch_shapes=[
            pltpu.VMEM((N_KV_HEADS, ATTN_BLOCK + ts, LANES), BF16),
            pltpu.VMEM((N_KV_HEADS, ATTN_BLOCK + ts, LANES), BF16),
            pltpu.VMEM((SUBLANES + ts, cw_dim), F32),
            pltpu.VMEM((ts, N_Q_HEADS * HEAD_DIM), F32),
        ],
        compiler_params=pltpu.CompilerParams(
            dimension_semantics=("arbitrary", "arbitrary"), vmem_limit_bytes=VMEM_LIMIT_BYTES),
        name="even_mixer",
    )(sinks.reshape(1, N_Q_HEADS), x, g.reshape(1, d), w_in.astype(BF16), qg, kg, conv_w,
      w_out.astype(BF16))


def _swiglu_rows(hb, wg_ref, wu_ref, wd_ref):
    d_ff = wg_ref.shape[1]
    acc = None
    for c in range(d_ff // FF_CHUNK):
        cols = slice(c * FF_CHUNK, (c + 1) * FF_CHUNK)
        gate = jnp.dot(hb, wg_ref[:, cols], preferred_element_type=F32)
        up = jnp.dot(hb, wu_ref[:, cols], preferred_element_type=F32)
        act = (gate * (1.0 / (1.0 + jnp.exp(-gate))) * up).astype(BF16)
        part = jnp.dot(act, wd_ref[cols, :], preferred_element_type=F32)
        acc = part if acc is None else acc + part
    return acc


def _load_cast(jobs, sem):
    chunks = []
    for src, dst, stage in jobs:
        chunk_rows = stage.shape[1]
        for c in range(dst.shape[0] // chunk_rows):
            chunks.append((src, dst, stage, pl.ds(c * chunk_rows, chunk_rows)))

    def copy(n):
        src, _, stage, rows = chunks[n]
        return pltpu.make_async_copy(src.at[rows, :], stage.at[n % 2], sem.at[n % 2])

    copy(0).start()
    for n, (_, dst, stage, rows) in enumerate(chunks):
        if n + 1 < len(chunks):
            copy(n + 1).start()
        copy(n).wait()
        dst[rows, :] = stage[n % 2].astype(BF16)


def _swiglu_weight_scratch(d, d_ff):
    in_rows = d // WEIGHT_CHUNKS
    out_rows = d_ff // WEIGHT_CHUNKS
    return [
        pltpu.VMEM((d, d_ff), BF16),
        pltpu.VMEM((d, d_ff), BF16),
        pltpu.VMEM((d_ff, d), BF16),
        pltpu.VMEM((2, in_rows, d_ff), F32),
        pltpu.VMEM((2, out_rows, d), F32),
        pltpu.SemaphoreType.DMA((2,)),
    ]


def _ffn_kernel(j, x_ref, g_ref, wg_hbm, wu_hbm, wd_hbm, o_ref,
                wg_ref, wu_ref, wd_ref, stage_in, stage_out, sem):
    @pl.when(pl.program_id(0) == 0)
    def _():
        _load_cast([(wg_hbm.at[j], wg_ref, stage_in), (wu_hbm.at[j], wu_ref, stage_in),
                    (wd_hbm.at[j], wd_ref, stage_out)], sem)

    x = x_ref[...]
    hb = _rms(x, g_ref[...]).astype(BF16)
    o_ref[...] = x + _swiglu_rows(hb, wg_ref, wu_ref, wd_ref)


def _ffn(x, g, j, w_gate, w_up, w_down):
    t, d = x.shape
    d_ff = w_gate.shape[2]
    tm = FFN_TILE
    return pl.pallas_call(
        functools.partial(_ffn_kernel, j),
        out_shape=jax.ShapeDtypeStruct((t, d), F32),
        grid=(t // tm,),
        in_specs=[
            pl.BlockSpec((tm, d), lambda i: (i, 0)),
            _resident((1, d), lambda i: (0, 0)),
            pl.BlockSpec(memory_space=pl.ANY),
            pl.BlockSpec(memory_space=pl.ANY),
            pl.BlockSpec(memory_space=pl.ANY),
        ],
        out_specs=pl.BlockSpec((tm, d), lambda i: (i, 0)),
        scratch_shapes=_swiglu_weight_scratch(d, d_ff),
        compiler_params=pltpu.CompilerParams(
            dimension_semantics=("arbitrary",), vmem_limit_bytes=VMEM_LIMIT_BYTES),
        name="dense_ffn",
    )(x, g.reshape(1, d), w_gate, w_up, w_down)


def _odd_mixer_kernel(x_ref, g_ref, win_ref, cw_ref, cb_ref, gw_ref, gab_ref, gxb_ref, lam_ref, wout_ref,
                      g2_ref, wr_ref, tri_ref, o_ref, h2_ref, route_ref, rt_ref, cnt_out_ref,
                      xbuf_ref, a_ref, b_ref, hs_ref, hcar_ref, cnt_ref):
    ts = x_ref.shape[0]
    w = cw_ref.shape[1]
    hd = w // LRU_HEADS
    st = pl.program_id(1)

    @pl.when(st == 0)
    def _():
        xbuf_ref[0:SUBLANES, :] = jnp.zeros((SUBLANES, w), F32)
        hcar_ref[...] = jnp.zeros((1, w), F32)

    @pl.when((st == 0) & (pl.program_id(0) == 0))
    def _():
        cnt_ref[...] = jnp.zeros((1, ROUTE_LANES), F32)

    x = x_ref[...]
    h = _rms(x, g_ref[...]).astype(BF16)
    proj = jnp.dot(h, win_ref[...], preferred_element_type=F32)
    y = proj[:, 0:w]
    y = y * (0.5 * (1.0 + jnp.tanh(GELU_C0 * (y + GELU_C1 * (y * y * y)))))
    xb = proj[:, w:2 * w]

    xbuf_ref[SUBLANES:SUBLANES + ts, :] = xb
    xc = cw_ref[CONV_K_C - 1:CONV_K_C, :] * xb + cb_ref[...]
    for kk in range(CONV_K_C - 1):
        back = CONV_K_C - 1 - kk
        xc = xc + cw_ref[kk:kk + 1, :] * xbuf_ref[SUBLANES - back:SUBLANES - back + ts, :]

    r_cols = []
    i_cols = []
    for hh in range(LRU_HEADS):
        ri = jnp.dot(xc[:, hh * hd:(hh + 1) * hd].astype(BF16), gw_ref[hh], preferred_element_type=F32)
        r_cols.append(ri[:, 0:hd])
        i_cols.append(ri[:, hd:2 * hd])
    r_pre = jnp.concatenate(r_cols, axis=1) + gab_ref[...]
    i_pre = jnp.concatenate(i_cols, axis=1) + gxb_ref[...]
    r = 0.5 * (1.0 + jnp.tanh(0.5 * r_pre))
    ig = 0.5 * (1.0 + jnp.tanh(0.5 * i_pre))
    neg_lam = -lam_ref[...]
    softplus = jnp.maximum(neg_lam, 0.0) + jnp.log1p(jnp.exp(-jnp.abs(neg_lam)))
    log_a = -LRU_C * r * softplus
    a = jnp.exp(log_a)
    mult = jnp.sqrt(jnp.maximum(1.0 - a * a, 0.0))
    row = lax.broadcasted_iota(jnp.int32, (ts, w), 0)
    mult = jnp.where((row == 0) & (st == 0), 1.0, mult)
    a_ref[...] = a
    b_ref[...] = mult * (ig * xc)

    srow = lax.broadcasted_iota(jnp.int32, (SUBLANES, w), 0)

    def group(i, hprev):
        off = pl.multiple_of(i * SUBLANES, SUBLANES)
        av = a_ref[pl.ds(off, SUBLANES), :]
        bv = b_ref[pl.ds(off, SUBLANES), :]
        for sh in (1, 2, 4):
            a_sh = jnp.where(srow >= sh, pltpu.roll(av, sh, axis=0), 1.0)
            b_sh = jnp.where(srow >= sh, pltpu.roll(bv, sh, axis=0), 0.0)
            bv = bv + av * b_sh
            av = av * a_sh
        hv = bv + av * hprev
        hs_ref[pl.ds(off, SUBLANES), :] = hv
        return hv[SUBLANES - 1:SUBLANES, :]

    hcar_ref[...] = lax.fori_loop(0, ts // SUBLANES, group, hcar_ref[...])

    out = x + jnp.dot((y * hs_ref[...]).astype(BF16), wout_ref[...], preferred_element_type=F32)
    o_ref[...] = out
    xbuf_ref[0:SUBLANES, :] = xbuf_ref[ts:ts + SUBLANES, :]

    h2 = _rms(out, g2_ref[...])
    _store_row_tiles(h2_ref, 0, h2)
    logits = jnp.dot(h2.astype(BF16), wr_ref[...].astype(BF16), preferred_element_type=F32)
    lane = lax.broadcasted_iota(jnp.int32, (ts, ROUTE_LANES), 1)
    lg = jnp.where(lane < N_EXPERTS, logits, -jnp.inf)
    m1 = jnp.max(lg, axis=-1, keepdims=True)
    i1 = jnp.min(jnp.where(lg == m1, lane, ROUTE_LANES), axis=-1, keepdims=True)
    lg2 = jnp.where(lane == i1, -jnp.inf, lg)
    m2 = jnp.max(lg2, axis=-1, keepdims=True)
    i2 = jnp.min(jnp.where(lg2 == m2, lane, ROUTE_LANES), axis=-1, keepdims=True)
    e2 = jnp.exp(m2 - m1)
    gate1 = 1.0 / (1.0 + e2)
    gate2 = e2 / (1.0 + e2)

    first = lane == i1
    second = lane == i2
    ind = jnp.where(first | second, 1.0, 0.0)
    cums = jnp.dot(tri_ref[...], ind.astype(BF16), preferred_element_type=F32)
    rank = cnt_ref[...] + cums - ind
    cnt_ref[...] = cnt_ref[...] + cums[ts - 1:ts, :]
    rank1 = jnp.sum(jnp.where(first, rank, 0.0), axis=-1, keepdims=True)
    rank2 = jnp.sum(jnp.where(second, rank, 0.0), axis=-1, keepdims=True)
    cnt_out_ref[...] = jnp.broadcast_to(cnt_ref[...], (SUBLANES, ROUTE_LANES)).astype(jnp.int32)

    record = jnp.where(lane == ROUTE_E0, i1.astype(F32), 0.0)
    record = jnp.where(lane == ROUTE_E0 + 1, i2.astype(F32), record)
    record = jnp.where(lane == ROUTE_GATE0, gate1, record)
    record = jnp.where(lane == ROUTE_GATE0 + 1, gate2, record)
    record = jnp.where(lane == ROUTE_RANK0, rank1, record)
    record = jnp.where(lane == ROUTE_RANK0 + 1, rank2, record)
    route_ref[...] = record
    rt_ref[...] = record.T[0:SUBLANES, :].astype(jnp.int32)


def _odd_mixer(x, g, w_in, conv_w, conv_b, ga_w, ga_b, gx_w, gx_b, lam, w_out, g2, w_router):
    b, s, d = x.shape
    ts = SEQ_TILE
    w = conv_w.shape[1]
    hd = w // LRU_HEADS
    gw = jnp.concatenate([ga_w, gx_w], axis=-1).astype(BF16)
    wr = jnp.pad(w_router, ((0, 0), (0, ROUTE_LANES - N_EXPERTS)))
    tri = jnp.tril(jnp.ones((ts, ts), BF16))
    nst = s // ts
    const = lambda bi, si: (0, 0)
    row = lambda a: a.reshape(1, -1)
    tile = lambda bi, si: (bi, si, 0)
    return pl.pallas_call(
        _odd_mixer_kernel,
        out_shape=(jax.ShapeDtypeStruct((b, s, d), F32),
                   jax.ShapeDtypeStruct((b * s * SUBLANES, LANES), F32),
                   jax.ShapeDtypeStruct((b, s, ROUTE_LANES), F32),
                   jax.ShapeDtypeStruct((b * nst, SUBLANES, ts), jnp.int32),
                   jax.ShapeDtypeStruct((SUBLANES, ROUTE_LANES), jnp.int32)),
        grid=(b, nst),
        in_specs=[
            pl.BlockSpec((None, ts, d), lambda bi, si: (bi, si, 0)),
            _resident((1, d), const),
            _resident((d, 2 * w), const),
            _resident((CONV_K_C, w), const),
            _resident((1, w), const),
            _resident((LRU_HEADS, hd, 2 * hd), lambda bi, si: (0, 0, 0)),
            _resident((1, w), const),
            _resident((1, w), const),
            _resident((1, w), const),
            _resident((w, d), const),
            _resident((1, d), const),
            _resident((d, ROUTE_LANES), const),
            _resident((ts, ts), const),
        ],
        out_specs=(pl.BlockSpec((None, ts, d), tile),
                   pl.BlockSpec((ts * SUBLANES, LANES), lambda bi, si: (bi * nst + si, 0)),
                   pl.BlockSpec((None, ts, ROUTE_LANES), tile),
                   pl.BlockSpec((None, SUBLANES, ts), lambda bi, si: (bi * nst + si, 0, 0)),
                   pl.BlockSpec((SUBLANES, ROUTE_LANES), const)),
        scratch_shapes=[
            pltpu.VMEM((SUBLANES + ts, w), F32),
            pltpu.VMEM((ts, w), F32),
            pltpu.VMEM((ts, w), F32),
            pltpu.VMEM((ts, w), F32),
            pltpu.VMEM((1, w), F32),
            pltpu.VMEM((1, ROUTE_LANES), F32),
        ],
        compiler_params=pltpu.CompilerParams(
            dimension_semantics=("arbitrary", "arbitrary"), vmem_limit_bytes=VMEM_LIMIT_BYTES),
        name="odd_mixer",
    )(x, row(g), w_in.astype(BF16), conv_w, row(conv_b), gw, row(ga_b), row(gx_b), row(lam),
      w_out.astype(BF16), row(g2), wr, tri)


def _moe_tables(rt, counts, t):
    r = MOE_ROWS
    n_blocks = TOP_K * t // r + N_EXPERTS
    counts = counts[0, 0:N_EXPERTS]
    padded = (counts + r - 1) // r * r
    pad_end = jnp.cumsum(padded)
    pad_start = pad_end - padded
    e = rt[:, ROUTE_E0:ROUTE_E0 + TOP_K, :]
    rank = rt[:, ROUTE_RANK0:ROUTE_RANK0 + TOP_K, :]
    start = jnp.zeros_like(e)
    for ex in range(N_EXPERTS):
        start = jnp.where(e == ex, pad_start[ex], start)
    dest = start + rank
    blk = jnp.arange(n_blocks, dtype=jnp.int32)
    blk_expert = jnp.minimum(jnp.sum(blk[:, None] * r >= pad_end[None, :], axis=1), N_EXPERTS - 1)
    n_used = pad_end[N_EXPERTS - 1] // r
    last_blk = jnp.where(padded > 0, pad_end // r - 1, -1)
    has_pad = jnp.any(blk[:, None] == last_blk[None, :], axis=1) | (blk >= n_used)
    dest = dest.astype(jnp.int32).reshape(dest.shape[0], TOP_K * dest.shape[2])
    return dest, blk_expert.astype(jnp.int32), has_pad.astype(jnp.int32), n_used.astype(jnp.int32).reshape(1)


def _row_copy(src, src_row, dst, dst_row, sem):
    return pltpu.make_async_copy(src.at[pl.ds(src_row * SUBLANES, SUBLANES), :],
                                 dst.at[pl.ds(dst_row * SUBLANES, SUBLANES), :], sem)


def _rows_wait(src, dst, n_rows, sem):
    pltpu.make_async_copy(src.at[pl.ds(0, n_rows * SUBLANES), :],
                          dst.at[pl.ds(0, n_rows * SUBLANES), :], sem).wait()


def _dispatch_kernel(pad_ref, dest_hbm, h_ref, xg_hbm, dsm, stage, zbuf, sem_idx, sem_row, sem_zero):
    i = pl.program_id(0)
    n = pl.num_programs(0)
    tm = h_ref.shape[0] // SUBLANES
    r = zbuf.shape[0] // SUBLANES
    n_blocks = pad_ref.shape[0]
    slot = lax.rem(i, 2)

    def idx_copy(step, s):
        return pltpu.make_async_copy(dest_hbm.at[step], dsm.at[pl.ds(s * TOP_K * tm, TOP_K * tm)],
                                     sem_idx.at[s])

    def zero_copy(blk):
        return pltpu.make_async_copy(zbuf, xg_hbm.at[pl.ds(blk * r * SUBLANES, r * SUBLANES), :], sem_zero)

    @pl.when(i == 0)
    def _():
        idx_copy(0, 0).start()
        zbuf[...] = jnp.zeros(zbuf.shape, zbuf.dtype)

        def zero_start(blk, c):
            @pl.when(pad_ref[blk] == 1)
            def _():
                zero_copy(blk).start()
            return c

        def zero_wait(blk, c):
            @pl.when(pad_ref[blk] == 1)
            def _():
                zero_copy(blk).wait()
            return c

        lax.fori_loop(0, n_blocks, zero_start, 0)
        lax.fori_loop(0, n_blocks, zero_wait, 0)

    @pl.when(i + 1 < n)
    def _():
        idx_copy(i + 1, 1 - slot).start()

    idx_copy(i, slot).wait()

    stage_row0 = slot * tm
    stage[pl.ds(stage_row0 * SUBLANES, tm * SUBLANES), :] = h_ref[...]

    def issue(row, c):
        for k in range(TOP_K):
            _row_copy(stage, stage_row0 + row, xg_hbm, dsm[(slot * TOP_K + k) * tm + row],
                      sem_row.at[slot]).start(priority=k % DMA_THREADS)
        return c

    lax.fori_loop(0, tm, issue, 0, unroll=8)

    def wait_slot(s):
        for _ in range(TOP_K):
            _rows_wait(stage, xg_hbm, tm, sem_row.at[s])

    @pl.when(i > 0)
    def _():
        wait_slot(1 - slot)

    @pl.when(i == n - 1)
    def _():
        wait_slot(slot)


def _dispatch(h2, dest, has_pad):
    tm = dest.shape[1] // TOP_K
    t = h2.shape[0] // SUBLANES
    r = MOE_ROWS
    n_rows = TOP_K * t + N_EXPERTS * r
    grid_spec = pltpu.PrefetchScalarGridSpec(
        num_scalar_prefetch=1,
        grid=(t // tm,),
        in_specs=[pl.BlockSpec(memory_space=pl.ANY),
                  pl.BlockSpec((tm * SUBLANES, LANES), lambda i, pad: (i, 0))],
        out_specs=pl.BlockSpec(memory_space=pl.ANY),
        scratch_shapes=[
            pltpu.SMEM((2 * TOP_K * tm,), jnp.int32),
            pltpu.VMEM((2 * tm * SUBLANES, LANES), F32),
            pltpu.VMEM((r * SUBLANES, LANES), F32),
            pltpu.SemaphoreType.DMA((2,)),
            pltpu.SemaphoreType.DMA((2,)),
            pltpu.SemaphoreType.DMA,
        ],
    )
    return pl.pallas_call(
        _dispatch_kernel,
        out_shape=jax.ShapeDtypeStruct((n_rows * SUBLANES, LANES), F32),
        grid_spec=grid_spec,
        compiler_params=pltpu.CompilerParams(dimension_semantics=("arbitrary",)),
        name="moe_dispatch",
    )(has_pad, dest, h2)


def _expert_kernel(j, be_ref, nu_ref, x_ref, wg_hbm, wu_hbm, wd_hbm, y_ref,
                   wg_ref, wu_ref, wd_ref, stage_in, stage_out, sem):
    i = pl.program_id(0)
    r = x_ref.shape[0] // SUBLANES
    e = be_ref[i]
    new_expert = (i == 0) | (e != be_ref[jnp.maximum(i - 1, 0)])

    @pl.when((i < nu_ref[0]) & new_expert)
    def _():
        _load_cast([(wg_hbm.at[j, e], wg_ref, stage_in), (wu_hbm.at[j, e], wu_ref, stage_in),
                    (wd_hbm.at[j, e], wd_ref, stage_out)], sem)

    @pl.when(i < nu_ref[0])
    def _():
        xb = _load_row_tiles(x_ref, 0, r).astype(BF16)
        _store_row_tiles(y_ref, 0, _swiglu_rows(xb, wg_ref, wu_ref, wd_ref))

    @pl.when(i >= nu_ref[0])
    def _():
        y_ref[...] = jnp.zeros(y_ref.shape, y_ref.dtype)


def _experts(xg, blk_expert, n_used, j, w_gate, w_up, w_down):
    d, d_ff = w_gate.shape[2], w_gate.shape[3]
    r = MOE_ROWS
    rows = lambda i, be, nu: (i, 0)
    grid_spec = pltpu.PrefetchScalarGridSpec(
        num_scalar_prefetch=2,
        grid=(xg.shape[0] // (r * SUBLANES),),
        in_specs=[
            pl.BlockSpec((r * SUBLANES, LANES), rows),
            pl.BlockSpec(memory_space=pl.ANY),
            pl.BlockSpec(memory_space=pl.ANY),
            pl.BlockSpec(memory_space=pl.ANY),
        ],
        out_specs=pl.BlockSpec((r * SUBLANES, LANES), rows),
        scratch_shapes=_swiglu_weight_scratch(d, d_ff),
    )
    return pl.pallas_call(
        functools.partial(_expert_kernel, j),
        out_shape=jax.ShapeDtypeStruct(xg.shape, F32),
        grid_spec=grid_spec,
        compiler_params=pltpu.CompilerParams(
            dimension_semantics=("arbitrary",), vmem_limit_bytes=VMEM_LIMIT_BYTES),
        name="moe_experts",
    )(blk_expert, n_used, xg, w_gate, w_up, w_down)


def _combine_kernel(dest_hbm, y_hbm, x_ref, route_ref, o_ref, dsm, ybuf, sem_idx, sem_row):
    i = pl.program_id(0)
    n = pl.num_programs(0)
    tm = x_ref.shape[0]
    slot = lax.rem(i, 2)

    def fetch(step, s):
        idx = pltpu.make_async_copy(dest_hbm.at[step], dsm.at[pl.ds(s * TOP_K * tm, TOP_K * tm)],
                                    sem_idx.at[s])
        idx.start()
        idx.wait()

        def issue(row, c):
            for k in range(TOP_K):
                at = (s * TOP_K + k) * tm + row
                _row_copy(y_hbm, dsm[at], ybuf, at, sem_row.at[s]).start(priority=k % DMA_THREADS)
            return c

        lax.fori_loop(0, tm, issue, 0, unroll=8)

    @pl.when(i == 0)
    def _():
        fetch(0, 0)

    @pl.when(i + 1 < n)
    def _():
        fetch(i + 1, 1 - slot)

    for _ in range(TOP_K):
        _rows_wait(y_hbm, ybuf, tm, sem_row.at[slot])
    rt = route_ref[...]
    moe = None
    for k in range(TOP_K):
        yk = _load_row_tiles(ybuf, (slot * TOP_K + k) * tm * SUBLANES, tm)
        term = yk * rt[:, ROUTE_GATE0 + k:ROUTE_GATE0 + k + 1]
        moe = term if moe is None else moe + term
    o_ref[...] = x_ref[...] + moe


def _combine(x, y, route, dest):
    t, d = x.shape
    tm = dest.shape[1] // TOP_K
    return pl.pallas_call(
        _combine_kernel,
        out_shape=jax.ShapeDtypeStruct((t, d), F32),
        grid=(t // tm,),
        in_specs=[
            pl.BlockSpec(memory_space=pl.ANY),
            pl.BlockSpec(memory_space=pl.ANY),
            pl.BlockSpec((tm, d), lambda i: (i, 0)),
            pl.BlockSpec((tm, ROUTE_LANES), lambda i: (i, 0)),
        ],
        out_specs=pl.BlockSpec((tm, d), lambda i: (i, 0)),
        scratch_shapes=[
            pltpu.SMEM((2 * TOP_K * tm,), jnp.int32),
            pltpu.VMEM((2 * TOP_K * tm * SUBLANES, LANES), F32),
            pltpu.SemaphoreType.DMA((2,)),
            pltpu.SemaphoreType.DMA((2,)),
        ],
        compiler_params=pltpu.CompilerParams(
            dimension_semantics=("arbitrary",), vmem_limit_bytes=VMEM_LIMIT_BYTES),
        name="moe_combine",
    )(dest, y, x, route)


def kernel(x, norm_mix, norm_ffn, hy_w_in, hy_q_gain, hy_k_gain, hy_sinks, hy_conv_w, hy_w_out, rg_w_in, rg_conv_w, rg_conv_b, rg_gate_a_w, rg_gate_a_b, rg_gate_x_w, rg_gate_x_b, rg_lambda, rg_w_out, ffn_w_gate, ffn_w_up, ffn_w_down, moe_router, moe_w_gate, moe_w_up, moe_w_down):
    b, s, d = x.shape
    t = b * s
    depth = norm_mix.shape[0]
    for layer in range(depth):
        j = layer // 2
        if layer % 2 == 0:
            x = _even_mixer(x, norm_mix[layer], hy_w_in[j], hy_q_gain[j], hy_k_gain[j], hy_sinks[j],
                            hy_conv_w[j], hy_w_out[j])
            x = _ffn(x.reshape(t, d), norm_ffn[layer], j, ffn_w_gate, ffn_w_up, ffn_w_down)
            x = x.reshape(b, s, d)
        else:
            x, h2, route, rt, counts = _odd_mixer(
                x, norm_mix[layer], rg_w_in[j], rg_conv_w[j], rg_conv_b[j], rg_gate_a_w[j], rg_gate_a_b[j],
                rg_gate_x_w[j], rg_gate_x_b[j], rg_lambda[j], rg_w_out[j], norm_ffn[layer], moe_router[j])
            dest, blk_expert, has_pad, n_used = _moe_tables(rt, counts, t)
            xg = _dispatch(h2, dest, has_pad)
            y = _experts(xg, blk_expert, n_used, j, moe_w_gate, moe_w_up, moe_w_down)
            x = _combine(x.reshape(t, d), y, route.reshape(t, ROUTE_LANES), dest).reshape(b, s, d)
    return x
```

```python
import functools

import jax
import jax.numpy as jnp
from jax import lax
from jax.experimental import pallas as pl
from jax.experimental.pallas import tpu as pltpu

F32 = jnp.float32
BF16 = jnp.bfloat16

LANES = 128
SUBLANES = 8
VMEM_LIMIT_BYTES = 56 * 1024 * 1024
DMA_THREADS = 2

N_Q_HEADS = 8
N_KV_HEADS = 2
HEAD_DIM = 64
Q_GROUP = N_Q_HEADS // N_KV_HEADS
WINDOW = 128
ATTN_BLOCK = 128
ALIBI_MAX = 8.0
CONV_K_B = 3
CONV_K_C = 4
LRU_HEADS = 8
LRU_C = 8.0
N_EXPERTS = 8
TOP_K = 2
NORM_EPS = 1e-6
NEG_INF = -1e30

SEQ_TILE = 512
FFN_TILE = 512
MOE_ROWS = 256
FF_CHUNK = 1792
WEIGHT_CHUNKS = 8
ROUTE_LANES = LANES
ROUTE_E0 = 0
ROUTE_GATE0 = 2
ROUTE_RANK0 = 4
GELU_C0 = 0.7978845608028654
GELU_C1 = 0.044715


def _rms(x, g):
    return x * lax.rsqrt(jnp.mean(x * x, axis=-1, keepdims=True) + NORM_EPS) * g


def _split_bf16(x):
    hi = x.astype(BF16)
    lo = (x - hi.astype(F32)).astype(BF16)
    return hi, lo


def _store_row_tiles(ref, start, x):
    n = x.shape[0]
    for c in range(x.shape[1] // LANES):
        ref[pl.ds(start + c, n, stride=SUBLANES), :] = x[:, c * LANES:(c + 1) * LANES]


def _load_row_tiles(ref, start, n):
    return jnp.concatenate([ref[pl.ds(start + c, n, stride=SUBLANES), :] for c in range(SUBLANES)], axis=1)


def _resident(shape, index_map):
    return pl.BlockSpec(shape, index_map, pipeline_mode=pl.Buffered(1))


def _even_mixer_kernel(sinks_ref, x_ref, g_ref, win_ref, qg_ref, kg_ref, cw_ref, wout_ref, o_ref,
                       kd_ref, vd_ref, cbuf_ref, attn_ref):
    ts = x_ref.shape[0]
    q_dim = N_Q_HEADS * HEAD_DIM
    kv_dim = N_KV_HEADS * HEAD_DIM
    cw_dim = cw_ref.shape[1]
    st = pl.program_id(1)

    @pl.when(st == 0)
    def _():
        kd_ref[:, 0:ATTN_BLOCK, :] = jnp.zeros((N_KV_HEADS, ATTN_BLOCK, LANES), BF16)
        vd_ref[:, 0:ATTN_BLOCK, :] = jnp.zeros((N_KV_HEADS, ATTN_BLOCK, LANES), BF16)
        cbuf_ref[0:SUBLANES, :] = jnp.zeros((SUBLANES, cw_dim), F32)

    x = x_ref[...]
    h = _rms(x, g_ref[...]).astype(BF16)
    proj = jnp.dot(h, win_ref[...], preferred_element_type=F32)
    q = proj[:, 0:q_dim]
    k = proj[:, q_dim:q_dim + kv_dim]
    v = proj[:, q_dim + kv_dim:q_dim + 2 * kv_dim]
    o0 = q_dim + 2 * kv_dim
    gate_b = proj[:, o0:o0 + cw_dim]
    gate_c = proj[:, o0 + cw_dim:o0 + 2 * cw_dim]
    u = proj[:, o0 + 2 * cw_dim:o0 + 3 * cw_dim]

    ri = lax.broadcasted_iota(jnp.int32, (LANES, LANES), 0)
    ci = lax.broadcasted_iota(jnp.int32, (LANES, LANES), 1)
    half_mean = jnp.where((ri < HEAD_DIM) == (ci < HEAD_DIM), 1.0 / HEAD_DIM, 0.0).astype(BF16)

    def half_mean_square(z):
        hi, lo = _split_bf16(z * z)
        cols = []
        for c in range(z.shape[1] // LANES):
            sl = slice(c * LANES, (c + 1) * LANES)
            cols.append(jnp.dot(hi[:, sl], half_mean, preferred_element_type=F32)
                        + jnp.dot(lo[:, sl], half_mean, preferred_element_type=F32))
        return cols[0] if len(cols) == 1 else jnp.concatenate(cols, axis=1)

    qn = q * lax.rsqrt(half_mean_square(q) + NORM_EPS) * qg_ref[...] * (HEAD_DIM ** -0.5)
    kn = k * lax.rsqrt(half_mean_square(k) + NORM_EPS) * kg_ref[...]

    low_half_t = lax.broadcasted_iota(jnp.int32, (ts, LANES), 1) < HEAD_DIM
    k_sw = pltpu.roll(kn, HEAD_DIM, axis=1)
    v_sw = pltpu.roll(v, HEAD_DIM, axis=1)
    kd_ref[0, ATTN_BLOCK:ATTN_BLOCK + ts, :] = jnp.where(low_half_t, kn, k_sw).astype(BF16)
    kd_ref[1, ATTN_BLOCK:ATTN_BLOCK + ts, :] = jnp.where(low_half_t, k_sw, kn).astype(BF16)
    vd_ref[0, ATTN_BLOCK:ATTN_BLOCK + ts, :] = jnp.where(low_half_t, v, v_sw).astype(BF16)
    vd_ref[1, ATTN_BLOCK:ATTN_BLOCK + ts, :] = jnp.where(low_half_t, v_sw, v).astype(BF16)

    low_half = lax.broadcasted_iota(jnp.int32, (ATTN_BLOCK, LANES), 1) < HEAD_DIM
    qi = lax.broadcasted_iota(jnp.int32, (ATTN_BLOCK, 2 * ATTN_BLOCK), 0)
    sj = lax.broadcasted_iota(jnp.int32, (ATTN_BLOCK, 2 * ATTN_BLOCK), 1)
    diff = ATTN_BLOCK + qi - sj
    in_window = (diff >= 0) & (diff < WINDOW)
    diff_f = diff.astype(F32)
    first_key = jnp.where(st == 0, ATTN_BLOCK, 0)

    for j in range(ts // ATTN_BLOCK):
        rows = slice(j * ATTN_BLOCK, (j + 1) * ATTN_BLOCK)
        band = slice(j * ATTN_BLOCK, (j + 2) * ATTN_BLOCK)
        valid = (in_window & (sj >= first_key)) if j == 0 else in_window
        pair_out = []
        for g in range(N_KV_HEADS):
            q_rows = []
            for p in range(Q_GROUP // 2):
                c = g * (Q_GROUP // 2) + p
                qp = qn[rows, c * LANES:(c + 1) * LANES]
                q_rows.append(jnp.where(low_half, qp, 0.0))
                q_rows.append(jnp.where(low_half, 0.0, qp))
            qs = jnp.concatenate(q_rows, axis=0).astype(BF16)
            s = lax.dot_general(qs, kd_ref[g, band, :], (((1,), (1,)), ((), ())),
                                preferred_element_type=F32)
            probs = []
            denom = []
            for hh in range(Q_GROUP):
                head = g * Q_GROUP + hh
                slope = 2.0 ** (-ALIBI_MAX * (head + 1) / N_Q_HEADS)
                sh = s[hh * ATTN_BLOCK:(hh + 1) * ATTN_BLOCK, :] - slope * diff_f
                sh = jnp.where(valid, sh, NEG_INF)
                sink = sinks_ref[0, head]
                m = jnp.maximum(jnp.max(sh, axis=-1, keepdims=True), sink)
                pr = jnp.exp(sh - m)
                probs.append(pr)
                denom.append(jnp.sum(pr, axis=-1, keepdims=True) + jnp.exp(sink - m))
            pm = jnp.concatenate(probs, axis=0).astype(BF16)
            o = jnp.dot(pm, vd_ref[g, band, :], preferred_element_type=F32)
            o = o / jnp.concatenate(denom, axis=0)
            for p in range(Q_GROUP // 2):
                lo_rows = o[(2 * p) * ATTN_BLOCK:(2 * p + 1) * ATTN_BLOCK, :]
                hi_rows = o[(2 * p + 1) * ATTN_BLOCK:(2 * p + 2) * ATTN_BLOCK, :]
                pair_out.append(jnp.where(low_half, lo_rows, hi_rows))
        attn_ref[rows, :] = jnp.concatenate(pair_out, axis=1)

    cu = gate_c * u
    cbuf_ref[SUBLANES:SUBLANES + ts, :] = cu
    conv = cw_ref[CONV_K_B - 1:CONV_K_B, :] * cu
    for kk in range(CONV_K_B - 1):
        back = CONV_K_B - 1 - kk
        conv = conv + cw_ref[kk:kk + 1, :] * cbuf_ref[SUBLANES - back:SUBLANES - back + ts, :]
    conv = gate_b * conv

    mix = jnp.concatenate([attn_ref[...], conv], axis=1).astype(BF16)
    o_ref[...] = x + jnp.dot(mix, wout_ref[...], preferred_element_type=F32)

    kd_ref[:, 0:ATTN_BLOCK, :] = kd_ref[:, ts:ts + ATTN_BLOCK, :]
    vd_ref[:, 0:ATTN_BLOCK, :] = vd_ref[:, ts:ts + ATTN_BLOCK, :]
    cbuf_ref[0:SUBLANES, :] = cbuf_ref[ts:ts + SUBLANES, :]


def _even_mixer(x, g, w_in, q_gain, k_gain, sinks, conv_w, w_out):
    b, s, d = x.shape
    ts = SEQ_TILE
    n_in = w_in.shape[1]
    cw_dim = conv_w.shape[1]
    mix_dim = w_out.shape[0]
    qg = jnp.tile(q_gain, N_Q_HEADS).reshape(1, N_Q_HEADS * HEAD_DIM)
    kg = jnp.tile(k_gain, N_KV_HEADS).reshape(1, N_KV_HEADS * HEAD_DIM)
    const = lambda bi, si: (0, 0)
    return pl.pallas_call(
        _even_mixer_kernel,
        out_shape=jax.ShapeDtypeStruct((b, s, d), F32),
        grid=(b, s // ts),
        in_specs=[
            pl.BlockSpec(memory_space=pltpu.SMEM),
            pl.BlockSpec((None, ts, d), lambda bi, si: (bi, si, 0)),
            _resident((1, d), const),
            _resident((d, n_in), const),
            _resident((1, N_Q_HEADS * HEAD_DIM), const),
            _resident((1, N_KV_HEADS * HEAD_DIM), const),
            _resident((CONV_K_B, cw_dim), const),
            _resident((mix_dim, d), const),
        ],
        out_specs=pl.BlockSpec((None, ts, d), lambda bi, si: (bi, si, 0)),
        scratch_shapes=[
            pltpu.VMEM((N_KV_HEADS, ATTN_BLOCK + ts, LANES), BF16),
            pltpu.VMEM((N_KV_HEADS, ATTN_BLOCK + ts, LANES), BF16),
            pltpu.VMEM((SUBLANES + ts, cw_dim), F32),
            pltpu.VMEM((ts, N_Q_HEADS * HEAD_DIM), F32),
        ],
        compiler_params=pltpu.CompilerParams(
            dimension_semantics=("arbitrary", "arbitrary"), vmem_limit_bytes=VMEM_LIMIT_BYTES),
        name="even_mixer",
    )(sinks.reshape(1, N_Q_HEADS), x, g.reshape(1, d), w_in.astype(BF16), qg, kg, conv_w,
      w_out.astype(BF16))


def _swiglu_rows(hb, wg_ref, wu_ref, wd_ref):
    d_ff = wg_ref.shape[1]
    acc = None
    for c in range(d_ff // FF_CHUNK):
        cols = slice(c * FF_CHUNK, (c + 1) * FF_CHUNK)
        gate = jnp.dot(hb, wg_ref[:, cols], preferred_element_type=F32)
        up = jnp.dot(hb, wu_ref[:, cols], preferred_element_type=F32)
        act = (gate * (1.0 / (1.0 + jnp.exp(-gate))) * up).astype(BF16)
        part = jnp.dot(act, wd_ref[cols, :], preferred_element_type=F32)
        acc = part if acc is None else acc + part
    return acc


def _load_cast(jobs, sem):
    chunks = []
    for src, dst, stage in jobs:
        chunk_rows = stage.shape[1]
        for c in range(dst.shape[0] // chunk_rows):
            chunks.append((src, dst, stage, pl.ds(c * chunk_rows, chunk_rows)))

    def copy(n):
        src, _, stage, rows = chunks[n]
        return pltpu.make_async_copy(src.at[rows, :], stage.at[n % 2], sem.at[n % 2])

    copy(0).start()
    for n, (_, dst, stage, rows) in enumerate(chunks):
        if n + 1 < len(chunks):
            copy(n + 1).start()
        copy(n).wait()
        dst[rows, :] = stage[n % 2].astype(BF16)


def _swiglu_weight_scratch(d, d_ff):
    in_rows = d // WEIGHT_CHUNKS
    out_rows = d_ff // WEIGHT_CHUNKS
    return [
        pltpu.VMEM((d, d_ff), BF16),
        pltpu.VMEM((d, d_ff), BF16),
        pltpu.VMEM((d_ff, d), BF16),
        pltpu.VMEM((2, in_rows, d_ff), F32),
        pltpu.VMEM((2, out_rows, d), F32),
        pltpu.SemaphoreType.DMA((2,)),
    ]


def _ffn_kernel(j, x_ref, g_ref, wg_hbm, wu_hbm, wd_hbm, o_ref,
                wg_ref, wu_ref, wd_ref, stage_in, stage_out, sem):
    @pl.when(pl.program_id(0) == 0)
    def _():
        _load_cast([(wg_hbm.at[j], wg_ref, stage_in), (wu_hbm.at[j], wu_ref, stage_in),
                    (wd_hbm.at[j], wd_ref, stage_out)], sem)

    x = x_ref[...]
    hb = _rms(x, g_ref[...]).astype(BF16)
    o_ref[...] = x + _swiglu_rows(hb, wg_ref, wu_ref, wd_ref)


def _ffn(x, g, j, w_gate, w_up, w_down):
    t, d = x.shape
    d_ff = w_gate.shape[2]
    tm = FFN_TILE
    return pl.pallas_call(
        functools.partial(_ffn_kernel, j),
        out_shape=jax.ShapeDtypeStruct((t, d), F32),
        grid=(t // tm,),
        in_specs=[
            pl.BlockSpec((tm, d), lambda i: (i, 0)),
            _resident((1, d), lambda i: (0, 0)),
            pl.BlockSpec(memory_space=pl.ANY),
            pl.BlockSpec(memory_space=pl.ANY),
            pl.BlockSpec(memory_space=pl.ANY),
        ],
        out_specs=pl.BlockSpec((tm, d), lambda i: (i, 0)),
        scratch_shapes=_swiglu_weight_scratch(d, d_ff),
        compiler_params=pltpu.CompilerParams(
            dimension_semantics=("arbitrary",), vmem_limit_bytes=VMEM_LIMIT_BYTES),
        name="dense_ffn",
    )(x, g.reshape(1, d), w_gate, w_up, w_down)


def _odd_mixer_kernel(x_ref, g_ref, win_ref, cw_ref, cb_ref, gw_ref, gab_ref, gxb_ref, lam_ref, wout_ref,
                      g2_ref, wr_ref, tri_ref, o_ref, h2_ref, route_ref, rt_ref, cnt_out_ref,
                      xbuf_ref, a_ref, b_ref, hs_ref, hcar_ref, cnt_ref):
    ts = x_ref.shape[0]
    w = cw_ref.shape[1]
    hd = w // LRU_HEADS
    st = pl.program_id(1)

    @pl.when(st == 0)
    def _():
        xbuf_ref[0:SUBLANES, :] = jnp.zeros((SUBLANES, w), F32)
        hcar_ref[...] = jnp.zeros((1, w), F32)

    @pl.when((st == 0) & (pl.program_id(0) == 0))
    def _():
        cnt_ref[...] = jnp.zeros((1, ROUTE_LANES), F32)

    x = x_ref[...]
    h = _rms(x, g_ref[...]).astype(BF16)
    proj = jnp.dot(h, win_ref[...], preferred_element_type=F32)
    y = proj[:, 0:w]
    y = y * (0.5 * (1.0 + jnp.tanh(GELU_C0 * (y + GELU_C1 * (y * y * y)))))
    xb = proj[:, w:2 * w]

    xbuf_ref[SUBLANES:SUBLANES + ts, :] = xb
    xc = cw_ref[CONV_K_C - 1:CONV_K_C, :] * xb + cb_ref[...]
    for kk in range(CONV_K_C - 1):
        back = CONV_K_C - 1 - kk
        xc = xc + cw_ref[kk:kk + 1, :] * xbuf_ref[SUBLANES - back:SUBLANES - back + ts, :]

    r_cols = []
    i_cols = []
    for hh in range(LRU_HEADS):
        ri = jnp.dot(xc[:, hh * hd:(hh + 1) * hd].astype(BF16), gw_ref[hh], preferred_element_type=F32)
        r_cols.append(ri[:, 0:hd])
        i_cols.append(ri[:, hd:2 * hd])
    r_pre = jnp.concatenate(r_cols, axis=1) + gab_ref[...]
    i_pre = jnp.concatenate(i_cols, axis=1) + gxb_ref[...]
    r = 0.5 * (1.0 + jnp.tanh(0.5 * r_pre))
    ig = 0.5 * (1.0 + jnp.tanh(0.5 * i_pre))
    neg_lam = -lam_ref[...]
    softplus = jnp.maximum(neg_lam, 0.0) + jnp.log1p(jnp.exp(-jnp.abs(neg_lam)))
    log_a = -LRU_C * r * softplus
    a = jnp.exp(log_a)
    mult = jnp.sqrt(jnp.maximum(1.0 - a * a, 0.0))
    row = lax.broadcasted_iota(jnp.int32, (ts, w), 0)
    mult = jnp.where((row == 0) & (st == 0), 1.0, mult)
    a_ref[...] = a
    b_ref[...] = mult * (ig * xc)

    srow = lax.broadcasted_iota(jnp.int32, (SUBLANES, w), 0)

    def group(i, hprev):
        off = pl.multiple_of(i * SUBLANES, SUBLANES)
        av = a_ref[pl.ds(off, SUBLANES), :]
        bv = b_ref[pl.ds(off, SUBLANES), :]
        for sh in (1, 2, 4):
            a_sh = jnp.where(srow >= sh, pltpu.roll(av, sh, axis=0), 1.0)
            b_sh = jnp.where(srow >= sh, pltpu.roll(bv, sh, axis=0), 0.0)
            bv = bv + av * b_sh
            av = av * a_sh
        hv = bv + av * hprev
        hs_ref[pl.ds(off, SUBLANES), :] = hv
        return hv[SUBLANES - 1:SUBLANES, :]

    hcar_ref[...] = lax.fori_loop(0, ts // SUBLANES, group, hcar_ref[...])

    out = x + jnp.dot((y * hs_ref[...]).astype(BF16), wout_ref[...], preferred_element_type=F32)
    o_ref[...] = out
    xbuf_ref[0:SUBLANES, :] = xbuf_ref[ts:ts + SUBLANES, :]

    h2 = _rms(out, g2_ref[...])
    _store_row_tiles(h2_ref, 0, h2)
    logits = jnp.dot(h2.astype(BF16), wr_ref[...].astype(BF16), preferred_element_type=F32)
    lane = lax.broadcasted_iota(jnp.int32, (ts, ROUTE_LANES), 1)
    lg = jnp.where(lane < N_EXPERTS, logits, -jnp.inf)
    m1 = jnp.max(lg, axis=-1, keepdims=True)
    i1 = jnp.min(jnp.where(lg == m1, lane, ROUTE_LANES), axis=-1, keepdims=True)
    lg2 = jnp.where(lane == i1, -jnp.inf, lg)
    m2 = jnp.max(lg2, axis=-1, keepdims=True)
    i2 = jnp.min(jnp.where(lg2 == m2, lane, ROUTE_LANES), axis=-1, keepdims=True)
    e2 = jnp.exp(m2 - m1)
    gate1 = 1.0 / (1.0 + e2)
    gate2 = e2 / (1.0 + e2)

    first = lane == i1
    second = lane == i2
    ind = jnp.where(first | second, 1.0, 0.0)
    cums = jnp.dot(tri_ref[...], ind.astype(BF16), preferred_element_type=F32)
    rank = cnt_ref[...] + cums - ind
    cnt_ref[...] = cnt_ref[...] + cums[ts - 1:ts, :]
    rank1 = jnp.sum(jnp.where(first, rank, 0.0), axis=-1, keepdims=True)
    rank2 = jnp.sum(jnp.where(second, rank, 0.0), axis=-1, keepdims=True)
    cnt_out_ref[...] = jnp.broadcast_to(cnt_ref[...], (SUBLANES, ROUTE_LANES)).astype(jnp.int32)

    record = jnp.where(lane == ROUTE_E0, i1.astype(F32), 0.0)
    record = jnp.where(lane == ROUTE_E0 + 1, i2.astype(F32), record)
    record = jnp.where(lane == ROUTE_GATE0, gate1, record)
    record = jnp.where(lane == ROUTE_GATE0 + 1, gate2, record)
    record = jnp.where(lane == ROUTE_RANK0, rank1, record)
    record = jnp.where(lane == ROUTE_RANK0 + 1, rank2, record)
    route_ref[...] = record
    rt_ref[...] = record.T[0:SUBLANES, :].astype(jnp.int32)


def _odd_mixer(x, g, w_in, conv_w, conv_b, ga_w, ga_b, gx_w, gx_b, lam, w_out, g2, w_router):
    b, s, d = x.shape
    ts = SEQ_TILE
    w = conv_w.shape[1]
    hd = w // LRU_HEADS
    gw = jnp.concatenate([ga_w, gx_w], axis=-1).astype(BF16)
    wr = jnp.pad(w_router, ((0, 0), (0, ROUTE_LANES - N_EXPERTS)))
    tri = jnp.tril(jnp.ones((ts, ts), BF16))
    nst = s // ts
    const = lambda bi, si: (0, 0)
    row = lambda a: a.reshape(1, -1)
    tile = lambda bi, si: (bi, si, 0)
    return pl.pallas_call(
        _odd_mixer_kernel,
        out_shape=(jax.ShapeDtypeStruct((b, s, d), F32),
                   jax.ShapeDtypeStruct((b * s * SUBLANES, LANES), F32),
                   jax.ShapeDtypeStruct((b, s, ROUTE_LANES), F32),
                   jax.ShapeDtypeStruct((b * nst, SUBLANES, ts), jnp.int32),
                   jax.ShapeDtypeStruct((SUBLANES, ROUTE_LANES), jnp.int32)),
        grid=(b, nst),
        in_specs=[
            pl.BlockSpec((None, ts, d), lambda bi, si: (bi, si, 0)),
            _resident((1, d), const),
            _resident((d, 2 * w), const),
            _resident((CONV_K_C, w), const),
            _resident((1, w), const),
            _resident((LRU_HEADS, hd, 2 * hd), lambda bi, si: (0, 0, 0)),
            _resident((1, w), const),
            _resident((1, w), const),
            _resident((1, w), const),
            _resident((w, d), const),
            _resident((1, d), const),
            _resident((d, ROUTE_LANES), const),
            _resident((ts, ts), const),
        ],
        out_specs=(pl.BlockSpec((None, ts, d), tile),
                   pl.BlockSpec((ts * SUBLANES, LANES), lambda bi, si: (bi * nst + si, 0)),
                   pl.BlockSpec((None, ts, ROUTE_LANES), tile),
                   pl.BlockSpec((None, SUBLANES, ts), lambda bi, si: (bi * nst + si, 0, 0)),
                   pl.BlockSpec((SUBLANES, ROUTE_LANES), const)),
        scratch_shapes=[
            pltpu.VMEM((SUBLANES + ts, w), F32),
            pltpu.VMEM((ts, w), F32),
            pltpu.VMEM((ts, w), F32),
            pltpu.VMEM((ts, w), F32),
            pltpu.VMEM((1, w), F32),
            pltpu.VMEM((1, ROUTE_LANES), F32),
        ],
        compiler_params=pltpu.CompilerParams(
            dimension_semantics=("arbitrary", "arbitrary"), vmem_limit_bytes=VMEM_LIMIT_BYTES),
        name="odd_mixer",
    )(x, row(g), w_in.astype(BF16), conv_w, row(conv_b), gw, row(ga_b), row(gx_b), row(lam),
      w_out.astype(BF16), row(g2), wr, tri)


def _moe_tables(rt, counts, t):
    r = MOE_ROWS
    n_blocks = TOP_K * t // r + N_EXPERTS
    counts = counts[0, 0:N_EXPERTS]
    padded = (counts + r - 1) // r * r
    pad_end = jnp.cumsum(padded)
    pad_start = pad_end - padded
    e = rt[:, ROUTE_E0:ROUTE_E0 + TOP_K, :]
    rank = rt[:, ROUTE_RANK0:ROUTE_RANK0 + TOP_K, :]
    start = jnp.zeros_like(e)
    for ex in range(N_EXPERTS):
        start = jnp.where(e == ex, pad_start[ex], start)
    dest = start + rank
    blk = jnp.arange(n_blocks, dtype=jnp.int32)
    blk_expert = jnp.minimum(jnp.sum(blk[:, None] * r >= pad_end[None, :], axis=1), N_EXPERTS - 1)
    n_used = pad_end[N_EXPERTS - 1] // r
    last_blk = jnp.where(padded > 0, pad_end // r - 1, -1)
    has_pad = jnp.any(blk[:, None] == last_blk[None, :], axis=1) | (blk >= n_used)
    dest = dest.astype(jnp.int32).reshape(dest.shape[0], TOP_K * dest.shape[2])
    return dest, blk_expert.astype(jnp.int32), has_pad.astype(jnp.int32), n_used.astype(jnp.int32).reshape(1)


def _row_copy(src, src_row, dst, dst_row, sem):
    return pltpu.make_async_copy(src.at[pl.ds(src_row * SUBLANES, SUBLANES), :],
                                 dst.at[pl.ds(dst_row * SUBLANES, SUBLANES), :], sem)


def _rows_wait(src, dst, n_rows, sem):
    pltpu.make_async_copy(src.at[pl.ds(0, n_rows * SUBLANES), :],
                          dst.at[pl.ds(0, n_rows * SUBLANES), :], sem).wait()


def _dispatch_kernel(pad_ref, dest_hbm, h_ref, xg_hbm, dsm, stage, zbuf, sem_idx, sem_row, sem_zero):
    i = pl.program_id(0)
    n = pl.num_programs(0)
    tm = h_ref.shape[0] // SUBLANES
    r = zbuf.shape[0] // SUBLANES
    n_blocks = pad_ref.shape[0]
    slot = lax.rem(i, 2)

    def idx_copy(step, s):
        return pltpu.make_async_copy(dest_hbm.at[step], dsm.at[pl.ds(s * TOP_K * tm, TOP_K * tm)],
                                     sem_idx.at[s])

    def zero_copy(blk):
        return pltpu.make_async_copy(zbuf, xg_hbm.at[pl.ds(blk * r * SUBLANES, r * SUBLANES), :], sem_zero)

    @pl.when(i == 0)
    def _():
        idx_copy(0, 0).start()
        zbuf[...] = jnp.zeros(zbuf.shape, zbuf.dtype)

        def zero_start(blk, c):
            @pl.when(pad_ref[blk] == 1)
            def _():
                zero_copy(blk).start()
            return c

        def zero_wait(blk, c):
            @pl.when(pad_ref[blk] == 1)
            def _():
                zero_copy(blk).wait()
            return c

        lax.fori_loop(0, n_blocks, zero_start, 0)
        lax.fori_loop(0, n_blocks, zero_wait, 0)

    @pl.when(i + 1 < n)
    def _():
        idx_copy(i + 1, 1 - slot).start()

    idx_copy(i, slot).wait()

    stage_row0 = slot * tm
    stage[pl.ds(stage_row0 * SUBLANES, tm * SUBLANES), :] = h_ref[...]

    def issue(row, c):
        for k in range(TOP_K):
            _row_copy(stage, stage_row0 + row, xg_hbm, dsm[(slot * TOP_K + k) * tm + row],
                      sem_row.at[slot]).start(priority=k % DMA_THREADS)
        return c

    lax.fori_loop(0, tm, issue, 0, unroll=8)

    def wait_slot(s):
        for _ in range(TOP_K):
            _rows_wait(stage, xg_hbm, tm, sem_row.at[s])

    @pl.when(i > 0)
    def _():
        wait_slot(1 - slot)

    @pl.when(i == n - 1)
    def _():
        wait_slot(slot)


def _dispatch(h2, dest, has_pad):
    tm = dest.shape[1] // TOP_K
    t = h2.shape[0] // SUBLANES
    r = MOE_ROWS
    n_rows = TOP_K * t + N_EXPERTS * r
    grid_spec = pltpu.PrefetchScalarGridSpec(
        num_scalar_prefetch=1,
        grid=(t // tm,),
        in_specs=[pl.BlockSpec(memory_space=pl.ANY),
                  pl.BlockSpec((tm * SUBLANES, LANES), lambda i, pad: (i, 0))],
        out_specs=pl.BlockSpec(memory_space=pl.ANY),
        scratch_shapes=[
            pltpu.SMEM((2 * TOP_K * tm,), jnp.int32),
            pltpu.VMEM((2 * tm * SUBLANES, LANES), F32),
            pltpu.VMEM((r * SUBLANES, LANES), F32),
            pltpu.SemaphoreType.DMA((2,)),
            pltpu.SemaphoreType.DMA((2,)),
            pltpu.SemaphoreType.DMA,
        ],
    )
    return pl.pallas_call(
        _dispatch_kernel,
        out_shape=jax.ShapeDtypeStruct((n_rows * SUBLANES, LANES), F32),
        grid_spec=grid_spec,
        compiler_params=pltpu.CompilerParams(dimension_semantics=("arbitrary",)),
        name="moe_dispatch",
    )(has_pad, dest, h2)


def _expert_kernel(j, be_ref, nu_ref, x_ref, wg_hbm, wu_hbm, wd_hbm, y_ref,
                   wg_ref, wu_ref, wd_ref, stage_in, stage_out, sem):
    i = pl.program_id(0)
    r = x_ref.shape[0] // SUBLANES
    e = be_ref[i]
    new_expert = (i == 0) | (e != be_ref[jnp.maximum(i - 1, 0)])

    @pl.when((i < nu_ref[0]) & new_expert)
    def _():
        _load_cast([(wg_hbm.at[j, e], wg_ref, stage_in), (wu_hbm.at[j, e], wu_ref, stage_in),
                    (wd_hbm.at[j, e], wd_ref, stage_out)], sem)

    @pl.when(i < nu_ref[0])
    def _():
        xb = _load_row_tiles(x_ref, 0, r).astype(BF16)
        _store_row_tiles(y_ref, 0, _swiglu_rows(xb, wg_ref, wu_ref, wd_ref))

    @pl.when(i >= nu_ref[0])
    def _():
        y_ref[...] = jnp.zeros(y_ref.shape, y_ref.dtype)


def _experts(xg, blk_expert, n_used, j, w_gate, w_up, w_down):
    d, d_ff = w_gate.shape[2], w_gate.shape[3]
    r = MOE_ROWS
    rows = lambda i, be, nu: (i, 0)
    grid_spec = pltpu.PrefetchScalarGridSpec(
        num_scalar_prefetch=2,
        grid=(xg.shape[0] // (r * SUBLANES),),
        in_specs=[
            pl.BlockSpec((r * SUBLANES, LANES), rows),
            pl.BlockSpec(memory_space=pl.ANY),
            pl.BlockSpec(memory_space=pl.ANY),
            pl.BlockSpec(memory_space=pl.ANY),
        ],
        out_specs=pl.BlockSpec((r * SUBLANES, LANES), rows),
        scratch_shapes=_swiglu_weight_scratch(d, d_ff),
    )
    return pl.pallas_call(
        functools.partial(_expert_kernel, j),
        out_shape=jax.ShapeDtypeStruct(xg.shape, F32),
        grid_spec=grid_spec,
        compiler_params=pltpu.CompilerParams(
            dimension_semantics=("arbitrary",), vmem_limit_bytes=VMEM_LIMIT_BYTES),
        name="moe_experts",
    )(blk_expert, n_used, xg, w_gate, w_up, w_down)


def _combine_kernel(dest_hbm, y_hbm, x_ref, route_ref, o_ref, dsm, ybuf, sem_idx, sem_row):
    i = pl.program_id(0)
    n = pl.num_programs(0)
    tm = x_ref.shape[0]
    slot = lax.rem(i, 2)

    def fetch(step, s):
        idx = pltpu.make_async_copy(dest_hbm.at[step], dsm.at[pl.ds(s * TOP_K * tm, TOP_K * tm)],
                                    sem_idx.at[s])
        idx.start()
        idx.wait()

        def issue(row, c):
            for k in range(TOP_K):
                at = (s * TOP_K + k) * tm + row
                _row_copy(y_hbm, dsm[at], ybuf, at, sem_row.at[s]).start(priority=k % DMA_THREADS)
            return c

        lax.fori_loop(0, tm, issue, 0, unroll=8)

    @pl.when(i == 0)
    def _():
        fetch(0, 0)

    @pl.when(i + 1 < n)
    def _():
        fetch(i + 1, 1 - slot)

    for _ in range(TOP_K):
        _rows_wait(y_hbm, ybuf, tm, sem_row.at[slot])
    rt = route_ref[...]
    moe = None
    for k in range(TOP_K):
        yk = _load_row_tiles(ybuf, (slot * TOP_K + k) * tm * SUBLANES, tm)
        term = yk * rt[:, ROUTE_GATE0 + k:ROUTE_GATE0 + k + 1]
        moe = term if moe is None else moe + term
    o_ref[...] = x_ref[...] + moe


def _combine(x, y, route, dest):
    t, d = x.shape
    tm = dest.shape[1] // TOP_K
    return pl.pallas_call(
        _combine_kernel,
        out_shape=jax.ShapeDtypeStruct((t, d), F32),
        grid=(t // tm,),
        in_specs=[
            pl.BlockSpec(memory_space=pl.ANY),
            pl.BlockSpec(memory_space=pl.ANY),
            pl.BlockSpec((tm, d), lambda i: (i, 0)),
            pl.BlockSpec((tm, ROUTE_LANES), lambda i: (i, 0)),
        ],
        out_specs=pl.BlockSpec((tm, d), lambda i: (i, 0)),
        scratch_shapes=[
            pltpu.SMEM((2 * TOP_K * tm,), jnp.int32),
            pltpu.VMEM((2 * TOP_K * tm * SUBLANES, LANES), F32),
            pltpu.SemaphoreType.DMA((2,)),
            pltpu.SemaphoreType.DMA((2,)),
        ],
        compiler_params=pltpu.CompilerParams(
            dimension_semantics=("arbitrary",), vmem_limit_bytes=VMEM_LIMIT_BYTES),
        name="moe_combine",
    )(dest, y, x, route)


def kernel(x, norm_mix, norm_ffn, hy_w_in, hy_q_gain, hy_k_gain, hy_sinks, hy_conv_w, hy_w_out, rg_w_in, rg_conv_w, rg_conv_b, rg_gate_a_w, rg_gate_a_b, rg_gate_x_w, rg_gate_x_b, rg_lambda, rg_w_out, ffn_w_gate, ffn_w_up, ffn_w_down, moe_router, moe_w_gate, moe_w_up, moe_w_down):
    b, s, d = x.shape
    t = b * s
    depth = norm_mix.shape[0]
    for layer in range(depth):
        j = layer // 2
        if layer % 2 == 0:
            x = _even_mixer(x, norm_mix[layer], hy_w_in[j], hy_q_gain[j], hy_k_gain[j], hy_sinks[j],
                            hy_conv_w[j], hy_w_out[j])
            x = _ffn(x.reshape(t, d), norm_ffn[layer], j, ffn_w_gate, ffn_w_up, ffn_w_down)
            x = x.reshape(b, s, d)
        else:
            x, h2, route, rt, counts = _odd_mixer(
                x, norm_mix[layer], rg_w_in[j], rg_conv_w[j], rg_conv_b[j], rg_gate_a_w[j], rg_gate_a_b[j],
                rg_gate_x_w[j], rg_gate_x_b[j], rg_lambda[j], rg_w_out[j], norm_ffn[layer], moe_router[j])
            dest, blk_expert, has_pad, n_used = _moe_tables(rt, counts, t)
            xg = _dispatch(h2, dest, has_pad)
            y = _experts(xg, blk_expert, n_used, j, moe_w_gate, moe_w_up, moe_w_down)
            x = _combine(x.reshape(t, d), y, route.reshape(t, ROUTE_LANES), dest).reshape(b, s, d)
    return x
```

```python
import functools

import jax
import jax.numpy as jnp
from jax import lax
from jax.experimental import pallas as pl
from jax.experimental.pallas import tpu as pltpu

F32 = jnp.float32
BF16 = jnp.bfloat16

LANES = 128
SUBLANES = 8
VMEM_LIMIT_BYTES = 56 * 1024 * 1024
DMA_THREADS = 2

N_Q_HEADS = 8
N_KV_HEADS = 2
HEAD_DIM = 64
Q_GROUP = N_Q_HEADS // N_KV_HEADS
WINDOW = 128
ATTN_BLOCK = 128
ALIBI_MAX = 8.0
CONV_K_B = 3
CONV_K_C = 4
LRU_HEADS = 8
LRU_C = 8.0
N_EXPERTS = 8
TOP_K = 2
NORM_EPS = 1e-6
NEG_INF = -1e30

SEQ_TILE = 512
FFN_TILE = 512
MOE_ROWS = 256
FF_CHUNK = 1792
WEIGHT_CHUNKS = 8
ROUTE_LANES = LANES
ROUTE_E0 = 0
ROUTE_GATE0 = 2
ROUTE_RANK0 = 4
GELU_C0 = 0.7978845608028654
GELU_C1 = 0.044715


def _rms(x, g):
    return x * lax.rsqrt(jnp.mean(x * x, axis=-1, keepdims=True) + NORM_EPS) * g


def _split_bf16(x):
    hi = x.astype(BF16)
    lo = (x - hi.astype(F32)).astype(BF16)
    return hi, lo


def _store_row_tiles(ref, start, x):
    n = x.shape[0]
    for c in range(x.shape[1] // LANES):
        ref[pl.ds(start + c, n, stride=SUBLANES), :] = x[:, c * LANES:(c + 1) * LANES]


def _load_row_tiles(ref, start, n):
    return jnp.concatenate([ref[pl.ds(start + c, n, stride=SUBLANES), :] for c in range(SUBLANES)], axis=1)


def _resident(shape, index_map):
    return pl.BlockSpec(shape, index_map, pipeline_mode=pl.Buffered(1))


def _even_mixer_kernel(sinks_ref, x_ref, g_ref, win_ref, qg_ref, kg_ref, cw_ref, wout_ref, o_ref,
                       kd_ref, vt_ref, cbuf_ref, attn_t_ref):
    ts = x_ref.shape[0]
    q_dim = N_Q_HEADS * HEAD_DIM
    kv_dim = N_KV_HEADS * HEAD_DIM
    cw_dim = cw_ref.shape[1]
    st = pl.program_id(1)

    @pl.when(st == 0)
    def _():
        kd_ref[:, 0:ATTN_BLOCK, :] = jnp.zeros((N_KV_HEADS, ATTN_BLOCK, LANES), BF16)
        vt_ref[:, 0:ATTN_BLOCK] = jnp.zeros((kv_dim, ATTN_BLOCK), BF16)
        cbuf_ref[0:SUBLANES, :] = jnp.zeros((SUBLANES, cw_dim), F32)

    x = x_ref[...]
    h = _rms(x, g_ref[...]).astype(BF16)
    proj = jnp.dot(h, win_ref[...], preferred_element_type=F32)
    q = proj[:, 0:q_dim]
    k = proj[:, q_dim:q_dim + kv_dim]
    v = proj[:, q_dim + kv_dim:q_dim + 2 * kv_dim]
    o0 = q_dim + 2 * kv_dim
    gate_b = proj[:, o0:o0 + cw_dim]
    gate_c = proj[:, o0 + cw_dim:o0 + 2 * cw_dim]
    u = proj[:, o0 + 2 * cw_dim:o0 + 3 * cw_dim]

    ri = lax.broadcasted_iota(jnp.int32, (LANES, LANES), 0)
    ci = lax.broadcasted_iota(jnp.int32, (LANES, LANES), 1)
    half_mean = jnp.where((ri < HEAD_DIM) == (ci < HEAD_DIM), 1.0 / HEAD_DIM, 0.0).astype(BF16)

    def half_mean_square(z):
        hi, lo = _split_bf16(z * z)
        cols = []
        for c in range(z.shape[1] // LANES):
            sl = slice(c * LANES, (c + 1) * LANES)
            cols.append(jnp.dot(hi[:, sl], half_mean, preferred_element_type=F32)
                        + jnp.dot(lo[:, sl], half_mean, preferred_element_type=F32))
        return cols[0] if len(cols) == 1 else jnp.concatenate(cols, axis=1)

    qn = q * lax.rsqrt(half_mean_square(q) + NORM_EPS) * qg_ref[...] * (HEAD_DIM ** -0.5)
    kn = k * lax.rsqrt(half_mean_square(k) + NORM_EPS) * kg_ref[...]

    low_half_t = lax.broadcasted_iota(jnp.int32, (ts, LANES), 1) < HEAD_DIM
    k_sw = pltpu.roll(kn, HEAD_DIM, axis=1)
    kd_ref[0, ATTN_BLOCK:ATTN_BLOCK + ts, :] = jnp.where(low_half_t, kn, k_sw).astype(BF16)
    kd_ref[1, ATTN_BLOCK:ATTN_BLOCK + ts, :] = jnp.where(low_half_t, k_sw, kn).astype(BF16)
    vt_ref[:, ATTN_BLOCK:ATTN_BLOCK + ts] = v.T.astype(BF16)

    low_half = lax.broadcasted_iota(jnp.int32, (ATTN_BLOCK, LANES), 1) < HEAD_DIM
    sj = lax.broadcasted_iota(jnp.int32, (2 * ATTN_BLOCK, ATTN_BLOCK), 0)
    qi = lax.broadcasted_iota(jnp.int32, (2 * ATTN_BLOCK, ATTN_BLOCK), 1)
    diff = ATTN_BLOCK + qi - sj
    in_window = (diff >= 0) & (diff < WINDOW)
    diff_f = diff.astype(F32)
    first_key = jnp.where(st == 0, ATTN_BLOCK, 0)
    in_window_first = in_window & (sj >= first_key)

    def penalty(valid, g):
        cols = []
        for hh in range(Q_GROUP):
            slope = 2.0 ** (-ALIBI_MAX * (g * Q_GROUP + hh + 1) / N_Q_HEADS)
            cols.append(jnp.where(valid, slope * diff_f, -NEG_INF))
        return jnp.concatenate(cols, axis=1)

    pen = [penalty(in_window, g) for g in range(N_KV_HEADS)]
    pen_first = [penalty(in_window_first, g) for g in range(N_KV_HEADS)]
    sink = [jnp.concatenate([jnp.full((1, ATTN_BLOCK), sinks_ref[0, g * Q_GROUP + hh], F32)
                             for hh in range(Q_GROUP)], axis=1) for g in range(N_KV_HEADS)]

    for j in range(ts // ATTN_BLOCK):
        rows = slice(j * ATTN_BLOCK, (j + 1) * ATTN_BLOCK)
        band = slice(j * ATTN_BLOCK, (j + 2) * ATTN_BLOCK)
        for g in range(N_KV_HEADS):
            q_rows = []
            for p in range(Q_GROUP // 2):
                c = g * (Q_GROUP // 2) + p
                qp = qn[rows, c * LANES:(c + 1) * LANES]
                q_rows.append(jnp.where(low_half, qp, 0.0))
                q_rows.append(jnp.where(low_half, 0.0, qp))
            qs = jnp.concatenate(q_rows, axis=0).astype(BF16)
            s = lax.dot_general(kd_ref[g, band, :], qs, (((1,), (1,)), ((), ())),
                                preferred_element_type=F32)
            sh = s - (pen_first[g] if j == 0 else pen[g])
            m = jnp.maximum(jnp.max(sh, axis=0, keepdims=True), sink[g])
            pr = jnp.exp(sh - m)
            denom = jnp.sum(pr, axis=0, keepdims=True) + jnp.exp(sink[g] - m)
            o = jnp.dot(vt_ref[g * HEAD_DIM:(g + 1) * HEAD_DIM, band], pr.astype(BF16),
                        preferred_element_type=F32)
            o = o / denom
            for hh in range(Q_GROUP):
                head = g * Q_GROUP + hh
                attn_t_ref[head * HEAD_DIM:(head + 1) * HEAD_DIM, rows] = (
                    o[:, hh * ATTN_BLOCK:(hh + 1) * ATTN_BLOCK])

    cu = gate_c * u
    cbuf_ref[SUBLANES:SUBLANES + ts, :] = cu
    conv = cw_ref[CONV_K_B - 1:CONV_K_B, :] * cu
    for kk in range(CONV_K_B - 1):
        back = CONV_K_B - 1 - kk
        conv = conv + cw_ref[kk:kk + 1, :] * cbuf_ref[SUBLANES - back:SUBLANES - back + ts, :]
    conv = gate_b * conv

    mix = jnp.concatenate([attn_t_ref[...].T, conv], axis=1).astype(BF16)
    o_ref[...] = x + jnp.dot(mix, wout_ref[...], preferred_element_type=F32)

    kd_ref[:, 0:ATTN_BLOCK, :] = kd_ref[:, ts:ts + ATTN_BLOCK, :]
    vt_ref[:, 0:ATTN_BLOCK] = vt_ref[:, ts:ts + ATTN_BLOCK]
    cbuf_ref[0:SUBLANES, :] = cbuf_ref[ts:ts + SUBLANES, :]


def _even_mixer(x, g, w_in, q_gain, k_gain, sinks, conv_w, w_out):
    b, s, d = x.shape
    ts = SEQ_TILE
    n_in = w_in.shape[1]
    cw_dim = conv_w.shape[1]
    mix_dim = w_out.shape[0]
    qg = jnp.tile(q_gain, N_Q_HEADS).reshape(1, N_Q_HEADS * HEAD_DIM)
    kg = jnp.tile(k_gain, N_KV_HEADS).reshape(1, N_KV_HEADS * HEAD_DIM)
    const = lambda bi, si: (0, 0)
    return pl.pallas_call(
        _even_mixer_kernel,
        out_shape=jax.ShapeDtypeStruct((b, s, d), F32),
        grid=(b, s // ts),
        in_specs=[
            pl.BlockSpec(memory_space=pltpu.SMEM),
            pl.BlockSpec((None, ts, d), lambda bi, si: (bi, si, 0)),
            _resident((1, d), const),
            _resident((d, n_in), const),
            _resident((1, N_Q_HEADS * HEAD_DIM), const),
            _resident((1, N_KV_HEADS * HEAD_DIM), const),
            _resident((CONV_K_B, cw_dim), const),
            _resident((mix_dim, d), const),
        ],
        out_specs=pl.BlockSpec((None, ts, d), lambda bi, si: (bi, si, 0)),
        scratch_shapes=[
            pltpu.VMEM((N_KV_HEADS, ATTN_BLOCK + ts, LANES), BF16),
            pltpu.VMEM((N_KV_HEADS * HEAD_DIM, ATTN_BLOCK + ts), BF16),
            pltpu.VMEM((SUBLANES + ts, cw_dim), F32),
            pltpu.VMEM((N_Q_HEADS * HEAD_DIM, ts), F32),
        ],
        compiler_params=pltpu.CompilerParams(
            dimension_semantics=("arbitrary", "arbitrary"), vmem_limit_bytes=VMEM_LIMIT_BYTES),
        name="even_mixer",
    )(sinks.reshape(1, N_Q_HEADS), x, g.reshape(1, d), w_in.astype(BF16), qg, kg, conv_w,
      w_out.astype(BF16))


def _swiglu_rows(hb, wg_ref, wu_ref, wd_ref):
    d_ff = wg_ref.shape[1]
    acc = None
    for c in range(d_ff // FF_CHUNK):
        cols = slice(c * FF_CHUNK, (c + 1) * FF_CHUNK)
        gate = jnp.dot(hb, wg_ref[:, cols], preferred_element_type=F32)
        up = jnp.dot(hb, wu_ref[:, cols], preferred_element_type=F32)
        act = (gate * (1.0 / (1.0 + jnp.exp(-gate))) * up).astype(BF16)
        part = jnp.dot(act, wd_ref[cols, :], preferred_element_type=F32)
        acc = part if acc is None else acc + part
    return acc


def _load_cast(jobs, sem):
    chunks = []
    for src, dst, stage in jobs:
        chunk_rows = stage.shape[1]
        for c in range(dst.shape[0] // chunk_rows):
            chunks.append((src, dst, stage, pl.ds(c * chunk_rows, chunk_rows)))

    def copy(n):
        src, _, stage, rows = chunks[n]
        return pltpu.make_async_copy(src.at[rows, :], stage.at[n % 2], sem.at[n % 2])

    copy(0).start()
    for n, (_, dst, stage, rows) in enumerate(chunks):
        if n + 1 < len(chunks):
            copy(n + 1).start()
        copy(n).wait()
        dst[rows, :] = stage[n % 2].astype(BF16)


def _swiglu_weight_scratch(d, d_ff):
    in_rows = d // WEIGHT_CHUNKS
    out_rows = d_ff // WEIGHT_CHUNKS
    return [
        pltpu.VMEM((d, d_ff), BF16),
        pltpu.VMEM((d, d_ff), BF16),
        pltpu.VMEM((d_ff, d), BF16),
        pltpu.VMEM((2, in_rows, d_ff), F32),
        pltpu.VMEM((2, out_rows, d), F32),
        pltpu.SemaphoreType.DMA((2,)),
    ]


def _ffn_kernel(j, x_ref, g_ref, wg_hbm, wu_hbm, wd_hbm, o_ref,
                wg_ref, wu_ref, wd_ref, stage_in, stage_out, sem):
    @pl.when(pl.program_id(0) == 0)
    def _():
        _load_cast([(wg_hbm.at[j], wg_ref, stage_in), (wu_hbm.at[j], wu_ref, stage_in),
                    (wd_hbm.at[j], wd_ref, stage_out)], sem)

    x = x_ref[...]
    hb = _rms(x, g_ref[...]).astype(BF16)
    o_ref[...] = x + _swiglu_rows(hb, wg_ref, wu_ref, wd_ref)


def _ffn(x, g, j, w_gate, w_up, w_down):
    t, d = x.shape
    d_ff = w_gate.shape[2]
    tm = FFN_TILE
    return pl.pallas_call(
        functools.partial(_ffn_kernel, j),
        out_shape=jax.ShapeDtypeStruct((t, d), F32),
        grid=(t // tm,),
        in_specs=[
            pl.BlockSpec((tm, d), lambda i: (i, 0)),
            _resident((1, d), lambda i: (0, 0)),
            pl.BlockSpec(memory_space=pl.ANY),
            pl.BlockSpec(memory_space=pl.ANY),
            pl.BlockSpec(memory_space=pl.ANY),
        ],
        out_specs=pl.BlockSpec((tm, d), lambda i: (i, 0)),
        scratch_shapes=_swiglu_weight_scratch(d, d_ff),
        compiler_params=pltpu.CompilerParams(
            dimension_semantics=("arbitrary",), vmem_limit_bytes=VMEM_LIMIT_BYTES),
        name="dense_ffn",
    )(x, g.reshape(1, d), w_gate, w_up, w_down)


def _odd_mixer_kernel(x_ref, g_ref, win_ref, cw_ref, cb_ref, gw_ref, gab_ref, gxb_ref, lam_ref, wout_ref,
                      g2_ref, wr_ref, tri_ref, o_ref, h2_ref, route_ref, rt_ref, cnt_out_ref,
                      xbuf_ref, a_ref, b_ref, hs_ref, hcar_ref, cnt_ref):
    ts = x_ref.shape[0]
    w = cw_ref.shape[1]
    hd = w // LRU_HEADS
    st = pl.program_id(1)

    @pl.when(st == 0)
    def _():
        xbuf_ref[0:SUBLANES, :] = jnp.zeros((SUBLANES, w), F32)
        hcar_ref[...] = jnp.zeros((1, w), F32)

    @pl.when((st == 0) & (pl.program_id(0) == 0))
    def _():
        cnt_ref[...] = jnp.zeros((1, ROUTE_LANES), F32)

    x = x_ref[...]
    h = _rms(x, g_ref[...]).astype(BF16)
    proj = jnp.dot(h, win_ref[...], preferred_element_type=F32)
    y = proj[:, 0:w]
    y = y * (0.5 * (1.0 + jnp.tanh(GELU_C0 * (y + GELU_C1 * (y * y * y)))))
    xb = proj[:, w:2 * w]

    xbuf_ref[SUBLANES:SUBLANES + ts, :] = xb
    xc = cw_ref[CONV_K_C - 1:CONV_K_C, :] * xb + cb_ref[...]
    for kk in range(CONV_K_C - 1):
        back = CONV_K_C - 1 - kk
        xc = xc + cw_ref[kk:kk + 1, :] * xbuf_ref[SUBLANES - back:SUBLANES - back + ts, :]

    r_cols = []
    i_cols = []
    for hh in range(LRU_HEADS):
        ri = jnp.dot(xc[:, hh * hd:(hh + 1) * hd].astype(BF16), gw_ref[hh], preferred_element_type=F32)
        r_cols.append(ri[:, 0:hd])
        i_cols.append(ri[:, hd:2 * hd])
    r_pre = jnp.concatenate(r_cols, axis=1) + gab_ref[...]
    i_pre = jnp.concatenate(i_cols, axis=1) + gxb_ref[...]
    r = 0.5 * (1.0 + jnp.tanh(0.5 * r_pre))
    ig = 0.5 * (1.0 + jnp.tanh(0.5 * i_pre))
    neg_lam = -lam_ref[...]
    softplus = jnp.maximum(neg_lam, 0.0) + jnp.log1p(jnp.exp(-jnp.abs(neg_lam)))
    log_a = -LRU_C * r * softplus
    a = jnp.exp(log_a)
    mult = jnp.sqrt(jnp.maximum(1.0 - a * a, 0.0))
    row = lax.broadcasted_iota(jnp.int32, (ts, w), 0)
    mult = jnp.where((row == 0) & (st == 0), 1.0, mult)
    a_ref[...] = a
    b_ref[...] = mult * (ig * xc)

    srow = lax.broadcasted_iota(jnp.int32, (SUBLANES, w), 0)

    def group(i, hprev):
        off = pl.multiple_of(i * SUBLANES, SUBLANES)
        av = a_ref[pl.ds(off, SUBLANES), :]
        bv = b_ref[pl.ds(off, SUBLANES), :]
        for sh in (1, 2, 4):
            a_sh = jnp.where(srow >= sh, pltpu.roll(av, sh, axis=0), 1.0)
            b_sh = jnp.where(srow >= sh, pltpu.roll(bv, sh, axis=0), 0.0)
            bv = bv + av * b_sh
            av = av * a_sh
        hv = bv + av * hprev
        hs_ref[pl.ds(off, SUBLANES), :] = hv
        return hv[SUBLANES - 1:SUBLANES, :]

    hcar_ref[...] = lax.fori_loop(0, ts // SUBLANES, group, hcar_ref[...])

    out = x + jnp.dot((y * hs_ref[...]).astype(BF16), wout_ref[...], preferred_element_type=F32)
    o_ref[...] = out
    xbuf_ref[0:SUBLANES, :] = xbuf_ref[ts:ts + SUBLANES, :]

    h2 = _rms(out, g2_ref[...])
    _store_row_tiles(h2_ref, 0, h2)
    logits = jnp.dot(h2.astype(BF16), wr_ref[...].astype(BF16), preferred_element_type=F32)
    lane = lax.broadcasted_iota(jnp.int32, (ts, ROUTE_LANES), 1)
    lg = jnp.where(lane < N_EXPERTS, logits, -jnp.inf)
    m1 = jnp.max(lg, axis=-1, keepdims=True)
    i1 = jnp.min(jnp.where(lg == m1, lane, ROUTE_LANES), axis=-1, keepdims=True)
    lg2 = jnp.where(lane == i1, -jnp.inf, lg)
    m2 = jnp.max(lg2, axis=-1, keepdims=True)
    i2 = jnp.min(jnp.where(lg2 == m2, lane, ROUTE_LANES), axis=-1, keepdims=True)
    e2 = jnp.exp(m2 - m1)
    gate1 = 1.0 / (1.0 + e2)
    gate2 = e2 / (1.0 + e2)

    first = lane == i1
    second = lane == i2
    ind = jnp.where(first | second, 1.0, 0.0)
    cums = jnp.dot(tri_ref[...], ind.astype(BF16), preferred_element_type=F32)
    rank = cnt_ref[...] + cums - ind
    cnt_ref[...] = cnt_ref[...] + cums[ts - 1:ts, :]
    rank1 = jnp.sum(jnp.where(first, rank, 0.0), axis=-1, keepdims=True)
    rank2 = jnp.sum(jnp.where(second, rank, 0.0), axis=-1, keepdims=True)
    cnt_out_ref[...] = jnp.broadcast_to(cnt_ref[...], (SUBLANES, ROUTE_LANES)).astype(jnp.int32)

    record = jnp.where(lane == ROUTE_E0, i1.astype(F32), 0.0)
    record = jnp.where(lane == ROUTE_E0 + 1, i2.astype(F32), record)
    record = jnp.where(lane == ROUTE_GATE0, gate1, record)
    record = jnp.where(lane == ROUTE_GATE0 + 1, gate2, record)
    record = jnp.where(lane == ROUTE_RANK0, rank1, record)
    record = jnp.where(lane == ROUTE_RANK0 + 1, rank2, record)
    route_ref[...] = record
    rt_ref[...] = record.T[0:SUBLANES, :].astype(jnp.int32)


def _odd_mixer(x, g, w_in, conv_w, conv_b, ga_w, ga_b, gx_w, gx_b, lam, w_out, g2, w_router):
    b, s, d = x.shape
    ts = SEQ_TILE
    w = conv_w.shape[1]
    hd = w // LRU_HEADS
    gw = jnp.concatenate([ga_w, gx_w], axis=-1).astype(BF16)
    wr = jnp.pad(w_router, ((0, 0), (0, ROUTE_LANES - N_EXPERTS)))
    tri = jnp.tril(jnp.ones((ts, ts), BF16))
    nst = s // ts
    const = lambda bi, si: (0, 0)
    row = lambda a: a.reshape(1, -1)
    tile = lambda bi, si: (bi, si, 0)
    return pl.pallas_call(
        _odd_mixer_kernel,
        out_shape=(jax.ShapeDtypeStruct((b, s, d), F32),
                   jax.ShapeDtypeStruct((b * s * SUBLANES, LANES), F32),
                   jax.ShapeDtypeStruct((b, s, ROUTE_LANES), F32),
                   jax.ShapeDtypeStruct((b * nst, SUBLANES, ts), jnp.int32),
                   jax.ShapeDtypeStruct((SUBLANES, ROUTE_LANES), jnp.int32)),
        grid=(b, nst),
        in_specs=[
            pl.BlockSpec((None, ts, d), lambda bi, si: (bi, si, 0)),
            _resident((1, d), const),
            _resident((d, 2 * w), const),
            _resident((CONV_K_C, w), const),
            _resident((1, w), const),
            _resident((LRU_HEADS, hd, 2 * hd), lambda bi, si: (0, 0, 0)),
            _resident((1, w), const),
            _resident((1, w), const),
            _resident((1, w), const),
            _resident((w, d), const),
            _resident((1, d), const),
            _resident((d, ROUTE_LANES), const),
            _resident((ts, ts), const),
        ],
        out_specs=(pl.BlockSpec((None, ts, d), tile),
                   pl.BlockSpec((ts * SUBLANES, LANES), lambda bi, si: (bi * nst + si, 0)),
                   pl.BlockSpec((None, ts, ROUTE_LANES), tile),
                   pl.BlockSpec((None, SUBLANES, ts), lambda bi, si: (bi * nst + si, 0, 0)),
                   pl.BlockSpec((SUBLANES, ROUTE_LANES), const)),
        scratch_shapes=[
            pltpu.VMEM((SUBLANES + ts, w), F32),
            pltpu.VMEM((ts, w), F32),
            pltpu.VMEM((ts, w), F32),
            pltpu.VMEM((ts, w), F32),
            pltpu.VMEM((1, w), F32),
            pltpu.VMEM((1, ROUTE_LANES), F32),
        ],
        compiler_params=pltpu.CompilerParams(
            dimension_semantics=("arbitrary", "arbitrary"), vmem_limit_bytes=VMEM_LIMIT_BYTES),
        name="odd_mixer",
    )(x, row(g), w_in.astype(BF16), conv_w, row(conv_b), gw, row(ga_b), row(gx_b), row(lam),
      w_out.astype(BF16), row(g2), wr, tri)


def _moe_tables(rt, counts, t):
    r = MOE_ROWS
    n_blocks = TOP_K * t // r + N_EXPERTS
    counts = counts[0, 0:N_EXPERTS]
    padded = (counts + r - 1) // r * r
    pad_end = jnp.cumsum(padded)
    pad_start = pad_end - padded
    e = rt[:, ROUTE_E0:ROUTE_E0 + TOP_K, :]
    rank = rt[:, ROUTE_RANK0:ROUTE_RANK0 + TOP_K, :]
    start = jnp.zeros_like(e)
    for ex in range(N_EXPERTS):
        start = jnp.where(e == ex, pad_start[ex], start)
    dest = start + rank
    blk = jnp.arange(n_blocks, dtype=jnp.int32)
    blk_expert = jnp.minimum(jnp.sum(blk[:, None] * r >= pad_end[None, :], axis=1), N_EXPERTS - 1)
    n_used = pad_end[N_EXPERTS - 1] // r
    last_blk = jnp.where(padded > 0, pad_end // r - 1, -1)
    has_pad = jnp.any(blk[:, None] == last_blk[None, :], axis=1) | (blk >= n_used)
    dest = dest.astype(jnp.int32).reshape(dest.shape[0], TOP_K * dest.shape[2])
    return dest, blk_expert.astype(jnp.int32), has_pad.astype(jnp.int32), n_used.astype(jnp.int32).reshape(1)


def _row_copy(src, src_row, dst, dst_row, sem):
    return pltpu.make_async_copy(src.at[pl.ds(src_row * SUBLANES, SUBLANES), :],
                                 dst.at[pl.ds(dst_row * SUBLANES, SUBLANES), :], sem)


def _rows_wait(src, dst, n_rows, sem):
    pltpu.make_async_copy(src.at[pl.ds(0, n_rows * SUBLANES), :],
                          dst.at[pl.ds(0, n_rows * SUBLANES), :], sem).wait()


def _dispatch_kernel(pad_ref, dest_hbm, h_ref, xg_hbm, dsm, stage, zbuf, sem_idx, sem_row, sem_zero):
    i = pl.program_id(0)
    n = pl.num_programs(0)
    tm = h_ref.shape[0] // SUBLANES
    r = zbuf.shape[0] // SUBLANES
    n_blocks = pad_ref.shape[0]
    slot = lax.rem(i, 2)

    def idx_copy(step, s):
        return pltpu.make_async_copy(dest_hbm.at[step], dsm.at[pl.ds(s * TOP_K * tm, TOP_K * tm)],
                                     sem_idx.at[s])

    def zero_copy(blk):
        return pltpu.make_async_copy(zbuf, xg_hbm.at[pl.ds(blk * r * SUBLANES, r * SUBLANES), :], sem_zero)

    @pl.when(i == 0)
    def _():
        idx_copy(0, 0).start()
        zbuf[...] = jnp.zeros(zbuf.shape, zbuf.dtype)

        def zero_start(blk, c):
            @pl.when(pad_ref[blk] == 1)
            def _():
                zero_copy(blk).start()
            return c

        def zero_wait(blk, c):
            @pl.when(pad_ref[blk] == 1)
            def _():
                zero_copy(blk).wait()
            return c

        lax.fori_loop(0, n_blocks, zero_start, 0)
        lax.fori_loop(0, n_blocks, zero_wait, 0)

    @pl.when(i + 1 < n)
    def _():
        idx_copy(i + 1, 1 - slot).start()

    idx_copy(i, slot).wait()

    stage_row0 = slot * tm
    stage[pl.ds(stage_row0 * SUBLANES, tm * SUBLANES), :] = h_ref[...]

    def issue(row, c):
        for k in range(TOP_K):
            _row_copy(stage, stage_row0 + row, xg_hbm, dsm[(slot * TOP_K + k) * tm + row],
                      sem_row.at[slot]).start(priority=k % DMA_THREADS)
        return c

    lax.fori_loop(0, tm, issue, 0, unroll=8)

    def wait_slot(s):
        for _ in range(TOP_K):
            _rows_wait(stage, xg_hbm, tm, sem_row.at[s])

    @pl.when(i > 0)
    def _():
        wait_slot(1 - slot)

    @pl.when(i == n - 1)
    def _():
        wait_slot(slot)


def _dispatch(h2, dest, has_pad):
    tm = dest.shape[1] // TOP_K
    t = h2.shape[0] // SUBLANES
    r = MOE_ROWS
    n_rows = TOP_K * t + N_EXPERTS * r
    grid_spec = pltpu.PrefetchScalarGridSpec(
        num_scalar_prefetch=1,
        grid=(t // tm,),
        in_specs=[pl.BlockSpec(memory_space=pl.ANY),
                  pl.BlockSpec((tm * SUBLANES, LANES), lambda i, pad: (i, 0))],
        out_specs=pl.BlockSpec(memory_space=pl.ANY),
        scratch_shapes=[
            pltpu.SMEM((2 * TOP_K * tm,), jnp.int32),
            pltpu.VMEM((2 * tm * SUBLANES, LANES), F32),
            pltpu.VMEM((r * SUBLANES, LANES), F32),
            pltpu.SemaphoreType.DMA((2,)),
            pltpu.SemaphoreType.DMA((2,)),
            pltpu.SemaphoreType.DMA,
        ],
    )
    return pl.pallas_call(
        _dispatch_kernel,
        out_shape=jax.ShapeDtypeStruct((n_rows * SUBLANES, LANES), F32),
        grid_spec=grid_spec,
        compiler_params=pltpu.CompilerParams(dimension_semantics=("arbitrary",)),
        name="moe_dispatch",
    )(has_pad, dest, h2)


def _expert_kernel(j, be_ref, nu_ref, x_ref, wg_hbm, wu_hbm, wd_hbm, y_ref,
                   wg_ref, wu_ref, wd_ref, stage_in, stage_out, sem):
    i = pl.program_id(0)
    r = x_ref.shape[0] // SUBLANES
    e = be_ref[i]
    new_expert = (i == 0) | (e != be_ref[jnp.maximum(i - 1, 0)])

    @pl.when((i < nu_ref[0]) & new_expert)
    def _():
        _load_cast([(wg_hbm.at[j, e], wg_ref, stage_in), (wu_hbm.at[j, e], wu_ref, stage_in),
                    (wd_hbm.at[j, e], wd_ref, stage_out)], sem)

    @pl.when(i < nu_ref[0])
    def _():
        xb = _load_row_tiles(x_ref, 0, r).astype(BF16)
        _store_row_tiles(y_ref, 0, _swiglu_rows(xb, wg_ref, wu_ref, wd_ref))

    @pl.when(i >= nu_ref[0])
    def _():
        y_ref[...] = jnp.zeros(y_ref.shape, y_ref.dtype)


def _experts(xg, blk_expert, n_used, j, w_gate, w_up, w_down):
    d, d_ff = w_gate.shape[2], w_gate.shape[3]
    r = MOE_ROWS
    rows = lambda i, be, nu: (i, 0)
    grid_spec = pltpu.PrefetchScalarGridSpec(
        num_scalar_prefetch=2,
        grid=(xg.shape[0] // (r * SUBLANES),),
        in_specs=[
            pl.BlockSpec((r * SUBLANES, LANES), rows),
            pl.BlockSpec(memory_space=pl.ANY),
            pl.BlockSpec(memory_space=pl.ANY),
            pl.BlockSpec(memory_space=pl.ANY),
        ],
        out_specs=pl.BlockSpec((r * SUBLANES, LANES), rows),
        scratch_shapes=_swiglu_weight_scratch(d, d_ff),
    )
    return pl.pallas_call(
        functools.partial(_expert_kernel, j),
        out_shape=jax.ShapeDtypeStruct(xg.shape, F32),
        grid_spec=grid_spec,
        compiler_params=pltpu.CompilerParams(
            dimension_semantics=("arbitrary",), vmem_limit_bytes=VMEM_LIMIT_BYTES),
        name="moe_experts",
    )(blk_expert, n_used, xg, w_gate, w_up, w_down)


def _combine_kernel(dest_hbm, y_hbm, x_ref, route_ref, o_ref, dsm, ybuf, sem_idx, sem_row):
    i = pl.program_id(0)
    n = pl.num_programs(0)
    tm = x_ref.shape[0]
    slot = lax.rem(i, 2)

    def fetch(step, s):
        idx = pltpu.make_async_copy(dest_hbm.at[step], dsm.at[pl.ds(s * TOP_K * tm, TOP_K * tm)],
                                    sem_idx.at[s])
        idx.start()
        idx.wait()

        def issue(row, c):
            for k in range(TOP_K):
                at = (s * TOP_K + k) * tm + row
                _row_copy(y_hbm, dsm[at], ybuf, at, sem_row.at[s]).start(priority=k % DMA_THREADS)
            return c

        lax.fori_loop(0, tm, issue, 0, unroll=8)

    @pl.when(i == 0)
    def _():
        fetch(0, 0)

    @pl.when(i + 1 < n)
    def _():
        fetch(i + 1, 1 - slot)

    for _ in range(TOP_K):
        _rows_wait(y_hbm, ybuf, tm, sem_row.at[slot])
    rt = route_ref[...]
    moe = None
    for k in range(TOP_K):
        yk = _load_row_tiles(ybuf, (slot * TOP_K + k) * tm * SUBLANES, tm)
        term = yk * rt[:, ROUTE_GATE0 + k:ROUTE_GATE0 + k + 1]
        moe = term if moe is None else moe + term
    o_ref[...] = x_ref[...] + moe


def _combine(x, y, route, dest):
    t, d = x.shape
    tm = dest.shape[1] // TOP_K
    return pl.pallas_call(
        _combine_kernel,
        out_shape=jax.ShapeDtypeStruct((t, d), F32),
        grid=(t // tm,),
        in_specs=[
            pl.BlockSpec(memory_space=pl.ANY),
            pl.BlockSpec(memory_space=pl.ANY),
            pl.BlockSpec((tm, d), lambda i: (i, 0)),
            pl.BlockSpec((tm, ROUTE_LANES), lambda i: (i, 0)),
        ],
        out_specs=pl.BlockSpec((tm, d), lambda i: (i, 0)),
        scratch_shapes=[
            pltpu.SMEM((2 * TOP_K * tm,), jnp.int32),
            pltpu.VMEM((2 * TOP_K * tm * SUBLANES, LANES), F32),
            pltpu.SemaphoreType.DMA((2,)),
            pltpu.SemaphoreType.DMA((2,)),
        ],
        compiler_params=pltpu.CompilerParams(
            dimension_semantics=("arbitrary",), vmem_limit_bytes=VMEM_LIMIT_BYTES),
        name="moe_combine",
    )(dest, y, x, route)


def kernel(x, norm_mix, norm_ffn, hy_w_in, hy_q_gain, hy_k_gain, hy_sinks, hy_conv_w, hy_w_out, rg_w_in, rg_conv_w, rg_conv_b, rg_gate_a_w, rg_gate_a_b, rg_gate_x_w, rg_gate_x_b, rg_lambda, rg_w_out, ffn_w_gate, ffn_w_up, ffn_w_down, moe_router, moe_w_gate, moe_w_up, moe_w_down):
    b, s, d = x.shape
    t = b * s
    depth = norm_mix.shape[0]
    for layer in range(depth):
        j = layer // 2
        if layer % 2 == 0:
            x = _even_mixer(x, norm_mix[layer], hy_w_in[j], hy_q_gain[j], hy_k_gain[j], hy_sinks[j],
                            hy_conv_w[j], hy_w_out[j])
            x = _ffn(x.reshape(t, d), norm_ffn[layer], j, ffn_w_gate, ffn_w_up, ffn_w_down)
            x = x.reshape(b, s, d)
        else:
            x, h2, route, rt, counts = _odd_mixer(
                x, norm_mix[layer], rg_w_in[j], rg_conv_w[j], rg_conv_b[j], rg_gate_a_w[j], rg_gate_a_b[j],
                rg_gate_x_w[j], rg_gate_x_b[j], rg_lambda[j], rg_w_out[j], norm_ffn[layer], moe_router[j])
            dest, blk_expert, has_pad, n_used = _moe_tables(rt, counts, t)
            xg = _dispatch(h2, dest, has_pad)
            y = _experts(xg, blk_expert, n_used, j, moe_w_gate, moe_w_up, moe_w_down)
            x = _combine(x.reshape(t, d), y, route.reshape(t, ROUTE_LANES), dest).reshape(b, s, d)
    return x
```

```python
import functools

import jax
import jax.numpy as jnp
from jax import lax
from jax.experimental import pallas as pl
from jax.experimental.pallas import tpu as pltpu

F32 = jnp.float32
BF16 = jnp.bfloat16

LANES = 128
SUBLANES = 8
VMEM_LIMIT_BYTES = 56 * 1024 * 1024
DMA_THREADS = 2

N_Q_HEADS = 8
N_KV_HEADS = 2
HEAD_DIM = 64
Q_GROUP = N_Q_HEADS // N_KV_HEADS
WINDOW = 128
ATTN_BLOCK = 128
ALIBI_MAX = 8.0
CONV_K_B = 3
CONV_K_C = 4
LRU_HEADS = 8
LRU_C = 8.0
N_EXPERTS = 8
TOP_K = 2
NORM_EPS = 1e-6
NEG_INF = -1e30

SEQ_TILE = 512
FFN_TILE = 512
MOE_ROWS = 256
FF_CHUNK = 1792
WEIGHT_CHUNKS = 8
ROUTE_ROWS = 16
ROUTE_LANES = LANES
ROUTE_E0 = 0
ROUTE_GATE0 = 2
ROUTE_RANK0 = 4
GELU_C0 = 0.7978845608028654
GELU_C1 = 0.044715


def _rms(x, g):
    return x * lax.rsqrt(jnp.mean(x * x, axis=-1, keepdims=True) + NORM_EPS) * g


def _split_bf16(x):
    hi = x.astype(BF16)
    lo = (x - hi.astype(F32)).astype(BF16)
    return hi, lo


def _store_row_tiles(ref, start, x):
    n = x.shape[0]
    for c in range(x.shape[1] // LANES):
        ref[pl.ds(start + c, n, stride=SUBLANES), :] = x[:, c * LANES:(c + 1) * LANES]


def _load_row_tiles(ref, start, n):
    return jnp.concatenate([ref[pl.ds(start + c, n, stride=SUBLANES), :] for c in range(SUBLANES)], axis=1)


def _resident(shape, index_map):
    return pl.BlockSpec(shape, index_map, pipeline_mode=pl.Buffered(1))


def _even_mixer_kernel(sinks_ref, x_ref, g_ref, win_ref, qg_ref, kg_ref, cw_ref, wout_ref, o_ref,
                       kd_ref, vt_ref, cbuf_ref, attn_t_ref):
    ts = x_ref.shape[0]
    q_dim = N_Q_HEADS * HEAD_DIM
    kv_dim = N_KV_HEADS * HEAD_DIM
    cw_dim = cw_ref.shape[1]
    st = pl.program_id(1)

    @pl.when(st == 0)
    def _():
        kd_ref[:, 0:ATTN_BLOCK, :] = jnp.zeros((N_KV_HEADS, ATTN_BLOCK, LANES), BF16)
        vt_ref[:, 0:ATTN_BLOCK] = jnp.zeros((kv_dim, ATTN_BLOCK), BF16)
        cbuf_ref[0:SUBLANES, :] = jnp.zeros((SUBLANES, cw_dim), F32)

    x = x_ref[...]
    h = _rms(x, g_ref[...]).astype(BF16)
    proj = jnp.dot(h, win_ref[...], preferred_element_type=F32)
    q = proj[:, 0:q_dim]
    k = proj[:, q_dim:q_dim + kv_dim]
    v = proj[:, q_dim + kv_dim:q_dim + 2 * kv_dim]
    o0 = q_dim + 2 * kv_dim
    gate_b = proj[:, o0:o0 + cw_dim]
    gate_c = proj[:, o0 + cw_dim:o0 + 2 * cw_dim]
    u = proj[:, o0 + 2 * cw_dim:o0 + 3 * cw_dim]

    ri = lax.broadcasted_iota(jnp.int32, (LANES, LANES), 0)
    ci = lax.broadcasted_iota(jnp.int32, (LANES, LANES), 1)
    half_mean = jnp.where((ri < HEAD_DIM) == (ci < HEAD_DIM), 1.0 / HEAD_DIM, 0.0).astype(BF16)

    def half_mean_square(z):
        hi, lo = _split_bf16(z * z)
        cols = []
        for c in range(z.shape[1] // LANES):
            sl = slice(c * LANES, (c + 1) * LANES)
            cols.append(jnp.dot(hi[:, sl], half_mean, preferred_element_type=F32)
                        + jnp.dot(lo[:, sl], half_mean, preferred_element_type=F32))
        return cols[0] if len(cols) == 1 else jnp.concatenate(cols, axis=1)

    qn = q * lax.rsqrt(half_mean_square(q) + NORM_EPS) * qg_ref[...] * (HEAD_DIM ** -0.5)
    kn = k * lax.rsqrt(half_mean_square(k) + NORM_EPS) * kg_ref[...]

    low_half_t = lax.broadcasted_iota(jnp.int32, (ts, LANES), 1) < HEAD_DIM
    k_sw = pltpu.roll(kn, HEAD_DIM, axis=1)
    kd_ref[0, ATTN_BLOCK:ATTN_BLOCK + ts, :] = jnp.where(low_half_t, kn, k_sw).astype(BF16)
    kd_ref[1, ATTN_BLOCK:ATTN_BLOCK + ts, :] = jnp.where(low_half_t, k_sw, kn).astype(BF16)
    vt_ref[:, ATTN_BLOCK:ATTN_BLOCK + ts] = v.T.astype(BF16)

    low_half = lax.broadcasted_iota(jnp.int32, (ATTN_BLOCK, LANES), 1) < HEAD_DIM
    sj = lax.broadcasted_iota(jnp.int32, (2 * ATTN_BLOCK, ATTN_BLOCK), 0)
    qi = lax.broadcasted_iota(jnp.int32, (2 * ATTN_BLOCK, ATTN_BLOCK), 1)
    diff = ATTN_BLOCK + qi - sj
    in_window = (diff >= 0) & (diff < WINDOW)
    diff_f = diff.astype(F32)
    first_key = jnp.where(st == 0, ATTN_BLOCK, 0)
    in_window_first = in_window & (sj >= first_key)

    def penalty(valid, g):
        cols = []
        for hh in range(Q_GROUP):
            slope = 2.0 ** (-ALIBI_MAX * (g * Q_GROUP + hh + 1) / N_Q_HEADS)
            cols.append(jnp.where(valid, slope * diff_f, -NEG_INF))
        return jnp.concatenate(cols, axis=1)

    pen = [penalty(in_window, g) for g in range(N_KV_HEADS)]
    pen_first = [penalty(in_window_first, g) for g in range(N_KV_HEADS)]
    sink = [jnp.concatenate([jnp.full((1, ATTN_BLOCK), sinks_ref[0, g * Q_GROUP + hh], F32)
                             for hh in range(Q_GROUP)], axis=1) for g in range(N_KV_HEADS)]

    for j in range(ts // ATTN_BLOCK):
        rows = slice(j * ATTN_BLOCK, (j + 1) * ATTN_BLOCK)
        band = slice(j * ATTN_BLOCK, (j + 2) * ATTN_BLOCK)
        for g in range(N_KV_HEADS):
            q_rows = []
            for p in range(Q_GROUP // 2):
                c = g * (Q_GROUP // 2) + p
                qp = qn[rows, c * LANES:(c + 1) * LANES]
                q_rows.append(jnp.where(low_half, qp, 0.0))
                q_rows.append(jnp.where(low_half, 0.0, qp))
            qs = jnp.concatenate(q_rows, axis=0).astype(BF16)
            s = lax.dot_general(kd_ref[g, band, :], qs, (((1,), (1,)), ((), ())),
                                preferred_element_type=F32)
            sh = s - (pen_first[g] if j == 0 else pen[g])
            m = jnp.maximum(jnp.max(sh, axis=0, keepdims=True), sink[g])
            pr = jnp.exp(sh - m)
            denom = jnp.sum(pr, axis=0, keepdims=True) + jnp.exp(sink[g] - m)
            o = jnp.dot(vt_ref[g * HEAD_DIM:(g + 1) * HEAD_DIM, band], pr.astype(BF16),
                        preferred_element_type=F32)
            o = o / denom
            for hh in range(Q_GROUP):
                head = g * Q_GROUP + hh
                attn_t_ref[head * HEAD_DIM:(head + 1) * HEAD_DIM, rows] = (
                    o[:, hh * ATTN_BLOCK:(hh + 1) * ATTN_BLOCK])

    cu = gate_c * u
    cbuf_ref[SUBLANES:SUBLANES + ts, :] = cu
    conv = cw_ref[CONV_K_B - 1:CONV_K_B, :] * cu
    for kk in range(CONV_K_B - 1):
        back = CONV_K_B - 1 - kk
        conv = conv + cw_ref[kk:kk + 1, :] * cbuf_ref[SUBLANES - back:SUBLANES - back + ts, :]
    conv = gate_b * conv

    mix = jnp.concatenate([attn_t_ref[...].T, conv], axis=1).astype(BF16)
    o_ref[...] = x + jnp.dot(mix, wout_ref[...], preferred_element_type=F32)

    kd_ref[:, 0:ATTN_BLOCK, :] = kd_ref[:, ts:ts + ATTN_BLOCK, :]
    vt_ref[:, 0:ATTN_BLOCK] = vt_ref[:, ts:ts + ATTN_BLOCK]
    cbuf_ref[0:SUBLANES, :] = cbuf_ref[ts:ts + SUBLANES, :]


def _even_mixer(x, g, w_in, q_gain, k_gain, sinks, conv_w, w_out):
    b, s, d = x.shape
    ts = SEQ_TILE
    n_in = w_in.shape[1]
    cw_dim = conv_w.shape[1]
    mix_dim = w_out.shape[0]
    qg = jnp.tile(q_gain, N_Q_HEADS).reshape(1, N_Q_HEADS * HEAD_DIM)
    kg = jnp.tile(k_gain, N_KV_HEADS).reshape(1, N_KV_HEADS * HEAD_DIM)
    const = lambda bi, si: (0, 0)
    return pl.pallas_call(
        _even_mixer_kernel,
        out_shape=jax.ShapeDtypeStruct((b, s, d), F32),
        grid=(b, s // ts),
        in_specs=[
            pl.BlockSpec(memory_space=pltpu.SMEM),
            pl.BlockSpec((None, ts, d), lambda bi, si: (bi, si, 0)),
            _resident((1, d), const),
            _resident((d, n_in), const),
            _resident((1, N_Q_HEADS * HEAD_DIM), const),
            _resident((1, N_KV_HEADS * HEAD_DIM), const),
            _resident((CONV_K_B, cw_dim), const),
            _resident((mix_dim, d), const),
        ],
        out_specs=pl.BlockSpec((None, ts, d), lambda bi, si: (bi, si, 0)),
        scratch_shapes=[
            pltpu.VMEM((N_KV_HEADS, ATTN_BLOCK + ts, LANES), BF16),
            pltpu.VMEM((N_KV_HEADS * HEAD_DIM, ATTN_BLOCK + ts), BF16),
            pltpu.VMEM((SUBLANES + ts, cw_dim), F32),
            pltpu.VMEM((N_Q_HEADS * HEAD_DIM, ts), F32),
        ],
        compiler_params=pltpu.CompilerParams(
            dimension_semantics=("arbitrary", "arbitrary"), vmem_limit_bytes=VMEM_LIMIT_BYTES),
        name="even_mixer",
    )(sinks.reshape(1, N_Q_HEADS), x, g.reshape(1, d), w_in.astype(BF16), qg, kg, conv_w,
      w_out.astype(BF16))


def _swiglu_rows(hb, wg_ref, wu_ref, wd_ref):
    d_ff = wg_ref.shape[1]
    acc = None
    for c in range(d_ff // FF_CHUNK):
        cols = slice(c * FF_CHUNK, (c + 1) * FF_CHUNK)
        gate = jnp.dot(hb, wg_ref[:, cols], preferred_element_type=F32)
        up = jnp.dot(hb, wu_ref[:, cols], preferred_element_type=F32)
        act = (gate * (1.0 / (1.0 + jnp.exp(-gate))) * up).astype(BF16)
        part = jnp.dot(act, wd_ref[cols, :], preferred_element_type=F32)
        acc = part if acc is None else acc + part
    return acc


def _load_cast(jobs, sem):
    chunks = []
    for src, dst, stage in jobs:
        chunk_rows = stage.shape[1]
        for c in range(dst.shape[0] // chunk_rows):
            chunks.append((src, dst, stage, pl.ds(c * chunk_rows, chunk_rows)))

    def copy(n):
        src, _, stage, rows = chunks[n]
        return pltpu.make_async_copy(src.at[rows, :], stage.at[n % 2], sem.at[n % 2])

    copy(0).start()
    for n, (_, dst, stage, rows) in enumerate(chunks):
        if n + 1 < len(chunks):
            copy(n + 1).start()
        copy(n).wait()
        dst[rows, :] = stage[n % 2].astype(BF16)


def _swiglu_weight_scratch(d, d_ff):
    in_rows = d // WEIGHT_CHUNKS
    out_rows = d_ff // WEIGHT_CHUNKS
    return [
        pltpu.VMEM((d, d_ff), BF16),
        pltpu.VMEM((d, d_ff), BF16),
        pltpu.VMEM((d_ff, d), BF16),
        pltpu.VMEM((2, in_rows, d_ff), F32),
        pltpu.VMEM((2, out_rows, d), F32),
        pltpu.SemaphoreType.DMA((2,)),
    ]


def _ffn_kernel(j, x_ref, g_ref, wg_hbm, wu_hbm, wd_hbm, o_ref,
                wg_ref, wu_ref, wd_ref, stage_in, stage_out, sem):
    @pl.when(pl.program_id(0) == 0)
    def _():
        _load_cast([(wg_hbm.at[j], wg_ref, stage_in), (wu_hbm.at[j], wu_ref, stage_in),
                    (wd_hbm.at[j], wd_ref, stage_out)], sem)

    x = x_ref[...]
    hb = _rms(x, g_ref[...]).astype(BF16)
    o_ref[...] = x + _swiglu_rows(hb, wg_ref, wu_ref, wd_ref)


def _ffn(x, g, j, w_gate, w_up, w_down):
    t, d = x.shape
    d_ff = w_gate.shape[2]
    tm = FFN_TILE
    return pl.pallas_call(
        functools.partial(_ffn_kernel, j),
        out_shape=jax.ShapeDtypeStruct((t, d), F32),
        grid=(t // tm,),
        in_specs=[
            pl.BlockSpec((tm, d), lambda i: (i, 0)),
            _resident((1, d), lambda i: (0, 0)),
            pl.BlockSpec(memory_space=pl.ANY),
            pl.BlockSpec(memory_space=pl.ANY),
            pl.BlockSpec(memory_space=pl.ANY),
        ],
        out_specs=pl.BlockSpec((tm, d), lambda i: (i, 0)),
        scratch_shapes=_swiglu_weight_scratch(d, d_ff),
        compiler_params=pltpu.CompilerParams(
            dimension_semantics=("arbitrary",), vmem_limit_bytes=VMEM_LIMIT_BYTES),
        name="dense_ffn",
    )(x, g.reshape(1, d), w_gate, w_up, w_down)


def _odd_mixer_kernel(x_ref, g_ref, win_ref, cw_ref, cb_ref, gw_ref, gab_ref, gxb_ref, lam_ref, wout_ref,
                      g2_ref, wr_ref, tri_ref, o_ref, h2_ref, route_ref, rt_ref, cnt_out_ref,
                      xbuf_ref, a_ref, b_ref, hs_ref, hcar_ref, cnt_ref):
    ts = x_ref.shape[0]
    w = cw_ref.shape[1]
    hd = w // LRU_HEADS
    st = pl.program_id(1)

    @pl.when(st == 0)
    def _():
        xbuf_ref[0:SUBLANES, :] = jnp.zeros((SUBLANES, w), F32)
        hcar_ref[...] = jnp.zeros((1, w), F32)

    @pl.when((st == 0) & (pl.program_id(0) == 0))
    def _():
        cnt_ref[...] = jnp.zeros(cnt_ref.shape, F32)

    x = x_ref[...]
    h = _rms(x, g_ref[...]).astype(BF16)
    proj = jnp.dot(h, win_ref[...], preferred_element_type=F32)
    y = proj[:, 0:w]
    y = y * (0.5 * (1.0 + jnp.tanh(GELU_C0 * (y + GELU_C1 * (y * y * y)))))
    xb = proj[:, w:2 * w]

    xbuf_ref[SUBLANES:SUBLANES + ts, :] = xb
    xc = cw_ref[CONV_K_C - 1:CONV_K_C, :] * xb + cb_ref[...]
    for kk in range(CONV_K_C - 1):
        back = CONV_K_C - 1 - kk
        xc = xc + cw_ref[kk:kk + 1, :] * xbuf_ref[SUBLANES - back:SUBLANES - back + ts, :]

    r_cols = []
    i_cols = []
    for hh in range(LRU_HEADS):
        ri = jnp.dot(xc[:, hh * hd:(hh + 1) * hd].astype(BF16), gw_ref[hh], preferred_element_type=F32)
        r_cols.append(ri[:, 0:hd])
        i_cols.append(ri[:, hd:2 * hd])
    r_pre = jnp.concatenate(r_cols, axis=1) + gab_ref[...]
    i_pre = jnp.concatenate(i_cols, axis=1) + gxb_ref[...]
    r = 0.5 * (1.0 + jnp.tanh(0.5 * r_pre))
    ig = 0.5 * (1.0 + jnp.tanh(0.5 * i_pre))
    neg_lam = -lam_ref[...]
    softplus = jnp.maximum(neg_lam, 0.0) + jnp.log1p(jnp.exp(-jnp.abs(neg_lam)))
    log_a = -LRU_C * r * softplus
    a = jnp.exp(log_a)
    mult = jnp.sqrt(jnp.maximum(1.0 - a * a, 0.0))
    row = lax.broadcasted_iota(jnp.int32, (ts, w), 0)
    mult = jnp.where((row == 0) & (st == 0), 1.0, mult)
    a_ref[...] = a
    b_ref[...] = mult * (ig * xc)

    srow = lax.broadcasted_iota(jnp.int32, (SUBLANES, w), 0)

    def group(i, hprev):
        off = pl.multiple_of(i * SUBLANES, SUBLANES)
        av = a_ref[pl.ds(off, SUBLANES), :]
        bv = b_ref[pl.ds(off, SUBLANES), :]
        for sh in (1, 2, 4):
            a_sh = jnp.where(srow >= sh, pltpu.roll(av, sh, axis=0), 1.0)
            b_sh = jnp.where(srow >= sh, pltpu.roll(bv, sh, axis=0), 0.0)
            bv = bv + av * b_sh
            av = av * a_sh
        hv = bv + av * hprev
        hs_ref[pl.ds(off, SUBLANES), :] = hv
        return hv[SUBLANES - 1:SUBLANES, :]

    hcar_ref[...] = lax.fori_loop(0, ts // SUBLANES, group, hcar_ref[...])

    out = x + jnp.dot((y * hs_ref[...]).astype(BF16), wout_ref[...], preferred_element_type=F32)
    o_ref[...] = out
    xbuf_ref[0:SUBLANES, :] = xbuf_ref[ts:ts + SUBLANES, :]

    h2 = _rms(out, g2_ref[...])
    _store_row_tiles(h2_ref, 0, h2)
    logits = lax.dot_general(wr_ref[...], h2.astype(BF16), (((1,), (1,)), ((), ())),
                             preferred_element_type=F32)
    erow = lax.broadcasted_iota(jnp.int32, (ROUTE_ROWS, ts), 0)
    lg = jnp.where(erow < N_EXPERTS, logits, -jnp.inf)
    m1 = jnp.max(lg, axis=0, keepdims=True)
    i1 = jnp.min(jnp.where(lg == m1, erow, ROUTE_ROWS), axis=0, keepdims=True)
    lg2 = jnp.where(erow == i1, -jnp.inf, lg)
    m2 = jnp.max(lg2, axis=0, keepdims=True)
    i2 = jnp.min(jnp.where(lg2 == m2, erow, ROUTE_ROWS), axis=0, keepdims=True)
    e2 = jnp.exp(m2 - m1)
    gate1 = 1.0 / (1.0 + e2)
    gate2 = e2 / (1.0 + e2)

    first = erow == i1
    second = erow == i2
    ind = jnp.where(first | second, 1.0, 0.0)
    cums = jnp.dot(ind.astype(BF16), tri_ref[...], preferred_element_type=F32)
    rank = cnt_ref[...] + cums - ind
    cnt_ref[...] = cnt_ref[...] + jnp.broadcast_to(cums[:, ts - 1:ts], cums.shape)
    rank1 = jnp.sum(jnp.where(first, rank, 0.0), axis=0, keepdims=True)
    rank2 = jnp.sum(jnp.where(second, rank, 0.0), axis=0, keepdims=True)
    cnt_out_ref[...] = cnt_ref[:, 0:ROUTE_LANES].astype(jnp.int32)

    fields = [None] * SUBLANES
    fields[ROUTE_E0], fields[ROUTE_E0 + 1] = i1.astype(F32), i2.astype(F32)
    fields[ROUTE_GATE0], fields[ROUTE_GATE0 + 1] = gate1, gate2
    fields[ROUTE_RANK0], fields[ROUTE_RANK0 + 1] = rank1, rank2
    zero_row = jnp.zeros((1, ts), F32)
    record_t = jnp.concatenate([zero_row if f is None else f for f in fields], axis=0)
    rt_ref[...] = record_t.astype(jnp.int32)
    route_ref[...] = jnp.concatenate(
        [record_t, jnp.zeros((ROUTE_LANES - SUBLANES, ts), F32)], axis=0).T


def _odd_mixer(x, g, w_in, conv_w, conv_b, ga_w, ga_b, gx_w, gx_b, lam, w_out, g2, w_router):
    b, s, d = x.shape
    ts = SEQ_TILE
    w = conv_w.shape[1]
    hd = w // LRU_HEADS
    gw = jnp.concatenate([ga_w, gx_w], axis=-1).astype(BF16)
    wr = jnp.pad(w_router.T, ((0, ROUTE_ROWS - N_EXPERTS), (0, 0))).astype(BF16)
    tri = jnp.triu(jnp.ones((ts, ts), BF16))
    nst = s // ts
    const = lambda bi, si: (0, 0)
    row = lambda a: a.reshape(1, -1)
    tile = lambda bi, si: (bi, si, 0)
    return pl.pallas_call(
        _odd_mixer_kernel,
        out_shape=(jax.ShapeDtypeStruct((b, s, d), F32),
                   jax.ShapeDtypeStruct((b * s * SUBLANES, LANES), F32),
                   jax.ShapeDtypeStruct((b, s, ROUTE_LANES), F32),
                   jax.ShapeDtypeStruct((b * nst, SUBLANES, ts), jnp.int32),
                   jax.ShapeDtypeStruct((ROUTE_ROWS, ROUTE_LANES), jnp.int32)),
        grid=(b, nst),
        in_specs=[
            pl.BlockSpec((None, ts, d), lambda bi, si: (bi, si, 0)),
            _resident((1, d), const),
            _resident((d, 2 * w), const),
            _resident((CONV_K_C, w), const),
            _resident((1, w), const),
            _resident((LRU_HEADS, hd, 2 * hd), lambda bi, si: (0, 0, 0)),
            _resident((1, w), const),
            _resident((1, w), const),
            _resident((1, w), const),
            _resident((w, d), const),
            _resident((1, d), const),
            _resident((ROUTE_ROWS, d), const),
            _resident((ts, ts), const),
        ],
        out_specs=(pl.BlockSpec((None, ts, d), tile),
                   pl.BlockSpec((ts * SUBLANES, LANES), lambda bi, si: (bi * nst + si, 0)),
                   pl.BlockSpec((None, ts, ROUTE_LANES), tile),
                   pl.BlockSpec((None, SUBLANES, ts), lambda bi, si: (bi * nst + si, 0, 0)),
                   pl.BlockSpec((ROUTE_ROWS, ROUTE_LANES), const)),
        scratch_shapes=[
            pltpu.VMEM((SUBLANES + ts, w), F32),
            pltpu.VMEM((ts, w), F32),
            pltpu.VMEM((ts, w), F32),
            pltpu.VMEM((ts, w), F32),
            pltpu.VMEM((1, w), F32),
            pltpu.VMEM((ROUTE_ROWS, ts), F32),
        ],
        compiler_params=pltpu.CompilerParams(
            dimension_semantics=("arbitrary", "arbitrary"), vmem_limit_bytes=VMEM_LIMIT_BYTES),
        name="odd_mixer",
    )(x, row(g), w_in.astype(BF16), conv_w, row(conv_b), gw, row(ga_b), row(gx_b), row(lam),
      w_out.astype(BF16), row(g2), wr, tri)


def _moe_tables(rt, counts, t):
    r = MOE_ROWS
    n_blocks = TOP_K * t // r + N_EXPERTS
    counts = counts[0:N_EXPERTS, 0]
    padded = (counts + r - 1) // r * r
    pad_end = jnp.cumsum(padded)
    pad_start = pad_end - padded
    e = rt[:, ROUTE_E0:ROUTE_E0 + TOP_K, :]
    rank = rt[:, ROUTE_RANK0:ROUTE_RANK0 + TOP_K, :]
    start = jnp.zeros_like(e)
    for ex in range(N_EXPERTS):
        start = jnp.where(e == ex, pad_start[ex], start)
    dest = start + rank
    blk = jnp.arange(n_blocks, dtype=jnp.int32)
    blk_expert = jnp.minimum(jnp.sum(blk[:, None] * r >= pad_end[None, :], axis=1), N_EXPERTS - 1)
    n_used = pad_end[N_EXPERTS - 1] // r
    last_blk = jnp.where(padded > 0, pad_end // r - 1, -1)
    has_pad = jnp.any(blk[:, None] == last_blk[None, :], axis=1) | (blk >= n_used)
    dest = dest.astype(jnp.int32).reshape(dest.shape[0], TOP_K * dest.shape[2])
    return dest, blk_expert.astype(jnp.int32), has_pad.astype(jnp.int32), n_used.astype(jnp.int32).reshape(1)


def _row_copy(src, src_row, dst, dst_row, sem):
    return pltpu.make_async_copy(src.at[pl.ds(src_row * SUBLANES, SUBLANES), :],
                                 dst.at[pl.ds(dst_row * SUBLANES, SUBLANES), :], sem)


def _rows_wait(src, dst, n_rows, sem):
    pltpu.make_async_copy(src.at[pl.ds(0, n_rows * SUBLANES), :],
                          dst.at[pl.ds(0, n_rows * SUBLANES), :], sem).wait()


def _dispatch_kernel(pad_ref, dest_hbm, h_ref, xg_hbm, dsm, stage, zbuf, sem_idx, sem_row, sem_zero):
    i = pl.program_id(0)
    n = pl.num_programs(0)
    tm = h_ref.shape[0] // SUBLANES
    r = zbuf.shape[0] // SUBLANES
    n_blocks = pad_ref.shape[0]
    slot = lax.rem(i, 2)

    def idx_copy(step, s):
        return pltpu.make_async_copy(dest_hbm.at[step], dsm.at[pl.ds(s * TOP_K * tm, TOP_K * tm)],
                                     sem_idx.at[s])

    def zero_copy(blk):
        return pltpu.make_async_copy(zbuf, xg_hbm.at[pl.ds(blk * r * SUBLANES, r * SUBLANES), :], sem_zero)

    @pl.when(i == 0)
    def _():
        idx_copy(0, 0).start()
        zbuf[...] = jnp.zeros(zbuf.shape, zbuf.dtype)

        def zero_start(blk, c):
            @pl.when(pad_ref[blk] == 1)
            def _():
                zero_copy(blk).start()
            return c

        def zero_wait(blk, c):
            @pl.when(pad_ref[blk] == 1)
            def _():
                zero_copy(blk).wait()
            return c

        lax.fori_loop(0, n_blocks, zero_start, 0)
        lax.fori_loop(0, n_blocks, zero_wait, 0)

    @pl.when(i + 1 < n)
    def _():
        idx_copy(i + 1, 1 - slot).start()

    idx_copy(i, slot).wait()

    stage_row0 = slot * tm
    stage[pl.ds(stage_row0 * SUBLANES, tm * SUBLANES), :] = h_ref[...]

    def issue(row, c):
        for k in range(TOP_K):
            _row_copy(stage, stage_row0 + row, xg_hbm, dsm[(slot * TOP_K + k) * tm + row],
                      sem_row.at[slot]).start(priority=k % DMA_THREADS)
        return c

    lax.fori_loop(0, tm, issue, 0, unroll=8)

    def wait_slot(s):
        for _ in range(TOP_K):
            _rows_wait(stage, xg_hbm, tm, sem_row.at[s])

    @pl.when(i > 0)
    def _():
        wait_slot(1 - slot)

    @pl.when(i == n - 1)
    def _():
        wait_slot(slot)


def _dispatch(h2, dest, has_pad):
    tm = dest.shape[1] // TOP_K
    t = h2.shape[0] // SUBLANES
    r = MOE_ROWS
    n_rows = TOP_K * t + N_EXPERTS * r
    grid_spec = pltpu.PrefetchScalarGridSpec(
        num_scalar_prefetch=1,
        grid=(t // tm,),
        in_specs=[pl.BlockSpec(memory_space=pl.ANY),
                  pl.BlockSpec((tm * SUBLANES, LANES), lambda i, pad: (i, 0))],
        out_specs=pl.BlockSpec(memory_space=pl.ANY),
        scratch_shapes=[
            pltpu.SMEM((2 * TOP_K * tm,), jnp.int32),
            pltpu.VMEM((2 * tm * SUBLANES, LANES), F32),
            pltpu.VMEM((r * SUBLANES, LANES), F32),
            pltpu.SemaphoreType.DMA((2,)),
            pltpu.SemaphoreType.DMA((2,)),
            pltpu.SemaphoreType.DMA,
        ],
    )
    return pl.pallas_call(
        _dispatch_kernel,
        out_shape=jax.ShapeDtypeStruct((n_rows * SUBLANES, LANES), F32),
        grid_spec=grid_spec,
        compiler_params=pltpu.CompilerParams(dimension_semantics=("arbitrary",)),
        name="moe_dispatch",
    )(has_pad, dest, h2)


def _expert_kernel(j, be_ref, nu_ref, x_ref, wg_hbm, wu_hbm, wd_hbm, y_ref,
                   wg_ref, wu_ref, wd_ref, stage_in, stage_out, sem):
    i = pl.program_id(0)
    r = x_ref.shape[0] // SUBLANES
    e = be_ref[i]
    new_expert = (i == 0) | (e != be_ref[jnp.maximum(i - 1, 0)])

    @pl.when((i < nu_ref[0]) & new_expert)
    def _():
        _load_cast([(wg_hbm.at[j, e], wg_ref, stage_in), (wu_hbm.at[j, e], wu_ref, stage_in),
                    (wd_hbm.at[j, e], wd_ref, stage_out)], sem)

    @pl.when(i < nu_ref[0])
    def _():
        xb = _load_row_tiles(x_ref, 0, r).astype(BF16)
        _store_row_tiles(y_ref, 0, _swiglu_rows(xb, wg_ref, wu_ref, wd_ref))

    @pl.when(i >= nu_ref[0])
    def _():
        y_ref[...] = jnp.zeros(y_ref.shape, y_ref.dtype)


def _experts(xg, blk_expert, n_used, j, w_gate, w_up, w_down):
    d, d_ff = w_gate.shape[2], w_gate.shape[3]
    r = MOE_ROWS
    rows = lambda i, be, nu: (i, 0)
    grid_spec = pltpu.PrefetchScalarGridSpec(
        num_scalar_prefetch=2,
        grid=(xg.shape[0] // (r * SUBLANES),),
        in_specs=[
            pl.BlockSpec((r * SUBLANES, LANES), rows),
            pl.BlockSpec(memory_space=pl.ANY),
            pl.BlockSpec(memory_space=pl.ANY),
            pl.BlockSpec(memory_space=pl.ANY),
        ],
        out_specs=pl.BlockSpec((r * SUBLANES, LANES), rows),
        scratch_shapes=_swiglu_weight_scratch(d, d_ff),
    )
    return pl.pallas_call(
        functools.partial(_expert_kernel, j),
        out_shape=jax.ShapeDtypeStruct(xg.shape, F32),
        grid_spec=grid_spec,
        compiler_params=pltpu.CompilerParams(
            dimension_semantics=("arbitrary",), vmem_limit_bytes=VMEM_LIMIT_BYTES),
        name="moe_experts",
    )(blk_expert, n_used, xg, w_gate, w_up, w_down)


def _combine_kernel(dest_hbm, y_hbm, x_ref, route_ref, o_ref, dsm, ybuf, sem_idx, sem_row):
    i = pl.program_id(0)
    n = pl.num_programs(0)
    tm = x_ref.shape[0]
    slot = lax.rem(i, 2)

    def fetch(step, s):
        idx = pltpu.make_async_copy(dest_hbm.at[step], dsm.at[pl.ds(s * TOP_K * tm, TOP_K * tm)],
                                    sem_idx.at[s])
        idx.start()
        idx.wait()

        def issue(row, c):
            for k in range(TOP_K):
                at = (s * TOP_K + k) * tm + row
                _row_copy(y_hbm, dsm[at], ybuf, at, sem_row.at[s]).start(priority=k % DMA_THREADS)
            return c

        lax.fori_loop(0, tm, issue, 0, unroll=8)

    @pl.when(i == 0)
    def _():
        fetch(0, 0)

    @pl.when(i + 1 < n)
    def _():
        fetch(i + 1, 1 - slot)

    for _ in range(TOP_K):
        _rows_wait(y_hbm, ybuf, tm, sem_row.at[slot])
    rt = route_ref[...]
    moe = None
    for k in range(TOP_K):
        yk = _load_row_tiles(ybuf, (slot * TOP_K + k) * tm * SUBLANES, tm)
        term = yk * rt[:, ROUTE_GATE0 + k:ROUTE_GATE0 + k + 1]
        moe = term if moe is None else moe + term
    o_ref[...] = x_ref[...] + moe


def _combine(x, y, route, dest):
    t, d = x.shape
    tm = dest.shape[1] // TOP_K
    return pl.pallas_call(
        _combine_kernel,
        out_shape=jax.ShapeDtypeStruct((t, d), F32),
        grid=(t // tm,),
        in_specs=[
            pl.BlockSpec(memory_space=pl.ANY),
            pl.BlockSpec(memory_space=pl.ANY),
            pl.BlockSpec((tm, d), lambda i: (i, 0)),
            pl.BlockSpec((tm, ROUTE_LANES), lambda i: (i, 0)),
        ],
        out_specs=pl.BlockSpec((tm, d), lambda i: (i, 0)),
        scratch_shapes=[
            pltpu.SMEM((2 * TOP_K * tm,), jnp.int32),
            pltpu.VMEM((2 * TOP_K * tm * SUBLANES, LANES), F32),
            pltpu.SemaphoreType.DMA((2,)),
            pltpu.SemaphoreType.DMA((2,)),
        ],
        compiler_params=pltpu.CompilerParams(
            dimension_semantics=("arbitrary",), vmem_limit_bytes=VMEM_LIMIT_BYTES),
        name="moe_combine",
    )(dest, y, x, route)


def kernel(x, norm_mix, norm_ffn, hy_w_in, hy_q_gain, hy_k_gain, hy_sinks, hy_conv_w, hy_w_out, rg_w_in, rg_conv_w, rg_conv_b, rg_gate_a_w, rg_gate_a_b, rg_gate_x_w, rg_gate_x_b, rg_lambda, rg_w_out, ffn_w_gate, ffn_w_up, ffn_w_down, moe_router, moe_w_gate, moe_w_up, moe_w_down):
    b, s, d = x.shape
    t = b * s
    depth = norm_mix.shape[0]
    for layer in range(depth):
        j = layer // 2
        if layer % 2 == 0:
            x = _even_mixer(x, norm_mix[layer], hy_w_in[j], hy_q_gain[j], hy_k_gain[j], hy_sinks[j],
                            hy_conv_w[j], hy_w_out[j])
            x = _ffn(x.reshape(t, d), norm_ffn[layer], j, ffn_w_gate, ffn_w_up, ffn_w_down)
            x = x.reshape(b, s, d)
        else:
            x, h2, route, rt, counts = _odd_mixer(
                x, norm_mix[layer], rg_w_in[j], rg_conv_w[j], rg_conv_b[j], rg_gate_a_w[j], rg_gate_a_b[j],
                rg_gate_x_w[j], rg_gate_x_b[j], rg_lambda[j], rg_w_out[j], norm_ffn[layer], moe_router[j])
            dest, blk_expert, has_pad, n_used = _moe_tables(rt, counts, t)
            xg = _dispatch(h2, dest, has_pad)
            y = _experts(xg, blk_expert, n_used, j, moe_w_gate, moe_w_up, moe_w_down)
            x = _combine(x.reshape(t, d), y, route.reshape(t, ROUTE_LANES), dest).reshape(b, s, d)
    return x
```

```python
import functools

import jax
import jax.numpy as jnp
from jax import lax
from jax.experimental import pallas as pl
from jax.experimental.pallas import tpu as pltpu

F32 = jnp.float32
BF16 = jnp.bfloat16

LANES = 128
SUBLANES = 8
VMEM_LIMIT_BYTES = 56 * 1024 * 1024
DMA_THREADS = 2

N_Q_HEADS = 8
N_KV_HEADS = 2
HEAD_DIM = 64
Q_GROUP = N_Q_HEADS // N_KV_HEADS
WINDOW = 128
ATTN_BLOCK = 128
ALIBI_MAX = 8.0
CONV_K_B = 3
CONV_K_C = 4
LRU_HEADS = 8
LRU_C = 8.0
N_EXPERTS = 8
TOP_K = 2
NORM_EPS = 1e-6
NEG_INF = -1e30

SEQ_TILE = 512
FFN_TILE = 512
MOE_ROWS = 256
FF_CHUNK = 1792
WEIGHT_CHUNKS = 8
ROUTE_ROWS = 16
ROUTE_LANES = LANES
ROUTE_E0 = 0
ROUTE_GATE0 = 2
ROUTE_RANK0 = 4
GELU_C0 = 0.7978845608028654
GELU_C1 = 0.044715


def _rms(x, g):
    return x * lax.rsqrt(jnp.mean(x * x, axis=-1, keepdims=True) + NORM_EPS) * g


def _split_bf16(x):
    hi = x.astype(BF16)
    lo = (x - hi.astype(F32)).astype(BF16)
    return hi, lo


def _store_row_tiles(ref, start, x):
    n = x.shape[0]
    for c in range(x.shape[1] // LANES):
        ref[pl.ds(start + c, n, stride=SUBLANES), :] = x[:, c * LANES:(c + 1) * LANES]


def _load_row_tiles(ref, start, n):
    return jnp.concatenate([ref[pl.ds(start + c, n, stride=SUBLANES), :] for c in range(SUBLANES)], axis=1)


def _resident(shape, index_map):
    return pl.BlockSpec(shape, index_map, pipeline_mode=pl.Buffered(1))


def _even_mixer_kernel(sinks_ref, x_ref, g_ref, win_ref, qg_ref, kg_ref, cw_ref, wout_ref, o_ref,
                       kd_ref, vt_ref, cbuf_ref, attn_t_ref):
    ts = x_ref.shape[0]
    q_dim = N_Q_HEADS * HEAD_DIM
    kv_dim = N_KV_HEADS * HEAD_DIM
    cw_dim = cw_ref.shape[1]
    st = pl.program_id(1)

    @pl.when(st == 0)
    def _():
        kd_ref[:, 0:ATTN_BLOCK, :] = jnp.zeros((N_KV_HEADS, ATTN_BLOCK, LANES), BF16)
        vt_ref[:, 0:ATTN_BLOCK] = jnp.zeros((kv_dim, ATTN_BLOCK), BF16)
        cbuf_ref[0:SUBLANES, :] = jnp.zeros((SUBLANES, cw_dim), F32)

    x = x_ref[...]
    h = _rms(x, g_ref[...]).astype(BF16)
    proj = jnp.dot(h, win_ref[...], preferred_element_type=F32)
    q = proj[:, 0:q_dim]
    k = proj[:, q_dim:q_dim + kv_dim]
    v = proj[:, q_dim + kv_dim:q_dim + 2 * kv_dim]
    o0 = q_dim + 2 * kv_dim
    gate_b = proj[:, o0:o0 + cw_dim]
    gate_c = proj[:, o0 + cw_dim:o0 + 2 * cw_dim]
    u = proj[:, o0 + 2 * cw_dim:o0 + 3 * cw_dim]

    ri = lax.broadcasted_iota(jnp.int32, (LANES, LANES), 0)
    ci = lax.broadcasted_iota(jnp.int32, (LANES, LANES), 1)
    half_mean = jnp.where((ri < HEAD_DIM) == (ci < HEAD_DIM), 1.0 / HEAD_DIM, 0.0).astype(BF16)

    def half_mean_square(z):
        hi, lo = _split_bf16(z * z)
        cols = []
        for c in range(z.shape[1] // LANES):
            sl = slice(c * LANES, (c + 1) * LANES)
            cols.append(jnp.dot(hi[:, sl], half_mean, preferred_element_type=F32)
                        + jnp.dot(lo[:, sl], half_mean, preferred_element_type=F32))
        return cols[0] if len(cols) == 1 else jnp.concatenate(cols, axis=1)

    qn = q * lax.rsqrt(half_mean_square(q) + NORM_EPS) * qg_ref[...] * (HEAD_DIM ** -0.5)
    kn = k * lax.rsqrt(half_mean_square(k) + NORM_EPS) * kg_ref[...]

    low_half_t = lax.broadcasted_iota(jnp.int32, (ts, LANES), 1) < HEAD_DIM
    k_sw = pltpu.roll(kn, HEAD_DIM, axis=1)
    kd_ref[0, ATTN_BLOCK:ATTN_BLOCK + ts, :] = jnp.where(low_half_t, kn, k_sw).astype(BF16)
    kd_ref[1, ATTN_BLOCK:ATTN_BLOCK + ts, :] = jnp.where(low_half_t, k_sw, kn).astype(BF16)
    vt_ref[:, ATTN_BLOCK:ATTN_BLOCK + ts] = v.T.astype(BF16)

    low_half = lax.broadcasted_iota(jnp.int32, (ATTN_BLOCK, LANES), 1) < HEAD_DIM
    sj = lax.broadcasted_iota(jnp.int32, (2 * ATTN_BLOCK, ATTN_BLOCK), 0)
    qi = lax.broadcasted_iota(jnp.int32, (2 * ATTN_BLOCK, ATTN_BLOCK), 1)
    diff = ATTN_BLOCK + qi - sj
    in_window = (diff >= 0) & (diff < WINDOW)
    diff_f = diff.astype(F32)
    first_key = jnp.where(st == 0, ATTN_BLOCK, 0)
    in_window_first = in_window & (sj >= first_key)

    def penalty(valid, g):
        cols = []
        for hh in range(Q_GROUP):
            slope = 2.0 ** (-ALIBI_MAX * (g * Q_GROUP + hh + 1) / N_Q_HEADS)
            cols.append(jnp.where(valid, slope * diff_f, -NEG_INF))
        return jnp.concatenate(cols, axis=1)

    pen = [penalty(in_window, g) for g in range(N_KV_HEADS)]
    pen_first = [penalty(in_window_first, g) for g in range(N_KV_HEADS)]
    sink = [jnp.concatenate([jnp.full((1, ATTN_BLOCK), sinks_ref[0, g * Q_GROUP + hh], F32)
                             for hh in range(Q_GROUP)], axis=1) for g in range(N_KV_HEADS)]

    for j in range(ts // ATTN_BLOCK):
        rows = slice(j * ATTN_BLOCK, (j + 1) * ATTN_BLOCK)
        band = slice(j * ATTN_BLOCK, (j + 2) * ATTN_BLOCK)
        for g in range(N_KV_HEADS):
            q_rows = []
            for p in range(Q_GROUP // 2):
                c = g * (Q_GROUP // 2) + p
                qp = qn[rows, c * LANES:(c + 1) * LANES]
                q_rows.append(jnp.where(low_half, qp, 0.0))
                q_rows.append(jnp.where(low_half, 0.0, qp))
            qs = jnp.concatenate(q_rows, axis=0).astype(BF16)
            s = lax.dot_general(kd_ref[g, band, :], qs, (((1,), (1,)), ((), ())),
                                preferred_element_type=F32)
            sh = s - (pen_first[g] if j == 0 else pen[g])
            m = jnp.maximum(jnp.max(sh, axis=0, keepdims=True), sink[g])
            pr = jnp.exp(sh - m)
            denom = jnp.sum(pr, axis=0, keepdims=True) + jnp.exp(sink[g] - m)
            o = jnp.dot(vt_ref[g * HEAD_DIM:(g + 1) * HEAD_DIM, band], pr.astype(BF16),
                        preferred_element_type=F32)
            o = o / denom
            for hh in range(Q_GROUP):
                head = g * Q_GROUP + hh
                attn_t_ref[head * HEAD_DIM:(head + 1) * HEAD_DIM, rows] = (
                    o[:, hh * ATTN_BLOCK:(hh + 1) * ATTN_BLOCK])

    cu = gate_c * u
    cbuf_ref[SUBLANES:SUBLANES + ts, :] = cu
    conv = cw_ref[CONV_K_B - 1:CONV_K_B, :] * cu
    for kk in range(CONV_K_B - 1):
        back = CONV_K_B - 1 - kk
        conv = conv + cw_ref[kk:kk + 1, :] * cbuf_ref[SUBLANES - back:SUBLANES - back + ts, :]
    conv = gate_b * conv

    mix = jnp.concatenate([attn_t_ref[...].T, conv], axis=1).astype(BF16)
    o_ref[...] = x + jnp.dot(mix, wout_ref[...], preferred_element_type=F32)

    kd_ref[:, 0:ATTN_BLOCK, :] = kd_ref[:, ts:ts + ATTN_BLOCK, :]
    vt_ref[:, 0:ATTN_BLOCK] = vt_ref[:, ts:ts + ATTN_BLOCK]
    cbuf_ref[0:SUBLANES, :] = cbuf_ref[ts:ts + SUBLANES, :]


def _even_mixer(x, g, w_in, q_gain, k_gain, sinks, conv_w, w_out):
    b, s, d = x.shape
    ts = SEQ_TILE
    n_in = w_in.shape[1]
    cw_dim = conv_w.shape[1]
    mix_dim = w_out.shape[0]
    qg = jnp.tile(q_gain, N_Q_HEADS).reshape(1, N_Q_HEADS * HEAD_DIM)
    kg = jnp.tile(k_gain, N_KV_HEADS).reshape(1, N_KV_HEADS * HEAD_DIM)
    const = lambda bi, si: (0, 0)
    return pl.pallas_call(
        _even_mixer_kernel,
        out_shape=jax.ShapeDtypeStruct((b, s, d), F32),
        grid=(b, s // ts),
        in_specs=[
            pl.BlockSpec(memory_space=pltpu.SMEM),
            pl.BlockSpec((None, ts, d), lambda bi, si: (bi, si, 0)),
            _resident((1, d), const),
            _resident((d, n_in), const),
            _resident((1, N_Q_HEADS * HEAD_DIM), const),
            _resident((1, N_KV_HEADS * HEAD_DIM), const),
            _resident((CONV_K_B, cw_dim), const),
            _resident((mix_dim, d), const),
        ],
        out_specs=pl.BlockSpec((None, ts, d), lambda bi, si: (bi, si, 0)),
        scratch_shapes=[
            pltpu.VMEM((N_KV_HEADS, ATTN_BLOCK + ts, LANES), BF16),
            pltpu.VMEM((N_KV_HEADS * HEAD_DIM, ATTN_BLOCK + ts), BF16),
            pltpu.VMEM((SUBLANES + ts, cw_dim), F32),
            pltpu.VMEM((N_Q_HEADS * HEAD_DIM, ts), F32),
        ],
        compiler_params=pltpu.CompilerParams(
            dimension_semantics=("arbitrary", "arbitrary"), vmem_limit_bytes=VMEM_LIMIT_BYTES),
        name="even_mixer",
    )(sinks.reshape(1, N_Q_HEADS), x, g.reshape(1, d), w_in.astype(BF16), qg, kg, conv_w,
      w_out.astype(BF16))


def _swiglu_rows(hb, wg_ref, wu_ref, wd_ref):
    d_ff = wg_ref.shape[1]
    acc = None
    for c in range(d_ff // FF_CHUNK):
        cols = slice(c * FF_CHUNK, (c + 1) * FF_CHUNK)
        gate = jnp.dot(hb, wg_ref[:, cols], preferred_element_type=F32)
        up = jnp.dot(hb, wu_ref[:, cols], preferred_element_type=F32)
        act = (gate * (1.0 / (1.0 + jnp.exp(-gate))) * up).astype(BF16)
        part = jnp.dot(act, wd_ref[cols, :], preferred_element_type=F32)
        acc = part if acc is None else acc + part
    return acc


def _load_cast(jobs, sem):
    chunks = []
    for src, dst, stage in jobs:
        chunk_rows = stage.shape[1]
        for c in range(dst.shape[0] // chunk_rows):
            chunks.append((src, dst, stage, pl.ds(c * chunk_rows, chunk_rows)))

    def copy(n):
        src, _, stage, rows = chunks[n]
        return pltpu.make_async_copy(src.at[rows, :], stage.at[n % 2], sem.at[n % 2])

    copy(0).start()
    for n, (_, dst, stage, rows) in enumerate(chunks):
        if n + 1 < len(chunks):
            copy(n + 1).start()
        copy(n).wait()
        dst[rows, :] = stage[n % 2].astype(BF16)


def _swiglu_weight_scratch(d, d_ff):
    in_rows = d // WEIGHT_CHUNKS
    out_rows = d_ff // WEIGHT_CHUNKS
    return [
        pltpu.VMEM((d, d_ff), BF16),
        pltpu.VMEM((d, d_ff), BF16),
        pltpu.VMEM((d_ff, d), BF16),
        pltpu.VMEM((2, in_rows, d_ff), F32),
        pltpu.VMEM((2, out_rows, d), F32),
        pltpu.SemaphoreType.DMA((2,)),
    ]


def _ffn_kernel(j, x_ref, g_ref, wg_hbm, wu_hbm, wd_hbm, o_ref,
                wg_ref, wu_ref, wd_ref, stage_in, stage_out, sem):
    @pl.when(pl.program_id(0) == 0)
    def _():
        _load_cast([(wg_hbm.at[j], wg_ref, stage_in), (wu_hbm.at[j], wu_ref, stage_in),
                    (wd_hbm.at[j], wd_ref, stage_out)], sem)

    x = x_ref[...]
    hb = _rms(x, g_ref[...]).astype(BF16)
    o_ref[...] = x + _swiglu_rows(hb, wg_ref, wu_ref, wd_ref)


def _ffn(x, g, j, w_gate, w_up, w_down):
    t, d = x.shape
    d_ff = w_gate.shape[2]
    tm = FFN_TILE
    return pl.pallas_call(
        functools.partial(_ffn_kernel, j),
        out_shape=jax.ShapeDtypeStruct((t, d), F32),
        grid=(t // tm,),
        in_specs=[
            pl.BlockSpec((tm, d), lambda i: (i, 0)),
            _resident((1, d), lambda i: (0, 0)),
            pl.BlockSpec(memory_space=pl.ANY),
            pl.BlockSpec(memory_space=pl.ANY),
            pl.BlockSpec(memory_space=pl.ANY),
        ],
        out_specs=pl.BlockSpec((tm, d), lambda i: (i, 0)),
        scratch_shapes=_swiglu_weight_scratch(d, d_ff),
        compiler_params=pltpu.CompilerParams(
            dimension_semantics=("arbitrary",), vmem_limit_bytes=VMEM_LIMIT_BYTES),
        name="dense_ffn",
    )(x, g.reshape(1, d), w_gate, w_up, w_down)


def _odd_mixer_kernel(x_ref, g_ref, win_ref, cw_ref, cb_ref, gw_ref, gab_ref, gxb_ref, lam_ref, wout_ref,
                      g2_ref, wr_ref, tri_ref, o_ref, h2_ref, route_ref, rt_ref, cnt_out_ref,
                      xbuf_ref, a_ref, b_ref, hs_ref, hcar_ref, cnt_ref):
    nb, tl, d = x_ref.shape
    ts = nb * tl
    w = cw_ref.shape[1]
    hd = w // LRU_HEADS
    st = pl.program_id(0)

    @pl.when(st == 0)
    def _():
        xbuf_ref[:, 0:SUBLANES, :] = jnp.zeros((nb, SUBLANES, w), F32)
        hcar_ref[...] = jnp.zeros(hcar_ref.shape, F32)
        cnt_ref[...] = jnp.zeros(cnt_ref.shape, F32)

    x = x_ref[...].reshape(ts, d)
    h = _rms(x, g_ref[...]).astype(BF16)
    proj = jnp.dot(h, win_ref[...], preferred_element_type=F32)
    y = proj[:, 0:w]
    y = y * (0.5 * (1.0 + jnp.tanh(GELU_C0 * (y + GELU_C1 * (y * y * y)))))
    xb = proj[:, w:2 * w]

    xc_rows = []
    for bi in range(nb):
        xb_seq = xb[bi * tl:(bi + 1) * tl, :]
        xbuf_ref[bi, SUBLANES:SUBLANES + tl, :] = xb_seq
        acc = cw_ref[CONV_K_C - 1:CONV_K_C, :] * xb_seq + cb_ref[...]
        for kk in range(CONV_K_C - 1):
            back = CONV_K_C - 1 - kk
            acc = acc + cw_ref[kk:kk + 1, :] * xbuf_ref[bi, SUBLANES - back:SUBLANES - back + tl, :]
        xc_rows.append(acc)
    xc = jnp.concatenate(xc_rows, axis=0)

    r_cols = []
    i_cols = []
    for hh in range(LRU_HEADS):
        ri = jnp.dot(xc[:, hh * hd:(hh + 1) * hd].astype(BF16), gw_ref[hh], preferred_element_type=F32)
        r_cols.append(ri[:, 0:hd])
        i_cols.append(ri[:, hd:2 * hd])
    r_pre = jnp.concatenate(r_cols, axis=1) + gab_ref[...]
    i_pre = jnp.concatenate(i_cols, axis=1) + gxb_ref[...]
    r = 0.5 * (1.0 + jnp.tanh(0.5 * r_pre))
    ig = 0.5 * (1.0 + jnp.tanh(0.5 * i_pre))
    neg_lam = -lam_ref[...]
    softplus = jnp.maximum(neg_lam, 0.0) + jnp.log1p(jnp.exp(-jnp.abs(neg_lam)))
    log_a = -LRU_C * r * softplus
    a = jnp.exp(log_a)
    mult = jnp.sqrt(jnp.maximum(1.0 - a * a, 0.0))
    row = lax.broadcasted_iota(jnp.int32, (ts, w), 0)
    mult = jnp.where((lax.rem(row, tl) == 0) & (st == 0), 1.0, mult)
    bb = mult * (ig * xc)

    for c in range(w // LANES):
        cols = slice(c * LANES, (c + 1) * LANES)
        for bi in range(nb):
            a_ref[c, pl.ds(bi, tl, stride=nb), :] = a[bi * tl:(bi + 1) * tl, cols]
            b_ref[c, pl.ds(bi, tl, stride=nb), :] = bb[bi * tl:(bi + 1) * tl, cols]

    def step(t, hprev):
        off = pl.multiple_of(t * nb, nb)
        hnew = []
        for c in range(w // LANES):
            hc = a_ref[c, pl.ds(off, nb), :] * hprev[c] + b_ref[c, pl.ds(off, nb), :]
            hs_ref[c, pl.ds(off, nb), :] = hc
            hnew.append(hc)
        return tuple(hnew)

    hlast = lax.fori_loop(0, tl, step, tuple(hcar_ref[c] for c in range(w // LANES)), unroll=4)
    for c in range(w // LANES):
        hcar_ref[c] = hlast[c]
    hs = jnp.concatenate(
        [jnp.concatenate([hs_ref[c, pl.ds(bi, tl, stride=nb), :] for c in range(w // LANES)], axis=1)
         for bi in range(nb)], axis=0)

    out = x + jnp.dot((y * hs).astype(BF16), wout_ref[...], preferred_element_type=F32)
    o_ref[...] = out.reshape(nb, tl, d)
    xbuf_ref[:, 0:SUBLANES, :] = xbuf_ref[:, tl:tl + SUBLANES, :]

    h2 = _rms(out, g2_ref[...])
    for bi in range(nb):
        _store_row_tiles(h2_ref.at[bi], 0, h2[bi * tl:(bi + 1) * tl, :])
    logits = lax.dot_general(wr_ref[...], h2.astype(BF16), (((1,), (1,)), ((), ())),
                             preferred_element_type=F32)
    erow = lax.broadcasted_iota(jnp.int32, (ROUTE_ROWS, ts), 0)
    lg = jnp.where(erow < N_EXPERTS, logits, -jnp.inf)
    m1 = jnp.max(lg, axis=0, keepdims=True)
    i1 = jnp.min(jnp.where(lg == m1, erow, ROUTE_ROWS), axis=0, keepdims=True)
    lg2 = jnp.where(erow == i1, -jnp.inf, lg)
    m2 = jnp.max(lg2, axis=0, keepdims=True)
    i2 = jnp.min(jnp.where(lg2 == m2, erow, ROUTE_ROWS), axis=0, keepdims=True)
    e2 = jnp.exp(m2 - m1)
    gate1 = 1.0 / (1.0 + e2)
    gate2 = e2 / (1.0 + e2)

    first = erow == i1
    second = erow == i2
    ind = jnp.where(first | second, 1.0, 0.0)
    cums = jnp.dot(ind.astype(BF16), tri_ref[...], preferred_element_type=F32)
    rank = cnt_ref[...] + cums - ind
    cnt_ref[...] = cnt_ref[...] + jnp.broadcast_to(cums[:, ts - 1:ts], cums.shape)
    rank1 = jnp.sum(jnp.where(first, rank, 0.0), axis=0, keepdims=True)
    rank2 = jnp.sum(jnp.where(second, rank, 0.0), axis=0, keepdims=True)
    cnt_out_ref[...] = cnt_ref[:, 0:ROUTE_LANES].astype(jnp.int32)

    fields = [None] * SUBLANES
    fields[ROUTE_E0], fields[ROUTE_E0 + 1] = i1.astype(F32), i2.astype(F32)
    fields[ROUTE_GATE0], fields[ROUTE_GATE0 + 1] = gate1, gate2
    fields[ROUTE_RANK0], fields[ROUTE_RANK0 + 1] = rank1, rank2
    zero_row = jnp.zeros((1, ts), F32)
    record_t = jnp.concatenate([zero_row if f is None else f for f in fields], axis=0)
    rt_ref[...] = record_t.astype(jnp.int32)
    route_ref[...] = jnp.concatenate(
        [record_t, jnp.zeros((ROUTE_LANES - SUBLANES, ts), F32)], axis=0).T.reshape(nb, tl, ROUTE_LANES)


def _odd_mixer(x, g, w_in, conv_w, conv_b, ga_w, ga_b, gx_w, gx_b, lam, w_out, g2, w_router):
    b, s, d = x.shape
    assert b == SUBLANES, "one timestep of all sequences must fill one sublane group"
    tl = SEQ_TILE // b
    ts = b * tl
    w = conv_w.shape[1]
    hd = w // LRU_HEADS
    gw = jnp.concatenate([ga_w, gx_w], axis=-1).astype(BF16)
    wr = jnp.pad(w_router.T, ((0, ROUTE_ROWS - N_EXPERTS), (0, 0))).astype(BF16)
    tri = jnp.triu(jnp.ones((ts, ts), BF16))
    nst = s // tl
    const = lambda si: (0, 0)
    row = lambda a: a.reshape(1, -1)
    tile = lambda si: (0, si, 0)
    out, h2, route, rt, counts = pl.pallas_call(
        _odd_mixer_kernel,
        out_shape=(jax.ShapeDtypeStruct((b, s, d), F32),
                   jax.ShapeDtypeStruct((b, s * SUBLANES, LANES), F32),
                   jax.ShapeDtypeStruct((b, s, ROUTE_LANES), F32),
                   jax.ShapeDtypeStruct((nst, SUBLANES, ts), jnp.int32),
                   jax.ShapeDtypeStruct((ROUTE_ROWS, ROUTE_LANES), jnp.int32)),
        grid=(nst,),
        in_specs=[
            pl.BlockSpec((b, tl, d), tile),
            _resident((1, d), const),
            _resident((d, 2 * w), const),
            _resident((CONV_K_C, w), const),
            _resident((1, w), const),
            _resident((LRU_HEADS, hd, 2 * hd), lambda si: (0, 0, 0)),
            _resident((1, w), const),
            _resident((1, w), const),
            _resident((1, w), const),
            _resident((w, d), const),
            _resident((1, d), const),
            _resident((ROUTE_ROWS, d), const),
            _resident((ts, ts), const),
        ],
        out_specs=(pl.BlockSpec((b, tl, d), tile),
                   pl.BlockSpec((b, tl * SUBLANES, LANES), tile),
                   pl.BlockSpec((b, tl, ROUTE_LANES), tile),
                   pl.BlockSpec((None, SUBLANES, ts), lambda si: (si, 0, 0)),
                   pl.BlockSpec((ROUTE_ROWS, ROUTE_LANES), const)),
        scratch_shapes=[
            pltpu.VMEM((b, SUBLANES + tl, w), F32),
            pltpu.VMEM((w // LANES, ts, LANES), F32),
            pltpu.VMEM((w // LANES, ts, LANES), F32),
            pltpu.VMEM((w // LANES, ts, LANES), F32),
            pltpu.VMEM((w // LANES, b, LANES), F32),
            pltpu.VMEM((ROUTE_ROWS, ts), F32),
        ],
        compiler_params=pltpu.CompilerParams(
            dimension_semantics=("arbitrary",), vmem_limit_bytes=VMEM_LIMIT_BYTES),
        name="odd_mixer",
    )(x, row(g), w_in.astype(BF16), conv_w, row(conv_b), gw, row(ga_b), row(gx_b), row(lam),
      w_out.astype(BF16), row(g2), wr, tri)
    rt = rt.reshape(nst, SUBLANES, b, tl).transpose(1, 2, 0, 3).reshape(SUBLANES, b * s)
    return out, h2.reshape(b * s * SUBLANES, LANES), route, rt, counts


def _moe_tables(rt, counts, t):
    r = MOE_ROWS
    tm = SEQ_TILE
    n_blocks = TOP_K * t // r + N_EXPERTS
    counts = counts[0:N_EXPERTS, 0]
    padded = (counts + r - 1) // r * r
    pad_end = jnp.cumsum(padded)
    pad_start = pad_end - padded
    e = rt[ROUTE_E0:ROUTE_E0 + TOP_K, :]
    rank = rt[ROUTE_RANK0:ROUTE_RANK0 + TOP_K, :]
    start = jnp.zeros_like(e)
    for ex in range(N_EXPERTS):
        start = jnp.where(e == ex, pad_start[ex], start)
    dest = start + rank
    blk = jnp.arange(n_blocks, dtype=jnp.int32)
    blk_expert = jnp.minimum(jnp.sum(blk[:, None] * r >= pad_end[None, :], axis=1), N_EXPERTS - 1)
    n_used = pad_end[N_EXPERTS - 1] // r
    last_blk = jnp.where(padded > 0, pad_end // r - 1, -1)
    has_pad = jnp.any(blk[:, None] == last_blk[None, :], axis=1) | (blk >= n_used)
    dest = dest.astype(jnp.int32).reshape(TOP_K, t // tm, tm).transpose(1, 0, 2).reshape(t // tm, TOP_K * tm)
    return dest, blk_expert.astype(jnp.int32), has_pad.astype(jnp.int32), n_used.astype(jnp.int32).reshape(1)


def _row_copy(src, src_row, dst, dst_row, sem):
    return pltpu.make_async_copy(src.at[pl.ds(src_row * SUBLANES, SUBLANES), :],
                                 dst.at[pl.ds(dst_row * SUBLANES, SUBLANES), :], sem)


def _rows_wait(src, dst, n_rows, sem):
    pltpu.make_async_copy(src.at[pl.ds(0, n_rows * SUBLANES), :],
                          dst.at[pl.ds(0, n_rows * SUBLANES), :], sem).wait()


def _dispatch_kernel(pad_ref, dest_hbm, h_ref, xg_hbm, dsm, stage, zbuf, sem_idx, sem_row, sem_zero):
    i = pl.program_id(0)
    n = pl.num_programs(0)
    tm = h_ref.shape[0] // SUBLANES
    r = zbuf.shape[0] // SUBLANES
    n_blocks = pad_ref.shape[0]
    slot = lax.rem(i, 2)

    def idx_copy(step, s):
        return pltpu.make_async_copy(dest_hbm.at[step], dsm.at[pl.ds(s * TOP_K * tm, TOP_K * tm)],
                                     sem_idx.at[s])

    def zero_copy(blk):
        return pltpu.make_async_copy(zbuf, xg_hbm.at[pl.ds(blk * r * SUBLANES, r * SUBLANES), :], sem_zero)

    @pl.when(i == 0)
    def _():
        idx_copy(0, 0).start()
        zbuf[...] = jnp.zeros(zbuf.shape, zbuf.dtype)

        def zero_start(blk, c):
            @pl.when(pad_ref[blk] == 1)
            def _():
                zero_copy(blk).start()
            return c

        def zero_wait(blk, c):
            @pl.when(pad_ref[blk] == 1)
            def _():
                zero_copy(blk).wait()
            return c

        lax.fori_loop(0, n_blocks, zero_start, 0)
        lax.fori_loop(0, n_blocks, zero_wait, 0)

    @pl.when(i + 1 < n)
    def _():
        idx_copy(i + 1, 1 - slot).start()

    idx_copy(i, slot).wait()

    stage_row0 = slot * tm
    stage[pl.ds(stage_row0 * SUBLANES, tm * SUBLANES), :] = h_ref[...]

    def issue(row, c):
        for k in range(TOP_K):
            _row_copy(stage, stage_row0 + row, xg_hbm, dsm[(slot * TOP_K + k) * tm + row],
                      sem_row.at[slot]).start(priority=k % DMA_THREADS)
        return c

    lax.fori_loop(0, tm, issue, 0, unroll=8)

    def wait_slot(s):
        for _ in range(TOP_K):
            _rows_wait(stage, xg_hbm, tm, sem_row.at[s])

    @pl.when(i > 0)
    def _():
        wait_slot(1 - slot)

    @pl.when(i == n - 1)
    def _():
        wait_slot(slot)


def _dispatch(h2, dest, has_pad):
    tm = dest.shape[1] // TOP_K
    t = h2.shape[0] // SUBLANES
    r = MOE_ROWS
    n_rows = TOP_K * t + N_EXPERTS * r
    grid_spec = pltpu.PrefetchScalarGridSpec(
        num_scalar_prefetch=1,
        grid=(t // tm,),
        in_specs=[pl.BlockSpec(memory_space=pl.ANY),
                  pl.BlockSpec((tm * SUBLANES, LANES), lambda i, pad: (i, 0))],
        out_specs=pl.BlockSpec(memory_space=pl.ANY),
        scratch_shapes=[
            pltpu.SMEM((2 * TOP_K * tm,), jnp.int32),
            pltpu.VMEM((2 * tm * SUBLANES, LANES), F32),
            pltpu.VMEM((r * SUBLANES, LANES), F32),
            pltpu.SemaphoreType.DMA((2,)),
            pltpu.SemaphoreType.DMA((2,)),
            pltpu.SemaphoreType.DMA,
        ],
    )
    return pl.pallas_call(
        _dispatch_kernel,
        out_shape=jax.ShapeDtypeStruct((n_rows * SUBLANES, LANES), F32),
        grid_spec=grid_spec,
        compiler_params=pltpu.CompilerParams(dimension_semantics=("arbitrary",)),
        name="moe_dispatch",
    )(has_pad, dest, h2)


def _expert_kernel(j, be_ref, nu_ref, x_ref, wg_hbm, wu_hbm, wd_hbm, y_ref,
                   wg_ref, wu_ref, wd_ref, stage_in, stage_out, sem):
    i = pl.program_id(0)
    r = x_ref.shape[0] // SUBLANES
    e = be_ref[i]
    new_expert = (i == 0) | (e != be_ref[jnp.maximum(i - 1, 0)])

    @pl.when((i < nu_ref[0]) & new_expert)
    def _():
        _load_cast([(wg_hbm.at[j, e], wg_ref, stage_in), (wu_hbm.at[j, e], wu_ref, stage_in),
                    (wd_hbm.at[j, e], wd_ref, stage_out)], sem)

    @pl.when(i < nu_ref[0])
    def _():
        xb = _load_row_tiles(x_ref, 0, r).astype(BF16)
        _store_row_tiles(y_ref, 0, _swiglu_rows(xb, wg_ref, wu_ref, wd_ref))

    @pl.when(i >= nu_ref[0])
    def _():
        y_ref[...] = jnp.zeros(y_ref.shape, y_ref.dtype)


def _experts(xg, blk_expert, n_used, j, w_gate, w_up, w_down):
    d, d_ff = w_gate.shape[2], w_gate.shape[3]
    r = MOE_ROWS
    rows = lambda i, be, nu: (i, 0)
    grid_spec = pltpu.PrefetchScalarGridSpec(
        num_scalar_prefetch=2,
        grid=(xg.shape[0] // (r * SUBLANES),),
        in_specs=[
            pl.BlockSpec((r * SUBLANES, LANES), rows),
            pl.BlockSpec(memory_space=pl.ANY),
            pl.BlockSpec(memory_space=pl.ANY),
            pl.BlockSpec(memory_space=pl.ANY),
        ],
        out_specs=pl.BlockSpec((r * SUBLANES, LANES), rows),
        scratch_shapes=_swiglu_weight_scratch(d, d_ff),
    )
    return pl.pallas_call(
        functools.partial(_expert_kernel, j),
        out_shape=jax.ShapeDtypeStruct(xg.shape, F32),
        grid_spec=grid_spec,
        compiler_params=pltpu.CompilerParams(
            dimension_semantics=("arbitrary",), vmem_limit_bytes=VMEM_LIMIT_BYTES),
        name="moe_experts",
    )(blk_expert, n_used, xg, w_gate, w_up, w_down)


def _combine_kernel(dest_hbm, y_hbm, x_ref, route_ref, o_ref, dsm, ybuf, sem_idx, sem_row):
    i = pl.program_id(0)
    n = pl.num_programs(0)
    tm = x_ref.shape[0]
    slot = lax.rem(i, 2)

    def fetch(step, s):
        idx = pltpu.make_async_copy(dest_hbm.at[step], dsm.at[pl.ds(s * TOP_K * tm, TOP_K * tm)],
                                    sem_idx.at[s])
        idx.start()
        idx.wait()

        def issue(row, c):
            for k in range(TOP_K):
                at = (s * TOP_K + k) * tm + row
                _row_copy(y_hbm, dsm[at], ybuf, at, sem_row.at[s]).start(priority=k % DMA_THREADS)
            return c

        lax.fori_loop(0, tm, issue, 0, unroll=8)

    @pl.when(i == 0)
    def _():
        fetch(0, 0)

    @pl.when(i + 1 < n)
    def _():
        fetch(i + 1, 1 - slot)

    for _ in range(TOP_K):
        _rows_wait(y_hbm, ybuf, tm, sem_row.at[slot])
    rt = route_ref[...]
    moe = None
    for k in range(TOP_K):
        yk = _load_row_tiles(ybuf, (slot * TOP_K + k) * tm * SUBLANES, tm)
        term = yk * rt[:, ROUTE_GATE0 + k:ROUTE_GATE0 + k + 1]
        moe = term if moe is None else moe + term
    o_ref[...] = x_ref[...] + moe


def _combine(x, y, route, dest):
    t, d = x.shape
    tm = dest.shape[1] // TOP_K
    return pl.pallas_call(
        _combine_kernel,
        out_shape=jax.ShapeDtypeStruct((t, d), F32),
        grid=(t // tm,),
        in_specs=[
            pl.BlockSpec(memory_space=pl.ANY),
            pl.BlockSpec(memory_space=pl.ANY),
            pl.BlockSpec((tm, d), lambda i: (i, 0)),
            pl.BlockSpec((tm, ROUTE_LANES), lambda i: (i, 0)),
        ],
        out_specs=pl.BlockSpec((tm, d), lambda i: (i, 0)),
        scratch_shapes=[
            pltpu.SMEM((2 * TOP_K * tm,), jnp.int32),
            pltpu.VMEM((2 * TOP_K * tm * SUBLANES, LANES), F32),
            pltpu.SemaphoreType.DMA((2,)),
            pltpu.SemaphoreType.DMA((2,)),
        ],
        compiler_params=pltpu.CompilerParams(
            dimension_semantics=("arbitrary",), vmem_limit_bytes=VMEM_LIMIT_BYTES),
        name="moe_combine",
    )(dest, y, x, route)


def kernel(x, norm_mix, norm_ffn, hy_w_in, hy_q_gain, hy_k_gain, hy_sinks, hy_conv_w, hy_w_out, rg_w_in, rg_conv_w, rg_conv_b, rg_gate_a_w, rg_gate_a_b, rg_gate_x_w, rg_gate_x_b, rg_lambda, rg_w_out, ffn_w_gate, ffn_w_up, ffn_w_down, moe_router, moe_w_gate, moe_w_up, moe_w_down):
    b, s, d = x.shape
    t = b * s
    depth = norm_mix.shape[0]
    for layer in range(depth):
        j = layer // 2
        if layer % 2 == 0:
            x = _even_mixer(x, norm_mix[layer], hy_w_in[j], hy_q_gain[j], hy_k_gain[j], hy_sinks[j],
                            hy_conv_w[j], hy_w_out[j])
            x = _ffn(x.reshape(t, d), norm_ffn[layer], j, ffn_w_gate, ffn_w_up, ffn_w_down)
            x = x.reshape(b, s, d)
        else:
            x, h2, route, rt, counts = _odd_mixer(
                x, norm_mix[layer], rg_w_in[j], rg_conv_w[j], rg_conv_b[j], rg_gate_a_w[j], rg_gate_a_b[j],
                rg_gate_x_w[j], rg_gate_x_b[j], rg_lambda[j], rg_w_out[j], norm_ffn[layer], moe_router[j])
            dest, blk_expert, has_pad, n_used = _moe_tables(rt, counts, t)
            xg = _dispatch(h2, dest, has_pad)
            y = _experts(xg, blk_expert, n_used, j, moe_w_gate, moe_w_up, moe_w_down)
            x = _combine(x.reshape(t, d), y, route.reshape(t, ROUTE_LANES), dest).reshape(b, s, d)
    return x
```

```python
import functools

import jax
import jax.numpy as jnp
from jax import lax
from jax.experimental import pallas as pl
from jax.experimental.pallas import tpu as pltpu

F32 = jnp.float32
BF16 = jnp.bfloat16

LANES = 128
SUBLANES = 8
VMEM_LIMIT_BYTES = 56 * 1024 * 1024
DMA_THREADS = 2

N_Q_HEADS = 8
N_KV_HEADS = 2
HEAD_DIM = 64
Q_GROUP = N_Q_HEADS // N_KV_HEADS
WINDOW = 128
ATTN_BLOCK = 128
ALIBI_MAX = 8.0
CONV_K_B = 3
CONV_K_C = 4
LRU_HEADS = 8
LRU_C = 8.0
N_EXPERTS = 8
TOP_K = 2
NORM_EPS = 1e-6
NEG_INF = -1e30

SEQ_TILE = 512
EVEN_TILE = 512
FFN_TILE = 512
MOE_ROWS = 256
FF_CHUNK = 1792
WEIGHT_COLS = 512
ROUTE_ROWS = 16
ROUTE_LANES = LANES
ROUTE_E0 = 0
ROUTE_GATE0 = 2
ROUTE_RANK0 = 4
GELU_C0 = 0.7978845608028654
GELU_C1 = 0.044715


def _rms(x, g):
    return x * lax.rsqrt(jnp.mean(x * x, axis=-1, keepdims=True) + NORM_EPS) * g


def _split_bf16(x):
    hi = x.astype(BF16)
    lo = (x - hi.astype(F32)).astype(BF16)
    return hi, lo


def _store_row_tiles(ref, start, x):
    n = x.shape[0]
    for c in range(x.shape[1] // LANES):
        ref[pl.ds(start + c, n, stride=SUBLANES), :] = x[:, c * LANES:(c + 1) * LANES]


def _load_row_tiles(ref, start, n):
    return jnp.concatenate([ref[pl.ds(start + c, n, stride=SUBLANES), :] for c in range(SUBLANES)], axis=1)


def _resident(shape, index_map):
    return pl.BlockSpec(shape, index_map, pipeline_mode=pl.Buffered(1))


def _even_mixer_kernel(sinks_ref, x_ref, g_ref, win_ref, qg_ref, kg_ref, cw_ref, wout_ref, o_ref,
                       kd_ref, vt_ref, cbuf_ref, attn_t_ref):
    ts = x_ref.shape[0]
    q_dim = N_Q_HEADS * HEAD_DIM
    kv_dim = N_KV_HEADS * HEAD_DIM
    cw_dim = cw_ref.shape[1]
    st = pl.program_id(1)

    @pl.when(st == 0)
    def _():
        kd_ref[:, 0:ATTN_BLOCK, :] = jnp.zeros((N_KV_HEADS, ATTN_BLOCK, LANES), BF16)
        vt_ref[:, 0:ATTN_BLOCK] = jnp.zeros((kv_dim, ATTN_BLOCK), BF16)
        cbuf_ref[0:SUBLANES, :] = jnp.zeros((SUBLANES, cw_dim), F32)

    x = x_ref[...]
    h = _rms(x, g_ref[...]).astype(BF16)
    proj = jnp.dot(h, win_ref[...], preferred_element_type=F32)
    q = proj[:, 0:q_dim]
    k = proj[:, q_dim:q_dim + kv_dim]
    v = proj[:, q_dim + kv_dim:q_dim + 2 * kv_dim]
    o0 = q_dim + 2 * kv_dim
    gate_b = proj[:, o0:o0 + cw_dim]
    gate_c = proj[:, o0 + cw_dim:o0 + 2 * cw_dim]
    u = proj[:, o0 + 2 * cw_dim:o0 + 3 * cw_dim]

    ri = lax.broadcasted_iota(jnp.int32, (LANES, LANES), 0)
    ci = lax.broadcasted_iota(jnp.int32, (LANES, LANES), 1)
    half_mean = jnp.where((ri < HEAD_DIM) == (ci < HEAD_DIM), 1.0 / HEAD_DIM, 0.0).astype(BF16)

    def half_mean_square(z):
        hi, lo = _split_bf16(z * z)
        cols = []
        for c in range(z.shape[1] // LANES):
            sl = slice(c * LANES, (c + 1) * LANES)
            cols.append(jnp.dot(hi[:, sl], half_mean, preferred_element_type=F32)
                        + jnp.dot(lo[:, sl], half_mean, preferred_element_type=F32))
        return cols[0] if len(cols) == 1 else jnp.concatenate(cols, axis=1)

    qn = q * lax.rsqrt(half_mean_square(q) + NORM_EPS) * qg_ref[...] * (HEAD_DIM ** -0.5)
    kn = k * lax.rsqrt(half_mean_square(k) + NORM_EPS) * kg_ref[...]

    low_half_t = lax.broadcasted_iota(jnp.int32, (ts, LANES), 1) < HEAD_DIM
    k_sw = pltpu.roll(kn, HEAD_DIM, axis=1)
    kd_ref[0, ATTN_BLOCK:ATTN_BLOCK + ts, :] = jnp.where(low_half_t, kn, k_sw).astype(BF16)
    kd_ref[1, ATTN_BLOCK:ATTN_BLOCK + ts, :] = jnp.where(low_half_t, k_sw, kn).astype(BF16)
    vt_ref[:, ATTN_BLOCK:ATTN_BLOCK + ts] = v.T.astype(BF16)

    low_half = lax.broadcasted_iota(jnp.int32, (ATTN_BLOCK, LANES), 1) < HEAD_DIM
    sj = lax.broadcasted_iota(jnp.int32, (2 * ATTN_BLOCK, ATTN_BLOCK), 0)
    qi = lax.broadcasted_iota(jnp.int32, (2 * ATTN_BLOCK, ATTN_BLOCK), 1)
    diff = ATTN_BLOCK + qi - sj
    in_window = (diff >= 0) & (diff < WINDOW)
    diff_f = diff.astype(F32)
    first_key = jnp.where(st == 0, ATTN_BLOCK, 0)
    in_window_first = in_window & (sj >= first_key)

    def penalty(valid, g):
        cols = []
        for hh in range(Q_GROUP):
            slope = 2.0 ** (-ALIBI_MAX * (g * Q_GROUP + hh + 1) / N_Q_HEADS)
            cols.append(jnp.where(valid, slope * diff_f, -NEG_INF))
        return jnp.concatenate(cols, axis=1)

    pen = [penalty(in_window, g) for g in range(N_KV_HEADS)]
    pen_first = [penalty(in_window_first, g) for g in range(N_KV_HEADS)]
    sink = [jnp.concatenate([jnp.full((1, ATTN_BLOCK), sinks_ref[0, g * Q_GROUP + hh], F32)
                             for hh in range(Q_GROUP)], axis=1) for g in range(N_KV_HEADS)]

    for j in range(ts // ATTN_BLOCK):
        rows = slice(j * ATTN_BLOCK, (j + 1) * ATTN_BLOCK)
        band = slice(j * ATTN_BLOCK, (j + 2) * ATTN_BLOCK)
        for g in range(N_KV_HEADS):
            q_rows = []
            for p in range(Q_GROUP // 2):
                c = g * (Q_GROUP // 2) + p
                qp = qn[rows, c * LANES:(c + 1) * LANES]
                q_rows.append(jnp.where(low_half, qp, 0.0))
                q_rows.append(jnp.where(low_half, 0.0, qp))
            qs = jnp.concatenate(q_rows, axis=0).astype(BF16)
            s = lax.dot_general(kd_ref[g, band, :], qs, (((1,), (1,)), ((), ())),
                                preferred_element_type=F32)
            sh = s - (pen_first[g] if j == 0 else pen[g])
            m = jnp.maximum(jnp.max(sh, axis=0, keepdims=True), sink[g])
            pr = jnp.exp(sh - m)
            denom = jnp.sum(pr, axis=0, keepdims=True) + jnp.exp(sink[g] - m)
            o = jnp.dot(vt_ref[g * HEAD_DIM:(g + 1) * HEAD_DIM, band], pr.astype(BF16),
                        preferred_element_type=F32)
            o = o / denom
            for hh in range(Q_GROUP):
                head = g * Q_GROUP + hh
                attn_t_ref[head * HEAD_DIM:(head + 1) * HEAD_DIM, rows] = (
                    o[:, hh * ATTN_BLOCK:(hh + 1) * ATTN_BLOCK])

    cu = gate_c * u
    cbuf_ref[SUBLANES:SUBLANES + ts, :] = cu
    conv = cw_ref[CONV_K_B - 1:CONV_K_B, :] * cu
    for kk in range(CONV_K_B - 1):
        back = CONV_K_B - 1 - kk
        conv = conv + cw_ref[kk:kk + 1, :] * cbuf_ref[SUBLANES - back:SUBLANES - back + ts, :]
    conv = gate_b * conv

    mix = jnp.concatenate([attn_t_ref[...].T, conv], axis=1).astype(BF16)
    o_ref[...] = x + jnp.dot(mix, wout_ref[...], preferred_element_type=F32)

    kd_ref[:, 0:ATTN_BLOCK, :] = kd_ref[:, ts:ts + ATTN_BLOCK, :]
    vt_ref[:, 0:ATTN_BLOCK] = vt_ref[:, ts:ts + ATTN_BLOCK]
    cbuf_ref[0:SUBLANES, :] = cbuf_ref[ts:ts + SUBLANES, :]


def _even_mixer(x, g, w_in, q_gain, k_gain, sinks, conv_w, w_out):
    b, s, d = x.shape
    ts = EVEN_TILE
    n_in = w_in.shape[1]
    cw_dim = conv_w.shape[1]
    mix_dim = w_out.shape[0]
    qg = jnp.tile(q_gain, N_Q_HEADS).reshape(1, N_Q_HEADS * HEAD_DIM)
    kg = jnp.tile(k_gain, N_KV_HEADS).reshape(1, N_KV_HEADS * HEAD_DIM)
    const = lambda bi, si: (0, 0)
    return pl.pallas_call(
        _even_mixer_kernel,
        out_shape=jax.ShapeDtypeStruct((b, s, d), F32),
        grid=(b, s // ts),
        in_specs=[
            pl.BlockSpec(memory_space=pltpu.SMEM),
            pl.BlockSpec((None, ts, d), lambda bi, si: (bi, si, 0)),
            _resident((1, d), const),
            _resident((d, n_in), const),
            _resident((1, N_Q_HEADS * HEAD_DIM), const),
            _resident((1, N_KV_HEADS * HEAD_DIM), const),
            _resident((CONV_K_B, cw_dim), const),
            _resident((mix_dim, d), const),
        ],
        out_specs=pl.BlockSpec((None, ts, d), lambda bi, si: (bi, si, 0)),
        scratch_shapes=[
            pltpu.VMEM((N_KV_HEADS, ATTN_BLOCK + ts, LANES), BF16),
            pltpu.VMEM((N_KV_HEADS * HEAD_DIM, ATTN_BLOCK + ts), BF16),
            pltpu.VMEM((SUBLANES + ts, cw_dim), F32),
            pltpu.VMEM((N_Q_HEADS * HEAD_DIM, ts), F32),
        ],
        compiler_params=pltpu.CompilerParams(
            dimension_semantics=("arbitrary", "arbitrary"), vmem_limit_bytes=VMEM_LIMIT_BYTES),
        name="even_mixer",
    )(sinks.reshape(1, N_Q_HEADS), x, g.reshape(1, d), w_in.astype(BF16), qg, kg, conv_w,
      w_out.astype(BF16))


def _swiglu_rows(hb, wg_ref, wu_ref, wd_ref):
    d_ff = wg_ref.shape[1]
    acc = None
    for c in range(d_ff // FF_CHUNK):
        cols = slice(c * FF_CHUNK, (c + 1) * FF_CHUNK)
        gate = jnp.dot(hb, wg_ref[:, cols], preferred_element_type=F32)
        up = jnp.dot(hb, wu_ref[:, cols], preferred_element_type=F32)
        act = (gate * (1.0 / (1.0 + jnp.exp(-gate))) * up).astype(BF16)
        part = jnp.dot(act, wd_ref[cols, :], preferred_element_type=F32)
        acc = part if acc is None else acc + part
    return acc


def _swiglu_rows_fetching(hb, w_hbm, w_ref, stage, sem):
    d_ff = w_ref[0].shape[1]
    n = d_ff // WEIGHT_COLS

    def copies(c, s):
        cols = pl.ds(c * WEIGHT_COLS, WEIGHT_COLS)
        srcs = (w_hbm[0].at[:, cols], w_hbm[1].at[:, cols], w_hbm[2].at[cols, :])
        return [pltpu.make_async_copy(src, stage[m].at[s], sem.at[s, m]) for m, src in enumerate(srcs)]

    for cp in copies(0, 0):
        cp.start()
    acc = None
    for c in range(n):
        s = c % 2
        if c + 1 < n:
            for cp in copies(c + 1, 1 - s):
                cp.start()
        for cp in copies(c, s):
            cp.wait()
        cols = slice(c * WEIGHT_COLS, (c + 1) * WEIGHT_COLS)
        w_ref[0][:, cols] = stage[0][s].astype(BF16)
        w_ref[1][:, cols] = stage[1][s].astype(BF16)
        w_ref[2][cols, :] = stage[2][s].astype(BF16)
        gate = jnp.dot(hb, w_ref[0][:, cols], preferred_element_type=F32)
        up = jnp.dot(hb, w_ref[1][:, cols], preferred_element_type=F32)
        act = (gate * (1.0 / (1.0 + jnp.exp(-gate))) * up).astype(BF16)
        part = jnp.dot(act, w_ref[2][cols, :], preferred_element_type=F32)
        acc = part if acc is None else acc + part
    return acc


def _swiglu_weight_scratch(d, d_ff):
    return [
        pltpu.VMEM((d, d_ff), BF16),
        pltpu.VMEM((d, d_ff), BF16),
        pltpu.VMEM((d_ff, d), BF16),
        pltpu.VMEM((2, d, WEIGHT_COLS), F32),
        pltpu.VMEM((2, d, WEIGHT_COLS), F32),
        pltpu.VMEM((2, WEIGHT_COLS, d), F32),
        pltpu.SemaphoreType.DMA((2, 3)),
    ]


def _ffn_kernel(j, x_ref, g_ref, wg_hbm, wu_hbm, wd_hbm, o_ref,
                wg_ref, wu_ref, wd_ref, stage_g, stage_u, stage_d, sem):
    x = x_ref[...]
    hb = _rms(x, g_ref[...]).astype(BF16)
    w_ref = (wg_ref, wu_ref, wd_ref)

    @pl.when(pl.program_id(0) == 0)
    def _():
        o_ref[...] = x + _swiglu_rows_fetching(hb, (wg_hbm.at[j], wu_hbm.at[j], wd_hbm.at[j]), w_ref,
                                               (stage_g, stage_u, stage_d), sem)

    @pl.when(pl.program_id(0) > 0)
    def _():
        o_ref[...] = x + _swiglu_rows(hb, *w_ref)


def _ffn(x, g, j, w_gate, w_up, w_down):
    t, d = x.shape
    d_ff = w_gate.shape[2]
    tm = FFN_TILE
    return pl.pallas_call(
        functools.partial(_ffn_kernel, j),
        out_shape=jax.ShapeDtypeStruct((t, d), F32),
        grid=(t // tm,),
        in_specs=[
            pl.BlockSpec((tm, d), lambda i: (i, 0)),
            _resident((1, d), lambda i: (0, 0)),
            pl.BlockSpec(memory_space=pl.ANY),
            pl.BlockSpec(memory_space=pl.ANY),
            pl.BlockSpec(memory_space=pl.ANY),
        ],
        out_specs=pl.BlockSpec((tm, d), lambda i: (i, 0)),
        scratch_shapes=_swiglu_weight_scratch(d, d_ff),
        compiler_params=pltpu.CompilerParams(
            dimension_semantics=("arbitrary",), vmem_limit_bytes=VMEM_LIMIT_BYTES),
        name="dense_ffn",
    )(x, g.reshape(1, d), w_gate, w_up, w_down)


def _odd_mixer_kernel(x_ref, g_ref, win_ref, cw_ref, cb_ref, gw_ref, gab_ref, gxb_ref, lam_ref, wout_ref,
                      g2_ref, wr_ref, tri_ref, o_ref, h2_ref, route_ref, rt_ref, cnt_out_ref,
                      xbuf_ref, a_ref, b_ref, hs_ref, hcar_ref, cnt_ref):
    nb, tl, d = x_ref.shape
    ts = nb * tl
    w = cw_ref.shape[1]
    hd = w // LRU_HEADS
    st = pl.program_id(0)

    @pl.when(st == 0)
    def _():
        xbuf_ref[:, 0:SUBLANES, :] = jnp.zeros((nb, SUBLANES, w), F32)
        hcar_ref[...] = jnp.zeros(hcar_ref.shape, F32)
        cnt_ref[...] = jnp.zeros(cnt_ref.shape, F32)

    x = x_ref[...].reshape(ts, d)
    h = _rms(x, g_ref[...]).astype(BF16)
    proj = jnp.dot(h, win_ref[...], preferred_element_type=F32)
    y = proj[:, 0:w]
    y = y * (0.5 * (1.0 + jnp.tanh(GELU_C0 * (y + GELU_C1 * (y * y * y)))))
    xb = proj[:, w:2 * w]

    xc_rows = []
    for bi in range(nb):
        xb_seq = xb[bi * tl:(bi + 1) * tl, :]
        xbuf_ref[bi, SUBLANES:SUBLANES + tl, :] = xb_seq
        acc = cw_ref[CONV_K_C - 1:CONV_K_C, :] * xb_seq + cb_ref[...]
        for kk in range(CONV_K_C - 1):
            back = CONV_K_C - 1 - kk
            acc = acc + cw_ref[kk:kk + 1, :] * xbuf_ref[bi, SUBLANES - back:SUBLANES - back + tl, :]
        xc_rows.append(acc)
    xc = jnp.concatenate(xc_rows, axis=0)

    r_cols = []
    i_cols = []
    for hh in range(LRU_HEADS):
        ri = jnp.dot(xc[:, hh * hd:(hh + 1) * hd].astype(BF16), gw_ref[hh], preferred_element_type=F32)
        r_cols.append(ri[:, 0:hd])
        i_cols.append(ri[:, hd:2 * hd])
    r_pre = jnp.concatenate(r_cols, axis=1) + gab_ref[...]
    i_pre = jnp.concatenate(i_cols, axis=1) + gxb_ref[...]
    r = 0.5 * (1.0 + jnp.tanh(0.5 * r_pre))
    ig = 0.5 * (1.0 + jnp.tanh(0.5 * i_pre))
    neg_lam = -lam_ref[...]
    softplus = jnp.maximum(neg_lam, 0.0) + jnp.log1p(jnp.exp(-jnp.abs(neg_lam)))
    log_a = -LRU_C * r * softplus
    a = jnp.exp(log_a)
    mult = jnp.sqrt(jnp.maximum(1.0 - a * a, 0.0))
    row = lax.broadcasted_iota(jnp.int32, (ts, w), 0)
    mult = jnp.where((lax.rem(row, tl) == 0) & (st == 0), 1.0, mult)
    bb = mult * (ig * xc)

    for c in range(w // LANES):
        cols = slice(c * LANES, (c + 1) * LANES)
        for bi in range(nb):
            a_ref[c, pl.ds(bi, tl, stride=nb), :] = a[bi * tl:(bi + 1) * tl, cols]
            b_ref[c, pl.ds(bi, tl, stride=nb), :] = bb[bi * tl:(bi + 1) * tl, cols]

    def step(t, hprev):
        off = pl.multiple_of(t * nb, nb)
        hnew = []
        for c in range(w // LANES):
            hc = a_ref[c, pl.ds(off, nb), :] * hprev[c] + b_ref[c, pl.ds(off, nb), :]
            hs_ref[c, pl.ds(off, nb), :] = hc
            hnew.append(hc)
        return tuple(hnew)

    hlast = lax.fori_loop(0, tl, step, tuple(hcar_ref[c] for c in range(w // LANES)), unroll=4)
    for c in range(w // LANES):
        hcar_ref[c] = hlast[c]
    hs = jnp.concatenate(
        [jnp.concatenate([hs_ref[c, pl.ds(bi, tl, stride=nb), :] for c in range(w // LANES)], axis=1)
         for bi in range(nb)], axis=0)

    out = x + jnp.dot((y * hs).astype(BF16), wout_ref[...], preferred_element_type=F32)
    o_ref[...] = out.reshape(nb, tl, d)
    xbuf_ref[:, 0:SUBLANES, :] = xbuf_ref[:, tl:tl + SUBLANES, :]

    h2 = _rms(out, g2_ref[...])
    for bi in range(nb):
        _store_row_tiles(h2_ref.at[bi], 0, h2[bi * tl:(bi + 1) * tl, :])
    logits = lax.dot_general(wr_ref[...], h2.astype(BF16), (((1,), (1,)), ((), ())),
                             preferred_element_type=F32)
    erow = lax.broadcasted_iota(jnp.int32, (ROUTE_ROWS, ts), 0)
    lg = jnp.where(erow < N_EXPERTS, logits, -jnp.inf)
    m1 = jnp.max(lg, axis=0, keepdims=True)
    i1 = jnp.min(jnp.where(lg == m1, erow, ROUTE_ROWS), axis=0, keepdims=True)
    lg2 = jnp.where(erow == i1, -jnp.inf, lg)
    m2 = jnp.max(lg2, axis=0, keepdims=True)
    i2 = jnp.min(jnp.where(lg2 == m2, erow, ROUTE_ROWS), axis=0, keepdims=True)
    e2 = jnp.exp(m2 - m1)
    gate1 = 1.0 / (1.0 + e2)
    gate2 = e2 / (1.0 + e2)

    first = erow == i1
    second = erow == i2
    ind = jnp.where(first | second, 1.0, 0.0)
    cums = jnp.dot(ind.astype(BF16), tri_ref[...], preferred_element_type=F32)
    rank = cnt_ref[...] + cums - ind
    cnt_ref[...] = cnt_ref[...] + jnp.broadcast_to(cums[:, ts - 1:ts], cums.shape)
    rank1 = jnp.sum(jnp.where(first, rank, 0.0), axis=0, keepdims=True)
    rank2 = jnp.sum(jnp.where(second, rank, 0.0), axis=0, keepdims=True)
    cnt_out_ref[...] = cnt_ref[:, 0:ROUTE_LANES].astype(jnp.int32)

    fields = [None] * SUBLANES
    fields[ROUTE_E0], fields[ROUTE_E0 + 1] = i1.astype(F32), i2.astype(F32)
    fields[ROUTE_GATE0], fields[ROUTE_GATE0 + 1] = gate1, gate2
    fields[ROUTE_RANK0], fields[ROUTE_RANK0 + 1] = rank1, rank2
    zero_row = jnp.zeros((1, ts), F32)
    record_t = jnp.concatenate([zero_row if f is None else f for f in fields], axis=0)
    rt_ref[...] = record_t.astype(jnp.int32)
    route_ref[...] = jnp.concatenate(
        [record_t, jnp.zeros((ROUTE_LANES - SUBLANES, ts), F32)], axis=0).T.reshape(nb, tl, ROUTE_LANES)


def _odd_mixer(x, g, w_in, conv_w, conv_b, ga_w, ga_b, gx_w, gx_b, lam, w_out, g2, w_router):
    b, s, d = x.shape
    assert b == SUBLANES, "one timestep of all sequences must fill one sublane group"
    tl = SEQ_TILE // b
    ts = b * tl
    w = conv_w.shape[1]
    hd = w // LRU_HEADS
    gw = jnp.concatenate([ga_w, gx_w], axis=-1).astype(BF16)
    wr = jnp.pad(w_router.T, ((0, ROUTE_ROWS - N_EXPERTS), (0, 0))).astype(BF16)
    tri = jnp.triu(jnp.ones((ts, ts), BF16))
    nst = s // tl
    const = lambda si: (0, 0)
    row = lambda a: a.reshape(1, -1)
    tile = lambda si: (0, si, 0)
    out, h2, route, rt, counts = pl.pallas_call(
        _odd_mixer_kernel,
        out_shape=(jax.ShapeDtypeStruct((b, s, d), F32),
                   jax.ShapeDtypeStruct((b, s * SUBLANES, LANES), F32),
                   jax.ShapeDtypeStruct((b, s, ROUTE_LANES), F32),
                   jax.ShapeDtypeStruct((nst, SUBLANES, ts), jnp.int32),
                   jax.ShapeDtypeStruct((ROUTE_ROWS, ROUTE_LANES), jnp.int32)),
        grid=(nst,),
        in_specs=[
            pl.BlockSpec((b, tl, d), tile),
            _resident((1, d), const),
            _resident((d, 2 * w), const),
            _resident((CONV_K_C, w), const),
            _resident((1, w), const),
            _resident((LRU_HEADS, hd, 2 * hd), lambda si: (0, 0, 0)),
            _resident((1, w), const),
            _resident((1, w), const),
            _resident((1, w), const),
            _resident((w, d), const),
            _resident((1, d), const),
            _resident((ROUTE_ROWS, d), const),
            _resident((ts, ts), const),
        ],
        out_specs=(pl.BlockSpec((b, tl, d), tile),
                   pl.BlockSpec((b, tl * SUBLANES, LANES), tile),
                   pl.BlockSpec((b, tl, ROUTE_LANES), tile),
                   pl.BlockSpec((None, SUBLANES, ts), lambda si: (si, 0, 0)),
                   pl.BlockSpec((ROUTE_ROWS, ROUTE_LANES), const)),
        scratch_shapes=[
            pltpu.VMEM((b, SUBLANES + tl, w), F32),
            pltpu.VMEM((w // LANES, ts, LANES), F32),
            pltpu.VMEM((w // LANES, ts, LANES), F32),
            pltpu.VMEM((w // LANES, ts, LANES), F32),
            pltpu.VMEM((w // LANES, b, LANES), F32),
            pltpu.VMEM((ROUTE_ROWS, ts), F32),
        ],
        compiler_params=pltpu.CompilerParams(
            dimension_semantics=("arbitrary",), vmem_limit_bytes=VMEM_LIMIT_BYTES),
        name="odd_mixer",
    )(x, row(g), w_in.astype(BF16), conv_w, row(conv_b), gw, row(ga_b), row(gx_b), row(lam),
      w_out.astype(BF16), row(g2), wr, tri)
    rt = rt.reshape(nst, SUBLANES, b, tl).transpose(1, 2, 0, 3).reshape(SUBLANES, b * s)
    return out, h2.reshape(b * s * SUBLANES, LANES), route, rt, counts


def _moe_tables(rt, counts, t):
    r = MOE_ROWS
    tm = SEQ_TILE
    n_blocks = TOP_K * t // r + N_EXPERTS
    counts = counts[0:N_EXPERTS, 0]
    padded = (counts + r - 1) // r * r
    pad_end = jnp.cumsum(padded)
    pad_start = pad_end - padded
    e = rt[ROUTE_E0:ROUTE_E0 + TOP_K, :]
    rank = rt[ROUTE_RANK0:ROUTE_RANK0 + TOP_K, :]
    start = jnp.zeros_like(e)
    for ex in range(N_EXPERTS):
        start = jnp.where(e == ex, pad_start[ex], start)
    dest = start + rank
    blk = jnp.arange(n_blocks, dtype=jnp.int32)
    blk_expert = jnp.minimum(jnp.sum(blk[:, None] * r >= pad_end[None, :], axis=1), N_EXPERTS - 1)
    n_used = pad_end[N_EXPERTS - 1] // r
    last_blk = jnp.where(padded > 0, pad_end // r - 1, -1)
    has_pad = jnp.any(blk[:, None] == last_blk[None, :], axis=1) | (blk >= n_used)
    dest = dest.astype(jnp.int32).reshape(TOP_K, t // tm, tm).transpose(1, 0, 2).reshape(t // tm, TOP_K * tm)
    return dest, blk_expert.astype(jnp.int32), has_pad.astype(jnp.int32), n_used.astype(jnp.int32).reshape(1)


def _row_copy(src, src_row, dst, dst_row, sem):
    return pltpu.make_async_copy(src.at[pl.ds(src_row * SUBLANES, SUBLANES), :],
                                 dst.at[pl.ds(dst_row * SUBLANES, SUBLANES), :], sem)


def _rows_wait(src, dst, n_rows, sem):
    pltpu.make_async_copy(src.at[pl.ds(0, n_rows * SUBLANES), :],
                          dst.at[pl.ds(0, n_rows * SUBLANES), :], sem).wait()


def _dispatch_kernel(pad_ref, dest_hbm, h_ref, xg_hbm, dsm, stage, zbuf, sem_idx, sem_row, sem_zero):
    i = pl.program_id(0)
    n = pl.num_programs(0)
    tm = h_ref.shape[0] // SUBLANES
    r = zbuf.shape[0] // SUBLANES
    n_blocks = pad_ref.shape[0]
    slot = lax.rem(i, 2)

    def idx_copy(step, s):
        return pltpu.make_async_copy(dest_hbm.at[step], dsm.at[pl.ds(s * TOP_K * tm, TOP_K * tm)],
                                     sem_idx.at[s])

    def zero_copy(blk):
        return pltpu.make_async_copy(zbuf, xg_hbm.at[pl.ds(blk * r * SUBLANES, r * SUBLANES), :], sem_zero)

    @pl.when(i == 0)
    def _():
        idx_copy(0, 0).start()
        zbuf[...] = jnp.zeros(zbuf.shape, zbuf.dtype)

        def zero_start(blk, c):
            @pl.when(pad_ref[blk] == 1)
            def _():
                zero_copy(blk).start()
            return c

        def zero_wait(blk, c):
            @pl.when(pad_ref[blk] == 1)
            def _():
                zero_copy(blk).wait()
            return c

        lax.fori_loop(0, n_blocks, zero_start, 0)
        lax.fori_loop(0, n_blocks, zero_wait, 0)

    @pl.when(i + 1 < n)
    def _():
        idx_copy(i + 1, 1 - slot).start()

    idx_copy(i, slot).wait()

    stage_row0 = slot * tm
    stage[pl.ds(stage_row0 * SUBLANES, tm * SUBLANES), :] = h_ref[...]

    def issue(row, c):
        for k in range(TOP_K):
            _row_copy(stage, stage_row0 + row, xg_hbm, dsm[(slot * TOP_K + k) * tm + row],
                      sem_row.at[slot]).start(priority=k % DMA_THREADS)
        return c

    lax.fori_loop(0, tm, issue, 0, unroll=8)

    def wait_slot(s):
        for _ in range(TOP_K):
            _rows_wait(stage, xg_hbm, tm, sem_row.at[s])

    @pl.when(i > 0)
    def _():
        wait_slot(1 - slot)

    @pl.when(i == n - 1)
    def _():
        wait_slot(slot)


def _dispatch(h2, dest, has_pad):
    tm = dest.shape[1] // TOP_K
    t = h2.shape[0] // SUBLANES
    r = MOE_ROWS
    n_rows = TOP_K * t + N_EXPERTS * r
    grid_spec = pltpu.PrefetchScalarGridSpec(
        num_scalar_prefetch=1,
        grid=(t // tm,),
        in_specs=[pl.BlockSpec(memory_space=pl.ANY),
                  pl.BlockSpec((tm * SUBLANES, LANES), lambda i, pad: (i, 0))],
        out_specs=pl.BlockSpec(memory_space=pl.ANY),
        scratch_shapes=[
            pltpu.SMEM((2 * TOP_K * tm,), jnp.int32),
            pltpu.VMEM((2 * tm * SUBLANES, LANES), F32),
            pltpu.VMEM((r * SUBLANES, LANES), F32),
            pltpu.SemaphoreType.DMA((2,)),
            pltpu.SemaphoreType.DMA((2,)),
            pltpu.SemaphoreType.DMA,
        ],
    )
    return pl.pallas_call(
        _dispatch_kernel,
        out_shape=jax.ShapeDtypeStruct((n_rows * SUBLANES, LANES), F32),
        grid_spec=grid_spec,
        compiler_params=pltpu.CompilerParams(dimension_semantics=("arbitrary",)),
        name="moe_dispatch",
    )(has_pad, dest, h2)


def _expert_kernel(j, be_ref, nu_ref, x_ref, wg_hbm, wu_hbm, wd_hbm, y_ref,
                   wg_ref, wu_ref, wd_ref, stage_g, stage_u, stage_d, sem):
    i = pl.program_id(0)
    r = x_ref.shape[0] // SUBLANES
    e = be_ref[i]
    used = i < nu_ref[0]
    new_expert = (i == 0) | (e != be_ref[jnp.maximum(i - 1, 0)])
    w_ref = (wg_ref, wu_ref, wd_ref)

    @pl.when(used & new_expert)
    def _():
        xb = _load_row_tiles(x_ref, 0, r).astype(BF16)
        y = _swiglu_rows_fetching(xb, (wg_hbm.at[j, e], wu_hbm.at[j, e], wd_hbm.at[j, e]), w_ref,
                                  (stage_g, stage_u, stage_d), sem)
        _store_row_tiles(y_ref, 0, y)

    @pl.when(used & jnp.logical_not(new_expert))
    def _():
        xb = _load_row_tiles(x_ref, 0, r).astype(BF16)
        _store_row_tiles(y_ref, 0, _swiglu_rows(xb, *w_ref))

    @pl.when(jnp.logical_not(used))
    def _():
        y_ref[...] = jnp.zeros(y_ref.shape, y_ref.dtype)


def _experts(xg, blk_expert, n_used, j, w_gate, w_up, w_down):
    d, d_ff = w_gate.shape[2], w_gate.shape[3]
    r = MOE_ROWS
    rows = lambda i, be, nu: (i, 0)
    grid_spec = pltpu.PrefetchScalarGridSpec(
        num_scalar_prefetch=2,
        grid=(xg.shape[0] // (r * SUBLANES),),
        in_specs=[
            pl.BlockSpec((r * SUBLANES, LANES), rows),
            pl.BlockSpec(memory_space=pl.ANY),
            pl.BlockSpec(memory_space=pl.ANY),
            pl.BlockSpec(memory_space=pl.ANY),
        ],
        out_specs=pl.BlockSpec((r * SUBLANES, LANES), rows),
        scratch_shapes=_swiglu_weight_scratch(d, d_ff),
    )
    return pl.pallas_call(
        functools.partial(_expert_kernel, j),
        out_shape=jax.ShapeDtypeStruct(xg.shape, F32),
        grid_spec=grid_spec,
        compiler_params=pltpu.CompilerParams(
            dimension_semantics=("arbitrary",), vmem_limit_bytes=VMEM_LIMIT_BYTES),
        name="moe_experts",
    )(blk_expert, n_used, xg, w_gate, w_up, w_down)


def _combine_kernel(dest_hbm, y_hbm, x_ref, route_ref, o_ref, dsm, ybuf, sem_idx, sem_row):
    i = pl.program_id(0)
    n = pl.num_programs(0)
    tm = x_ref.shape[0]
    slot = lax.rem(i, 2)

    def idx_copy(step):
        s3 = lax.rem(step, 3)
        return pltpu.make_async_copy(dest_hbm.at[step], dsm.at[pl.ds(s3 * TOP_K * tm, TOP_K * tm)],
                                     sem_idx.at[s3])

    def request_rows(step):
        s2 = lax.rem(step, 2)
        table = lax.rem(step, 3) * TOP_K * tm

        def issue(row, c):
            for k in range(TOP_K):
                _row_copy(y_hbm, dsm[table + k * tm + row], ybuf, (s2 * TOP_K + k) * tm + row,
                          sem_row.at[s2]).start(priority=k % DMA_THREADS)
            return c

        lax.fori_loop(0, tm, issue, 0, unroll=8)

    @pl.when(i == 0)
    def _():
        idx_copy(0).start()
        if_next = n > 1

        @pl.when(if_next)
        def _():
            idx_copy(1).start()

        idx_copy(0).wait()
        request_rows(0)

    @pl.when(i + 2 < n)
    def _():
        idx_copy(i + 2).start()

    @pl.when(i + 1 < n)
    def _():
        idx_copy(i + 1).wait()
        request_rows(i + 1)

    for _ in range(TOP_K):
        _rows_wait(y_hbm, ybuf, tm, sem_row.at[slot])
    rt = route_ref[...]
    moe = None
    for k in range(TOP_K):
        yk = _load_row_tiles(ybuf, (slot * TOP_K + k) * tm * SUBLANES, tm)
        term = yk * rt[:, ROUTE_GATE0 + k:ROUTE_GATE0 + k + 1]
        moe = term if moe is None else moe + term
    o_ref[...] = x_ref[...] + moe


def _combine(x, y, route, dest):
    t, d = x.shape
    tm = dest.shape[1] // TOP_K
    return pl.pallas_call(
        _combine_kernel,
        out_shape=jax.ShapeDtypeStruct((t, d), F32),
        grid=(t // tm,),
        in_specs=[
            pl.BlockSpec(memory_space=pl.ANY),
            pl.BlockSpec(memory_space=pl.ANY),
            pl.BlockSpec((tm, d), lambda i: (i, 0)),
            pl.BlockSpec((tm, ROUTE_LANES), lambda i: (i, 0)),
        ],
        out_specs=pl.BlockSpec((tm, d), lambda i: (i, 0)),
        scratch_shapes=[
            pltpu.SMEM((3 * TOP_K * tm,), jnp.int32),
            pltpu.VMEM((2 * TOP_K * tm * SUBLANES, LANES), F32),
            pltpu.SemaphoreType.DMA((3,)),
            pltpu.SemaphoreType.DMA((2,)),
        ],
        compiler_params=pltpu.CompilerParams(
            dimension_semantics=("arbitrary",), vmem_limit_bytes=VMEM_LIMIT_BYTES),
        name="moe_combine",
    )(dest, y, x, route)


def kernel(x, norm_mix, norm_ffn, hy_w_in, hy_q_gain, hy_k_gain, hy_sinks, hy_conv_w, hy_w_out, rg_w_in, rg_conv_w, rg_conv_b, rg_gate_a_w, rg_gate_a_b, rg_gate_x_w, rg_gate_x_b, rg_lambda, rg_w_out, ffn_w_gate, ffn_w_up, ffn_w_down, moe_router, moe_w_gate, moe_w_up, moe_w_down):
    b, s, d = x.shape
    t = b * s
    depth = norm_mix.shape[0]
    for layer in range(depth):
        j = layer // 2
        if layer % 2 == 0:
            x = _even_mixer(x, norm_mix[layer], hy_w_in[j], hy_q_gain[j], hy_k_gain[j], hy_sinks[j],
                            hy_conv_w[j], hy_w_out[j])
            x = _ffn(x.reshape(t, d), norm_ffn[layer], j, ffn_w_gate, ffn_w_up, ffn_w_down)
            x = x.reshape(b, s, d)
        else:
            x, h2, route, rt, counts = _odd_mixer(
                x, norm_mix[layer], rg_w_in[j], rg_conv_w[j], rg_conv_b[j], rg_gate_a_w[j], rg_gate_a_b[j],
                rg_gate_x_w[j], rg_gate_x_b[j], rg_lambda[j], rg_w_out[j], norm_ffn[layer], moe_router[j])
            dest, blk_expert, has_pad, n_used = _moe_tables(rt, counts, t)
            xg = _dispatch(h2, dest, has_pad)
            y = _experts(xg, blk_expert, n_used, j, moe_w_gate, moe_w_up, moe_w_down)
            x = _combine(x.reshape(t, d), y, route.reshape(t, ROUTE_LANES), dest).reshape(b, s, d)
    return x
```

```python
import functools

import jax
import jax.numpy as jnp
from jax import lax
from jax.experimental import pallas as pl
from jax.experimental.pallas import tpu as pltpu

F32 = jnp.float32
BF16 = jnp.bfloat16

LANES = 128
SUBLANES = 8
VMEM_LIMIT_BYTES = 56 * 1024 * 1024
DMA_THREADS = 2

N_Q_HEADS = 8
N_KV_HEADS = 2
HEAD_DIM = 64
Q_GROUP = N_Q_HEADS // N_KV_HEADS
WINDOW = 128
ATTN_BLOCK = 128
ALIBI_MAX = 8.0
CONV_K_B = 3
CONV_K_C = 4
LRU_HEADS = 8
LRU_C = 8.0
N_EXPERTS = 8
TOP_K = 2
NORM_EPS = 1e-6
NEG_INF = -1e30

SEQ_TILE = 512
EVEN_TILE = 512
FFN_TILE = 512
MOE_ROWS = 512
FF_CHUNK = 1792
WEIGHT_COLS = 512
ROUTE_ROWS = 16
ROUTE_LANES = LANES
ROUTE_E0 = 0
ROUTE_GATE0 = 2
ROUTE_RANK0 = 4
GELU_C0 = 0.7978845608028654
GELU_C1 = 0.044715


def _rms(x, g):
    return x * lax.rsqrt(jnp.mean(x * x, axis=-1, keepdims=True) + NORM_EPS) * g


def _split_bf16(x):
    hi = x.astype(BF16)
    lo = (x - hi.astype(F32)).astype(BF16)
    return hi, lo


def _store_row_tiles(ref, start, x):
    n = x.shape[0]
    for c in range(x.shape[1] // LANES):
        ref[pl.ds(start + c, n, stride=SUBLANES), :] = x[:, c * LANES:(c + 1) * LANES]


def _load_row_tiles(ref, start, n):
    return jnp.concatenate([ref[pl.ds(start + c, n, stride=SUBLANES), :] for c in range(SUBLANES)], axis=1)


def _resident(shape, index_map):
    return pl.BlockSpec(shape, index_map, pipeline_mode=pl.Buffered(1))


def _even_mixer_kernel(sinks_ref, x_ref, g_ref, win_ref, qg_ref, kg_ref, cw_ref, wout_ref, o_ref,
                       kd_ref, vt_ref, cbuf_ref, attn_t_ref):
    ts = x_ref.shape[0]
    q_dim = N_Q_HEADS * HEAD_DIM
    kv_dim = N_KV_HEADS * HEAD_DIM
    cw_dim = cw_ref.shape[1]
    st = pl.program_id(1)

    @pl.when(st == 0)
    def _():
        kd_ref[:, 0:ATTN_BLOCK, :] = jnp.zeros((N_KV_HEADS, ATTN_BLOCK, LANES), BF16)
        vt_ref[:, 0:ATTN_BLOCK] = jnp.zeros((kv_dim, ATTN_BLOCK), BF16)
        cbuf_ref[0:SUBLANES, :] = jnp.zeros((SUBLANES, cw_dim), F32)

    x = x_ref[...]
    h = _rms(x, g_ref[...]).astype(BF16)
    proj = jnp.dot(h, win_ref[...], preferred_element_type=F32)
    q = proj[:, 0:q_dim]
    k = proj[:, q_dim:q_dim + kv_dim]
    v = proj[:, q_dim + kv_dim:q_dim + 2 * kv_dim]
    o0 = q_dim + 2 * kv_dim
    gate_b = proj[:, o0:o0 + cw_dim]
    gate_c = proj[:, o0 + cw_dim:o0 + 2 * cw_dim]
    u = proj[:, o0 + 2 * cw_dim:o0 + 3 * cw_dim]

    ri = lax.broadcasted_iota(jnp.int32, (LANES, LANES), 0)
    ci = lax.broadcasted_iota(jnp.int32, (LANES, LANES), 1)
    half_mean = jnp.where((ri < HEAD_DIM) == (ci < HEAD_DIM), 1.0 / HEAD_DIM, 0.0).astype(BF16)

    def half_mean_square(z):
        hi, lo = _split_bf16(z * z)
        cols = []
        for c in range(z.shape[1] // LANES):
            sl = slice(c * LANES, (c + 1) * LANES)
            cols.append(jnp.dot(hi[:, sl], half_mean, preferred_element_type=F32)
                        + jnp.dot(lo[:, sl], half_mean, preferred_element_type=F32))
        return cols[0] if len(cols) == 1 else jnp.concatenate(cols, axis=1)

    qn = q * lax.rsqrt(half_mean_square(q) + NORM_EPS) * qg_ref[...] * (HEAD_DIM ** -0.5)
    kn = k * lax.rsqrt(half_mean_square(k) + NORM_EPS) * kg_ref[...]

    low_half_t = lax.broadcasted_iota(jnp.int32, (ts, LANES), 1) < HEAD_DIM
    k_sw = pltpu.roll(kn, HEAD_DIM, axis=1)
    kd_ref[0, ATTN_BLOCK:ATTN_BLOCK + ts, :] = jnp.where(low_half_t, kn, k_sw).astype(BF16)
    kd_ref[1, ATTN_BLOCK:ATTN_BLOCK + ts, :] = jnp.where(low_half_t, k_sw, kn).astype(BF16)
    vt_ref[:, ATTN_BLOCK:ATTN_BLOCK + ts] = v.T.astype(BF16)

    low_half = lax.broadcasted_iota(jnp.int32, (ATTN_BLOCK, LANES), 1) < HEAD_DIM
    sj = lax.broadcasted_iota(jnp.int32, (2 * ATTN_BLOCK, ATTN_BLOCK), 0)
    qi = lax.broadcasted_iota(jnp.int32, (2 * ATTN_BLOCK, ATTN_BLOCK), 1)
    diff = ATTN_BLOCK + qi - sj
    in_window = (diff >= 0) & (diff < WINDOW)
    diff_f = diff.astype(F32)
    first_key = jnp.where(st == 0, ATTN_BLOCK, 0)
    in_window_first = in_window & (sj >= first_key)

    def penalty(valid, g):
        cols = []
        for hh in range(Q_GROUP):
            slope = 2.0 ** (-ALIBI_MAX * (g * Q_GROUP + hh + 1) / N_Q_HEADS)
            cols.append(jnp.where(valid, slope * diff_f, -NEG_INF))
        return jnp.concatenate(cols, axis=1)

    pen = [penalty(in_window, g) for g in range(N_KV_HEADS)]
    pen_first = [penalty(in_window_first, g) for g in range(N_KV_HEADS)]
    sink = [jnp.concatenate([jnp.full((1, ATTN_BLOCK), sinks_ref[0, g * Q_GROUP + hh], F32)
                             for hh in range(Q_GROUP)], axis=1) for g in range(N_KV_HEADS)]

    for j in range(ts // ATTN_BLOCK):
        rows = slice(j * ATTN_BLOCK, (j + 1) * ATTN_BLOCK)
        band = slice(j * ATTN_BLOCK, (j + 2) * ATTN_BLOCK)
        for g in range(N_KV_HEADS):
            q_rows = []
            for p in range(Q_GROUP // 2):
                c = g * (Q_GROUP // 2) + p
                qp = qn[rows, c * LANES:(c + 1) * LANES]
                q_rows.append(jnp.where(low_half, qp, 0.0))
                q_rows.append(jnp.where(low_half, 0.0, qp))
            qs = jnp.concatenate(q_rows, axis=0).astype(BF16)
            s = lax.dot_general(kd_ref[g, band, :], qs, (((1,), (1,)), ((), ())),
                                preferred_element_type=F32)
            sh = s - (pen_first[g] if j == 0 else pen[g])
            m = jnp.maximum(jnp.max(sh, axis=0, keepdims=True), sink[g])
            pr = jnp.exp(sh - m)
            denom = jnp.sum(pr, axis=0, keepdims=True) + jnp.exp(sink[g] - m)
            o = jnp.dot(vt_ref[g * HEAD_DIM:(g + 1) * HEAD_DIM, band], pr.astype(BF16),
                        preferred_element_type=F32)
            o = o / denom
            for hh in range(Q_GROUP):
                head = g * Q_GROUP + hh
                attn_t_ref[head * HEAD_DIM:(head + 1) * HEAD_DIM, rows] = (
                    o[:, hh * ATTN_BLOCK:(hh + 1) * ATTN_BLOCK])

    cu = gate_c * u
    cbuf_ref[SUBLANES:SUBLANES + ts, :] = cu
    conv = cw_ref[CONV_K_B - 1:CONV_K_B, :] * cu
    for kk in range(CONV_K_B - 1):
        back = CONV_K_B - 1 - kk
        conv = conv + cw_ref[kk:kk + 1, :] * cbuf_ref[SUBLANES - back:SUBLANES - back + ts, :]
    conv = gate_b * conv

    mix = jnp.concatenate([attn_t_ref[...].T, conv], axis=1).astype(BF16)
    o_ref[...] = x + jnp.dot(mix, wout_ref[...], preferred_element_type=F32)

    kd_ref[:, 0:ATTN_BLOCK, :] = kd_ref[:, ts:ts + ATTN_BLOCK, :]
    vt_ref[:, 0:ATTN_BLOCK] = vt_ref[:, ts:ts + ATTN_BLOCK]
    cbuf_ref[0:SUBLANES, :] = cbuf_ref[ts:ts + SUBLANES, :]


def _even_mixer(x, g, w_in, q_gain, k_gain, sinks, conv_w, w_out):
    b, s, d = x.shape
    ts = EVEN_TILE
    n_in = w_in.shape[1]
    cw_dim = conv_w.shape[1]
    mix_dim = w_out.shape[0]
    qg = jnp.tile(q_gain, N_Q_HEADS).reshape(1, N_Q_HEADS * HEAD_DIM)
    kg = jnp.tile(k_gain, N_KV_HEADS).reshape(1, N_KV_HEADS * HEAD_DIM)
    const = lambda bi, si: (0, 0)
    return pl.pallas_call(
        _even_mixer_kernel,
        out_shape=jax.ShapeDtypeStruct((b, s, d), F32),
        grid=(b, s // ts),
        in_specs=[
            pl.BlockSpec(memory_space=pltpu.SMEM),
            pl.BlockSpec((None, ts, d), lambda bi, si: (bi, si, 0)),
            _resident((1, d), const),
            _resident((d, n_in), const),
            _resident((1, N_Q_HEADS * HEAD_DIM), const),
            _resident((1, N_KV_HEADS * HEAD_DIM), const),
            _resident((CONV_K_B, cw_dim), const),
            _resident((mix_dim, d), const),
        ],
        out_specs=pl.BlockSpec((None, ts, d), lambda bi, si: (bi, si, 0)),
        scratch_shapes=[
            pltpu.VMEM((N_KV_HEADS, ATTN_BLOCK + ts, LANES), BF16),
            pltpu.VMEM((N_KV_HEADS * HEAD_DIM, ATTN_BLOCK + ts), BF16),
            pltpu.VMEM((SUBLANES + ts, cw_dim), F32),
            pltpu.VMEM((N_Q_HEADS * HEAD_DIM, ts), F32),
        ],
        compiler_params=pltpu.CompilerParams(
            dimension_semantics=("arbitrary", "arbitrary"), vmem_limit_bytes=VMEM_LIMIT_BYTES),
        name="even_mixer",
    )(sinks.reshape(1, N_Q_HEADS), x, g.reshape(1, d), w_in.astype(BF16), qg, kg, conv_w,
      w_out.astype(BF16))


def _swiglu_rows(hb, wg_ref, wu_ref, wd_ref):
    d_ff = wg_ref.shape[1]
    acc = None
    for c in range(d_ff // FF_CHUNK):
        cols = slice(c * FF_CHUNK, (c + 1) * FF_CHUNK)
        gate = jnp.dot(hb, wg_ref[:, cols], preferred_element_type=F32)
        up = jnp.dot(hb, wu_ref[:, cols], preferred_element_type=F32)
        act = (gate * (1.0 / (1.0 + jnp.exp(-gate))) * up).astype(BF16)
        part = jnp.dot(act, wd_ref[cols, :], preferred_element_type=F32)
        acc = part if acc is None else acc + part
    return acc


def _swiglu_rows_fetching(hb, w_hbm, w_ref, stage, sem):
    d_ff = w_ref[0].shape[1]
    n = d_ff // WEIGHT_COLS

    def copies(c, s):
        cols = pl.ds(c * WEIGHT_COLS, WEIGHT_COLS)
        srcs = (w_hbm[0].at[:, cols], w_hbm[1].at[:, cols], w_hbm[2].at[cols, :])
        return [pltpu.make_async_copy(src, stage[m].at[s], sem.at[s, m]) for m, src in enumerate(srcs)]

    for cp in copies(0, 0):
        cp.start()
    acc = None
    for c in range(n):
        s = c % 2
        if c + 1 < n:
            for cp in copies(c + 1, 1 - s):
                cp.start()
        for cp in copies(c, s):
            cp.wait()
        cols = slice(c * WEIGHT_COLS, (c + 1) * WEIGHT_COLS)
        w_ref[0][:, cols] = stage[0][s].astype(BF16)
        w_ref[1][:, cols] = stage[1][s].astype(BF16)
        w_ref[2][cols, :] = stage[2][s].astype(BF16)
        gate = jnp.dot(hb, w_ref[0][:, cols], preferred_element_type=F32)
        up = jnp.dot(hb, w_ref[1][:, cols], preferred_element_type=F32)
        act = (gate * (1.0 / (1.0 + jnp.exp(-gate))) * up).astype(BF16)
        part = jnp.dot(act, w_ref[2][cols, :], preferred_element_type=F32)
        acc = part if acc is None else acc + part
    return acc


def _swiglu_weight_scratch(d, d_ff):
    return [
        pltpu.VMEM((d, d_ff), BF16),
        pltpu.VMEM((d, d_ff), BF16),
        pltpu.VMEM((d_ff, d), BF16),
        pltpu.VMEM((2, d, WEIGHT_COLS), F32),
        pltpu.VMEM((2, d, WEIGHT_COLS), F32),
        pltpu.VMEM((2, WEIGHT_COLS, d), F32),
        pltpu.SemaphoreType.DMA((2, 3)),
    ]


def _ffn_kernel(j, x_ref, g_ref, wg_hbm, wu_hbm, wd_hbm, o_ref,
                wg_ref, wu_ref, wd_ref, stage_g, stage_u, stage_d, sem):
    x = x_ref[...]
    hb = _rms(x, g_ref[...]).astype(BF16)
    w_ref = (wg_ref, wu_ref, wd_ref)

    @pl.when(pl.program_id(0) == 0)
    def _():
        o_ref[...] = x + _swiglu_rows_fetching(hb, (wg_hbm.at[j], wu_hbm.at[j], wd_hbm.at[j]), w_ref,
                                               (stage_g, stage_u, stage_d), sem)

    @pl.when(pl.program_id(0) > 0)
    def _():
        o_ref[...] = x + _swiglu_rows(hb, *w_ref)


def _ffn(x, g, j, w_gate, w_up, w_down):
    t, d = x.shape
    d_ff = w_gate.shape[2]
    tm = FFN_TILE
    return pl.pallas_call(
        functools.partial(_ffn_kernel, j),
        out_shape=jax.ShapeDtypeStruct((t, d), F32),
        grid=(t // tm,),
        in_specs=[
            pl.BlockSpec((tm, d), lambda i: (i, 0)),
            _resident((1, d), lambda i: (0, 0)),
            pl.BlockSpec(memory_space=pl.ANY),
            pl.BlockSpec(memory_space=pl.ANY),
            pl.BlockSpec(memory_space=pl.ANY),
        ],
        out_specs=pl.BlockSpec((tm, d), lambda i: (i, 0)),
        scratch_shapes=_swiglu_weight_scratch(d, d_ff),
        compiler_params=pltpu.CompilerParams(
            dimension_semantics=("arbitrary",), vmem_limit_bytes=VMEM_LIMIT_BYTES),
        name="dense_ffn",
    )(x, g.reshape(1, d), w_gate, w_up, w_down)


def _odd_mixer_kernel(x_ref, g_ref, win_ref, cw_ref, cb_ref, gw_ref, gab_ref, gxb_ref, lam_ref, wout_ref,
                      g2_ref, wr_ref, tri_ref, o_ref, h2_ref, route_ref, rt_ref, cnt_out_ref,
                      xbuf_ref, a_ref, b_ref, hs_ref, hcar_ref, cnt_ref):
    nb, tl, d = x_ref.shape
    ts = nb * tl
    w = cw_ref.shape[1]
    hd = w // LRU_HEADS
    st = pl.program_id(0)

    @pl.when(st == 0)
    def _():
        xbuf_ref[:, 0:SUBLANES, :] = jnp.zeros((nb, SUBLANES, w), F32)
        hcar_ref[...] = jnp.zeros(hcar_ref.shape, F32)
        cnt_ref[...] = jnp.zeros(cnt_ref.shape, F32)

    x = x_ref[...].reshape(ts, d)
    h = _rms(x, g_ref[...]).astype(BF16)
    proj = jnp.dot(h, win_ref[...], preferred_element_type=F32)
    y = proj[:, 0:w]
    y = y * (0.5 * (1.0 + jnp.tanh(GELU_C0 * (y + GELU_C1 * (y * y * y)))))
    xb = proj[:, w:2 * w]

    xc_rows = []
    for bi in range(nb):
        xb_seq = xb[bi * tl:(bi + 1) * tl, :]
        xbuf_ref[bi, SUBLANES:SUBLANES + tl, :] = xb_seq
        acc = cw_ref[CONV_K_C - 1:CONV_K_C, :] * xb_seq + cb_ref[...]
        for kk in range(CONV_K_C - 1):
            back = CONV_K_C - 1 - kk
            acc = acc + cw_ref[kk:kk + 1, :] * xbuf_ref[bi, SUBLANES - back:SUBLANES - back + tl, :]
        xc_rows.append(acc)
    xc = jnp.concatenate(xc_rows, axis=0)

    r_cols = []
    i_cols = []
    for hh in range(LRU_HEADS):
        ri = jnp.dot(xc[:, hh * hd:(hh + 1) * hd].astype(BF16), gw_ref[hh], preferred_element_type=F32)
        r_cols.append(ri[:, 0:hd])
        i_cols.append(ri[:, hd:2 * hd])
    r_pre = jnp.concatenate(r_cols, axis=1) + gab_ref[...]
    i_pre = jnp.concatenate(i_cols, axis=1) + gxb_ref[...]
    r = 0.5 * (1.0 + jnp.tanh(0.5 * r_pre))
    ig = 0.5 * (1.0 + jnp.tanh(0.5 * i_pre))
    neg_lam = -lam_ref[...]
    softplus = jnp.maximum(neg_lam, 0.0) + jnp.log1p(jnp.exp(-jnp.abs(neg_lam)))
    log_a = -LRU_C * r * softplus
    a = jnp.exp(log_a)
    mult = jnp.sqrt(jnp.maximum(1.0 - a * a, 0.0))
    row = lax.broadcasted_iota(jnp.int32, (ts, w), 0)
    mult = jnp.where((lax.rem(row, tl) == 0) & (st == 0), 1.0, mult)
    bb = mult * (ig * xc)

    for c in range(w // LANES):
        cols = slice(c * LANES, (c + 1) * LANES)
        for bi in range(nb):
            a_ref[c, pl.ds(bi, tl, stride=nb), :] = a[bi * tl:(bi + 1) * tl, cols]
            b_ref[c, pl.ds(bi, tl, stride=nb), :] = bb[bi * tl:(bi + 1) * tl, cols]

    def step(t, hprev):
        off = pl.multiple_of(t * nb, nb)
        hnew = []
        for c in range(w // LANES):
            hc = a_ref[c, pl.ds(off, nb), :] * hprev[c] + b_ref[c, pl.ds(off, nb), :]
            hs_ref[c, pl.ds(off, nb), :] = hc
            hnew.append(hc)
        return tuple(hnew)

    hlast = lax.fori_loop(0, tl, step, tuple(hcar_ref[c] for c in range(w // LANES)), unroll=4)
    for c in range(w // LANES):
        hcar_ref[c] = hlast[c]
    hs = jnp.concatenate(
        [jnp.concatenate([hs_ref[c, pl.ds(bi, tl, stride=nb), :] for c in range(w // LANES)], axis=1)
         for bi in range(nb)], axis=0)

    out = x + jnp.dot((y * hs).astype(BF16), wout_ref[...], preferred_element_type=F32)
    o_ref[...] = out.reshape(nb, tl, d)
    xbuf_ref[:, 0:SUBLANES, :] = xbuf_ref[:, tl:tl + SUBLANES, :]

    h2 = _rms(out, g2_ref[...])
    for bi in range(nb):
        _store_row_tiles(h2_ref.at[bi], 0, h2[bi * tl:(bi + 1) * tl, :])
    logits = lax.dot_general(wr_ref[...], h2.astype(BF16), (((1,), (1,)), ((), ())),
                             preferred_element_type=F32)
    erow = lax.broadcasted_iota(jnp.int32, (ROUTE_ROWS, ts), 0)
    lg = jnp.where(erow < N_EXPERTS, logits, -jnp.inf)
    m1 = jnp.max(lg, axis=0, keepdims=True)
    i1 = jnp.min(jnp.where(lg == m1, erow, ROUTE_ROWS), axis=0, keepdims=True)
    lg2 = jnp.where(erow == i1, -jnp.inf, lg)
    m2 = jnp.max(lg2, axis=0, keepdims=True)
    i2 = jnp.min(jnp.where(lg2 == m2, erow, ROUTE_ROWS), axis=0, keepdims=True)
    e2 = jnp.exp(m2 - m1)
    gate1 = 1.0 / (1.0 + e2)
    gate2 = e2 / (1.0 + e2)

    first = erow == i1
    second = erow == i2
    ind = jnp.where(first | second, 1.0, 0.0)
    cums = jnp.dot(ind.astype(BF16), tri_ref[...], preferred_element_type=F32)
    rank = cnt_ref[...] + cums - ind
    cnt_ref[...] = cnt_ref[...] + jnp.broadcast_to(cums[:, ts - 1:ts], cums.shape)
    rank1 = jnp.sum(jnp.where(first, rank, 0.0), axis=0, keepdims=True)
    rank2 = jnp.sum(jnp.where(second, rank, 0.0), axis=0, keepdims=True)
    cnt_out_ref[...] = cnt_ref[:, 0:ROUTE_LANES].astype(jnp.int32)

    fields = [None] * SUBLANES
    fields[ROUTE_E0], fields[ROUTE_E0 + 1] = i1.astype(F32), i2.astype(F32)
    fields[ROUTE_GATE0], fields[ROUTE_GATE0 + 1] = gate1, gate2
    fields[ROUTE_RANK0], fields[ROUTE_RANK0 + 1] = rank1, rank2
    zero_row = jnp.zeros((1, ts), F32)
    record_t = jnp.concatenate([zero_row if f is None else f for f in fields], axis=0)
    rt_ref[...] = record_t.astype(jnp.int32)
    route_ref[...] = jnp.concatenate(
        [record_t, jnp.zeros((ROUTE_LANES - SUBLANES, ts), F32)], axis=0).T.reshape(nb, tl, ROUTE_LANES)


def _odd_mixer(x, g, w_in, conv_w, conv_b, ga_w, ga_b, gx_w, gx_b, lam, w_out, g2, w_router):
    b, s, d = x.shape
    assert b == SUBLANES, "one timestep of all sequences must fill one sublane group"
    tl = SEQ_TILE // b
    ts = b * tl
    w = conv_w.shape[1]
    hd = w // LRU_HEADS
    gw = jnp.concatenate([ga_w, gx_w], axis=-1).astype(BF16)
    wr = jnp.pad(w_router.T, ((0, ROUTE_ROWS - N_EXPERTS), (0, 0))).astype(BF16)
    tri = jnp.triu(jnp.ones((ts, ts), BF16))
    nst = s // tl
    const = lambda si: (0, 0)
    row = lambda a: a.reshape(1, -1)
    tile = lambda si: (0, si, 0)
    out, h2, route, rt, counts = pl.pallas_call(
        _odd_mixer_kernel,
        out_shape=(jax.ShapeDtypeStruct((b, s, d), F32),
                   jax.ShapeDtypeStruct((b, s * SUBLANES, LANES), F32),
                   jax.ShapeDtypeStruct((b, s, ROUTE_LANES), F32),
                   jax.ShapeDtypeStruct((nst, SUBLANES, ts), jnp.int32),
                   jax.ShapeDtypeStruct((ROUTE_ROWS, ROUTE_LANES), jnp.int32)),
        grid=(nst,),
        in_specs=[
            pl.BlockSpec((b, tl, d), tile),
            _resident((1, d), const),
            _resident((d, 2 * w), const),
            _resident((CONV_K_C, w), const),
            _resident((1, w), const),
            _resident((LRU_HEADS, hd, 2 * hd), lambda si: (0, 0, 0)),
            _resident((1, w), const),
            _resident((1, w), const),
            _resident((1, w), const),
            _resident((w, d), const),
            _resident((1, d), const),
            _resident((ROUTE_ROWS, d), const),
            _resident((ts, ts), const),
        ],
        out_specs=(pl.BlockSpec((b, tl, d), tile),
                   pl.BlockSpec((b, tl * SUBLANES, LANES), tile),
                   pl.BlockSpec((b, tl, ROUTE_LANES), tile),
                   pl.BlockSpec((None, SUBLANES, ts), lambda si: (si, 0, 0)),
                   pl.BlockSpec((ROUTE_ROWS, ROUTE_LANES), const)),
        scratch_shapes=[
            pltpu.VMEM((b, SUBLANES + tl, w), F32),
            pltpu.VMEM((w // LANES, ts, LANES), F32),
            pltpu.VMEM((w // LANES, ts, LANES), F32),
            pltpu.VMEM((w // LANES, ts, LANES), F32),
            pltpu.VMEM((w // LANES, b, LANES), F32),
            pltpu.VMEM((ROUTE_ROWS, ts), F32),
        ],
        compiler_params=pltpu.CompilerParams(
            dimension_semantics=("arbitrary",), vmem_limit_bytes=VMEM_LIMIT_BYTES),
        name="odd_mixer",
    )(x, row(g), w_in.astype(BF16), conv_w, row(conv_b), gw, row(ga_b), row(gx_b), row(lam),
      w_out.astype(BF16), row(g2), wr, tri)
    rt = rt.reshape(nst, SUBLANES, b, tl).transpose(1, 2, 0, 3).reshape(SUBLANES, b * s)
    return out, h2.reshape(b * s * SUBLANES, LANES), route, rt, counts


def _moe_tables(rt, counts, t):
    r = MOE_ROWS
    tm = SEQ_TILE
    n_blocks = TOP_K * t // r + N_EXPERTS
    counts = counts[0:N_EXPERTS, 0]
    padded = (counts + r - 1) // r * r
    pad_end = jnp.cumsum(padded)
    pad_start = pad_end - padded
    e = rt[ROUTE_E0:ROUTE_E0 + TOP_K, :]
    rank = rt[ROUTE_RANK0:ROUTE_RANK0 + TOP_K, :]
    start = jnp.zeros_like(e)
    for ex in range(N_EXPERTS):
        start = jnp.where(e == ex, pad_start[ex], start)
    dest = start + rank
    blk = jnp.arange(n_blocks, dtype=jnp.int32)
    blk_expert = jnp.minimum(jnp.sum(blk[:, None] * r >= pad_end[None, :], axis=1), N_EXPERTS - 1)
    n_used = pad_end[N_EXPERTS - 1] // r
    last_blk = jnp.where(padded > 0, pad_end // r - 1, -1)
    has_pad = jnp.any(blk[:, None] == last_blk[None, :], axis=1) | (blk >= n_used)
    dest = dest.astype(jnp.int32).reshape(TOP_K, t // tm, tm).transpose(1, 0, 2).reshape(t // tm, TOP_K * tm)
    return dest, blk_expert.astype(jnp.int32), has_pad.astype(jnp.int32), n_used.astype(jnp.int32).reshape(1)


def _row_copy(src, src_row, dst, dst_row, sem):
    return pltpu.make_async_copy(src.at[pl.ds(src_row * SUBLANES, SUBLANES), :],
                                 dst.at[pl.ds(dst_row * SUBLANES, SUBLANES), :], sem)


def _rows_wait(src, dst, n_rows, sem):
    pltpu.make_async_copy(src.at[pl.ds(0, n_rows * SUBLANES), :],
                          dst.at[pl.ds(0, n_rows * SUBLANES), :], sem).wait()


def _dispatch_kernel(pad_ref, dest_hbm, h_ref, xg_hbm, dsm, stage, zbuf, sem_idx, sem_row, sem_zero):
    i = pl.program_id(0)
    n = pl.num_programs(0)
    tm = h_ref.shape[0] // SUBLANES
    r = zbuf.shape[0] // SUBLANES
    n_blocks = pad_ref.shape[0]
    slot = lax.rem(i, 2)

    def idx_copy(step, s):
        return pltpu.make_async_copy(dest_hbm.at[step], dsm.at[pl.ds(s * TOP_K * tm, TOP_K * tm)],
                                     sem_idx.at[s])

    def zero_copy(blk):
        return pltpu.make_async_copy(zbuf, xg_hbm.at[pl.ds(blk * r * SUBLANES, r * SUBLANES), :], sem_zero)

    @pl.when(i == 0)
    def _():
        idx_copy(0, 0).start()
        zbuf[...] = jnp.zeros(zbuf.shape, zbuf.dtype)

        def zero_start(blk, c):
            @pl.when(pad_ref[blk] == 1)
            def _():
                zero_copy(blk).start()
            return c

        def zero_wait(blk, c):
            @pl.when(pad_ref[blk] == 1)
            def _():
                zero_copy(blk).wait()
            return c

        lax.fori_loop(0, n_blocks, zero_start, 0)
        lax.fori_loop(0, n_blocks, zero_wait, 0)

    @pl.when(i + 1 < n)
    def _():
        idx_copy(i + 1, 1 - slot).start()

    idx_copy(i, slot).wait()

    stage_row0 = slot * tm
    stage[pl.ds(stage_row0 * SUBLANES, tm * SUBLANES), :] = h_ref[...]

    def issue(row, c):
        for k in range(TOP_K):
            _row_copy(stage, stage_row0 + row, xg_hbm, dsm[(slot * TOP_K + k) * tm + row],
                      sem_row.at[slot]).start(priority=k % DMA_THREADS)
        return c

    lax.fori_loop(0, tm, issue, 0, unroll=8)

    def wait_slot(s):
        for _ in range(TOP_K):
            _rows_wait(stage, xg_hbm, tm, sem_row.at[s])

    @pl.when(i > 0)
    def _():
        wait_slot(1 - slot)

    @pl.when(i == n - 1)
    def _():
        wait_slot(slot)


def _dispatch(h2, dest, has_pad):
    tm = dest.shape[1] // TOP_K
    t = h2.shape[0] // SUBLANES
    r = MOE_ROWS
    n_rows = TOP_K * t + N_EXPERTS * r
    grid_spec = pltpu.PrefetchScalarGridSpec(
        num_scalar_prefetch=1,
        grid=(t // tm,),
        in_specs=[pl.BlockSpec(memory_space=pl.ANY),
                  pl.BlockSpec((tm * SUBLANES, LANES), lambda i, pad: (i, 0))],
        out_specs=pl.BlockSpec(memory_space=pl.ANY),
        scratch_shapes=[
            pltpu.SMEM((2 * TOP_K * tm,), jnp.int32),
            pltpu.VMEM((2 * tm * SUBLANES, LANES), F32),
            pltpu.VMEM((r * SUBLANES, LANES), F32),
            pltpu.SemaphoreType.DMA((2,)),
            pltpu.SemaphoreType.DMA((2,)),
            pltpu.SemaphoreType.DMA,
        ],
    )
    return pl.pallas_call(
        _dispatch_kernel,
        out_shape=jax.ShapeDtypeStruct((n_rows * SUBLANES, LANES), F32),
        grid_spec=grid_spec,
        compiler_params=pltpu.CompilerParams(dimension_semantics=("arbitrary",)),
        name="moe_dispatch",
    )(has_pad, dest, h2)


def _expert_kernel(j, be_ref, nu_ref, x_ref, wg_hbm, wu_hbm, wd_hbm, y_ref,
                   wg_ref, wu_ref, wd_ref, stage_g, stage_u, stage_d, sem):
    i = pl.program_id(0)
    r = x_ref.shape[0] // SUBLANES
    e = be_ref[i]
    used = i < nu_ref[0]
    new_expert = (i == 0) | (e != be_ref[jnp.maximum(i - 1, 0)])
    w_ref = (wg_ref, wu_ref, wd_ref)

    @pl.when(used & new_expert)
    def _():
        xb = _load_row_tiles(x_ref, 0, r).astype(BF16)
        y = _swiglu_rows_fetching(xb, (wg_hbm.at[j, e], wu_hbm.at[j, e], wd_hbm.at[j, e]), w_ref,
                                  (stage_g, stage_u, stage_d), sem)
        _store_row_tiles(y_ref, 0, y)

    @pl.when(used & jnp.logical_not(new_expert))
    def _():
        xb = _load_row_tiles(x_ref, 0, r).astype(BF16)
        _store_row_tiles(y_ref, 0, _swiglu_rows(xb, *w_ref))

    @pl.when(jnp.logical_not(used))
    def _():
        y_ref[...] = jnp.zeros(y_ref.shape, y_ref.dtype)


def _experts(xg, blk_expert, n_used, j, w_gate, w_up, w_down):
    d, d_ff = w_gate.shape[2], w_gate.shape[3]
    r = MOE_ROWS
    rows = lambda i, be, nu: (i, 0)
    grid_spec = pltpu.PrefetchScalarGridSpec(
        num_scalar_prefetch=2,
        grid=(xg.shape[0] // (r * SUBLANES),),
        in_specs=[
            pl.BlockSpec((r * SUBLANES, LANES), rows),
            pl.BlockSpec(memory_space=pl.ANY),
            pl.BlockSpec(memory_space=pl.ANY),
            pl.BlockSpec(memory_space=pl.ANY),
        ],
        out_specs=pl.BlockSpec((r * SUBLANES, LANES), rows),
        scratch_shapes=_swiglu_weight_scratch(d, d_ff),
    )
    return pl.pallas_call(
        functools.partial(_expert_kernel, j),
        out_shape=jax.ShapeDtypeStruct(xg.shape, F32),
        grid_spec=grid_spec,
        compiler_params=pltpu.CompilerParams(
            dimension_semantics=("arbitrary",), vmem_limit_bytes=VMEM_LIMIT_BYTES),
        name="moe_experts",
    )(blk_expert, n_used, xg, w_gate, w_up, w_down)


def _combine_kernel(dest_hbm, y_hbm, x_ref, route_ref, o_ref, dsm, ybuf, sem_idx, sem_row):
    i = pl.program_id(0)
    n = pl.num_programs(0)
    tm = x_ref.shape[0]
    slot = lax.rem(i, 2)

    def idx_copy(step):
        s3 = lax.rem(step, 3)
        return pltpu.make_async_copy(dest_hbm.at[step], dsm.at[pl.ds(s3 * TOP_K * tm, TOP_K * tm)],
                                     sem_idx.at[s3])

    def request_rows(step):
        s2 = lax.rem(step, 2)
        table = lax.rem(step, 3) * TOP_K * tm

        def issue(row, c):
            for k in range(TOP_K):
                _row_copy(y_hbm, dsm[table + k * tm + row], ybuf, (s2 * TOP_K + k) * tm + row,
                          sem_row.at[s2]).start(priority=k % DMA_THREADS)
            return c

        lax.fori_loop(0, tm, issue, 0, unroll=8)

    @pl.when(i == 0)
    def _():
        idx_copy(0).start()
        if_next = n > 1

        @pl.when(if_next)
        def _():
            idx_copy(1).start()

        idx_copy(0).wait()
        request_rows(0)

    @pl.when(i + 2 < n)
    def _():
        idx_copy(i + 2).start()

    @pl.when(i + 1 < n)
    def _():
        idx_copy(i + 1).wait()
        request_rows(i + 1)

    for _ in range(TOP_K):
        _rows_wait(y_hbm, ybuf, tm, sem_row.at[slot])
    rt = route_ref[...]
    moe = None
    for k in range(TOP_K):
        yk = _load_row_tiles(ybuf, (slot * TOP_K + k) * tm * SUBLANES, tm)
        term = yk * rt[:, ROUTE_GATE0 + k:ROUTE_GATE0 + k + 1]
        moe = term if moe is None else moe + term
    o_ref[...] = x_ref[...] + moe


def _combine(x, y, route, dest):
    t, d = x.shape
    tm = dest.shape[1] // TOP_K
    return pl.pallas_call(
        _combine_kernel,
        out_shape=jax.ShapeDtypeStruct((t, d), F32),
        grid=(t // tm,),
        in_specs=[
            pl.BlockSpec(memory_space=pl.ANY),
            pl.BlockSpec(memory_space=pl.ANY),
            pl.BlockSpec((tm, d), lambda i: (i, 0)),
            pl.BlockSpec((tm, ROUTE_LANES), lambda i: (i, 0)),
        ],
        out_specs=pl.BlockSpec((tm, d), lambda i: (i, 0)),
        scratch_shapes=[
            pltpu.SMEM((3 * TOP_K * tm,), jnp.int32),
            pltpu.VMEM((2 * TOP_K * tm * SUBLANES, LANES), F32),
            pltpu.SemaphoreType.DMA((3,)),
            pltpu.SemaphoreType.DMA((2,)),
        ],
        compiler_params=pltpu.CompilerParams(
            dimension_semantics=("arbitrary",), vmem_limit_bytes=VMEM_LIMIT_BYTES),
        name="moe_combine",
    )(dest, y, x, route)


def kernel(x, norm_mix, norm_ffn, hy_w_in, hy_q_gain, hy_k_gain, hy_sinks, hy_conv_w, hy_w_out, rg_w_in, rg_conv_w, rg_conv_b, rg_gate_a_w, rg_gate_a_b, rg_gate_x_w, rg_gate_x_b, rg_lambda, rg_w_out, ffn_w_gate, ffn_w_up, ffn_w_down, moe_router, moe_w_gate, moe_w_up, moe_w_down):
    b, s, d = x.shape
    t = b * s
    depth = norm_mix.shape[0]
    for layer in range(depth):
        j = layer // 2
        if layer % 2 == 0:
            x = _even_mixer(x, norm_mix[layer], hy_w_in[j], hy_q_gain[j], hy_k_gain[j], hy_sinks[j],
                            hy_conv_w[j], hy_w_out[j])
            x = _ffn(x.reshape(t, d), norm_ffn[layer], j, ffn_w_gate, ffn_w_up, ffn_w_down)
            x = x.reshape(b, s, d)
        else:
            x, h2, route, rt, counts = _odd_mixer(
                x, norm_mix[layer], rg_w_in[j], rg_conv_w[j], rg_conv_b[j], rg_gate_a_w[j], rg_gate_a_b[j],
                rg_gate_x_w[j], rg_gate_x_b[j], rg_lambda[j], rg_w_out[j], norm_ffn[layer], moe_router[j])
            dest, blk_expert, has_pad, n_used = _moe_tables(rt, counts, t)
            xg = _dispatch(h2, dest, has_pad)
            y = _experts(xg, blk_expert, n_used, j, moe_w_gate, moe_w_up, moe_w_down)
            x = _combine(x.reshape(t, d), y, route.reshape(t, ROUTE_LANES), dest).reshape(b, s, d)
    return x
```

```python
import functools

import jax
import jax.numpy as jnp
from jax import lax
from jax.experimental import pallas as pl
from jax.experimental.pallas import tpu as pltpu

F32 = jnp.float32
BF16 = jnp.bfloat16

LANES = 128
SUBLANES = 8
VMEM_LIMIT_BYTES = 56 * 1024 * 1024
DMA_THREADS = 2
ROW_DMA_UNROLL = 32

N_Q_HEADS = 8
N_KV_HEADS = 2
HEAD_DIM = 64
Q_GROUP = N_Q_HEADS // N_KV_HEADS
WINDOW = 128
ATTN_BLOCK = 128
ALIBI_MAX = 8.0
CONV_K_B = 3
CONV_K_C = 4
LRU_HEADS = 8
LRU_C = 8.0
N_EXPERTS = 8
TOP_K = 2
NORM_EPS = 1e-6
NEG_INF = -1e30

SEQ_TILE = 512
MOVE_TILE = 1024
EVEN_TILE = 512
FFN_TILE = 512
MOE_ROWS = 512
FF_CHUNK = 1792
WEIGHT_COLS = 512
ROUTE_ROWS = 16
ROUTE_LANES = LANES
ROUTE_E0 = 0
ROUTE_GATE0 = 2
ROUTE_RANK0 = 4
GELU_C0 = 0.7978845608028654
GELU_C1 = 0.044715


def _rms(x, g):
    return x * lax.rsqrt(jnp.mean(x * x, axis=-1, keepdims=True) + NORM_EPS) * g


def _split_bf16(x):
    hi = x.astype(BF16)
    lo = (x - hi.astype(F32)).astype(BF16)
    return hi, lo


def _store_row_tiles(ref, start, x):
    n = x.shape[0]
    for c in range(x.shape[1] // LANES):
        ref[pl.ds(start + c, n, stride=SUBLANES), :] = x[:, c * LANES:(c + 1) * LANES]


def _load_row_tiles(ref, start, n):
    return jnp.concatenate([ref[pl.ds(start + c, n, stride=SUBLANES), :] for c in range(SUBLANES)], axis=1)


def _resident(shape, index_map):
    return pl.BlockSpec(shape, index_map, pipeline_mode=pl.Buffered(1))


def _even_mixer_kernel(sinks_ref, x_ref, g_ref, win_ref, qg_ref, kg_ref, cw_ref, wout_ref, o_ref,
                       kd_ref, vt_ref, cbuf_ref, attn_t_ref):
    ts = x_ref.shape[0]
    q_dim = N_Q_HEADS * HEAD_DIM
    kv_dim = N_KV_HEADS * HEAD_DIM
    cw_dim = cw_ref.shape[1]
    st = pl.program_id(1)

    @pl.when(st == 0)
    def _():
        kd_ref[:, 0:ATTN_BLOCK, :] = jnp.zeros((N_KV_HEADS, ATTN_BLOCK, LANES), BF16)
        vt_ref[:, 0:ATTN_BLOCK] = jnp.zeros((kv_dim, ATTN_BLOCK), BF16)
        cbuf_ref[0:SUBLANES, :] = jnp.zeros((SUBLANES, cw_dim), F32)

    x = x_ref[...]
    h = _rms(x, g_ref[...]).astype(BF16)
    proj = jnp.dot(h, win_ref[...], preferred_element_type=F32)
    q = proj[:, 0:q_dim]
    k = proj[:, q_dim:q_dim + kv_dim]
    v = proj[:, q_dim + kv_dim:q_dim + 2 * kv_dim]
    o0 = q_dim + 2 * kv_dim
    gate_b = proj[:, o0:o0 + cw_dim]
    gate_c = proj[:, o0 + cw_dim:o0 + 2 * cw_dim]
    u = proj[:, o0 + 2 * cw_dim:o0 + 3 * cw_dim]

    ri = lax.broadcasted_iota(jnp.int32, (LANES, LANES), 0)
    ci = lax.broadcasted_iota(jnp.int32, (LANES, LANES), 1)
    half_mean = jnp.where((ri < HEAD_DIM) == (ci < HEAD_DIM), 1.0 / HEAD_DIM, 0.0).astype(BF16)

    def half_mean_square(z):
        hi, lo = _split_bf16(z * z)
        cols = []
        for c in range(z.shape[1] // LANES):
            sl = slice(c * LANES, (c + 1) * LANES)
            cols.append(jnp.dot(hi[:, sl], half_mean, preferred_element_type=F32)
                        + jnp.dot(lo[:, sl], half_mean, preferred_element_type=F32))
        return cols[0] if len(cols) == 1 else jnp.concatenate(cols, axis=1)

    qn = q * lax.rsqrt(half_mean_square(q) + NORM_EPS) * qg_ref[...] * (HEAD_DIM ** -0.5)
    kn = k * lax.rsqrt(half_mean_square(k) + NORM_EPS) * kg_ref[...]

    low_half_t = lax.broadcasted_iota(jnp.int32, (ts, LANES), 1) < HEAD_DIM
    k_sw = pltpu.roll(kn, HEAD_DIM, axis=1)
    kd_ref[0, ATTN_BLOCK:ATTN_BLOCK + ts, :] = jnp.where(low_half_t, kn, k_sw).astype(BF16)
    kd_ref[1, ATTN_BLOCK:ATTN_BLOCK + ts, :] = jnp.where(low_half_t, k_sw, kn).astype(BF16)
    vt_ref[:, ATTN_BLOCK:ATTN_BLOCK + ts] = v.T.astype(BF16)

    low_half = lax.broadcasted_iota(jnp.int32, (ATTN_BLOCK, LANES), 1) < HEAD_DIM
    sj = lax.broadcasted_iota(jnp.int32, (2 * ATTN_BLOCK, ATTN_BLOCK), 0)
    qi = lax.broadcasted_iota(jnp.int32, (2 * ATTN_BLOCK, ATTN_BLOCK), 1)
    diff = ATTN_BLOCK + qi - sj
    in_window = (diff >= 0) & (diff < WINDOW)
    diff_f = diff.astype(F32)
    first_key = jnp.where(st == 0, ATTN_BLOCK, 0)
    in_window_first = in_window & (sj >= first_key)

    def penalty(valid, g):
        cols = []
        for hh in range(Q_GROUP):
            slope = 2.0 ** (-ALIBI_MAX * (g * Q_GROUP + hh + 1) / N_Q_HEADS)
            cols.append(jnp.where(valid, slope * diff_f, -NEG_INF))
        return jnp.concatenate(cols, axis=1)

    pen = [penalty(in_window, g) for g in range(N_KV_HEADS)]
    pen_first = [penalty(in_window_first, g) for g in range(N_KV_HEADS)]
    sink = [jnp.concatenate([jnp.full((1, ATTN_BLOCK), sinks_ref[0, g * Q_GROUP + hh], F32)
                             for hh in range(Q_GROUP)], axis=1) for g in range(N_KV_HEADS)]

    for j in range(ts // ATTN_BLOCK):
        rows = slice(j * ATTN_BLOCK, (j + 1) * ATTN_BLOCK)
        band = slice(j * ATTN_BLOCK, (j + 2) * ATTN_BLOCK)
        for g in range(N_KV_HEADS):
            q_rows = []
            for p in range(Q_GROUP // 2):
                c = g * (Q_GROUP // 2) + p
                qp = qn[rows, c * LANES:(c + 1) * LANES]
                q_rows.append(jnp.where(low_half, qp, 0.0))
                q_rows.append(jnp.where(low_half, 0.0, qp))
            qs = jnp.concatenate(q_rows, axis=0).astype(BF16)
            s = lax.dot_general(kd_ref[g, band, :], qs, (((1,), (1,)), ((), ())),
                                preferred_element_type=F32)
            sh = s - (pen_first[g] if j == 0 else pen[g])
            m = jnp.maximum(jnp.max(sh, axis=0, keepdims=True), sink[g])
            pr = jnp.exp(sh - m)
            denom = jnp.sum(pr, axis=0, keepdims=True) + jnp.exp(sink[g] - m)
            o = jnp.dot(vt_ref[g * HEAD_DIM:(g + 1) * HEAD_DIM, band], pr.astype(BF16),
                        preferred_element_type=F32)
            o = o / denom
            for hh in range(Q_GROUP):
                head = g * Q_GROUP + hh
                attn_t_ref[head * HEAD_DIM:(head + 1) * HEAD_DIM, rows] = (
                    o[:, hh * ATTN_BLOCK:(hh + 1) * ATTN_BLOCK])

    cu = gate_c * u
    cbuf_ref[SUBLANES:SUBLANES + ts, :] = cu
    conv = cw_ref[CONV_K_B - 1:CONV_K_B, :] * cu
    for kk in range(CONV_K_B - 1):
        back = CONV_K_B - 1 - kk
        conv = conv + cw_ref[kk:kk + 1, :] * cbuf_ref[SUBLANES - back:SUBLANES - back + ts, :]
    conv = gate_b * conv

    mix = jnp.concatenate([attn_t_ref[...].T, conv], axis=1).astype(BF16)
    o_ref[...] = x + jnp.dot(mix, wout_ref[...], preferred_element_type=F32)

    kd_ref[:, 0:ATTN_BLOCK, :] = kd_ref[:, ts:ts + ATTN_BLOCK, :]
    vt_ref[:, 0:ATTN_BLOCK] = vt_ref[:, ts:ts + ATTN_BLOCK]
    cbuf_ref[0:SUBLANES, :] = cbuf_ref[ts:ts + SUBLANES, :]


def _even_mixer(x, g, w_in, q_gain, k_gain, sinks, conv_w, w_out):
    b, s, d = x.shape
    ts = EVEN_TILE
    n_in = w_in.shape[1]
    cw_dim = conv_w.shape[1]
    mix_dim = w_out.shape[0]
    qg = jnp.tile(q_gain, N_Q_HEADS).reshape(1, N_Q_HEADS * HEAD_DIM)
    kg = jnp.tile(k_gain, N_KV_HEADS).reshape(1, N_KV_HEADS * HEAD_DIM)
    const = lambda bi, si: (0, 0)
    return pl.pallas_call(
        _even_mixer_kernel,
        out_shape=jax.ShapeDtypeStruct((b, s, d), F32),
        grid=(b, s // ts),
        in_specs=[
            pl.BlockSpec(memory_space=pltpu.SMEM),
            pl.BlockSpec((None, ts, d), lambda bi, si: (bi, si, 0)),
            _resident((1, d), const),
            _resident((d, n_in), const),
            _resident((1, N_Q_HEADS * HEAD_DIM), const),
            _resident((1, N_KV_HEADS * HEAD_DIM), const),
            _resident((CONV_K_B, cw_dim), const),
            _resident((mix_dim, d), const),
        ],
        out_specs=pl.BlockSpec((None, ts, d), lambda bi, si: (bi, si, 0)),
        scratch_shapes=[
            pltpu.VMEM((N_KV_HEADS, ATTN_BLOCK + ts, LANES), BF16),
            pltpu.VMEM((N_KV_HEADS * HEAD_DIM, ATTN_BLOCK + ts), BF16),
            pltpu.VMEM((SUBLANES + ts, cw_dim), F32),
            pltpu.VMEM((N_Q_HEADS * HEAD_DIM, ts), F32),
        ],
        compiler_params=pltpu.CompilerParams(
            dimension_semantics=("arbitrary", "arbitrary"), vmem_limit_bytes=VMEM_LIMIT_BYTES),
        name="even_mixer",
    )(sinks.reshape(1, N_Q_HEADS), x, g.reshape(1, d), w_in.astype(BF16), qg, kg, conv_w,
      w_out.astype(BF16))


def _swiglu_rows(hb, wg_ref, wu_ref, wd_ref):
    d_ff = wg_ref.shape[1]
    acc = None
    for c in range(d_ff // FF_CHUNK):
        cols = slice(c * FF_CHUNK, (c + 1) * FF_CHUNK)
        gate = jnp.dot(hb, wg_ref[:, cols], preferred_element_type=F32)
        up = jnp.dot(hb, wu_ref[:, cols], preferred_element_type=F32)
        act = (gate * (1.0 / (1.0 + jnp.exp(-gate))) * up).astype(BF16)
        part = jnp.dot(act, wd_ref[cols, :], preferred_element_type=F32)
        acc = part if acc is None else acc + part
    return acc


def _swiglu_rows_fetching(hb, w_hbm, w_ref, stage, sem):
    d_ff = w_ref[0].shape[1]
    n = d_ff // WEIGHT_COLS

    def copies(c, s):
        cols = pl.ds(c * WEIGHT_COLS, WEIGHT_COLS)
        srcs = (w_hbm[0].at[:, cols], w_hbm[1].at[:, cols], w_hbm[2].at[cols, :])
        return [pltpu.make_async_copy(src, stage[m].at[s], sem.at[s, m]) for m, src in enumerate(srcs)]

    for cp in copies(0, 0):
        cp.start()
    acc = None
    for c in range(n):
        s = c % 2
        if c + 1 < n:
            for cp in copies(c + 1, 1 - s):
                cp.start()
        for cp in copies(c, s):
            cp.wait()
        cols = slice(c * WEIGHT_COLS, (c + 1) * WEIGHT_COLS)
        w_ref[0][:, cols] = stage[0][s].astype(BF16)
        w_ref[1][:, cols] = stage[1][s].astype(BF16)
        w_ref[2][cols, :] = stage[2][s].astype(BF16)
        gate = jnp.dot(hb, w_ref[0][:, cols], preferred_element_type=F32)
        up = jnp.dot(hb, w_ref[1][:, cols], preferred_element_type=F32)
        act = (gate * (1.0 / (1.0 + jnp.exp(-gate))) * up).astype(BF16)
        part = jnp.dot(act, w_ref[2][cols, :], preferred_element_type=F32)
        acc = part if acc is None else acc + part
    return acc


def _swiglu_weight_scratch(d, d_ff):
    return [
        pltpu.VMEM((d, d_ff), BF16),
        pltpu.VMEM((d, d_ff), BF16),
        pltpu.VMEM((d_ff, d), BF16),
        pltpu.VMEM((2, d, WEIGHT_COLS), F32),
        pltpu.VMEM((2, d, WEIGHT_COLS), F32),
        pltpu.VMEM((2, WEIGHT_COLS, d), F32),
        pltpu.SemaphoreType.DMA((2, 3)),
    ]


def _ffn_kernel(j, x_ref, g_ref, wg_hbm, wu_hbm, wd_hbm, o_ref,
                wg_ref, wu_ref, wd_ref, stage_g, stage_u, stage_d, sem):
    x = x_ref[...]
    hb = _rms(x, g_ref[...]).astype(BF16)
    w_ref = (wg_ref, wu_ref, wd_ref)

    @pl.when(pl.program_id(0) == 0)
    def _():
        o_ref[...] = x + _swiglu_rows_fetching(hb, (wg_hbm.at[j], wu_hbm.at[j], wd_hbm.at[j]), w_ref,
                                               (stage_g, stage_u, stage_d), sem)

    @pl.when(pl.program_id(0) > 0)
    def _():
        o_ref[...] = x + _swiglu_rows(hb, *w_ref)


def _ffn(x, g, j, w_gate, w_up, w_down):
    t, d = x.shape
    d_ff = w_gate.shape[2]
    tm = FFN_TILE
    return pl.pallas_call(
        functools.partial(_ffn_kernel, j),
        out_shape=jax.ShapeDtypeStruct((t, d), F32),
        grid=(t // tm,),
        in_specs=[
            pl.BlockSpec((tm, d), lambda i: (i, 0)),
            _resident((1, d), lambda i: (0, 0)),
            pl.BlockSpec(memory_space=pl.ANY),
            pl.BlockSpec(memory_space=pl.ANY),
            pl.BlockSpec(memory_space=pl.ANY),
        ],
        out_specs=pl.BlockSpec((tm, d), lambda i: (i, 0)),
        scratch_shapes=_swiglu_weight_scratch(d, d_ff),
        compiler_params=pltpu.CompilerParams(
            dimension_semantics=("arbitrary",), vmem_limit_bytes=VMEM_LIMIT_BYTES),
        name="dense_ffn",
    )(x, g.reshape(1, d), w_gate, w_up, w_down)


def _odd_mixer_kernel(x_ref, g_ref, win_ref, cw_ref, cb_ref, gw_ref, gab_ref, gxb_ref, lam_ref, wout_ref,
                      g2_ref, wr_ref, tri_ref, o_ref, h2_ref, route_ref, rt_ref, cnt_out_ref,
                      xbuf_ref, a_ref, b_ref, hs_ref, hcar_ref, cnt_ref):
    nb, tl, d = x_ref.shape
    ts = nb * tl
    w = cw_ref.shape[1]
    hd = w // LRU_HEADS
    st = pl.program_id(0)

    @pl.when(st == 0)
    def _():
        xbuf_ref[:, 0:SUBLANES, :] = jnp.zeros((nb, SUBLANES, w), F32)
        hcar_ref[...] = jnp.zeros(hcar_ref.shape, F32)
        cnt_ref[...] = jnp.zeros(cnt_ref.shape, F32)

    x = x_ref[...].reshape(ts, d)
    h = _rms(x, g_ref[...]).astype(BF16)
    proj = jnp.dot(h, win_ref[...], preferred_element_type=F32)
    y = proj[:, 0:w]
    half_y = 0.5 * y
    y = half_y + half_y * jnp.tanh(y * (GELU_C0 + (GELU_C0 * GELU_C1) * (y * y)))
    xb = proj[:, w:2 * w]

    xc_rows = []
    for bi in range(nb):
        xb_seq = xb[bi * tl:(bi + 1) * tl, :]
        xbuf_ref[bi, SUBLANES:SUBLANES + tl, :] = xb_seq
        acc = cw_ref[CONV_K_C - 1:CONV_K_C, :] * xb_seq + cb_ref[...]
        for kk in range(CONV_K_C - 1):
            back = CONV_K_C - 1 - kk
            acc = acc + cw_ref[kk:kk + 1, :] * xbuf_ref[bi, SUBLANES - back:SUBLANES - back + tl, :]
        xc_rows.append(acc)
    xc = jnp.concatenate(xc_rows, axis=0)

    r_cols = []
    i_cols = []
    for hh in range(LRU_HEADS):
        ri = jnp.dot(xc[:, hh * hd:(hh + 1) * hd].astype(BF16), gw_ref[hh], preferred_element_type=F32)
        r_cols.append(ri[:, 0:hd])
        i_cols.append(ri[:, hd:2 * hd])
    tanh_r = jnp.tanh(jnp.concatenate(r_cols, axis=1) + gab_ref[...])
    ig = 0.5 + 0.5 * jnp.tanh(jnp.concatenate(i_cols, axis=1) + gxb_ref[...])
    neg_lam = -lam_ref[...]
    softplus = jnp.maximum(neg_lam, 0.0) + jnp.log1p(jnp.exp(-jnp.abs(neg_lam)))
    half_rate = (-0.5 * LRU_C) * softplus
    log_a = half_rate + half_rate * tanh_r
    a = jnp.exp(log_a)
    mult = jnp.sqrt(jnp.maximum(1.0 - a * a, 0.0))
    row = lax.broadcasted_iota(jnp.int32, (ts, w), 0)
    mult = jnp.where((lax.rem(row, tl) == 0) & (st == 0), 1.0, mult)
    bb = mult * (ig * xc)

    for c in range(w // LANES):
        cols = slice(c * LANES, (c + 1) * LANES)
        for bi in range(nb):
            a_ref[c, pl.ds(bi, tl, stride=nb), :] = a[bi * tl:(bi + 1) * tl, cols]
            b_ref[c, pl.ds(bi, tl, stride=nb), :] = bb[bi * tl:(bi + 1) * tl, cols]

    def step(t, hprev):
        off = pl.multiple_of(t * nb, nb)
        hnew = []
        for c in range(w // LANES):
            hc = a_ref[c, pl.ds(off, nb), :] * hprev[c] + b_ref[c, pl.ds(off, nb), :]
            hs_ref[c, pl.ds(off, nb), :] = hc
            hnew.append(hc)
        return tuple(hnew)

    hlast = lax.fori_loop(0, tl, step, tuple(hcar_ref[c] for c in range(w // LANES)), unroll=4)
    for c in range(w // LANES):
        hcar_ref[c] = hlast[c]
    hs = jnp.concatenate(
        [jnp.concatenate([hs_ref[c, pl.ds(bi, tl, stride=nb), :] for c in range(w // LANES)], axis=1)
         for bi in range(nb)], axis=0)

    out = x + jnp.dot((y * hs).astype(BF16), wout_ref[...], preferred_element_type=F32)
    o_ref[...] = out.reshape(nb, tl, d)
    xbuf_ref[:, 0:SUBLANES, :] = xbuf_ref[:, tl:tl + SUBLANES, :]

    h2 = _rms(out, g2_ref[...])
    for bi in range(nb):
        _store_row_tiles(h2_ref.at[bi], 0, h2[bi * tl:(bi + 1) * tl, :])
    logits = lax.dot_general(wr_ref[...], h2.astype(BF16), (((1,), (1,)), ((), ())),
                             preferred_element_type=F32)
    erow = lax.broadcasted_iota(jnp.int32, (ROUTE_ROWS, ts), 0)
    lg = jnp.where(erow < N_EXPERTS, logits, -jnp.inf)
    m1 = jnp.max(lg, axis=0, keepdims=True)
    i1 = jnp.min(jnp.where(lg == m1, erow, ROUTE_ROWS), axis=0, keepdims=True)
    lg2 = jnp.where(erow == i1, -jnp.inf, lg)
    m2 = jnp.max(lg2, axis=0, keepdims=True)
    i2 = jnp.min(jnp.where(lg2 == m2, erow, ROUTE_ROWS), axis=0, keepdims=True)
    e2 = jnp.exp(m2 - m1)
    gate1 = 1.0 / (1.0 + e2)
    gate2 = e2 / (1.0 + e2)

    first = erow == i1
    second = erow == i2
    ind = jnp.where(first | second, 1.0, 0.0)
    cums = jnp.dot(ind.astype(BF16), tri_ref[...], preferred_element_type=F32)
    rank = cnt_ref[...] + cums - ind
    cnt_ref[...] = cnt_ref[...] + jnp.broadcast_to(cums[:, ts - 1:ts], cums.shape)
    rank1 = jnp.sum(jnp.where(first, rank, 0.0), axis=0, keepdims=True)
    rank2 = jnp.sum(jnp.where(second, rank, 0.0), axis=0, keepdims=True)
    cnt_out_ref[...] = cnt_ref[:, 0:ROUTE_LANES].astype(jnp.int32)

    fields = [None] * SUBLANES
    fields[ROUTE_E0], fields[ROUTE_E0 + 1] = i1.astype(F32), i2.astype(F32)
    fields[ROUTE_GATE0], fields[ROUTE_GATE0 + 1] = gate1, gate2
    fields[ROUTE_RANK0], fields[ROUTE_RANK0 + 1] = rank1, rank2
    zero_row = jnp.zeros((1, ts), F32)
    record_t = jnp.concatenate([zero_row if f is None else f for f in fields], axis=0)
    rt_ref[...] = record_t.astype(jnp.int32)
    route_ref[...] = jnp.concatenate(
        [record_t, jnp.zeros((ROUTE_LANES - SUBLANES, ts), F32)], axis=0).T.reshape(nb, tl, ROUTE_LANES)


def _odd_mixer(x, g, w_in, conv_w, conv_b, ga_w, ga_b, gx_w, gx_b, lam, w_out, g2, w_router):
    b, s, d = x.shape
    assert b == SUBLANES, "one timestep of all sequences must fill one sublane group"
    tl = SEQ_TILE // b
    ts = b * tl
    w = conv_w.shape[1]
    hd = w // LRU_HEADS
    gw = (0.5 * jnp.concatenate([ga_w, gx_w], axis=-1)).astype(BF16)
    ga_b, gx_b = 0.5 * ga_b, 0.5 * gx_b
    wr = jnp.pad(w_router.T, ((0, ROUTE_ROWS - N_EXPERTS), (0, 0))).astype(BF16)
    tri = jnp.triu(jnp.ones((ts, ts), BF16))
    nst = s // tl
    const = lambda si: (0, 0)
    row = lambda a: a.reshape(1, -1)
    tile = lambda si: (0, si, 0)
    out, h2, route, rt, counts = pl.pallas_call(
        _odd_mixer_kernel,
        out_shape=(jax.ShapeDtypeStruct((b, s, d), F32),
                   jax.ShapeDtypeStruct((b, s * SUBLANES, LANES), F32),
                   jax.ShapeDtypeStruct((b, s, ROUTE_LANES), F32),
                   jax.ShapeDtypeStruct((nst, SUBLANES, ts), jnp.int32),
                   jax.ShapeDtypeStruct((ROUTE_ROWS, ROUTE_LANES), jnp.int32)),
        grid=(nst,),
        in_specs=[
            pl.BlockSpec((b, tl, d), tile),
            _resident((1, d), const),
            _resident((d, 2 * w), const),
            _resident((CONV_K_C, w), const),
            _resident((1, w), const),
            _resident((LRU_HEADS, hd, 2 * hd), lambda si: (0, 0, 0)),
            _resident((1, w), const),
            _resident((1, w), const),
            _resident((1, w), const),
            _resident((w, d), const),
            _resident((1, d), const),
            _resident((ROUTE_ROWS, d), const),
            _resident((ts, ts), const),
        ],
        out_specs=(pl.BlockSpec((b, tl, d), tile),
                   pl.BlockSpec((b, tl * SUBLANES, LANES), tile),
                   pl.BlockSpec((b, tl, ROUTE_LANES), tile),
                   pl.BlockSpec((None, SUBLANES, ts), lambda si: (si, 0, 0)),
                   pl.BlockSpec((ROUTE_ROWS, ROUTE_LANES), const)),
        scratch_shapes=[
            pltpu.VMEM((b, SUBLANES + tl, w), F32),
            pltpu.VMEM((w // LANES, ts, LANES), F32),
            pltpu.VMEM((w // LANES, ts, LANES), F32),
            pltpu.VMEM((w // LANES, ts, LANES), F32),
            pltpu.VMEM((w // LANES, b, LANES), F32),
            pltpu.VMEM((ROUTE_ROWS, ts), F32),
        ],
        compiler_params=pltpu.CompilerParams(
            dimension_semantics=("arbitrary",), vmem_limit_bytes=VMEM_LIMIT_BYTES),
        name="odd_mixer",
    )(x, row(g), w_in.astype(BF16), conv_w, row(conv_b), gw, row(ga_b), row(gx_b), row(lam),
      w_out.astype(BF16), row(g2), wr, tri)
    rt = rt.reshape(nst, SUBLANES, b, tl).transpose(1, 2, 0, 3).reshape(SUBLANES, b * s)
    return out, h2.reshape(b * s * SUBLANES, LANES), route, rt, counts


def _moe_tables(rt, counts, t):
    r = MOE_ROWS
    tm = MOVE_TILE
    n_blocks = TOP_K * t // r + N_EXPERTS
    counts = counts[0:N_EXPERTS, 0]
    padded = (counts + r - 1) // r * r
    pad_end = jnp.cumsum(padded)
    pad_start = pad_end - padded
    e = rt[ROUTE_E0:ROUTE_E0 + TOP_K, :]
    rank = rt[ROUTE_RANK0:ROUTE_RANK0 + TOP_K, :]
    start = jnp.zeros_like(e)
    for ex in range(N_EXPERTS):
        start = jnp.where(e == ex, pad_start[ex], start)
    dest = start + rank
    blk = jnp.arange(n_blocks, dtype=jnp.int32)
    blk_expert = jnp.minimum(jnp.sum(blk[:, None] * r >= pad_end[None, :], axis=1), N_EXPERTS - 1)
    n_used = pad_end[N_EXPERTS - 1] // r
    last_blk = jnp.where(padded > 0, pad_end // r - 1, -1)
    has_pad = jnp.any(blk[:, None] == last_blk[None, :], axis=1) | (blk >= n_used)
    dest = dest.astype(jnp.int32).reshape(TOP_K, t // tm, tm).transpose(1, 0, 2).reshape(t // tm, TOP_K * tm)
    return dest, blk_expert.astype(jnp.int32), has_pad.astype(jnp.int32), n_used.astype(jnp.int32).reshape(1)


def _row_copy(src, src_row, dst, dst_row, sem):
    return pltpu.make_async_copy(src.at[pl.ds(src_row * SUBLANES, SUBLANES), :],
                                 dst.at[pl.ds(dst_row * SUBLANES, SUBLANES), :], sem)


def _rows_wait(src, dst, n_rows, sem):
    pltpu.make_async_copy(src.at[pl.ds(0, n_rows * SUBLANES), :],
                          dst.at[pl.ds(0, n_rows * SUBLANES), :], sem).wait()


def _dispatch_kernel(pad_ref, dest_hbm, h_ref, xg_hbm, dsm, stage, zbuf, sem_idx, sem_row, sem_zero):
    i = pl.program_id(0)
    n = pl.num_programs(0)
    tm = h_ref.shape[0] // SUBLANES
    r = zbuf.shape[0] // SUBLANES
    n_blocks = pad_ref.shape[0]
    slot = lax.rem(i, 2)

    def idx_copy(step, s):
        return pltpu.make_async_copy(dest_hbm.at[step], dsm.at[pl.ds(s * TOP_K * tm, TOP_K * tm)],
                                     sem_idx.at[s])

    def zero_copy(blk):
        return pltpu.make_async_copy(zbuf, xg_hbm.at[pl.ds(blk * r * SUBLANES, r * SUBLANES), :], sem_zero)

    @pl.when(i == 0)
    def _():
        idx_copy(0, 0).start()
        zbuf[...] = jnp.zeros(zbuf.shape, zbuf.dtype)

        def zero_start(blk, c):
            @pl.when(pad_ref[blk] == 1)
            def _():
                zero_copy(blk).start()
            return c

        def zero_wait(blk, c):
            @pl.when(pad_ref[blk] == 1)
            def _():
                zero_copy(blk).wait()
            return c

        lax.fori_loop(0, n_blocks, zero_start, 0)
        lax.fori_loop(0, n_blocks, zero_wait, 0)

    @pl.when(i + 1 < n)
    def _():
        idx_copy(i + 1, 1 - slot).start()

    idx_copy(i, slot).wait()

    stage_row0 = slot * tm
    stage[pl.ds(stage_row0 * SUBLANES, tm * SUBLANES), :] = h_ref[...]

    def issue(row, c):
        for k in range(TOP_K):
            _row_copy(stage, stage_row0 + row, xg_hbm, dsm[(slot * TOP_K + k) * tm + row],
                      sem_row.at[slot]).start(priority=k % DMA_THREADS)
        return c

    lax.fori_loop(0, tm, issue, 0, unroll=ROW_DMA_UNROLL)

    def wait_slot(s):
        for _ in range(TOP_K):
            _rows_wait(stage, xg_hbm, tm, sem_row.at[s])

    @pl.when(i > 0)
    def _():
        wait_slot(1 - slot)

    @pl.when(i == n - 1)
    def _():
        wait_slot(slot)


def _dispatch(h2, dest, has_pad):
    tm = dest.shape[1] // TOP_K
    t = h2.shape[0] // SUBLANES
    r = MOE_ROWS
    n_rows = TOP_K * t + N_EXPERTS * r
    grid_spec = pltpu.PrefetchScalarGridSpec(
        num_scalar_prefetch=1,
        grid=(t // tm,),
        in_specs=[pl.BlockSpec(memory_space=pl.ANY),
                  pl.BlockSpec((tm * SUBLANES, LANES), lambda i, pad: (i, 0))],
        out_specs=pl.BlockSpec(memory_space=pl.ANY),
        scratch_shapes=[
            pltpu.SMEM((2 * TOP_K * tm,), jnp.int32),
            pltpu.VMEM((2 * tm * SUBLANES, LANES), F32),
            pltpu.VMEM((r * SUBLANES, LANES), F32),
            pltpu.SemaphoreType.DMA((2,)),
            pltpu.SemaphoreType.DMA((2,)),
            pltpu.SemaphoreType.DMA,
        ],
    )
    return pl.pallas_call(
        _dispatch_kernel,
        out_shape=jax.ShapeDtypeStruct((n_rows * SUBLANES, LANES), F32),
        grid_spec=grid_spec,
        compiler_params=pltpu.CompilerParams(dimension_semantics=("arbitrary",)),
        name="moe_dispatch",
    )(has_pad, dest, h2)


def _expert_kernel(j, be_ref, nu_ref, x_ref, wg_hbm, wu_hbm, wd_hbm, y_ref,
                   wg_ref, wu_ref, wd_ref, stage_g, stage_u, stage_d, sem):
    i = pl.program_id(0)
    r = x_ref.shape[0] // SUBLANES
    e = be_ref[i]
    used = i < nu_ref[0]
    new_expert = (i == 0) | (e != be_ref[jnp.maximum(i - 1, 0)])
    w_ref = (wg_ref, wu_ref, wd_ref)

    @pl.when(used & new_expert)
    def _():
        xb = _load_row_tiles(x_ref, 0, r).astype(BF16)
        y = _swiglu_rows_fetching(xb, (wg_hbm.at[j, e], wu_hbm.at[j, e], wd_hbm.at[j, e]), w_ref,
                                  (stage_g, stage_u, stage_d), sem)
        _store_row_tiles(y_ref, 0, y)

    @pl.when(used & jnp.logical_not(new_expert))
    def _():
        xb = _load_row_tiles(x_ref, 0, r).astype(BF16)
        _store_row_tiles(y_ref, 0, _swiglu_rows(xb, *w_ref))

    @pl.when(jnp.logical_not(used))
    def _():
        y_ref[...] = jnp.zeros(y_ref.shape, y_ref.dtype)


def _experts(xg, blk_expert, n_used, j, w_gate, w_up, w_down):
    d, d_ff = w_gate.shape[2], w_gate.shape[3]
    r = MOE_ROWS
    rows = lambda i, be, nu: (i, 0)
    grid_spec = pltpu.PrefetchScalarGridSpec(
        num_scalar_prefetch=2,
        grid=(xg.shape[0] // (r * SUBLANES),),
        in_specs=[
            pl.BlockSpec((r * SUBLANES, LANES), rows),
            pl.BlockSpec(memory_space=pl.ANY),
            pl.BlockSpec(memory_space=pl.ANY),
            pl.BlockSpec(memory_space=pl.ANY),
        ],
        out_specs=pl.BlockSpec((r * SUBLANES, LANES), rows),
        scratch_shapes=_swiglu_weight_scratch(d, d_ff),
    )
    return pl.pallas_call(
        functools.partial(_expert_kernel, j),
        out_shape=jax.ShapeDtypeStruct(xg.shape, F32),
        grid_spec=grid_spec,
        compiler_params=pltpu.CompilerParams(
            dimension_semantics=("arbitrary",), vmem_limit_bytes=VMEM_LIMIT_BYTES),
        name="moe_experts",
    )(blk_expert, n_used, xg, w_gate, w_up, w_down)


def _combine_kernel(dest_hbm, y_hbm, x_ref, route_ref, o_ref, dsm, ybuf, sem_idx, sem_row):
    i = pl.program_id(0)
    n = pl.num_programs(0)
    tm = x_ref.shape[0]
    slot = lax.rem(i, 2)

    def idx_copy(step):
        s3 = lax.rem(step, 3)
        return pltpu.make_async_copy(dest_hbm.at[step], dsm.at[pl.ds(s3 * TOP_K * tm, TOP_K * tm)],
                                     sem_idx.at[s3])

    def request_rows(step):
        s2 = lax.rem(step, 2)
        table = lax.rem(step, 3) * TOP_K * tm

        def issue(row, c):
            for k in range(TOP_K):
                _row_copy(y_hbm, dsm[table + k * tm + row], ybuf, (s2 * TOP_K + k) * tm + row,
                          sem_row.at[s2]).start(priority=k % DMA_THREADS)
            return c

        lax.fori_loop(0, tm, issue, 0, unroll=ROW_DMA_UNROLL)

    @pl.when(i == 0)
    def _():
        idx_copy(0).start()
        if_next = n > 1

        @pl.when(if_next)
        def _():
            idx_copy(1).start()

        idx_copy(0).wait()
        request_rows(0)

    @pl.when(i + 2 < n)
    def _():
        idx_copy(i + 2).start()

    @pl.when(i + 1 < n)
    def _():
        idx_copy(i + 1).wait()
        request_rows(i + 1)

    for _ in range(TOP_K):
        _rows_wait(y_hbm, ybuf, tm, sem_row.at[slot])
    rt = route_ref[...]
    moe = None
    for k in range(TOP_K):
        yk = _load_row_tiles(ybuf, (slot * TOP_K + k) * tm * SUBLANES, tm)
        term = yk * rt[:, ROUTE_GATE0 + k:ROUTE_GATE0 + k + 1]
        moe = term if moe is None else moe + term
    o_ref[...] = x_ref[...] + moe


def _combine(x, y, route, dest):
    t, d = x.shape
    tm = dest.shape[1] // TOP_K
    return pl.pallas_call(
        _combine_kernel,
        out_shape=jax.ShapeDtypeStruct((t, d), F32),
        grid=(t // tm,),
        in_specs=[
            pl.BlockSpec(memory_space=pl.ANY),
            pl.BlockSpec(memory_space=pl.ANY),
            pl.BlockSpec((tm, d), lambda i: (i, 0)),
            pl.BlockSpec((tm, ROUTE_LANES), lambda i: (i, 0)),
        ],
        out_specs=pl.BlockSpec((tm, d), lambda i: (i, 0)),
        scratch_shapes=[
            pltpu.SMEM((3 * TOP_K * tm,), jnp.int32),
            pltpu.VMEM((2 * TOP_K * tm * SUBLANES, LANES), F32),
            pltpu.SemaphoreType.DMA((3,)),
            pltpu.SemaphoreType.DMA((2,)),
        ],
        compiler_params=pltpu.CompilerParams(
            dimension_semantics=("arbitrary",), vmem_limit_bytes=VMEM_LIMIT_BYTES),
        name="moe_combine",
    )(dest, y, x, route)


def kernel(x, norm_mix, norm_ffn, hy_w_in, hy_q_gain, hy_k_gain, hy_sinks, hy_conv_w, hy_w_out, rg_w_in, rg_conv_w, rg_conv_b, rg_gate_a_w, rg_gate_a_b, rg_gate_x_w, rg_gate_x_b, rg_lambda, rg_w_out, ffn_w_gate, ffn_w_up, ffn_w_down, moe_router, moe_w_gate, moe_w_up, moe_w_down):
    b, s, d = x.shape
    t = b * s
    depth = norm_mix.shape[0]
    for layer in range(depth):
        j = layer // 2
        if layer % 2 == 0:
            x = _even_mixer(x, norm_mix[layer], hy_w_in[j], hy_q_gain[j], hy_k_gain[j], hy_sinks[j],
                            hy_conv_w[j], hy_w_out[j])
            x = _ffn(x.reshape(t, d), norm_ffn[layer], j, ffn_w_gate, ffn_w_up, ffn_w_down)
            x = x.reshape(b, s, d)
        else:
            x, h2, route, rt, counts = _odd_mixer(
                x, norm_mix[layer], rg_w_in[j], rg_conv_w[j], rg_conv_b[j], rg_gate_a_w[j], rg_gate_a_b[j],
                rg_gate_x_w[j], rg_gate_x_b[j], rg_lambda[j], rg_w_out[j], norm_ffn[layer], moe_router[j])
            dest, blk_expert, has_pad, n_used = _moe_tables(rt, counts, t)
            xg = _dispatch(h2, dest, has_pad)
            y = _experts(xg, blk_expert, n_used, j, moe_w_gate, moe_w_up, moe_w_down)
            x = _combine(x.reshape(t, d), y, route.reshape(t, ROUTE_LANES), dest).reshape(b, s, d)
    return x
```

```python
import functools

import jax
import jax.numpy as jnp
from jax import lax
from jax.experimental import pallas as pl
from jax.experimental.pallas import tpu as pltpu

F32 = jnp.float32
BF16 = jnp.bfloat16

LANES = 128
SUBLANES = 8
VMEM_LIMIT_BYTES = 56 * 1024 * 1024
DMA_THREADS = 2
ROW_DMA_UNROLL = 32

N_Q_HEADS = 8
N_KV_HEADS = 2
HEAD_DIM = 64
Q_GROUP = N_Q_HEADS // N_KV_HEADS
WINDOW = 128
ATTN_BLOCK = 128
ALIBI_MAX = 8.0
CONV_K_B = 3
CONV_K_C = 4
LRU_HEADS = 8
LRU_C = 8.0
N_EXPERTS = 8
TOP_K = 2
NORM_EPS = 1e-6
NEG_INF = -1e30

SEQ_TILE = 512
MOVE_TILE = 512
EVEN_TILE = 512
FFN_TILE = 512
MOE_ROWS = 512
FF_CHUNK = 1792
WEIGHT_COLS = 512
ROUTE_ROWS = 16
ROUTE_LANES = LANES
ROUTE_E0 = 0
ROUTE_GATE0 = 2
ROUTE_RANK0 = 4
GELU_C0 = 0.7978845608028654
GELU_C1 = 0.044715


def _rms(x, g):
    return x * lax.rsqrt(jnp.mean(x * x, axis=-1, keepdims=True) + NORM_EPS) * g


def _split_bf16(x):
    hi = x.astype(BF16)
    lo = (x - hi.astype(F32)).astype(BF16)
    return hi, lo


def _store_row_tiles(ref, start, x):
    n = x.shape[0]
    for c in range(x.shape[1] // LANES):
        ref[pl.ds(start + c, n, stride=SUBLANES), :] = x[:, c * LANES:(c + 1) * LANES]


def _load_row_tiles(ref, start, n):
    return jnp.concatenate([ref[pl.ds(start + c, n, stride=SUBLANES), :] for c in range(SUBLANES)], axis=1)


def _resident(shape, index_map):
    return pl.BlockSpec(shape, index_map, pipeline_mode=pl.Buffered(1))


def _even_mixer_kernel(sinks_ref, x_ref, g_ref, win_ref, qg_ref, kg_ref, cw_ref, wout_ref, o_ref,
                       kd_ref, vt_ref, cbuf_ref, attn_t_ref):
    ts = x_ref.shape[0]
    q_dim = N_Q_HEADS * HEAD_DIM
    kv_dim = N_KV_HEADS * HEAD_DIM
    cw_dim = cw_ref.shape[1]
    st = pl.program_id(1)

    @pl.when(st == 0)
    def _():
        kd_ref[:, 0:ATTN_BLOCK, :] = jnp.zeros((N_KV_HEADS, ATTN_BLOCK, LANES), BF16)
        vt_ref[:, 0:ATTN_BLOCK] = jnp.zeros((kv_dim, ATTN_BLOCK), BF16)
        cbuf_ref[0:SUBLANES, :] = jnp.zeros((SUBLANES, cw_dim), F32)

    x = x_ref[...]
    h = _rms(x, g_ref[...]).astype(BF16)
    proj = jnp.dot(h, win_ref[...], preferred_element_type=F32)
    q = proj[:, 0:q_dim]
    k = proj[:, q_dim:q_dim + kv_dim]
    v = proj[:, q_dim + kv_dim:q_dim + 2 * kv_dim]
    o0 = q_dim + 2 * kv_dim
    gate_b = proj[:, o0:o0 + cw_dim]
    gate_c = proj[:, o0 + cw_dim:o0 + 2 * cw_dim]
    u = proj[:, o0 + 2 * cw_dim:o0 + 3 * cw_dim]

    ri = lax.broadcasted_iota(jnp.int32, (LANES, LANES), 0)
    ci = lax.broadcasted_iota(jnp.int32, (LANES, LANES), 1)
    half_mean = jnp.where((ri < HEAD_DIM) == (ci < HEAD_DIM), 1.0 / HEAD_DIM, 0.0).astype(BF16)

    def half_mean_square(z):
        hi, lo = _split_bf16(z * z)
        cols = []
        for c in range(z.shape[1] // LANES):
            sl = slice(c * LANES, (c + 1) * LANES)
            cols.append(jnp.dot(hi[:, sl], half_mean, preferred_element_type=F32)
                        + jnp.dot(lo[:, sl], half_mean, preferred_element_type=F32))
        return cols[0] if len(cols) == 1 else jnp.concatenate(cols, axis=1)

    qn = q * lax.rsqrt(half_mean_square(q) + NORM_EPS) * qg_ref[...] * (HEAD_DIM ** -0.5)
    kn = k * lax.rsqrt(half_mean_square(k) + NORM_EPS) * kg_ref[...]

    low_half_t = lax.broadcasted_iota(jnp.int32, (ts, LANES), 1) < HEAD_DIM
    k_sw = pltpu.roll(kn, HEAD_DIM, axis=1)
    kd_ref[0, ATTN_BLOCK:ATTN_BLOCK + ts, :] = jnp.where(low_half_t, kn, k_sw).astype(BF16)
    kd_ref[1, ATTN_BLOCK:ATTN_BLOCK + ts, :] = jnp.where(low_half_t, k_sw, kn).astype(BF16)
    vt_ref[:, ATTN_BLOCK:ATTN_BLOCK + ts] = v.T.astype(BF16)

    low_half = lax.broadcasted_iota(jnp.int32, (ATTN_BLOCK, LANES), 1) < HEAD_DIM
    sj = lax.broadcasted_iota(jnp.int32, (2 * ATTN_BLOCK, ATTN_BLOCK), 0)
    qi = lax.broadcasted_iota(jnp.int32, (2 * ATTN_BLOCK, ATTN_BLOCK), 1)
    diff = ATTN_BLOCK + qi - sj
    in_window = (diff >= 0) & (diff < WINDOW)
    diff_f = diff.astype(F32)
    first_key = jnp.where(st == 0, ATTN_BLOCK, 0)
    in_window_first = in_window & (sj >= first_key)

    def penalty(valid, g):
        cols = []
        for hh in range(Q_GROUP):
            slope = 2.0 ** (-ALIBI_MAX * (g * Q_GROUP + hh + 1) / N_Q_HEADS)
            cols.append(jnp.where(valid, slope * diff_f, -NEG_INF))
        return jnp.concatenate(cols, axis=1)

    pen = [penalty(in_window, g) for g in range(N_KV_HEADS)]
    pen_first = [penalty(in_window_first, g) for g in range(N_KV_HEADS)]
    sink = [jnp.concatenate([jnp.full((1, ATTN_BLOCK), sinks_ref[0, g * Q_GROUP + hh], F32)
                             for hh in range(Q_GROUP)], axis=1) for g in range(N_KV_HEADS)]

    problems = [(j, g) for j in range(ts // ATTN_BLOCK) for g in range(N_KV_HEADS)]
    rows = lambda j: slice(j * ATTN_BLOCK, (j + 1) * ATTN_BLOCK)
    band = lambda j: slice(j * ATTN_BLOCK, (j + 2) * ATTN_BLOCK)

    scores = []
    for j, g in problems:
        q_rows = []
        for p in range(Q_GROUP // 2):
            c = g * (Q_GROUP // 2) + p
            qp = qn[rows(j), c * LANES:(c + 1) * LANES]
            q_rows.append(jnp.where(low_half, qp, 0.0))
            q_rows.append(jnp.where(low_half, 0.0, qp))
        qs = jnp.concatenate(q_rows, axis=0).astype(BF16)
        s = lax.dot_general(kd_ref[g, band(j), :], qs, (((1,), (1,)), ((), ())),
                            preferred_element_type=F32)
        scores.append(s - (pen_first[g] if j == 0 else pen[g]))
    peaks = [jnp.maximum(jnp.max(sh, axis=0, keepdims=True), sink[g]) for sh, (j, g) in zip(scores, problems)]
    probs = [jnp.exp(sh - m) for sh, m in zip(scores, peaks)]
    denoms = [jnp.sum(pr, axis=0, keepdims=True) + jnp.exp(sink[g] - m)
              for pr, m, (j, g) in zip(probs, peaks, problems)]
    for pr, denom, (j, g) in zip(probs, denoms, problems):
        o = jnp.dot(vt_ref[g * HEAD_DIM:(g + 1) * HEAD_DIM, band(j)], pr.astype(BF16),
                    preferred_element_type=F32)
        o = o / denom
        for hh in range(Q_GROUP):
            head = g * Q_GROUP + hh
            attn_t_ref[head * HEAD_DIM:(head + 1) * HEAD_DIM, rows(j)] = (
                o[:, hh * ATTN_BLOCK:(hh + 1) * ATTN_BLOCK])

    cu = gate_c * u
    cbuf_ref[SUBLANES:SUBLANES + ts, :] = cu
    conv = cw_ref[CONV_K_B - 1:CONV_K_B, :] * cu
    for kk in range(CONV_K_B - 1):
        back = CONV_K_B - 1 - kk
        conv = conv + cw_ref[kk:kk + 1, :] * cbuf_ref[SUBLANES - back:SUBLANES - back + ts, :]
    conv = gate_b * conv

    mix = jnp.concatenate([attn_t_ref[...].T, conv], axis=1).astype(BF16)
    o_ref[...] = x + jnp.dot(mix, wout_ref[...], preferred_element_type=F32)

    kd_ref[:, 0:ATTN_BLOCK, :] = kd_ref[:, ts:ts + ATTN_BLOCK, :]
    vt_ref[:, 0:ATTN_BLOCK] = vt_ref[:, ts:ts + ATTN_BLOCK]
    cbuf_ref[0:SUBLANES, :] = cbuf_ref[ts:ts + SUBLANES, :]


def _even_mixer(x, g, w_in, q_gain, k_gain, sinks, conv_w, w_out):
    b, s, d = x.shape
    ts = EVEN_TILE
    n_in = w_in.shape[1]
    cw_dim = conv_w.shape[1]
    mix_dim = w_out.shape[0]
    qg = jnp.tile(q_gain, N_Q_HEADS).reshape(1, N_Q_HEADS * HEAD_DIM)
    kg = jnp.tile(k_gain, N_KV_HEADS).reshape(1, N_KV_HEADS * HEAD_DIM)
    const = lambda bi, si: (0, 0)
    return pl.pallas_call(
        _even_mixer_kernel,
        out_shape=jax.ShapeDtypeStruct((b, s, d), F32),
        grid=(b, s // ts),
        in_specs=[
            pl.BlockSpec(memory_space=pltpu.SMEM),
            pl.BlockSpec((None, ts, d), lambda bi, si: (bi, si, 0)),
            _resident((1, d), const),
            _resident((d, n_in), const),
            _resident((1, N_Q_HEADS * HEAD_DIM), const),
            _resident((1, N_KV_HEADS * HEAD_DIM), const),
            _resident((CONV_K_B, cw_dim), const),
            _resident((mix_dim, d), const),
        ],
        out_specs=pl.BlockSpec((None, ts, d), lambda bi, si: (bi, si, 0)),
        scratch_shapes=[
            pltpu.VMEM((N_KV_HEADS, ATTN_BLOCK + ts, LANES), BF16),
            pltpu.VMEM((N_KV_HEADS * HEAD_DIM, ATTN_BLOCK + ts), BF16),
            pltpu.VMEM((SUBLANES + ts, cw_dim), F32),
            pltpu.VMEM((N_Q_HEADS * HEAD_DIM, ts), F32),
        ],
        compiler_params=pltpu.CompilerParams(
            dimension_semantics=("arbitrary", "arbitrary"), vmem_limit_bytes=VMEM_LIMIT_BYTES),
        name="even_mixer",
    )(sinks.reshape(1, N_Q_HEADS), x, g.reshape(1, d), w_in.astype(BF16), qg, kg, conv_w,
      w_out.astype(BF16))


def _swiglu_rows(hb, wg_ref, wu_ref, wd_ref):
    d_ff = wg_ref.shape[1]
    acc = None
    for c in range(d_ff // FF_CHUNK):
        cols = slice(c * FF_CHUNK, (c + 1) * FF_CHUNK)
        gate = jnp.dot(hb, wg_ref[:, cols], preferred_element_type=F32)
        up = jnp.dot(hb, wu_ref[:, cols], preferred_element_type=F32)
        act = (gate * (1.0 / (1.0 + jnp.exp(-gate))) * up).astype(BF16)
        part = jnp.dot(act, wd_ref[cols, :], preferred_element_type=F32)
        acc = part if acc is None else acc + part
    return acc


def _swiglu_rows_fetching(hb, w_hbm, w_ref, stage, sem):
    d_ff = w_ref[0].shape[1]
    n = d_ff // WEIGHT_COLS

    def copies(c, s):
        cols = pl.ds(c * WEIGHT_COLS, WEIGHT_COLS)
        srcs = (w_hbm[0].at[:, cols], w_hbm[1].at[:, cols], w_hbm[2].at[cols, :])
        return [pltpu.make_async_copy(src, stage[m].at[s], sem.at[s, m]) for m, src in enumerate(srcs)]

    for cp in copies(0, 0):
        cp.start()
    acc = None
    for c in range(n):
        s = c % 2
        if c + 1 < n:
            for cp in copies(c + 1, 1 - s):
                cp.start()
        for cp in copies(c, s):
            cp.wait()
        cols = slice(c * WEIGHT_COLS, (c + 1) * WEIGHT_COLS)
        w_ref[0][:, cols] = stage[0][s].astype(BF16)
        w_ref[1][:, cols] = stage[1][s].astype(BF16)
        w_ref[2][cols, :] = stage[2][s].astype(BF16)
        gate = jnp.dot(hb, w_ref[0][:, cols], preferred_element_type=F32)
        up = jnp.dot(hb, w_ref[1][:, cols], preferred_element_type=F32)
        act = (gate * (1.0 / (1.0 + jnp.exp(-gate))) * up).astype(BF16)
        part = jnp.dot(act, w_ref[2][cols, :], preferred_element_type=F32)
        acc = part if acc is None else acc + part
    return acc


def _swiglu_weight_scratch(d, d_ff):
    return [
        pltpu.VMEM((d, d_ff), BF16),
        pltpu.VMEM((d, d_ff), BF16),
        pltpu.VMEM((d_ff, d), BF16),
        pltpu.VMEM((2, d, WEIGHT_COLS), F32),
        pltpu.VMEM((2, d, WEIGHT_COLS), F32),
        pltpu.VMEM((2, WEIGHT_COLS, d), F32),
        pltpu.SemaphoreType.DMA((2, 3)),
    ]


def _ffn_kernel(j, x_ref, g_ref, wg_hbm, wu_hbm, wd_hbm, o_ref,
                wg_ref, wu_ref, wd_ref, stage_g, stage_u, stage_d, sem):
    x = x_ref[...]
    hb = _rms(x, g_ref[...]).astype(BF16)
    w_ref = (wg_ref, wu_ref, wd_ref)

    @pl.when(pl.program_id(0) == 0)
    def _():
        o_ref[...] = x + _swiglu_rows_fetching(hb, (wg_hbm.at[j], wu_hbm.at[j], wd_hbm.at[j]), w_ref,
                                               (stage_g, stage_u, stage_d), sem)

    @pl.when(pl.program_id(0) > 0)
    def _():
        o_ref[...] = x + _swiglu_rows(hb, *w_ref)


def _ffn(x, g, j, w_gate, w_up, w_down):
    t, d = x.shape
    d_ff = w_gate.shape[2]
    tm = FFN_TILE
    return pl.pallas_call(
        functools.partial(_ffn_kernel, j),
        out_shape=jax.ShapeDtypeStruct((t, d), F32),
        grid=(t // tm,),
        in_specs=[
            pl.BlockSpec((tm, d), lambda i: (i, 0)),
            _resident((1, d), lambda i: (0, 0)),
            pl.BlockSpec(memory_space=pl.ANY),
            pl.BlockSpec(memory_space=pl.ANY),
            pl.BlockSpec(memory_space=pl.ANY),
        ],
        out_specs=pl.BlockSpec((tm, d), lambda i: (i, 0)),
        scratch_shapes=_swiglu_weight_scratch(d, d_ff),
        compiler_params=pltpu.CompilerParams(
            dimension_semantics=("arbitrary",), vmem_limit_bytes=VMEM_LIMIT_BYTES),
        name="dense_ffn",
    )(x, g.reshape(1, d), w_gate, w_up, w_down)


def _odd_mixer_kernel(x_ref, g_ref, win_ref, cw_ref, cb_ref, gw_ref, gab_ref, gxb_ref, lam_ref, wout_ref,
                      g2_ref, wr_ref, tri_ref, o_ref, h2_ref, route_ref, rt_ref, cnt_out_ref,
                      xbuf_ref, a_ref, b_ref, hs_ref, hcar_ref, cnt_ref):
    nb, tl, d = x_ref.shape
    ts = nb * tl
    w = cw_ref.shape[1]
    hd = w // LRU_HEADS
    st = pl.program_id(0)

    @pl.when(st == 0)
    def _():
        xbuf_ref[:, 0:SUBLANES, :] = jnp.zeros((nb, SUBLANES, w), F32)
        hcar_ref[...] = jnp.zeros(hcar_ref.shape, F32)
        cnt_ref[...] = jnp.zeros(cnt_ref.shape, F32)

    x = x_ref[...].reshape(ts, d)
    h = _rms(x, g_ref[...]).astype(BF16)
    proj = jnp.dot(h, win_ref[...], preferred_element_type=F32)
    y = proj[:, 0:w]
    half_y = 0.5 * y
    y = half_y + half_y * jnp.tanh(y * (GELU_C0 + (GELU_C0 * GELU_C1) * (y * y)))
    xb = proj[:, w:2 * w]

    xc_rows = []
    for bi in range(nb):
        xb_seq = xb[bi * tl:(bi + 1) * tl, :]
        xbuf_ref[bi, SUBLANES:SUBLANES + tl, :] = xb_seq
        acc = cw_ref[CONV_K_C - 1:CONV_K_C, :] * xb_seq + cb_ref[...]
        for kk in range(CONV_K_C - 1):
            back = CONV_K_C - 1 - kk
            acc = acc + cw_ref[kk:kk + 1, :] * xbuf_ref[bi, SUBLANES - back:SUBLANES - back + tl, :]
        xc_rows.append(acc)
    xc = jnp.concatenate(xc_rows, axis=0)

    r_cols = []
    i_cols = []
    for hh in range(LRU_HEADS):
        ri = jnp.dot(xc[:, hh * hd:(hh + 1) * hd].astype(BF16), gw_ref[hh], preferred_element_type=F32)
        r_cols.append(ri[:, 0:hd])
        i_cols.append(ri[:, hd:2 * hd])
    tanh_r = jnp.tanh(jnp.concatenate(r_cols, axis=1) + gab_ref[...])
    ig = 0.5 + 0.5 * jnp.tanh(jnp.concatenate(i_cols, axis=1) + gxb_ref[...])
    neg_lam = -lam_ref[...]
    softplus = jnp.maximum(neg_lam, 0.0) + jnp.log1p(jnp.exp(-jnp.abs(neg_lam)))
    half_rate = (-0.5 * LRU_C) * softplus
    log_a = half_rate + half_rate * tanh_r
    a = jnp.exp(log_a)
    mult = jnp.sqrt(jnp.maximum(1.0 - a * a, 0.0))
    row = lax.broadcasted_iota(jnp.int32, (ts, w), 0)
    mult = jnp.where((lax.rem(row, tl) == 0) & (st == 0), 1.0, mult)
    bb = mult * (ig * xc)

    for c in range(w // LANES):
        cols = slice(c * LANES, (c + 1) * LANES)
        for bi in range(nb):
            a_ref[c, pl.ds(bi, tl, stride=nb), :] = a[bi * tl:(bi + 1) * tl, cols]
            b_ref[c, pl.ds(bi, tl, stride=nb), :] = bb[bi * tl:(bi + 1) * tl, cols]

    def step(t, hprev):
        off = pl.multiple_of(t * nb, nb)
        hnew = []
        for c in range(w // LANES):
            hc = a_ref[c, pl.ds(off, nb), :] * hprev[c] + b_ref[c, pl.ds(off, nb), :]
            hs_ref[c, pl.ds(off, nb), :] = hc
            hnew.append(hc)
        return tuple(hnew)

    hlast = lax.fori_loop(0, tl, step, tuple(hcar_ref[c] for c in range(w // LANES)), unroll=4)
    for c in range(w // LANES):
        hcar_ref[c] = hlast[c]
    hs = jnp.concatenate(
        [jnp.concatenate([hs_ref[c, pl.ds(bi, tl, stride=nb), :] for c in range(w // LANES)], axis=1)
         for bi in range(nb)], axis=0)

    out = x + jnp.dot((y * hs).astype(BF16), wout_ref[...], preferred_element_type=F32)
    o_ref[...] = out.reshape(nb, tl, d)
    xbuf_ref[:, 0:SUBLANES, :] = xbuf_ref[:, tl:tl + SUBLANES, :]

    h2 = _rms(out, g2_ref[...])
    for bi in range(nb):
        _store_row_tiles(h2_ref.at[bi], 0, h2[bi * tl:(bi + 1) * tl, :])
    logits = lax.dot_general(wr_ref[...], h2.astype(BF16), (((1,), (1,)), ((), ())),
                             preferred_element_type=F32)
    erow = lax.broadcasted_iota(jnp.int32, (ROUTE_ROWS, ts), 0)
    lg = jnp.where(erow < N_EXPERTS, logits, -jnp.inf)
    m1 = jnp.max(lg, axis=0, keepdims=True)
    i1 = jnp.min(jnp.where(lg == m1, erow, ROUTE_ROWS), axis=0, keepdims=True)
    lg2 = jnp.where(erow == i1, -jnp.inf, lg)
    m2 = jnp.max(lg2, axis=0, keepdims=True)
    i2 = jnp.min(jnp.where(lg2 == m2, erow, ROUTE_ROWS), axis=0, keepdims=True)
    e2 = jnp.exp(m2 - m1)
    gate1 = 1.0 / (1.0 + e2)
    gate2 = e2 / (1.0 + e2)

    first = erow == i1
    second = erow == i2
    ind = jnp.where(first | second, 1.0, 0.0)
    cums = jnp.dot(ind.astype(BF16), tri_ref[...], preferred_element_type=F32)
    rank = cnt_ref[...] + cums - ind
    cnt_ref[...] = cnt_ref[...] + jnp.broadcast_to(cums[:, ts - 1:ts], cums.shape)
    rank1 = jnp.sum(jnp.where(first, rank, 0.0), axis=0, keepdims=True)
    rank2 = jnp.sum(jnp.where(second, rank, 0.0), axis=0, keepdims=True)
    cnt_out_ref[...] = cnt_ref[:, 0:ROUTE_LANES].astype(jnp.int32)

    fields = [None] * SUBLANES
    fields[ROUTE_E0], fields[ROUTE_E0 + 1] = i1.astype(F32), i2.astype(F32)
    fields[ROUTE_GATE0], fields[ROUTE_GATE0 + 1] = gate1, gate2
    fields[ROUTE_RANK0], fields[ROUTE_RANK0 + 1] = rank1, rank2
    zero_row = jnp.zeros((1, ts), F32)
    record_t = jnp.concatenate([zero_row if f is None else f for f in fields], axis=0)
    rt_ref[...] = record_t.astype(jnp.int32)
    route_ref[...] = jnp.concatenate(
        [record_t, jnp.zeros((ROUTE_LANES - SUBLANES, ts), F32)], axis=0).T.reshape(nb, tl, ROUTE_LANES)


def _odd_mixer(x, g, w_in, conv_w, conv_b, ga_w, ga_b, gx_w, gx_b, lam, w_out, g2, w_router):
    b, s, d = x.shape
    assert b == SUBLANES, "one timestep of all sequences must fill one sublane group"
    tl = SEQ_TILE // b
    ts = b * tl
    w = conv_w.shape[1]
    hd = w // LRU_HEADS
    gw = (0.5 * jnp.concatenate([ga_w, gx_w], axis=-1)).astype(BF16)
    ga_b, gx_b = 0.5 * ga_b, 0.5 * gx_b
    wr = jnp.pad(w_router.T, ((0, ROUTE_ROWS - N_EXPERTS), (0, 0))).astype(BF16)
    tri = jnp.triu(jnp.ones((ts, ts), BF16))
    nst = s // tl
    const = lambda si: (0, 0)
    row = lambda a: a.reshape(1, -1)
    tile = lambda si: (0, si, 0)
    out, h2, route, rt, counts = pl.pallas_call(
        _odd_mixer_kernel,
        out_shape=(jax.ShapeDtypeStruct((b, s, d), F32),
                   jax.ShapeDtypeStruct((b, s * SUBLANES, LANES), F32),
                   jax.ShapeDtypeStruct((b, s, ROUTE_LANES), F32),
                   jax.ShapeDtypeStruct((nst, SUBLANES, ts), jnp.int32),
                   jax.ShapeDtypeStruct((ROUTE_ROWS, ROUTE_LANES), jnp.int32)),
        grid=(nst,),
        in_specs=[
            pl.BlockSpec((b, tl, d), tile),
            _resident((1, d), const),
            _resident((d, 2 * w), const),
            _resident((CONV_K_C, w), const),
            _resident((1, w), const),
            _resident((LRU_HEADS, hd, 2 * hd), lambda si: (0, 0, 0)),
            _resident((1, w), const),
            _resident((1, w), const),
            _resident((1, w), const),
            _resident((w, d), const),
            _resident((1, d), const),
            _resident((ROUTE_ROWS, d), const),
            _resident((ts, ts), const),
        ],
        out_specs=(pl.BlockSpec((b, tl, d), tile),
                   pl.BlockSpec((b, tl * SUBLANES, LANES), tile),
                   pl.BlockSpec((b, tl, ROUTE_LANES), tile),
                   pl.BlockSpec((None, SUBLANES, ts), lambda si: (si, 0, 0)),
                   pl.BlockSpec((ROUTE_ROWS, ROUTE_LANES), const)),
        scratch_shapes=[
            pltpu.VMEM((b, SUBLANES + tl, w), F32),
            pltpu.VMEM((w // LANES, ts, LANES), F32),
            pltpu.VMEM((w // LANES, ts, LANES), F32),
            pltpu.VMEM((w // LANES, ts, LANES), F32),
            pltpu.VMEM((w // LANES, b, LANES), F32),
            pltpu.VMEM((ROUTE_ROWS, ts), F32),
        ],
        compiler_params=pltpu.CompilerParams(
            dimension_semantics=("arbitrary",), vmem_limit_bytes=VMEM_LIMIT_BYTES),
        name="odd_mixer",
    )(x, row(g), w_in.astype(BF16), conv_w, row(conv_b), gw, row(ga_b), row(gx_b), row(lam),
      w_out.astype(BF16), row(g2), wr, tri)
    rt = rt.reshape(nst, SUBLANES, b, tl).transpose(1, 2, 0, 3).reshape(SUBLANES, b * s)
    return out, h2.reshape(b * s * SUBLANES, LANES), route, rt, counts


def _moe_tables(rt, counts, t):
    r = MOE_ROWS
    tm = MOVE_TILE
    n_blocks = TOP_K * t // r + N_EXPERTS
    counts = counts[0:N_EXPERTS, 0]
    padded = (counts + r - 1) // r * r
    pad_end = jnp.cumsum(padded)
    pad_start = pad_end - padded
    e = rt[ROUTE_E0:ROUTE_E0 + TOP_K, :]
    rank = rt[ROUTE_RANK0:ROUTE_RANK0 + TOP_K, :]
    start = jnp.zeros_like(e)
    for ex in range(N_EXPERTS):
        start = jnp.where(e == ex, pad_start[ex], start)
    dest = start + rank
    blk = jnp.arange(n_blocks, dtype=jnp.int32)
    blk_expert = jnp.minimum(jnp.sum(blk[:, None] * r >= pad_end[None, :], axis=1), N_EXPERTS - 1)
    n_used = pad_end[N_EXPERTS - 1] // r
    last_blk = jnp.where(padded > 0, pad_end // r - 1, -1)
    has_pad = jnp.any(blk[:, None] == last_blk[None, :], axis=1) | (blk >= n_used)
    dest = dest.astype(jnp.int32).reshape(TOP_K, t // tm, tm).transpose(1, 0, 2).reshape(t // tm, TOP_K * tm)
    return dest, blk_expert.astype(jnp.int32), has_pad.astype(jnp.int32), n_used.astype(jnp.int32).reshape(1)


def _row_copy(src, src_row, dst, dst_row, sem):
    return pltpu.make_async_copy(src.at[pl.ds(src_row * SUBLANES, SUBLANES), :],
                                 dst.at[pl.ds(dst_row * SUBLANES, SUBLANES), :], sem)


def _rows_wait(src, dst, n_rows, sem):
    pltpu.make_async_copy(src.at[pl.ds(0, n_rows * SUBLANES), :],
                          dst.at[pl.ds(0, n_rows * SUBLANES), :], sem).wait()


def _dispatch_kernel(pad_ref, dest_hbm, h_ref, xg_hbm, dsm, stage, zbuf, sem_idx, sem_row, sem_zero):
    i = pl.program_id(0)
    n = pl.num_programs(0)
    tm = h_ref.shape[0] // SUBLANES
    r = zbuf.shape[0] // SUBLANES
    n_blocks = pad_ref.shape[0]
    slot = lax.rem(i, 2)

    def idx_copy(step, s):
        return pltpu.make_async_copy(dest_hbm.at[step], dsm.at[pl.ds(s * TOP_K * tm, TOP_K * tm)],
                                     sem_idx.at[s])

    def zero_copy(blk):
        return pltpu.make_async_copy(zbuf, xg_hbm.at[pl.ds(blk * r * SUBLANES, r * SUBLANES), :], sem_zero)

    @pl.when(i == 0)
    def _():
        idx_copy(0, 0).start()
        zbuf[...] = jnp.zeros(zbuf.shape, zbuf.dtype)

        def zero_start(blk, c):
            @pl.when(pad_ref[blk] == 1)
            def _():
                zero_copy(blk).start()
            return c

        def zero_wait(blk, c):
            @pl.when(pad_ref[blk] == 1)
            def _():
                zero_copy(blk).wait()
            return c

        lax.fori_loop(0, n_blocks, zero_start, 0)
        lax.fori_loop(0, n_blocks, zero_wait, 0)

    @pl.when(i + 1 < n)
    def _():
        idx_copy(i + 1, 1 - slot).start()

    idx_copy(i, slot).wait()

    stage_row0 = slot * tm
    stage[pl.ds(stage_row0 * SUBLANES, tm * SUBLANES), :] = h_ref[...]

    def issue(row, c):
        for k in range(TOP_K):
            _row_copy(stage, stage_row0 + row, xg_hbm, dsm[(slot * TOP_K + k) * tm + row],
                      sem_row.at[slot]).start(priority=k % DMA_THREADS)
        return c

    lax.fori_loop(0, tm, issue, 0, unroll=ROW_DMA_UNROLL)

    def wait_slot(s):
        for _ in range(TOP_K):
            _rows_wait(stage, xg_hbm, tm, sem_row.at[s])

    @pl.when(i > 0)
    def _():
        wait_slot(1 - slot)

    @pl.when(i == n - 1)
    def _():
        wait_slot(slot)


def _dispatch(h2, dest, has_pad):
    tm = dest.shape[1] // TOP_K
    t = h2.shape[0] // SUBLANES
    r = MOE_ROWS
    n_rows = TOP_K * t + N_EXPERTS * r
    grid_spec = pltpu.PrefetchScalarGridSpec(
        num_scalar_prefetch=1,
        grid=(t // tm,),
        in_specs=[pl.BlockSpec(memory_space=pl.ANY),
                  pl.BlockSpec((tm * SUBLANES, LANES), lambda i, pad: (i, 0))],
        out_specs=pl.BlockSpec(memory_space=pl.ANY),
        scratch_shapes=[
            pltpu.SMEM((2 * TOP_K * tm,), jnp.int32),
            pltpu.VMEM((2 * tm * SUBLANES, LANES), F32),
            pltpu.VMEM((r * SUBLANES, LANES), F32),
            pltpu.SemaphoreType.DMA((2,)),
            pltpu.SemaphoreType.DMA((2,)),
            pltpu.SemaphoreType.DMA,
        ],
    )
    return pl.pallas_call(
        _dispatch_kernel,
        out_shape=jax.ShapeDtypeStruct((n_rows * SUBLANES, LANES), F32),
        grid_spec=grid_spec,
        compiler_params=pltpu.CompilerParams(dimension_semantics=("arbitrary",)),
        name="moe_dispatch",
    )(has_pad, dest, h2)


def _expert_kernel(j, be_ref, nu_ref, x_ref, wg_hbm, wu_hbm, wd_hbm, y_ref,
                   wg_ref, wu_ref, wd_ref, stage_g, stage_u, stage_d, sem):
    i = pl.program_id(0)
    r = x_ref.shape[0] // SUBLANES
    e = be_ref[i]
    used = i < nu_ref[0]
    new_expert = (i == 0) | (e != be_ref[jnp.maximum(i - 1, 0)])
    w_ref = (wg_ref, wu_ref, wd_ref)

    @pl.when(used & new_expert)
    def _():
        xb = _load_row_tiles(x_ref, 0, r).astype(BF16)
        y = _swiglu_rows_fetching(xb, (wg_hbm.at[j, e], wu_hbm.at[j, e], wd_hbm.at[j, e]), w_ref,
                                  (stage_g, stage_u, stage_d), sem)
        _store_row_tiles(y_ref, 0, y)

    @pl.when(used & jnp.logical_not(new_expert))
    def _():
        xb = _load_row_tiles(x_ref, 0, r).astype(BF16)
        _store_row_tiles(y_ref, 0, _swiglu_rows(xb, *w_ref))

    @pl.when(jnp.logical_not(used))
    def _():
        y_ref[...] = jnp.zeros(y_ref.shape, y_ref.dtype)


def _experts(xg, blk_expert, n_used, j, w_gate, w_up, w_down):
    d, d_ff = w_gate.shape[2], w_gate.shape[3]
    r = MOE_ROWS
    rows = lambda i, be, nu: (i, 0)
    grid_spec = pltpu.PrefetchScalarGridSpec(
        num_scalar_prefetch=2,
        grid=(xg.shape[0] // (r * SUBLANES),),
        in_specs=[
            pl.BlockSpec((r * SUBLANES, LANES), rows),
            pl.BlockSpec(memory_space=pl.ANY),
            pl.BlockSpec(memory_space=pl.ANY),
            pl.BlockSpec(memory_space=pl.ANY),
        ],
        out_specs=pl.BlockSpec((r * SUBLANES, LANES), rows),
        scratch_shapes=_swiglu_weight_scratch(d, d_ff),
    )
    return pl.pallas_call(
        functools.partial(_expert_kernel, j),
        out_shape=jax.ShapeDtypeStruct(xg.shape, F32),
        grid_spec=grid_spec,
        compiler_params=pltpu.CompilerParams(
            dimension_semantics=("arbitrary",), vmem_limit_bytes=VMEM_LIMIT_BYTES),
        name="moe_experts",
    )(blk_expert, n_used, xg, w_gate, w_up, w_down)


def _combine_kernel(dest_hbm, y_hbm, x_ref, route_ref, o_ref, dsm, ybuf, sem_idx, sem_row):
    i = pl.program_id(0)
    n = pl.num_programs(0)
    tm = x_ref.shape[0]
    slot = lax.rem(i, 2)

    def idx_copy(step):
        s3 = lax.rem(step, 3)
        return pltpu.make_async_copy(dest_hbm.at[step], dsm.at[pl.ds(s3 * TOP_K * tm, TOP_K * tm)],
                                     sem_idx.at[s3])

    def request_rows(step):
        s2 = lax.rem(step, 2)
        table = lax.rem(step, 3) * TOP_K * tm

        def issue(row, c):
            for k in range(TOP_K):
                _row_copy(y_hbm, dsm[table + k * tm + row], ybuf, (s2 * TOP_K + k) * tm + row,
                          sem_row.at[s2]).start(priority=k % DMA_THREADS)
            return c

        lax.fori_loop(0, tm, issue, 0, unroll=ROW_DMA_UNROLL)

    @pl.when(i == 0)
    def _():
        idx_copy(0).start()
        if_next = n > 1

        @pl.when(if_next)
        def _():
            idx_copy(1).start()

        idx_copy(0).wait()
        request_rows(0)

    @pl.when(i + 2 < n)
    def _():
        idx_copy(i + 2).start()

    @pl.when(i + 1 < n)
    def _():
        idx_copy(i + 1).wait()
        request_rows(i + 1)

    for _ in range(TOP_K):
        _rows_wait(y_hbm, ybuf, tm, sem_row.at[slot])
    rt = route_ref[...]
    moe = None
    for k in range(TOP_K):
        yk = _load_row_tiles(ybuf, (slot * TOP_K + k) * tm * SUBLANES, tm)
        term = yk * rt[:, ROUTE_GATE0 + k:ROUTE_GATE0 + k + 1]
        moe = term if moe is None else moe + term
    o_ref[...] = x_ref[...] + moe


def _combine(x, y, route, dest):
    t, d = x.shape
    tm = dest.shape[1] // TOP_K
    return pl.pallas_call(
        _combine_kernel,
        out_shape=jax.ShapeDtypeStruct((t, d), F32),
        grid=(t // tm,),
        in_specs=[
            pl.BlockSpec(memory_space=pl.ANY),
            pl.BlockSpec(memory_space=pl.ANY),
            pl.BlockSpec((tm, d), lambda i: (i, 0)),
            pl.BlockSpec((tm, ROUTE_LANES), lambda i: (i, 0)),
        ],
        out_specs=pl.BlockSpec((tm, d), lambda i: (i, 0)),
        scratch_shapes=[
            pltpu.SMEM((3 * TOP_K * tm,), jnp.int32),
            pltpu.VMEM((2 * TOP_K * tm * SUBLANES, LANES), F32),
            pltpu.SemaphoreType.DMA((3,)),
            pltpu.SemaphoreType.DMA((2,)),
        ],
        compiler_params=pltpu.CompilerParams(
            dimension_semantics=("arbitrary",), vmem_limit_bytes=VMEM_LIMIT_BYTES),
        name="moe_combine",
    )(dest, y, x, route)


def kernel(x, norm_mix, norm_ffn, hy_w_in, hy_q_gain, hy_k_gain, hy_sinks, hy_conv_w, hy_w_out, rg_w_in, rg_conv_w, rg_conv_b, rg_gate_a_w, rg_gate_a_b, rg_gate_x_w, rg_gate_x_b, rg_lambda, rg_w_out, ffn_w_gate, ffn_w_up, ffn_w_down, moe_router, moe_w_gate, moe_w_up, moe_w_down):
    b, s, d = x.shape
    t = b * s
    depth = norm_mix.shape[0]
    for layer in range(depth):
        j = layer // 2
        if layer % 2 == 0:
            x = _even_mixer(x, norm_mix[layer], hy_w_in[j], hy_q_gain[j], hy_k_gain[j], hy_sinks[j],
                            hy_conv_w[j], hy_w_out[j])
            x = _ffn(x.reshape(t, d), norm_ffn[layer], j, ffn_w_gate, ffn_w_up, ffn_w_down)
            x = x.reshape(b, s, d)
        else:
            x, h2, route, rt, counts = _odd_mixer(
                x, norm_mix[layer], rg_w_in[j], rg_conv_w[j], rg_conv_b[j], rg_gate_a_w[j], rg_gate_a_b[j],
                rg_gate_x_w[j], rg_gate_x_b[j], rg_lambda[j], rg_w_out[j], norm_ffn[layer], moe_router[j])
            dest, blk_expert, has_pad, n_used = _moe_tables(rt, counts, t)
            xg = _dispatch(h2, dest, has_pad)
            y = _experts(xg, blk_expert, n_used, j, moe_w_gate, moe_w_up, moe_w_down)
            x = _combine(x.reshape(t, d), y, route.reshape(t, ROUTE_LANES), dest).reshape(b, s, d)
    return x
```

```python
import functools

import jax
import jax.numpy as jnp
from jax import lax
from jax.experimental import pallas as pl
from jax.experimental.pallas import tpu as pltpu

F32 = jnp.float32
BF16 = jnp.bfloat16

LANES = 128
SUBLANES = 8
VMEM_LIMIT_BYTES = 56 * 1024 * 1024
DMA_THREADS = 2
ROW_DMA_UNROLL = 32

N_Q_HEADS = 8
N_KV_HEADS = 2
HEAD_DIM = 64
Q_GROUP = N_Q_HEADS // N_KV_HEADS
WINDOW = 128
ATTN_BLOCK = 128
ALIBI_MAX = 8.0
CONV_K_B = 3
CONV_K_C = 4
LRU_HEADS = 8
LRU_C = 8.0
N_EXPERTS = 8
TOP_K = 2
NORM_EPS = 1e-6
NEG_INF = -1e30

SEQ_TILE = 512
MOVE_TILE = 512
EVEN_TILE = 512
FFN_TILE = 512
MOE_ROWS = 512
FF_CHUNK = 1792
WEIGHT_COLS = 512
ROUTE_ROWS = 16
ROUTE_LANES = LANES
ROUTE_E0 = 0
ROUTE_GATE0 = 2
ROUTE_RANK0 = 4
GELU_C0 = 0.7978845608028654
GELU_C1 = 0.044715


def _rms(x, g):
    return x * lax.rsqrt(jnp.mean(x * x, axis=-1, keepdims=True) + NORM_EPS) * g


def _split_bf16(x):
    hi = x.astype(BF16)
    lo = (x - hi.astype(F32)).astype(BF16)
    return hi, lo


def _store_row_tiles(ref, start, x):
    n = x.shape[0]
    for c in range(x.shape[1] // LANES):
        ref[pl.ds(start + c, n, stride=SUBLANES), :] = x[:, c * LANES:(c + 1) * LANES]


def _load_row_tiles(ref, start, n):
    return jnp.concatenate([ref[pl.ds(start + c, n, stride=SUBLANES), :] for c in range(SUBLANES)], axis=1)


def _resident(shape, index_map):
    return pl.BlockSpec(shape, index_map, pipeline_mode=pl.Buffered(1))


def _even_mixer_kernel(sinks_ref, x_ref, g_ref, win_ref, qg_ref, kg_ref, cw_ref, wout_ref, o_ref,
                       kd_ref, vt_ref, cbuf_ref, attn_t_ref):
    ts = x_ref.shape[0]
    q_dim = N_Q_HEADS * HEAD_DIM
    kv_dim = N_KV_HEADS * HEAD_DIM
    cw_dim = cw_ref.shape[1]
    st = pl.program_id(1)

    @pl.when(st == 0)
    def _():
        kd_ref[:, 0:ATTN_BLOCK, :] = jnp.zeros((N_KV_HEADS, ATTN_BLOCK, LANES), BF16)
        vt_ref[:, 0:ATTN_BLOCK] = jnp.zeros((kv_dim, ATTN_BLOCK), BF16)
        cbuf_ref[0:SUBLANES, :] = jnp.zeros((SUBLANES, cw_dim), F32)

    x = x_ref[...]
    h = _rms(x, g_ref[...]).astype(BF16)
    proj = jnp.dot(h, win_ref[...], preferred_element_type=F32)
    q = proj[:, 0:q_dim]
    k = proj[:, q_dim:q_dim + kv_dim]
    v = proj[:, q_dim + kv_dim:q_dim + 2 * kv_dim]
    o0 = q_dim + 2 * kv_dim
    gate_b = proj[:, o0:o0 + cw_dim]
    gate_c = proj[:, o0 + cw_dim:o0 + 2 * cw_dim]
    u = proj[:, o0 + 2 * cw_dim:o0 + 3 * cw_dim]

    ri = lax.broadcasted_iota(jnp.int32, (LANES, LANES), 0)
    ci = lax.broadcasted_iota(jnp.int32, (LANES, LANES), 1)
    half_mean = jnp.where((ri < HEAD_DIM) == (ci < HEAD_DIM), 1.0 / HEAD_DIM, 0.0).astype(BF16)

    def half_mean_square(z):
        hi, lo = _split_bf16(z * z)
        cols = []
        for c in range(z.shape[1] // LANES):
            sl = slice(c * LANES, (c + 1) * LANES)
            cols.append(jnp.dot(hi[:, sl], half_mean, preferred_element_type=F32)
                        + jnp.dot(lo[:, sl], half_mean, preferred_element_type=F32))
        return cols[0] if len(cols) == 1 else jnp.concatenate(cols, axis=1)

    qn = q * lax.rsqrt(half_mean_square(q) + NORM_EPS) * qg_ref[...] * (HEAD_DIM ** -0.5)
    kn = k * lax.rsqrt(half_mean_square(k) + NORM_EPS) * kg_ref[...]

    low_half_t = lax.broadcasted_iota(jnp.int32, (ts, LANES), 1) < HEAD_DIM
    k_sw = pltpu.roll(kn, HEAD_DIM, axis=1)
    kd_ref[0, ATTN_BLOCK:ATTN_BLOCK + ts, :] = jnp.where(low_half_t, kn, k_sw).astype(BF16)
    kd_ref[1, ATTN_BLOCK:ATTN_BLOCK + ts, :] = jnp.where(low_half_t, k_sw, kn).astype(BF16)
    vt_ref[:, ATTN_BLOCK:ATTN_BLOCK + ts] = v.T.astype(BF16)

    low_half = lax.broadcasted_iota(jnp.int32, (ATTN_BLOCK, LANES), 1) < HEAD_DIM
    sj = lax.broadcasted_iota(jnp.int32, (2 * ATTN_BLOCK, ATTN_BLOCK), 0)
    qi = lax.broadcasted_iota(jnp.int32, (2 * ATTN_BLOCK, ATTN_BLOCK), 1)
    diff = ATTN_BLOCK + qi - sj
    in_window = (diff >= 0) & (diff < WINDOW)
    diff_f = diff.astype(F32)
    first_key = jnp.where(st == 0, ATTN_BLOCK, 0)
    in_window_first = in_window & (sj >= first_key)

    def penalty(valid, g):
        cols = []
        for hh in range(Q_GROUP):
            slope = 2.0 ** (-ALIBI_MAX * (g * Q_GROUP + hh + 1) / N_Q_HEADS)
            cols.append(jnp.where(valid, slope * diff_f, -NEG_INF))
        return jnp.concatenate(cols, axis=1)

    pen = [penalty(in_window, g) for g in range(N_KV_HEADS)]
    pen_first = [penalty(in_window_first, g) for g in range(N_KV_HEADS)]
    sink = [jnp.concatenate([jnp.full((1, ATTN_BLOCK), sinks_ref[0, g * Q_GROUP + hh], F32)
                             for hh in range(Q_GROUP)], axis=1) for g in range(N_KV_HEADS)]

    problems = [(j, g) for j in range(ts // ATTN_BLOCK) for g in range(N_KV_HEADS)]
    rows = lambda j: slice(j * ATTN_BLOCK, (j + 1) * ATTN_BLOCK)
    band = lambda j: slice(j * ATTN_BLOCK, (j + 2) * ATTN_BLOCK)

    scores = []
    for j, g in problems:
        q_rows = []
        for p in range(Q_GROUP // 2):
            c = g * (Q_GROUP // 2) + p
            qp = qn[rows(j), c * LANES:(c + 1) * LANES]
            q_rows.append(jnp.where(low_half, qp, 0.0))
            q_rows.append(jnp.where(low_half, 0.0, qp))
        qs = jnp.concatenate(q_rows, axis=0).astype(BF16)
        s = lax.dot_general(kd_ref[g, band(j), :], qs, (((1,), (1,)), ((), ())),
                            preferred_element_type=F32)
        scores.append(s - (pen_first[g] if j == 0 else pen[g]))
    peaks = [jnp.maximum(jnp.max(sh, axis=0, keepdims=True), sink[g]) for sh, (j, g) in zip(scores, problems)]
    probs = [jnp.exp(sh - m) for sh, m in zip(scores, peaks)]
    denoms = [jnp.sum(pr, axis=0, keepdims=True) + jnp.exp(sink[g] - m)
              for pr, m, (j, g) in zip(probs, peaks, problems)]
    for pr, denom, (j, g) in zip(probs, denoms, problems):
        o = jnp.dot(vt_ref[g * HEAD_DIM:(g + 1) * HEAD_DIM, band(j)], pr.astype(BF16),
                    preferred_element_type=F32)
        o = o / denom
        for hh in range(Q_GROUP):
            head = g * Q_GROUP + hh
            attn_t_ref[head * HEAD_DIM:(head + 1) * HEAD_DIM, rows(j)] = (
                o[:, hh * ATTN_BLOCK:(hh + 1) * ATTN_BLOCK])

    cu = gate_c * u
    cbuf_ref[SUBLANES:SUBLANES + ts, :] = cu
    conv = cw_ref[CONV_K_B - 1:CONV_K_B, :] * cu
    for kk in range(CONV_K_B - 1):
        back = CONV_K_B - 1 - kk
        conv = conv + cw_ref[kk:kk + 1, :] * cbuf_ref[SUBLANES - back:SUBLANES - back + ts, :]
    conv = gate_b * conv

    mix = jnp.concatenate([attn_t_ref[...].T, conv], axis=1).astype(BF16)
    o_ref[...] = x + jnp.dot(mix, wout_ref[...], preferred_element_type=F32)

    kd_ref[:, 0:ATTN_BLOCK, :] = kd_ref[:, ts:ts + ATTN_BLOCK, :]
    vt_ref[:, 0:ATTN_BLOCK] = vt_ref[:, ts:ts + ATTN_BLOCK]
    cbuf_ref[0:SUBLANES, :] = cbuf_ref[ts:ts + SUBLANES, :]


def _even_mixer(x, g, w_in, q_gain, k_gain, sinks, conv_w, w_out):
    b, s, d = x.shape
    ts = EVEN_TILE
    n_in = w_in.shape[1]
    cw_dim = conv_w.shape[1]
    mix_dim = w_out.shape[0]
    qg = jnp.tile(q_gain, N_Q_HEADS).reshape(1, N_Q_HEADS * HEAD_DIM)
    kg = jnp.tile(k_gain, N_KV_HEADS).reshape(1, N_KV_HEADS * HEAD_DIM)
    const = lambda bi, si: (0, 0)
    return pl.pallas_call(
        _even_mixer_kernel,
        out_shape=jax.ShapeDtypeStruct((b, s, d), F32),
        grid=(b, s // ts),
        in_specs=[
            pl.BlockSpec(memory_space=pltpu.SMEM),
            pl.BlockSpec((None, ts, d), lambda bi, si: (bi, si, 0)),
            _resident((1, d), const),
            _resident((d, n_in), const),
            _resident((1, N_Q_HEADS * HEAD_DIM), const),
            _resident((1, N_KV_HEADS * HEAD_DIM), const),
            _resident((CONV_K_B, cw_dim), const),
            _resident((mix_dim, d), const),
        ],
        out_specs=pl.BlockSpec((None, ts, d), lambda bi, si: (bi, si, 0)),
        scratch_shapes=[
            pltpu.VMEM((N_KV_HEADS, ATTN_BLOCK + ts, LANES), BF16),
            pltpu.VMEM((N_KV_HEADS * HEAD_DIM, ATTN_BLOCK + ts), BF16),
            pltpu.VMEM((SUBLANES + ts, cw_dim), F32),
            pltpu.VMEM((N_Q_HEADS * HEAD_DIM, ts), F32),
        ],
        compiler_params=pltpu.CompilerParams(
            dimension_semantics=("arbitrary", "arbitrary"), vmem_limit_bytes=VMEM_LIMIT_BYTES),
        name="even_mixer",
    )(sinks.reshape(1, N_Q_HEADS), x, g.reshape(1, d), w_in.astype(BF16), qg, kg, conv_w,
      w_out.astype(BF16))


def _swiglu_rows(hb, wg_ref, wu_ref, wd_ref):
    d_ff = wg_ref.shape[1]
    acc = None
    for c in range(d_ff // FF_CHUNK):
        cols = slice(c * FF_CHUNK, (c + 1) * FF_CHUNK)
        gate = jnp.dot(hb, wg_ref[:, cols], preferred_element_type=F32)
        up = jnp.dot(hb, wu_ref[:, cols], preferred_element_type=F32)
        act = (gate * (1.0 / (1.0 + jnp.exp(-gate))) * up).astype(BF16)
        part = jnp.dot(act, wd_ref[cols, :], preferred_element_type=F32)
        acc = part if acc is None else acc + part
    return acc


def _swiglu_rows_fetching(hb, w_hbm, w_ref, stage, sem):
    d_ff = w_ref[0].shape[1]
    n = d_ff // WEIGHT_COLS

    def copies(c, s):
        cols = pl.ds(c * WEIGHT_COLS, WEIGHT_COLS)
        srcs = (w_hbm[0].at[:, cols], w_hbm[1].at[:, cols], w_hbm[2].at[cols, :])
        return [pltpu.make_async_copy(src, stage[m].at[s], sem.at[s, m]) for m, src in enumerate(srcs)]

    for cp in copies(0, 0):
        cp.start()
    acc = None
    for c in range(n):
        s = c % 2
        if c + 1 < n:
            for cp in copies(c + 1, 1 - s):
                cp.start()
        for cp in copies(c, s):
            cp.wait()
        cols = slice(c * WEIGHT_COLS, (c + 1) * WEIGHT_COLS)
        w_ref[0][:, cols] = stage[0][s].astype(BF16)
        w_ref[1][:, cols] = stage[1][s].astype(BF16)
        w_ref[2][cols, :] = stage[2][s].astype(BF16)
        gate = jnp.dot(hb, w_ref[0][:, cols], preferred_element_type=F32)
        up = jnp.dot(hb, w_ref[1][:, cols], preferred_element_type=F32)
        act = (gate * (1.0 / (1.0 + jnp.exp(-gate))) * up).astype(BF16)
        part = jnp.dot(act, w_ref[2][cols, :], preferred_element_type=F32)
        acc = part if acc is None else acc + part
    return acc


def _swiglu_weight_scratch(d, d_ff):
    return [
        pltpu.VMEM((d, d_ff), BF16),
        pltpu.VMEM((d, d_ff), BF16),
        pltpu.VMEM((d_ff, d), BF16),
        pltpu.VMEM((2, d, WEIGHT_COLS), F32),
        pltpu.VMEM((2, d, WEIGHT_COLS), F32),
        pltpu.VMEM((2, WEIGHT_COLS, d), F32),
        pltpu.SemaphoreType.DMA((2, 3)),
    ]


def _ffn_kernel(j, x_ref, g_ref, wg_hbm, wu_hbm, wd_hbm, o_ref,
                wg_ref, wu_ref, wd_ref, stage_g, stage_u, stage_d, sem):
    x = x_ref[...]
    hb = _rms(x, g_ref[...]).astype(BF16)
    w_ref = (wg_ref, wu_ref, wd_ref)

    @pl.when(pl.program_id(0) == 0)
    def _():
        o_ref[...] = x + _swiglu_rows_fetching(hb, (wg_hbm.at[j], wu_hbm.at[j], wd_hbm.at[j]), w_ref,
                                               (stage_g, stage_u, stage_d), sem)

    @pl.when(pl.program_id(0) > 0)
    def _():
        o_ref[...] = x + _swiglu_rows(hb, *w_ref)


def _ffn(x, g, j, w_gate, w_up, w_down):
    t, d = x.shape
    d_ff = w_gate.shape[2]
    tm = FFN_TILE
    return pl.pallas_call(
        functools.partial(_ffn_kernel, j),
        out_shape=jax.ShapeDtypeStruct((t, d), F32),
        grid=(t // tm,),
        in_specs=[
            pl.BlockSpec((tm, d), lambda i: (i, 0)),
            _resident((1, d), lambda i: (0, 0)),
            pl.BlockSpec(memory_space=pl.ANY),
            pl.BlockSpec(memory_space=pl.ANY),
            pl.BlockSpec(memory_space=pl.ANY),
        ],
        out_specs=pl.BlockSpec((tm, d), lambda i: (i, 0)),
        scratch_shapes=_swiglu_weight_scratch(d, d_ff),
        compiler_params=pltpu.CompilerParams(
            dimension_semantics=("arbitrary",), vmem_limit_bytes=VMEM_LIMIT_BYTES),
        name="dense_ffn",
    )(x, g.reshape(1, d), w_gate, w_up, w_down)


def _odd_mixer_kernel(x_ref, g_ref, win_ref, cw_ref, cb_ref, gw_ref, gab_ref, gxb_ref, lam_ref, wout_ref,
                      g2_ref, wr_ref, tri_ref, o_ref, h2_ref, route_ref, rt_ref, cnt_out_ref,
                      xbuf_ref, a_ref, b_ref, hs_ref, hcar_ref, cnt_ref):
    nb, tl, d = x_ref.shape
    ts = nb * tl
    w = cw_ref.shape[1]
    hd = w // LRU_HEADS
    st = pl.program_id(0)

    @pl.when(st == 0)
    def _():
        xbuf_ref[:, 0:SUBLANES, :] = jnp.zeros((nb, SUBLANES, w), F32)
        hcar_ref[...] = jnp.zeros(hcar_ref.shape, F32)
        cnt_ref[...] = jnp.zeros(cnt_ref.shape, F32)

    x = x_ref[...].reshape(ts, d)
    h = _rms(x, g_ref[...]).astype(BF16)
    proj = jnp.dot(h, win_ref[...], preferred_element_type=F32)
    y = proj[:, 0:w]
    half_y = 0.5 * y
    y = half_y + half_y * jnp.tanh(y * (GELU_C0 + (GELU_C0 * GELU_C1) * (y * y)))
    xb = proj[:, w:2 * w]

    xc_rows = []
    for bi in range(nb):
        xb_seq = xb[bi * tl:(bi + 1) * tl, :]
        xbuf_ref[bi, SUBLANES:SUBLANES + tl, :] = xb_seq
        acc = cw_ref[CONV_K_C - 1:CONV_K_C, :] * xb_seq + cb_ref[...]
        for kk in range(CONV_K_C - 1):
            back = CONV_K_C - 1 - kk
            acc = acc + cw_ref[kk:kk + 1, :] * xbuf_ref[bi, SUBLANES - back:SUBLANES - back + tl, :]
        xc_rows.append(acc)
    xc = jnp.concatenate(xc_rows, axis=0)

    r_cols = []
    i_cols = []
    for hh in range(LRU_HEADS):
        ri = jnp.dot(xc[:, hh * hd:(hh + 1) * hd].astype(BF16), gw_ref[hh], preferred_element_type=F32)
        r_cols.append(ri[:, 0:hd])
        i_cols.append(ri[:, hd:2 * hd])
    tanh_r = jnp.tanh(jnp.concatenate(r_cols, axis=1) + gab_ref[...])
    ig = 0.5 + 0.5 * jnp.tanh(jnp.concatenate(i_cols, axis=1) + gxb_ref[...])
    neg_lam = -lam_ref[...]
    softplus = jnp.maximum(neg_lam, 0.0) + jnp.log1p(jnp.exp(-jnp.abs(neg_lam)))
    half_rate = (-0.5 * LRU_C) * softplus
    log_a = half_rate + half_rate * tanh_r
    a = jnp.exp(log_a)
    m2 = 1.0 - a * a
    mult = jnp.where(m2 > 0.0, m2 * lax.rsqrt(m2), 0.0)
    gated = ig * xc
    bb = mult * gated

    for c in range(w // LANES):
        cols = slice(c * LANES, (c + 1) * LANES)
        for bi in range(nb):
            a_ref[c, pl.ds(bi, tl, stride=nb), :] = a[bi * tl:(bi + 1) * tl, cols]
            b_ref[c, pl.ds(bi, tl, stride=nb), :] = bb[bi * tl:(bi + 1) * tl, cols]

    @pl.when(st == 0)
    def _():
        for c in range(w // LANES):
            cols = slice(c * LANES, (c + 1) * LANES)
            b_ref[c, 0:nb, :] = jnp.concatenate([gated[bi * tl:bi * tl + 1, cols] for bi in range(nb)], axis=0)

    def step(t, hprev):
        off = pl.multiple_of(t * nb, nb)
        hnew = []
        for c in range(w // LANES):
            hc = a_ref[c, pl.ds(off, nb), :] * hprev[c] + b_ref[c, pl.ds(off, nb), :]
            hs_ref[c, pl.ds(off, nb), :] = hc
            hnew.append(hc)
        return tuple(hnew)

    hlast = lax.fori_loop(0, tl, step, tuple(hcar_ref[c] for c in range(w // LANES)), unroll=4)
    for c in range(w // LANES):
        hcar_ref[c] = hlast[c]
    hs = jnp.concatenate(
        [jnp.concatenate([hs_ref[c, pl.ds(bi, tl, stride=nb), :] for c in range(w // LANES)], axis=1)
         for bi in range(nb)], axis=0)

    out = x + jnp.dot((y * hs).astype(BF16), wout_ref[...], preferred_element_type=F32)
    o_ref[...] = out.reshape(nb, tl, d)
    xbuf_ref[:, 0:SUBLANES, :] = xbuf_ref[:, tl:tl + SUBLANES, :]

    h2 = _rms(out, g2_ref[...])
    for bi in range(nb):
        _store_row_tiles(h2_ref.at[bi], 0, h2[bi * tl:(bi + 1) * tl, :])
    logits = lax.dot_general(wr_ref[...], h2.astype(BF16), (((1,), (1,)), ((), ())),
                             preferred_element_type=F32)
    erow = lax.broadcasted_iota(jnp.int32, (ROUTE_ROWS, ts), 0)
    lg = jnp.where(erow < N_EXPERTS, logits, -jnp.inf)
    m1 = jnp.max(lg, axis=0, keepdims=True)
    i1 = jnp.min(jnp.where(lg == m1, erow, ROUTE_ROWS), axis=0, keepdims=True)
    lg2 = jnp.where(erow == i1, -jnp.inf, lg)
    m2 = jnp.max(lg2, axis=0, keepdims=True)
    i2 = jnp.min(jnp.where(lg2 == m2, erow, ROUTE_ROWS), axis=0, keepdims=True)
    e2 = jnp.exp(m2 - m1)
    gate1 = 1.0 / (1.0 + e2)
    gate2 = e2 / (1.0 + e2)

    first = erow == i1
    second = erow == i2
    ind = jnp.where(first | second, 1.0, 0.0)
    cums = jnp.dot(ind.astype(BF16), tri_ref[...], preferred_element_type=F32)
    rank = cnt_ref[...] + cums - ind
    cnt_ref[...] = cnt_ref[...] + jnp.broadcast_to(cums[:, ts - 1:ts], cums.shape)
    rank1 = jnp.sum(jnp.where(first, rank, 0.0), axis=0, keepdims=True)
    rank2 = jnp.sum(jnp.where(second, rank, 0.0), axis=0, keepdims=True)
    cnt_out_ref[...] = cnt_ref[:, 0:ROUTE_LANES].astype(jnp.int32)

    fields = [None] * SUBLANES
    fields[ROUTE_E0], fields[ROUTE_E0 + 1] = i1.astype(F32), i2.astype(F32)
    fields[ROUTE_GATE0], fields[ROUTE_GATE0 + 1] = gate1, gate2
    fields[ROUTE_RANK0], fields[ROUTE_RANK0 + 1] = rank1, rank2
    zero_row = jnp.zeros((1, ts), F32)
    record_t = jnp.concatenate([zero_row if f is None else f for f in fields], axis=0)
    rt_ref[...] = record_t.astype(jnp.int32)
    route_ref[...] = jnp.concatenate(
        [record_t, jnp.zeros((ROUTE_LANES - SUBLANES, ts), F32)], axis=0).T.reshape(nb, tl, ROUTE_LANES)


def _odd_mixer(x, g, w_in, conv_w, conv_b, ga_w, ga_b, gx_w, gx_b, lam, w_out, g2, w_router):
    b, s, d = x.shape
    assert b == SUBLANES, "one timestep of all sequences must fill one sublane group"
    tl = SEQ_TILE // b
    ts = b * tl
    w = conv_w.shape[1]
    hd = w // LRU_HEADS
    gw = (0.5 * jnp.concatenate([ga_w, gx_w], axis=-1)).astype(BF16)
    ga_b, gx_b = 0.5 * ga_b, 0.5 * gx_b
    wr = jnp.pad(w_router.T, ((0, ROUTE_ROWS - N_EXPERTS), (0, 0))).astype(BF16)
    tri = jnp.triu(jnp.ones((ts, ts), BF16))
    nst = s // tl
    const = lambda si: (0, 0)
    row = lambda a: a.reshape(1, -1)
    tile = lambda si: (0, si, 0)
    out, h2, route, rt, counts = pl.pallas_call(
        _odd_mixer_kernel,
        out_shape=(jax.ShapeDtypeStruct((b, s, d), F32),
                   jax.ShapeDtypeStruct((b, s * SUBLANES, LANES), F32),
                   jax.ShapeDtypeStruct((b, s, ROUTE_LANES), F32),
                   jax.ShapeDtypeStruct((nst, SUBLANES, ts), jnp.int32),
                   jax.ShapeDtypeStruct((ROUTE_ROWS, ROUTE_LANES), jnp.int32)),
        grid=(nst,),
        in_specs=[
            pl.BlockSpec((b, tl, d), tile),
            _resident((1, d), const),
            _resident((d, 2 * w), const),
            _resident((CONV_K_C, w), const),
            _resident((1, w), const),
            _resident((LRU_HEADS, hd, 2 * hd), lambda si: (0, 0, 0)),
            _resident((1, w), const),
            _resident((1, w), const),
            _resident((1, w), const),
            _resident((w, d), const),
            _resident((1, d), const),
            _resident((ROUTE_ROWS, d), const),
            _resident((ts, ts), const),
        ],
        out_specs=(pl.BlockSpec((b, tl, d), tile),
                   pl.BlockSpec((b, tl * SUBLANES, LANES), tile),
                   pl.BlockSpec((b, tl, ROUTE_LANES), tile),
                   pl.BlockSpec((None, SUBLANES, ts), lambda si: (si, 0, 0)),
                   pl.BlockSpec((ROUTE_ROWS, ROUTE_LANES), const)),
        scratch_shapes=[
            pltpu.VMEM((b, SUBLANES + tl, w), F32),
            pltpu.VMEM((w // LANES, ts, LANES), F32),
            pltpu.VMEM((w // LANES, ts, LANES), F32),
            pltpu.VMEM((w // LANES, ts, LANES), F32),
            pltpu.VMEM((w // LANES, b, LANES), F32),
            pltpu.VMEM((ROUTE_ROWS, ts), F32),
        ],
        compiler_params=pltpu.CompilerParams(
            dimension_semantics=("arbitrary",), vmem_limit_bytes=VMEM_LIMIT_BYTES),
        name="odd_mixer",
    )(x, row(g), w_in.astype(BF16), conv_w, row(conv_b), gw, row(ga_b), row(gx_b), row(lam),
      w_out.astype(BF16), row(g2), wr, tri)
    rt = rt.reshape(nst, SUBLANES, b, tl).transpose(1, 2, 0, 3).reshape(SUBLANES, b * s)
    return out, h2.reshape(b * s * SUBLANES, LANES), route, rt, counts


def _moe_tables(rt, counts, t):
    r = MOE_ROWS
    tm = MOVE_TILE
    n_blocks = TOP_K * t // r + N_EXPERTS
    counts = counts[0:N_EXPERTS, 0]
    padded = (counts + r - 1) // r * r
    pad_end = jnp.cumsum(padded)
    pad_start = pad_end - padded
    e = rt[ROUTE_E0:ROUTE_E0 + TOP_K, :]
    rank = rt[ROUTE_RANK0:ROUTE_RANK0 + TOP_K, :]
    start = jnp.zeros_like(e)
    for ex in range(N_EXPERTS):
        start = jnp.where(e == ex, pad_start[ex], start)
    dest = start + rank
    blk = jnp.arange(n_blocks, dtype=jnp.int32)
    blk_expert = jnp.minimum(jnp.sum(blk[:, None] * r >= pad_end[None, :], axis=1), N_EXPERTS - 1)
    n_used = pad_end[N_EXPERTS - 1] // r
    last_blk = jnp.where(padded > 0, pad_end // r - 1, -1)
    has_pad = jnp.any(blk[:, None] == last_blk[None, :], axis=1) | (blk >= n_used)
    dest = dest.astype(jnp.int32).reshape(TOP_K, t // tm, tm).transpose(1, 0, 2).reshape(t // tm, TOP_K * tm)
    return dest, blk_expert.astype(jnp.int32), has_pad.astype(jnp.int32), n_used.astype(jnp.int32).reshape(1)


def _row_copy(src, src_row, dst, dst_row, sem):
    return pltpu.make_async_copy(src.at[pl.ds(src_row * SUBLANES, SUBLANES), :],
                                 dst.at[pl.ds(dst_row * SUBLANES, SUBLANES), :], sem)


def _rows_wait(src, dst, n_rows, sem):
    pltpu.make_async_copy(src.at[pl.ds(0, n_rows * SUBLANES), :],
                          dst.at[pl.ds(0, n_rows * SUBLANES), :], sem).wait()


def _dispatch_kernel(pad_ref, dest_hbm, h_ref, xg_hbm, dsm, stage, zbuf, sem_idx, sem_row, sem_zero):
    i = pl.program_id(0)
    n = pl.num_programs(0)
    tm = h_ref.shape[0] // SUBLANES
    r = zbuf.shape[0] // SUBLANES
    n_blocks = pad_ref.shape[0]
    slot = lax.rem(i, 2)

    def idx_copy(step, s):
        return pltpu.make_async_copy(dest_hbm.at[step], dsm.at[pl.ds(s * TOP_K * tm, TOP_K * tm)],
                                     sem_idx.at[s])

    def zero_copy(blk):
        return pltpu.make_async_copy(zbuf, xg_hbm.at[pl.ds(blk * r * SUBLANES, r * SUBLANES), :], sem_zero)

    @pl.when(i == 0)
    def _():
        idx_copy(0, 0).start()
        zbuf[...] = jnp.zeros(zbuf.shape, zbuf.dtype)

        def zero_start(blk, c):
            @pl.when(pad_ref[blk] == 1)
            def _():
                zero_copy(blk).start()
            return c

        def zero_wait(blk, c):
            @pl.when(pad_ref[blk] == 1)
            def _():
                zero_copy(blk).wait()
            return c

        lax.fori_loop(0, n_blocks, zero_start, 0)
        lax.fori_loop(0, n_blocks, zero_wait, 0)

    @pl.when(i + 1 < n)
    def _():
        idx_copy(i + 1, 1 - slot).start()

    idx_copy(i, slot).wait()

    stage_row0 = slot * tm
    stage[pl.ds(stage_row0 * SUBLANES, tm * SUBLANES), :] = h_ref[...]

    def issue(row, c):
        for k in range(TOP_K):
            _row_copy(stage, stage_row0 + row, xg_hbm, dsm[(slot * TOP_K + k) * tm + row],
                      sem_row.at[slot]).start(priority=k % DMA_THREADS)
        return c

    lax.fori_loop(0, tm, issue, 0, unroll=ROW_DMA_UNROLL)

    def wait_slot(s):
        for _ in range(TOP_K):
            _rows_wait(stage, xg_hbm, tm, sem_row.at[s])

    @pl.when(i > 0)
    def _():
        wait_slot(1 - slot)

    @pl.when(i == n - 1)
    def _():
        wait_slot(slot)


def _dispatch(h2, dest, has_pad):
    tm = dest.shape[1] // TOP_K
    t = h2.shape[0] // SUBLANES
    r = MOE_ROWS
    n_rows = TOP_K * t + N_EXPERTS * r
    grid_spec = pltpu.PrefetchScalarGridSpec(
        num_scalar_prefetch=1,
        grid=(t // tm,),
        in_specs=[pl.BlockSpec(memory_space=pl.ANY),
                  pl.BlockSpec((tm * SUBLANES, LANES), lambda i, pad: (i, 0))],
        out_specs=pl.BlockSpec(memory_space=pl.ANY),
        scratch_shapes=[
            pltpu.SMEM((2 * TOP_K * tm,), jnp.int32),
            pltpu.VMEM((2 * tm * SUBLANES, LANES), F32),
            pltpu.VMEM((r * SUBLANES, LANES), F32),
            pltpu.SemaphoreType.DMA((2,)),
            pltpu.SemaphoreType.DMA((2,)),
            pltpu.SemaphoreType.DMA,
        ],
    )
    return pl.pallas_call(
        _dispatch_kernel,
        out_shape=jax.ShapeDtypeStruct((n_rows * SUBLANES, LANES), F32),
        grid_spec=grid_spec,
        compiler_params=pltpu.CompilerParams(dimension_semantics=("arbitrary",)),
        name="moe_dispatch",
    )(has_pad, dest, h2)


def _expert_kernel(j, be_ref, nu_ref, x_ref, wg_hbm, wu_hbm, wd_hbm, y_ref,
                   wg_ref, wu_ref, wd_ref, stage_g, stage_u, stage_d, sem):
    i = pl.program_id(0)
    r = x_ref.shape[0] // SUBLANES
    e = be_ref[i]
    used = i < nu_ref[0]
    new_expert = (i == 0) | (e != be_ref[jnp.maximum(i - 1, 0)])
    w_ref = (wg_ref, wu_ref, wd_ref)

    @pl.when(used & new_expert)
    def _():
        xb = _load_row_tiles(x_ref, 0, r).astype(BF16)
        y = _swiglu_rows_fetching(xb, (wg_hbm.at[j, e], wu_hbm.at[j, e], wd_hbm.at[j, e]), w_ref,
                                  (stage_g, stage_u, stage_d), sem)
        _store_row_tiles(y_ref, 0, y)

    @pl.when(used & jnp.logical_not(new_expert))
    def _():
        xb = _load_row_tiles(x_ref, 0, r).astype(BF16)
        _store_row_tiles(y_ref, 0, _swiglu_rows(xb, *w_ref))

    @pl.when(jnp.logical_not(used))
    def _():
        y_ref[...] = jnp.zeros(y_ref.shape, y_ref.dtype)


def _experts(xg, blk_expert, n_used, j, w_gate, w_up, w_down):
    d, d_ff = w_gate.shape[2], w_gate.shape[3]
    r = MOE_ROWS
    rows = lambda i, be, nu: (i, 0)
    grid_spec = pltpu.PrefetchScalarGridSpec(
        num_scalar_prefetch=2,
        grid=(xg.shape[0] // (r * SUBLANES),),
        in_specs=[
            pl.BlockSpec((r * SUBLANES, LANES), rows),
            pl.BlockSpec(memory_space=pl.ANY),
            pl.BlockSpec(memory_space=pl.ANY),
            pl.BlockSpec(memory_space=pl.ANY),
        ],
        out_specs=pl.BlockSpec((r * SUBLANES, LANES), rows),
        scratch_shapes=_swiglu_weight_scratch(d, d_ff),
    )
    return pl.pallas_call(
        functools.partial(_expert_kernel, j),
        out_shape=jax.ShapeDtypeStruct(xg.shape, F32),
        grid_spec=grid_spec,
        compiler_params=pltpu.CompilerParams(
            dimension_semantics=("arbitrary",), vmem_limit_bytes=VMEM_LIMIT_BYTES),
        name="moe_experts",
    )(blk_expert, n_used, xg, w_gate, w_up, w_down)


def _combine_kernel(dest_hbm, y_hbm, x_ref, route_ref, o_ref, dsm, ybuf, sem_idx, sem_row):
    i = pl.program_id(0)
    n = pl.num_programs(0)
    tm = x_ref.shape[0]
    slot = lax.rem(i, 2)

    def idx_copy(step):
        s3 = lax.rem(step, 3)
        return pltpu.make_async_copy(dest_hbm.at[step], dsm.at[pl.ds(s3 * TOP_K * tm, TOP_K * tm)],
                                     sem_idx.at[s3])

    def request_rows(step):
        s2 = lax.rem(step, 2)
        table = lax.rem(step, 3) * TOP_K * tm

        def issue(row, c):
            for k in range(TOP_K):
                _row_copy(y_hbm, dsm[table + k * tm + row], ybuf, (s2 * TOP_K + k) * tm + row,
                          sem_row.at[s2]).start(priority=k % DMA_THREADS)
            return c

        lax.fori_loop(0, tm, issue, 0, unroll=ROW_DMA_UNROLL)

    @pl.when(i == 0)
    def _():
        idx_copy(0).start()
        if_next = n > 1

        @pl.when(if_next)
        def _():
            idx_copy(1).start()

        idx_copy(0).wait()
        request_rows(0)

    @pl.when(i + 2 < n)
    def _():
        idx_copy(i + 2).start()

    @pl.when(i + 1 < n)
    def _():
        idx_copy(i + 1).wait()
        request_rows(i + 1)

    for _ in range(TOP_K):
        _rows_wait(y_hbm, ybuf, tm, sem_row.at[slot])
    rt = route_ref[...]
    moe = None
    for k in range(TOP_K):
        yk = _load_row_tiles(ybuf, (slot * TOP_K + k) * tm * SUBLANES, tm)
        term = yk * rt[:, ROUTE_GATE0 + k:ROUTE_GATE0 + k + 1]
        moe = term if moe is None else moe + term
    o_ref[...] = x_ref[...] + moe


def _combine(x, y, route, dest):
    t, d = x.shape
    tm = dest.shape[1] // TOP_K
    return pl.pallas_call(
        _combine_kernel,
        out_shape=jax.ShapeDtypeStruct((t, d), F32),
        grid=(t // tm,),
        in_specs=[
            pl.BlockSpec(memory_space=pl.ANY),
            pl.BlockSpec(memory_space=pl.ANY),
            pl.BlockSpec((tm, d), lambda i: (i, 0)),
            pl.BlockSpec((tm, ROUTE_LANES), lambda i: (i, 0)),
        ],
        out_specs=pl.BlockSpec((tm, d), lambda i: (i, 0)),
        scratch_shapes=[
            pltpu.SMEM((3 * TOP_K * tm,), jnp.int32),
            pltpu.VMEM((2 * TOP_K * tm * SUBLANES, LANES), F32),
            pltpu.SemaphoreType.DMA((3,)),
            pltpu.SemaphoreType.DMA((2,)),
        ],
        compiler_params=pltpu.CompilerParams(
            dimension_semantics=("arbitrary",), vmem_limit_bytes=VMEM_LIMIT_BYTES),
        name="moe_combine",
    )(dest, y, x, route)


def kernel(x, norm_mix, norm_ffn, hy_w_in, hy_q_gain, hy_k_gain, hy_sinks, hy_conv_w, hy_w_out, rg_w_in, rg_conv_w, rg_conv_b, rg_gate_a_w, rg_gate_a_b, rg_gate_x_w, rg_gate_x_b, rg_lambda, rg_w_out, ffn_w_gate, ffn_w_up, ffn_w_down, moe_router, moe_w_gate, moe_w_up, moe_w_down):
    b, s, d = x.shape
    t = b * s
    depth = norm_mix.shape[0]
    for layer in range(depth):
        j = layer // 2
        if layer % 2 == 0:
            x = _even_mixer(x, norm_mix[layer], hy_w_in[j], hy_q_gain[j], hy_k_gain[j], hy_sinks[j],
                            hy_conv_w[j], hy_w_out[j])
            x = _ffn(x.reshape(t, d), norm_ffn[layer], j, ffn_w_gate, ffn_w_up, ffn_w_down)
            x = x.reshape(b, s, d)
        else:
            x, h2, route, rt, counts = _odd_mixer(
                x, norm_mix[layer], rg_w_in[j], rg_conv_w[j], rg_conv_b[j], rg_gate_a_w[j], rg_gate_a_b[j],
                rg_gate_x_w[j], rg_gate_x_b[j], rg_lambda[j], rg_w_out[j], norm_ffn[layer], moe_router[j])
            dest, blk_expert, has_pad, n_used = _moe_tables(rt, counts, t)
            xg = _dispatch(h2, dest, has_pad)
            y = _experts(xg, blk_expert, n_used, j, moe_w_gate, moe_w_up, moe_w_down)
            x = _combine(x.reshape(t, d), y, route.reshape(t, ROUTE_LANES), dest).reshape(b, s, d)
    return x
```

```python
import functools

import jax
import jax.numpy as jnp
from jax import lax
from jax.experimental import pallas as pl
from jax.experimental.pallas import tpu as pltpu

F32 = jnp.float32
BF16 = jnp.bfloat16

LANES = 128
SUBLANES = 8
VMEM_LIMIT_BYTES = 56 * 1024 * 1024
DMA_THREADS = 2
ROW_DMA_UNROLL = 32

N_Q_HEADS = 8
N_KV_HEADS = 2
HEAD_DIM = 64
Q_GROUP = N_Q_HEADS // N_KV_HEADS
WINDOW = 128
ATTN_BLOCK = 128
ALIBI_MAX = 8.0
CONV_K_B = 3
CONV_K_C = 4
LRU_HEADS = 8
LRU_C = 8.0
N_EXPERTS = 8
TOP_K = 2
NORM_EPS = 1e-6
NEG_INF = -1e30

SEQ_TILE = 512
MOVE_TILE = 512
EVEN_TILE = 512
FFN_TILE = 512
MOE_ROWS = 512
FF_CHUNK = 1792
WEIGHT_COLS = 512
ROUTE_ROWS = 16
ROUTE_LANES = LANES
ROUTE_E0 = 0
ROUTE_GATE0 = 2
ROUTE_RANK0 = 4
GELU_C0 = 0.7978845608028654
GELU_C1 = 0.044715


def _rms(x, g):
    return x * lax.rsqrt(jnp.mean(x * x, axis=-1, keepdims=True) + NORM_EPS) * g


def _split_bf16(x):
    hi = x.astype(BF16)
    lo = (x - hi.astype(F32)).astype(BF16)
    return hi, lo


def _store_row_tiles(ref, start, x):
    n = x.shape[0]
    for c in range(x.shape[1] // LANES):
        ref[pl.ds(start + c, n, stride=SUBLANES), :] = x[:, c * LANES:(c + 1) * LANES]


def _load_row_tiles(ref, start, n):
    return jnp.concatenate([ref[pl.ds(start + c, n, stride=SUBLANES), :] for c in range(SUBLANES)], axis=1)


def _resident(shape, index_map):
    return pl.BlockSpec(shape, index_map, pipeline_mode=pl.Buffered(1))


def _even_mixer_kernel(sinks_ref, x_ref, g_ref, win_ref, qg_ref, kg_ref, cw_ref, wout_ref, o_ref,
                       kd_ref, vt_ref, cbuf_ref, attn_t_ref):
    ts = x_ref.shape[0]
    q_dim = N_Q_HEADS * HEAD_DIM
    kv_dim = N_KV_HEADS * HEAD_DIM
    cw_dim = cw_ref.shape[1]
    st = pl.program_id(1)

    @pl.when(st == 0)
    def _():
        kd_ref[:, 0:ATTN_BLOCK, :] = jnp.zeros((N_KV_HEADS, ATTN_BLOCK, LANES), BF16)
        vt_ref[:, 0:ATTN_BLOCK] = jnp.zeros((kv_dim, ATTN_BLOCK), BF16)
        cbuf_ref[0:SUBLANES, :] = jnp.zeros((SUBLANES, cw_dim), F32)

    x = x_ref[...]
    h = _rms(x, g_ref[...]).astype(BF16)
    proj = jnp.dot(h, win_ref[...], preferred_element_type=F32)
    q = proj[:, 0:q_dim]
    k = proj[:, q_dim:q_dim + kv_dim]
    v = proj[:, q_dim + kv_dim:q_dim + 2 * kv_dim]
    o0 = q_dim + 2 * kv_dim
    gate_b = proj[:, o0:o0 + cw_dim]
    gate_c = proj[:, o0 + cw_dim:o0 + 2 * cw_dim]
    u = proj[:, o0 + 2 * cw_dim:o0 + 3 * cw_dim]

    ri = lax.broadcasted_iota(jnp.int32, (LANES, LANES), 0)
    ci = lax.broadcasted_iota(jnp.int32, (LANES, LANES), 1)
    half_mean = jnp.where((ri < HEAD_DIM) == (ci < HEAD_DIM), 1.0 / HEAD_DIM, 0.0).astype(BF16)

    def half_mean_square(z):
        hi, lo = _split_bf16(z * z)
        cols = []
        for c in range(z.shape[1] // LANES):
            sl = slice(c * LANES, (c + 1) * LANES)
            cols.append(jnp.dot(hi[:, sl], half_mean, preferred_element_type=F32)
                        + jnp.dot(lo[:, sl], half_mean, preferred_element_type=F32))
        return cols[0] if len(cols) == 1 else jnp.concatenate(cols, axis=1)

    qn = q * lax.rsqrt(half_mean_square(q) + NORM_EPS) * qg_ref[...] * (HEAD_DIM ** -0.5)
    kn = k * lax.rsqrt(half_mean_square(k) + NORM_EPS) * kg_ref[...]

    low_half_t = lax.broadcasted_iota(jnp.int32, (ts, LANES), 1) < HEAD_DIM
    k_sw = pltpu.roll(kn, HEAD_DIM, axis=1)
    kd_ref[0, ATTN_BLOCK:ATTN_BLOCK + ts, :] = jnp.where(low_half_t, kn, k_sw).astype(BF16)
    kd_ref[1, ATTN_BLOCK:ATTN_BLOCK + ts, :] = jnp.where(low_half_t, k_sw, kn).astype(BF16)
    vt_ref[:, ATTN_BLOCK:ATTN_BLOCK + ts] = v.T.astype(BF16)

    low_half = lax.broadcasted_iota(jnp.int32, (ATTN_BLOCK, LANES), 1) < HEAD_DIM
    sj = lax.broadcasted_iota(jnp.int32, (2 * ATTN_BLOCK, ATTN_BLOCK), 0)
    qi = lax.broadcasted_iota(jnp.int32, (2 * ATTN_BLOCK, ATTN_BLOCK), 1)
    diff = ATTN_BLOCK + qi - sj
    in_window = (diff >= 0) & (diff < WINDOW)
    diff_f = diff.astype(F32)
    first_key = jnp.where(st == 0, ATTN_BLOCK, 0)
    in_window_first = in_window & (sj >= first_key)

    def penalty(valid, g):
        cols = []
        for hh in range(Q_GROUP):
            slope = 2.0 ** (-ALIBI_MAX * (g * Q_GROUP + hh + 1) / N_Q_HEADS)
            cols.append(jnp.where(valid, slope * diff_f, -NEG_INF))
        return jnp.concatenate(cols, axis=1)

    pen = [penalty(in_window, g) for g in range(N_KV_HEADS)]
    pen_first = [penalty(in_window_first, g) for g in range(N_KV_HEADS)]
    sink = [jnp.concatenate([jnp.full((1, ATTN_BLOCK), sinks_ref[0, g * Q_GROUP + hh], F32)
                             for hh in range(Q_GROUP)], axis=1) for g in range(N_KV_HEADS)]

    problems = [(j, g) for j in range(ts // ATTN_BLOCK) for g in range(N_KV_HEADS)]
    rows = lambda j: slice(j * ATTN_BLOCK, (j + 1) * ATTN_BLOCK)
    band = lambda j: slice(j * ATTN_BLOCK, (j + 2) * ATTN_BLOCK)

    scores = []
    for j, g in problems:
        q_rows = []
        for p in range(Q_GROUP // 2):
            c = g * (Q_GROUP // 2) + p
            qp = qn[rows(j), c * LANES:(c + 1) * LANES]
            q_rows.append(jnp.where(low_half, qp, 0.0))
            q_rows.append(jnp.where(low_half, 0.0, qp))
        qs = jnp.concatenate(q_rows, axis=0).astype(BF16)
        s = lax.dot_general(kd_ref[g, band(j), :], qs, (((1,), (1,)), ((), ())),
                            preferred_element_type=F32)
        scores.append(s - (pen_first[g] if j == 0 else pen[g]))
    peaks = [jnp.maximum(jnp.max(sh, axis=0, keepdims=True), sink[g]) for sh, (j, g) in zip(scores, problems)]
    probs = [jnp.exp(sh - m) for sh, m in zip(scores, peaks)]
    denoms = [jnp.sum(pr, axis=0, keepdims=True) + jnp.exp(sink[g] - m)
              for pr, m, (j, g) in zip(probs, peaks, problems)]
    for pr, denom, (j, g) in zip(probs, denoms, problems):
        o = jnp.dot(vt_ref[g * HEAD_DIM:(g + 1) * HEAD_DIM, band(j)], pr.astype(BF16),
                    preferred_element_type=F32)
        o = o / denom
        for hh in range(Q_GROUP):
            head = g * Q_GROUP + hh
            attn_t_ref[head * HEAD_DIM:(head + 1) * HEAD_DIM, rows(j)] = (
                o[:, hh * ATTN_BLOCK:(hh + 1) * ATTN_BLOCK])

    cu = gate_c * u
    cbuf_ref[SUBLANES:SUBLANES + ts, :] = cu
    conv = cw_ref[CONV_K_B - 1:CONV_K_B, :] * cu
    for kk in range(CONV_K_B - 1):
        back = CONV_K_B - 1 - kk
        conv = conv + cw_ref[kk:kk + 1, :] * cbuf_ref[SUBLANES - back:SUBLANES - back + ts, :]
    conv = gate_b * conv

    mix = jnp.concatenate([attn_t_ref[...].T, conv], axis=1).astype(BF16)
    o_ref[...] = x + jnp.dot(mix, wout_ref[...], preferred_element_type=F32)

    kd_ref[:, 0:ATTN_BLOCK, :] = kd_ref[:, ts:ts + ATTN_BLOCK, :]
    vt_ref[:, 0:ATTN_BLOCK] = vt_ref[:, ts:ts + ATTN_BLOCK]
    cbuf_ref[0:SUBLANES, :] = cbuf_ref[ts:ts + SUBLANES, :]


def _even_mixer(x, g, w_in, q_gain, k_gain, sinks, conv_w, w_out):
    b, s, d = x.shape
    ts = EVEN_TILE
    n_in = w_in.shape[1]
    cw_dim = conv_w.shape[1]
    mix_dim = w_out.shape[0]
    qg = jnp.tile(q_gain, N_Q_HEADS).reshape(1, N_Q_HEADS * HEAD_DIM)
    kg = jnp.tile(k_gain, N_KV_HEADS).reshape(1, N_KV_HEADS * HEAD_DIM)
    const = lambda bi, si: (0, 0)
    return pl.pallas_call(
        _even_mixer_kernel,
        out_shape=jax.ShapeDtypeStruct((b, s, d), F32),
        grid=(b, s // ts),
        in_specs=[
            pl.BlockSpec(memory_space=pltpu.SMEM),
            pl.BlockSpec((None, ts, d), lambda bi, si: (bi, si, 0)),
            _resident((1, d), const),
            _resident((d, n_in), const),
            _resident((1, N_Q_HEADS * HEAD_DIM), const),
            _resident((1, N_KV_HEADS * HEAD_DIM), const),
            _resident((CONV_K_B, cw_dim), const),
            _resident((mix_dim, d), const),
        ],
        out_specs=pl.BlockSpec((None, ts, d), lambda bi, si: (bi, si, 0)),
        scratch_shapes=[
            pltpu.VMEM((N_KV_HEADS, ATTN_BLOCK + ts, LANES), BF16),
            pltpu.VMEM((N_KV_HEADS * HEAD_DIM, ATTN_BLOCK + ts), BF16),
            pltpu.VMEM((SUBLANES + ts, cw_dim), F32),
            pltpu.VMEM((N_Q_HEADS * HEAD_DIM, ts), F32),
        ],
        compiler_params=pltpu.CompilerParams(
            dimension_semantics=("arbitrary", "arbitrary"), vmem_limit_bytes=VMEM_LIMIT_BYTES),
        name="even_mixer",
    )(sinks.reshape(1, N_Q_HEADS), x, g.reshape(1, d), w_in.astype(BF16), qg, kg, conv_w,
      w_out.astype(BF16))


def _swiglu_rows(hb, wg_ref, wu_ref, wd_ref):
    d_ff = wg_ref.shape[1]
    acc = None
    for c in range(d_ff // FF_CHUNK):
        cols = slice(c * FF_CHUNK, (c + 1) * FF_CHUNK)
        gate = jnp.dot(hb, wg_ref[:, cols], preferred_element_type=F32)
        up = jnp.dot(hb, wu_ref[:, cols], preferred_element_type=F32)
        act = (gate * (1.0 / (1.0 + jnp.exp(-gate))) * up).astype(BF16)
        part = jnp.dot(act, wd_ref[cols, :], preferred_element_type=F32)
        acc = part if acc is None else acc + part
    return acc


def _swiglu_rows_fetching(hb, w_hbm, w_ref, stage, sem):
    d_ff = w_ref[0].shape[1]
    n = d_ff // WEIGHT_COLS

    def copies(c, s):
        cols = pl.ds(c * WEIGHT_COLS, WEIGHT_COLS)
        srcs = (w_hbm[0].at[:, cols], w_hbm[1].at[:, cols], w_hbm[2].at[cols, :])
        return [pltpu.make_async_copy(src, stage[m].at[s], sem.at[s, m]) for m, src in enumerate(srcs)]

    for cp in copies(0, 0):
        cp.start()
    acc = None
    for c in range(n):
        s = c % 2
        if c + 1 < n:
            for cp in copies(c + 1, 1 - s):
                cp.start()
        for cp in copies(c, s):
            cp.wait()
        cols = slice(c * WEIGHT_COLS, (c + 1) * WEIGHT_COLS)
        w_ref[0][:, cols] = stage[0][s].astype(BF16)
        w_ref[1][:, cols] = stage[1][s].astype(BF16)
        w_ref[2][cols, :] = stage[2][s].astype(BF16)
        gate = jnp.dot(hb, w_ref[0][:, cols], preferred_element_type=F32)
        up = jnp.dot(hb, w_ref[1][:, cols], preferred_element_type=F32)
        act = (gate * (1.0 / (1.0 + jnp.exp(-gate))) * up).astype(BF16)
        part = jnp.dot(act, w_ref[2][cols, :], preferred_element_type=F32)
        acc = part if acc is None else acc + part
    return acc


def _swiglu_weight_scratch(d, d_ff):
    return [
        pltpu.VMEM((d, d_ff), BF16),
        pltpu.VMEM((d, d_ff), BF16),
        pltpu.VMEM((d_ff, d), BF16),
        pltpu.VMEM((2, d, WEIGHT_COLS), F32),
        pltpu.VMEM((2, d, WEIGHT_COLS), F32),
        pltpu.VMEM((2, WEIGHT_COLS, d), F32),
        pltpu.SemaphoreType.DMA((2, 3)),
    ]


def _ffn_kernel(j, x_ref, g_ref, wg_hbm, wu_hbm, wd_hbm, o_ref,
                wg_ref, wu_ref, wd_ref, stage_g, stage_u, stage_d, sem):
    x = x_ref[...]
    hb = _rms(x, g_ref[...]).astype(BF16)
    w_ref = (wg_ref, wu_ref, wd_ref)

    @pl.when(pl.program_id(0) == 0)
    def _():
        o_ref[...] = x + _swiglu_rows_fetching(hb, (wg_hbm.at[j], wu_hbm.at[j], wd_hbm.at[j]), w_ref,
                                               (stage_g, stage_u, stage_d), sem)

    @pl.when(pl.program_id(0) > 0)
    def _():
        o_ref[...] = x + _swiglu_rows(hb, *w_ref)


def _ffn(x, g, j, w_gate, w_up, w_down):
    t, d = x.shape
    d_ff = w_gate.shape[2]
    tm = FFN_TILE
    return pl.pallas_call(
        functools.partial(_ffn_kernel, j),
        out_shape=jax.ShapeDtypeStruct((t, d), F32),
        grid=(t // tm,),
        in_specs=[
            pl.BlockSpec((tm, d), lambda i: (i, 0)),
            _resident((1, d), lambda i: (0, 0)),
            pl.BlockSpec(memory_space=pl.ANY),
            pl.BlockSpec(memory_space=pl.ANY),
            pl.BlockSpec(memory_space=pl.ANY),
        ],
        out_specs=pl.BlockSpec((tm, d), lambda i: (i, 0)),
        scratch_shapes=_swiglu_weight_scratch(d, d_ff),
        compiler_params=pltpu.CompilerParams(
            dimension_semantics=("arbitrary",), vmem_limit_bytes=VMEM_LIMIT_BYTES),
        name="dense_ffn",
    )(x, g.reshape(1, d), w_gate, w_up, w_down)


def _odd_mixer_kernel(x_ref, g_ref, win_ref, cw_ref, cb_ref, gw_ref, gab_ref, gxb_ref, lam_ref, wout_ref,
                      g2_ref, wr_ref, tri_ref, o_ref, h2_ref, route_ref, rt_ref, cnt_out_ref,
                      xbuf_ref, a_ref, b_ref, hs_ref, hcar_ref, cnt_ref):
    nb, tl, d = x_ref.shape
    ts = nb * tl
    w = cw_ref.shape[1]
    hd = w // LRU_HEADS
    st = pl.program_id(0)

    n_slabs = w // LANES
    past = (CONV_K_C - 1) * nb

    @pl.when(st == 0)
    def _():
        xbuf_ref[:, 0:past, :] = jnp.zeros((n_slabs, past, LANES), F32)
        hcar_ref[...] = jnp.zeros(hcar_ref.shape, F32)
        cnt_ref[...] = jnp.zeros(cnt_ref.shape, F32)

    x = x_ref[...].reshape(ts, d)
    h = _rms(x, g_ref[...]).astype(BF16)
    proj = jnp.dot(h, win_ref[...], preferred_element_type=F32)
    y = proj[:, 0:w]
    half_y = 0.5 * y
    y = half_y + half_y * jnp.tanh(y * (GELU_C0 + (GELU_C0 * GELU_C1) * (y * y)))
    xb = proj[:, w:2 * w]

    xc_slabs = []
    for c in range(n_slabs):
        cols = slice(c * LANES, (c + 1) * LANES)
        for bi in range(nb):
            xbuf_ref[c, pl.ds(past + bi, tl, stride=nb), :] = xb[bi * tl:(bi + 1) * tl, cols]
        acc = cb_ref[:, cols]
        for kk in range(CONV_K_C):
            back = (CONV_K_C - 1 - kk) * nb
            acc = acc + cw_ref[kk:kk + 1, cols] * xbuf_ref[c, past - back:past - back + ts, :]
        xc_slabs.append(acc)
    xc = jnp.concatenate(xc_slabs, axis=1)

    r_cols = []
    i_cols = []
    for hh in range(LRU_HEADS):
        ri = jnp.dot(xc[:, hh * hd:(hh + 1) * hd].astype(BF16), gw_ref[hh], preferred_element_type=F32)
        r_cols.append(ri[:, 0:hd])
        i_cols.append(ri[:, hd:2 * hd])
    tanh_r = jnp.tanh(jnp.concatenate(r_cols, axis=1) + gab_ref[...])
    ig = 0.5 + 0.5 * jnp.tanh(jnp.concatenate(i_cols, axis=1) + gxb_ref[...])
    neg_lam = -lam_ref[...]
    softplus = jnp.maximum(neg_lam, 0.0) + jnp.log1p(jnp.exp(-jnp.abs(neg_lam)))
    half_rate = (-0.5 * LRU_C) * softplus
    log_a = half_rate + half_rate * tanh_r
    a = jnp.exp(log_a)
    m2 = 1.0 - a * a
    mult = jnp.where(m2 > 0.0, m2 * lax.rsqrt(m2), 0.0)
    gated = ig * xc
    bb = mult * gated

    for c in range(n_slabs):
        cols = slice(c * LANES, (c + 1) * LANES)
        a_ref[c] = a[:, cols]
        b_ref[c] = bb[:, cols]

    @pl.when(st == 0)
    def _():
        for c in range(n_slabs):
            b_ref[c, 0:nb, :] = gated[0:nb, c * LANES:(c + 1) * LANES]

    def step(t, hprev):
        off = pl.multiple_of(t * nb, nb)
        hnew = []
        for c in range(w // LANES):
            hc = a_ref[c, pl.ds(off, nb), :] * hprev[c] + b_ref[c, pl.ds(off, nb), :]
            hs_ref[c, pl.ds(off, nb), :] = hc
            hnew.append(hc)
        return tuple(hnew)

    hlast = lax.fori_loop(0, tl, step, tuple(hcar_ref[c] for c in range(w // LANES)), unroll=4)
    for c in range(w // LANES):
        hcar_ref[c] = hlast[c]
    hs = jnp.concatenate(
        [jnp.concatenate([hs_ref[c, pl.ds(bi, tl, stride=nb), :] for c in range(w // LANES)], axis=1)
         for bi in range(nb)], axis=0)

    out = x + jnp.dot((y * hs).astype(BF16), wout_ref[...], preferred_element_type=F32)
    o_ref[...] = out.reshape(nb, tl, d)
    xbuf_ref[:, 0:past, :] = xbuf_ref[:, ts:ts + past, :]

    h2 = _rms(out, g2_ref[...])
    for bi in range(nb):
        _store_row_tiles(h2_ref.at[bi], 0, h2[bi * tl:(bi + 1) * tl, :])
    logits = lax.dot_general(wr_ref[...], h2.astype(BF16), (((1,), (1,)), ((), ())),
                             preferred_element_type=F32)
    erow = lax.broadcasted_iota(jnp.int32, (ROUTE_ROWS, ts), 0)
    lg = jnp.where(erow < N_EXPERTS, logits, -jnp.inf)
    m1 = jnp.max(lg, axis=0, keepdims=True)
    i1 = jnp.min(jnp.where(lg == m1, erow, ROUTE_ROWS), axis=0, keepdims=True)
    lg2 = jnp.where(erow == i1, -jnp.inf, lg)
    m2 = jnp.max(lg2, axis=0, keepdims=True)
    i2 = jnp.min(jnp.where(lg2 == m2, erow, ROUTE_ROWS), axis=0, keepdims=True)
    e2 = jnp.exp(m2 - m1)
    gate1 = 1.0 / (1.0 + e2)
    gate2 = e2 / (1.0 + e2)

    first = erow == i1
    second = erow == i2
    ind = jnp.where(first | second, 1.0, 0.0)
    cums = jnp.dot(ind.astype(BF16), tri_ref[...], preferred_element_type=F32)
    rank = cnt_ref[...] + cums - ind
    cnt_ref[...] = cnt_ref[...] + jnp.broadcast_to(cums[:, ts - 1:ts], cums.shape)
    rank1 = jnp.sum(jnp.where(first, rank, 0.0), axis=0, keepdims=True)
    rank2 = jnp.sum(jnp.where(second, rank, 0.0), axis=0, keepdims=True)
    cnt_out_ref[...] = cnt_ref[:, 0:ROUTE_LANES].astype(jnp.int32)

    fields = [None] * SUBLANES
    fields[ROUTE_E0], fields[ROUTE_E0 + 1] = i1.astype(F32), i2.astype(F32)
    fields[ROUTE_GATE0], fields[ROUTE_GATE0 + 1] = gate1, gate2
    fields[ROUTE_RANK0], fields[ROUTE_RANK0 + 1] = rank1, rank2
    zero_row = jnp.zeros((1, ts), F32)
    record_t = jnp.concatenate([zero_row if f is None else f for f in fields], axis=0)
    rt_ref[...] = record_t.astype(jnp.int32)
    route_ref[...] = jnp.concatenate(
        [record_t, jnp.zeros((ROUTE_LANES - SUBLANES, ts), F32)], axis=0).T.reshape(nb, tl, ROUTE_LANES)


def _odd_mixer(x, g, w_in, conv_w, conv_b, ga_w, ga_b, gx_w, gx_b, lam, w_out, g2, w_router):
    b, s, d = x.shape
    assert b == SUBLANES, "one timestep of all sequences must fill one sublane group"
    tl = SEQ_TILE // b
    ts = b * tl
    w = conv_w.shape[1]
    hd = w // LRU_HEADS
    gw = (0.5 * jnp.concatenate([ga_w, gx_w], axis=-1)).astype(BF16)
    ga_b, gx_b = 0.5 * ga_b, 0.5 * gx_b
    wr = jnp.pad(w_router.T, ((0, ROUTE_ROWS - N_EXPERTS), (0, 0))).astype(BF16)
    tri = jnp.triu(jnp.ones((ts, ts), BF16))
    nst = s // tl
    const = lambda si: (0, 0)
    row = lambda a: a.reshape(1, -1)
    tile = lambda si: (0, si, 0)
    out, h2, route, rt, counts = pl.pallas_call(
        _odd_mixer_kernel,
        out_shape=(jax.ShapeDtypeStruct((b, s, d), F32),
                   jax.ShapeDtypeStruct((b, s * SUBLANES, LANES), F32),
                   jax.ShapeDtypeStruct((b, s, ROUTE_LANES), F32),
                   jax.ShapeDtypeStruct((nst, SUBLANES, ts), jnp.int32),
                   jax.ShapeDtypeStruct((ROUTE_ROWS, ROUTE_LANES), jnp.int32)),
        grid=(nst,),
        in_specs=[
            pl.BlockSpec((b, tl, d), tile),
            _resident((1, d), const),
            _resident((d, 2 * w), const),
            _resident((CONV_K_C, w), const),
            _resident((1, w), const),
            _resident((LRU_HEADS, hd, 2 * hd), lambda si: (0, 0, 0)),
            _resident((1, w), const),
            _resident((1, w), const),
            _resident((1, w), const),
            _resident((w, d), const),
            _resident((1, d), const),
            _resident((ROUTE_ROWS, d), const),
            _resident((ts, ts), const),
        ],
        out_specs=(pl.BlockSpec((b, tl, d), tile),
                   pl.BlockSpec((b, tl * SUBLANES, LANES), tile),
                   pl.BlockSpec((b, tl, ROUTE_LANES), tile),
                   pl.BlockSpec((None, SUBLANES, ts), lambda si: (si, 0, 0)),
                   pl.BlockSpec((ROUTE_ROWS, ROUTE_LANES), const)),
        scratch_shapes=[
            pltpu.VMEM((w // LANES, (CONV_K_C - 1) * b + ts, LANES), F32),
            pltpu.VMEM((w // LANES, ts, LANES), F32),
            pltpu.VMEM((w // LANES, ts, LANES), F32),
            pltpu.VMEM((w // LANES, ts, LANES), F32),
            pltpu.VMEM((w // LANES, b, LANES), F32),
            pltpu.VMEM((ROUTE_ROWS, ts), F32),
        ],
        compiler_params=pltpu.CompilerParams(
            dimension_semantics=("arbitrary",), vmem_limit_bytes=VMEM_LIMIT_BYTES),
        name="odd_mixer",
    )(x, row(g), w_in.astype(BF16), conv_w, row(conv_b), gw, row(ga_b), row(gx_b), row(lam),
      w_out.astype(BF16), row(g2), wr, tri)
    rt = rt.reshape(nst, SUBLANES, b, tl).transpose(1, 2, 0, 3).reshape(SUBLANES, b * s)
    return out, h2.reshape(b * s * SUBLANES, LANES), route, rt, counts


def _moe_tables(rt, counts, t):
    r = MOE_ROWS
    tm = MOVE_TILE
    n_blocks = TOP_K * t // r + N_EXPERTS
    counts = counts[0:N_EXPERTS, 0]
    padded = (counts + r - 1) // r * r
    pad_end = jnp.cumsum(padded)
    pad_start = pad_end - padded
    e = rt[ROUTE_E0:ROUTE_E0 + TOP_K, :]
    rank = rt[ROUTE_RANK0:ROUTE_RANK0 + TOP_K, :]
    start = jnp.zeros_like(e)
    for ex in range(N_EXPERTS):
        start = jnp.where(e == ex, pad_start[ex], start)
    dest = start + rank
    blk = jnp.arange(n_blocks, dtype=jnp.int32)
    blk_expert = jnp.minimum(jnp.sum(blk[:, None] * r >= pad_end[None, :], axis=1), N_EXPERTS - 1)
    n_used = pad_end[N_EXPERTS - 1] // r
    last_blk = jnp.where(padded > 0, pad_end // r - 1, -1)
    has_pad = jnp.any(blk[:, None] == last_blk[None, :], axis=1) | (blk >= n_used)
    dest = dest.astype(jnp.int32).reshape(TOP_K, t // tm, tm).transpose(1, 0, 2).reshape(t // tm, TOP_K * tm)
    return dest, blk_expert.astype(jnp.int32), has_pad.astype(jnp.int32), n_used.astype(jnp.int32).reshape(1)


def _row_copy(src, src_row, dst, dst_row, sem):
    return pltpu.make_async_copy(src.at[pl.ds(src_row * SUBLANES, SUBLANES), :],
                                 dst.at[pl.ds(dst_row * SUBLANES, SUBLANES), :], sem)


def _rows_wait(src, dst, n_rows, sem):
    pltpu.make_async_copy(src.at[pl.ds(0, n_rows * SUBLANES), :],
                          dst.at[pl.ds(0, n_rows * SUBLANES), :], sem).wait()


def _dispatch_kernel(pad_ref, dest_hbm, h_ref, xg_hbm, dsm, stage, zbuf, sem_idx, sem_row, sem_zero):
    i = pl.program_id(0)
    n = pl.num_programs(0)
    tm = h_ref.shape[0] // SUBLANES
    r = zbuf.shape[0] // SUBLANES
    n_blocks = pad_ref.shape[0]
    slot = lax.rem(i, 2)

    def idx_copy(step, s):
        return pltpu.make_async_copy(dest_hbm.at[step], dsm.at[pl.ds(s * TOP_K * tm, TOP_K * tm)],
                                     sem_idx.at[s])

    def zero_copy(blk):
        return pltpu.make_async_copy(zbuf, xg_hbm.at[pl.ds(blk * r * SUBLANES, r * SUBLANES), :], sem_zero)

    @pl.when(i == 0)
    def _():
        idx_copy(0, 0).start()
        zbuf[...] = jnp.zeros(zbuf.shape, zbuf.dtype)

        def zero_start(blk, c):
            @pl.when(pad_ref[blk] == 1)
            def _():
                zero_copy(blk).start()
            return c

        def zero_wait(blk, c):
            @pl.when(pad_ref[blk] == 1)
            def _():
                zero_copy(blk).wait()
            return c

        lax.fori_loop(0, n_blocks, zero_start, 0)
        lax.fori_loop(0, n_blocks, zero_wait, 0)

    @pl.when(i + 1 < n)
    def _():
        idx_copy(i + 1, 1 - slot).start()

    idx_copy(i, slot).wait()

    stage_row0 = slot * tm
    stage[pl.ds(stage_row0 * SUBLANES, tm * SUBLANES), :] = h_ref[...]

    def issue(row, c):
        for k in range(TOP_K):
            _row_copy(stage, stage_row0 + row, xg_hbm, dsm[(slot * TOP_K + k) * tm + row],
                      sem_row.at[slot]).start(priority=k % DMA_THREADS)
        return c

    lax.fori_loop(0, tm, issue, 0, unroll=ROW_DMA_UNROLL)

    def wait_slot(s):
        for _ in range(TOP_K):
            _rows_wait(stage, xg_hbm, tm, sem_row.at[s])

    @pl.when(i > 0)
    def _():
        wait_slot(1 - slot)

    @pl.when(i == n - 1)
    def _():
        wait_slot(slot)


def _dispatch(h2, dest, has_pad):
    tm = dest.shape[1] // TOP_K
    t = h2.shape[0] // SUBLANES
    r = MOE_ROWS
    n_rows = TOP_K * t + N_EXPERTS * r
    grid_spec = pltpu.PrefetchScalarGridSpec(
        num_scalar_prefetch=1,
        grid=(t // tm,),
        in_specs=[pl.BlockSpec(memory_space=pl.ANY),
                  pl.BlockSpec((tm * SUBLANES, LANES), lambda i, pad: (i, 0))],
        out_specs=pl.BlockSpec(memory_space=pl.ANY),
        scratch_shapes=[
            pltpu.SMEM((2 * TOP_K * tm,), jnp.int32),
            pltpu.VMEM((2 * tm * SUBLANES, LANES), F32),
            pltpu.VMEM((r * SUBLANES, LANES), F32),
            pltpu.SemaphoreType.DMA((2,)),
            pltpu.SemaphoreType.DMA((2,)),
            pltpu.SemaphoreType.DMA,
        ],
    )
    return pl.pallas_call(
        _dispatch_kernel,
        out_shape=jax.ShapeDtypeStruct((n_rows * SUBLANES, LANES), F32),
        grid_spec=grid_spec,
        compiler_params=pltpu.CompilerParams(dimension_semantics=("arbitrary",)),
        name="moe_dispatch",
    )(has_pad, dest, h2)


def _expert_kernel(j, be_ref, nu_ref, x_ref, wg_hbm, wu_hbm, wd_hbm, y_ref,
                   wg_ref, wu_ref, wd_ref, stage_g, stage_u, stage_d, sem):
    i = pl.program_id(0)
    r = x_ref.shape[0] // SUBLANES
    e = be_ref[i]
    used = i < nu_ref[0]
    new_expert = (i == 0) | (e != be_ref[jnp.maximum(i - 1, 0)])
    w_ref = (wg_ref, wu_ref, wd_ref)

    @pl.when(used & new_expert)
    def _():
        xb = _load_row_tiles(x_ref, 0, r).astype(BF16)
        y = _swiglu_rows_fetching(xb, (wg_hbm.at[j, e], wu_hbm.at[j, e], wd_hbm.at[j, e]), w_ref,
                                  (stage_g, stage_u, stage_d), sem)
        _store_row_tiles(y_ref, 0, y)

    @pl.when(used & jnp.logical_not(new_expert))
    def _():
        xb = _load_row_tiles(x_ref, 0, r).astype(BF16)
        _store_row_tiles(y_ref, 0, _swiglu_rows(xb, *w_ref))

    @pl.when(jnp.logical_not(used))
    def _():
        y_ref[...] = jnp.zeros(y_ref.shape, y_ref.dtype)


def _experts(xg, blk_expert, n_used, j, w_gate, w_up, w_down):
    d, d_ff = w_gate.shape[2], w_gate.shape[3]
    r = MOE_ROWS
    rows = lambda i, be, nu: (i, 0)
    grid_spec = pltpu.PrefetchScalarGridSpec(
        num_scalar_prefetch=2,
        grid=(xg.shape[0] // (r * SUBLANES),),
        in_specs=[
            pl.BlockSpec((r * SUBLANES, LANES), rows),
            pl.BlockSpec(memory_space=pl.ANY),
            pl.BlockSpec(memory_space=pl.ANY),
            pl.BlockSpec(memory_space=pl.ANY),
        ],
        out_specs=pl.BlockSpec((r * SUBLANES, LANES), rows),
        scratch_shapes=_swiglu_weight_scratch(d, d_ff),
    )
    return pl.pallas_call(
        functools.partial(_expert_kernel, j),
        out_shape=jax.ShapeDtypeStruct(xg.shape, F32),
        grid_spec=grid_spec,
        compiler_params=pltpu.CompilerParams(
            dimension_semantics=("arbitrary",), vmem_limit_bytes=VMEM_LIMIT_BYTES),
        name="moe_experts",
    )(blk_expert, n_used, xg, w_gate, w_up, w_down)


def _combine_kernel(dest_hbm, y_hbm, x_ref, route_ref, o_ref, dsm, ybuf, sem_idx, sem_row):
    i = pl.program_id(0)
    n = pl.num_programs(0)
    tm = x_ref.shape[0]
    slot = lax.rem(i, 2)

    def idx_copy(step):
        s3 = lax.rem(step, 3)
        return pltpu.make_async_copy(dest_hbm.at[step], dsm.at[pl.ds(s3 * TOP_K * tm, TOP_K * tm)],
                                     sem_idx.at[s3])

    def request_rows(step):
        s2 = lax.rem(step, 2)
        table = lax.rem(step, 3) * TOP_K * tm

        def issue(row, c):
            for k in range(TOP_K):
                _row_copy(y_hbm, dsm[table + k * tm + row], ybuf, (s2 * TOP_K + k) * tm + row,
                          sem_row.at[s2]).start(priority=k % DMA_THREADS)
            return c

        lax.fori_loop(0, tm, issue, 0, unroll=ROW_DMA_UNROLL)

    @pl.when(i == 0)
    def _():
        idx_copy(0).start()
        if_next = n > 1

        @pl.when(if_next)
        def _():
            idx_copy(1).start()

        idx_copy(0).wait()
        request_rows(0)

    @pl.when(i + 2 < n)
    def _():
        idx_copy(i + 2).start()

    @pl.when(i + 1 < n)
    def _():
        idx_copy(i + 1).wait()
        request_rows(i + 1)

    for _ in range(TOP_K):
        _rows_wait(y_hbm, ybuf, tm, sem_row.at[slot])
    rt = route_ref[...]
    moe = None
    for k in range(TOP_K):
        yk = _load_row_tiles(ybuf, (slot * TOP_K + k) * tm * SUBLANES, tm)
        term = yk * rt[:, ROUTE_GATE0 + k:ROUTE_GATE0 + k + 1]
        moe = term if moe is None else moe + term
    o_ref[...] = x_ref[...] + moe


def _combine(x, y, route, dest):
    t, d = x.shape
    tm = dest.shape[1] // TOP_K
    return pl.pallas_call(
        _combine_kernel,
        out_shape=jax.ShapeDtypeStruct((t, d), F32),
        grid=(t // tm,),
        in_specs=[
            pl.BlockSpec(memory_space=pl.ANY),
            pl.BlockSpec(memory_space=pl.ANY),
            pl.BlockSpec((tm, d), lambda i: (i, 0)),
            pl.BlockSpec((tm, ROUTE_LANES), lambda i: (i, 0)),
        ],
        out_specs=pl.BlockSpec((tm, d), lambda i: (i, 0)),
        scratch_shapes=[
            pltpu.SMEM((3 * TOP_K * tm,), jnp.int32),
            pltpu.VMEM((2 * TOP_K * tm * SUBLANES, LANES), F32),
            pltpu.SemaphoreType.DMA((3,)),
            pltpu.SemaphoreType.DMA((2,)),
        ],
        compiler_params=pltpu.CompilerParams(
            dimension_semantics=("arbitrary",), vmem_limit_bytes=VMEM_LIMIT_BYTES),
        name="moe_combine",
    )(dest, y, x, route)


def kernel(x, norm_mix, norm_ffn, hy_w_in, hy_q_gain, hy_k_gain, hy_sinks, hy_conv_w, hy_w_out, rg_w_in, rg_conv_w, rg_conv_b, rg_gate_a_w, rg_gate_a_b, rg_gate_x_w, rg_gate_x_b, rg_lambda, rg_w_out, ffn_w_gate, ffn_w_up, ffn_w_down, moe_router, moe_w_gate, moe_w_up, moe_w_down):
    b, s, d = x.shape
    t = b * s
    depth = norm_mix.shape[0]
    for layer in range(depth):
        j = layer // 2
        if layer % 2 == 0:
            x = _even_mixer(x, norm_mix[layer], hy_w_in[j], hy_q_gain[j], hy_k_gain[j], hy_sinks[j],
                            hy_conv_w[j], hy_w_out[j])
            x = _ffn(x.reshape(t, d), norm_ffn[layer], j, ffn_w_gate, ffn_w_up, ffn_w_down)
            x = x.reshape(b, s, d)
        else:
            x, h2, route, rt, counts = _odd_mixer(
                x, norm_mix[layer], rg_w_in[j], rg_conv_w[j], rg_conv_b[j], rg_gate_a_w[j], rg_gate_a_b[j],
                rg_gate_x_w[j], rg_gate_x_b[j], rg_lambda[j], rg_w_out[j], norm_ffn[layer], moe_router[j])
            dest, blk_expert, has_pad, n_used = _moe_tables(rt, counts, t)
            xg = _dispatch(h2, dest, has_pad)
            y = _experts(xg, blk_expert, n_used, j, moe_w_gate, moe_w_up, moe_w_down)
            x = _combine(x.reshape(t, d), y, route.reshape(t, ROUTE_LANES), dest).reshape(b, s, d)
    return x
```

```python
import functools

import jax
import jax.numpy as jnp
from jax import lax
from jax.experimental import pallas as pl
from jax.experimental.pallas import tpu as pltpu

F32 = jnp.float32
BF16 = jnp.bfloat16

LANES = 128
SUBLANES = 8
VMEM_LIMIT_BYTES = 56 * 1024 * 1024
DMA_THREADS = 2
ROW_DMA_UNROLL = 32

N_Q_HEADS = 8
N_KV_HEADS = 2
HEAD_DIM = 64
Q_GROUP = N_Q_HEADS // N_KV_HEADS
WINDOW = 128
ATTN_BLOCK = 128
ALIBI_MAX = 8.0
CONV_K_B = 3
CONV_K_C = 4
LRU_HEADS = 8
LRU_C = 8.0
N_EXPERTS = 8
TOP_K = 2
NORM_EPS = 1e-6
NEG_INF = -1e30

SEQ_TILE = 512
MOVE_TILE = 512
EVEN_TILE = 1024
FFN_TILE = 512
MOE_ROWS = 512
FF_CHUNK = 1792
WEIGHT_COLS = 512
ROUTE_ROWS = 16
ROUTE_LANES = LANES
ROUTE_E0 = 0
ROUTE_GATE0 = 2
ROUTE_RANK0 = 4
GELU_C0 = 0.7978845608028654
GELU_C1 = 0.044715


def _rms(x, g):
    return x * lax.rsqrt(jnp.mean(x * x, axis=-1, keepdims=True) + NORM_EPS) * g


def _split_bf16(x):
    hi = x.astype(BF16)
    lo = (x - hi.astype(F32)).astype(BF16)
    return hi, lo


def _store_row_tiles(ref, start, x):
    n = x.shape[0]
    for c in range(x.shape[1] // LANES):
        ref[pl.ds(start + c, n, stride=SUBLANES), :] = x[:, c * LANES:(c + 1) * LANES]


def _load_row_tiles(ref, start, n):
    return jnp.concatenate([ref[pl.ds(start + c, n, stride=SUBLANES), :] for c in range(SUBLANES)], axis=1)


def _resident(shape, index_map):
    return pl.BlockSpec(shape, index_map, pipeline_mode=pl.Buffered(1))


def _even_mixer_kernel(sinks_ref, x_ref, g_ref, win_ref, qg_ref, kg_ref, cw_ref, wout_ref, o_ref,
                       kd_ref, vt_ref, cbuf_ref, attn_t_ref):
    ts = x_ref.shape[0]
    q_dim = N_Q_HEADS * HEAD_DIM
    kv_dim = N_KV_HEADS * HEAD_DIM
    cw_dim = cw_ref.shape[1]
    st = pl.program_id(1)

    @pl.when(st == 0)
    def _():
        kd_ref[:, 0:ATTN_BLOCK, :] = jnp.zeros((N_KV_HEADS, ATTN_BLOCK, LANES), BF16)
        vt_ref[:, 0:ATTN_BLOCK] = jnp.zeros((kv_dim, ATTN_BLOCK), BF16)
        cbuf_ref[0:SUBLANES, :] = jnp.zeros((SUBLANES, cw_dim), F32)

    x = x_ref[...]
    h = _rms(x, g_ref[...]).astype(BF16)
    proj = jnp.dot(h, win_ref[...], preferred_element_type=F32)
    q = proj[:, 0:q_dim]
    k = proj[:, q_dim:q_dim + kv_dim]
    v = proj[:, q_dim + kv_dim:q_dim + 2 * kv_dim]
    o0 = q_dim + 2 * kv_dim
    gate_b = proj[:, o0:o0 + cw_dim]
    gate_c = proj[:, o0 + cw_dim:o0 + 2 * cw_dim]
    u = proj[:, o0 + 2 * cw_dim:o0 + 3 * cw_dim]

    ri = lax.broadcasted_iota(jnp.int32, (LANES, LANES), 0)
    ci = lax.broadcasted_iota(jnp.int32, (LANES, LANES), 1)
    half_mean = jnp.where((ri < HEAD_DIM) == (ci < HEAD_DIM), 1.0 / HEAD_DIM, 0.0).astype(BF16)

    def half_mean_square(z):
        hi, lo = _split_bf16(z * z)
        cols = []
        for c in range(z.shape[1] // LANES):
            sl = slice(c * LANES, (c + 1) * LANES)
            cols.append(jnp.dot(hi[:, sl], half_mean, preferred_element_type=F32)
                        + jnp.dot(lo[:, sl], half_mean, preferred_element_type=F32))
        return cols[0] if len(cols) == 1 else jnp.concatenate(cols, axis=1)

    qn = q * lax.rsqrt(half_mean_square(q) + NORM_EPS) * qg_ref[...] * (HEAD_DIM ** -0.5)
    kn = k * lax.rsqrt(half_mean_square(k) + NORM_EPS) * kg_ref[...]

    low_half_t = lax.broadcasted_iota(jnp.int32, (ts, LANES), 1) < HEAD_DIM
    k_sw = pltpu.roll(kn, HEAD_DIM, axis=1)
    kd_ref[0, ATTN_BLOCK:ATTN_BLOCK + ts, :] = jnp.where(low_half_t, kn, k_sw).astype(BF16)
    kd_ref[1, ATTN_BLOCK:ATTN_BLOCK + ts, :] = jnp.where(low_half_t, k_sw, kn).astype(BF16)
    vt_ref[:, ATTN_BLOCK:ATTN_BLOCK + ts] = v.T.astype(BF16)

    low_half = lax.broadcasted_iota(jnp.int32, (ATTN_BLOCK, LANES), 1) < HEAD_DIM
    sj = lax.broadcasted_iota(jnp.int32, (2 * ATTN_BLOCK, ATTN_BLOCK), 0)
    qi = lax.broadcasted_iota(jnp.int32, (2 * ATTN_BLOCK, ATTN_BLOCK), 1)
    diff = ATTN_BLOCK + qi - sj
    in_window = (diff >= 0) & (diff < WINDOW)
    diff_f = diff.astype(F32)
    first_key = jnp.where(st == 0, ATTN_BLOCK, 0)
    in_window_first = in_window & (sj >= first_key)

    def penalty(valid, g):
        cols = []
        for hh in range(Q_GROUP):
            slope = 2.0 ** (-ALIBI_MAX * (g * Q_GROUP + hh + 1) / N_Q_HEADS)
            cols.append(jnp.where(valid, slope * diff_f, -NEG_INF))
        return jnp.concatenate(cols, axis=1)

    pen = [penalty(in_window, g) for g in range(N_KV_HEADS)]
    pen_first = [penalty(in_window_first, g) for g in range(N_KV_HEADS)]
    sink = [jnp.concatenate([jnp.full((1, ATTN_BLOCK), sinks_ref[0, g * Q_GROUP + hh], F32)
                             for hh in range(Q_GROUP)], axis=1) for g in range(N_KV_HEADS)]

    problems = [(j, g) for j in range(ts // ATTN_BLOCK) for g in range(N_KV_HEADS)]
    rows = lambda j: slice(j * ATTN_BLOCK, (j + 1) * ATTN_BLOCK)
    band = lambda j: slice(j * ATTN_BLOCK, (j + 2) * ATTN_BLOCK)

    scores = []
    for j, g in problems:
        q_rows = []
        for p in range(Q_GROUP // 2):
            c = g * (Q_GROUP // 2) + p
            qp = qn[rows(j), c * LANES:(c + 1) * LANES]
            q_rows.append(jnp.where(low_half, qp, 0.0))
            q_rows.append(jnp.where(low_half, 0.0, qp))
        qs = jnp.concatenate(q_rows, axis=0).astype(BF16)
        s = lax.dot_general(kd_ref[g, band(j), :], qs, (((1,), (1,)), ((), ())),
                            preferred_element_type=F32)
        scores.append(s - (pen_first[g] if j == 0 else pen[g]))
    peaks = [jnp.maximum(jnp.max(sh, axis=0, keepdims=True), sink[g]) for sh, (j, g) in zip(scores, problems)]
    probs = [jnp.exp(sh - m) for sh, m in zip(scores, peaks)]
    denoms = [jnp.sum(pr, axis=0, keepdims=True) + jnp.exp(sink[g] - m)
              for pr, m, (j, g) in zip(probs, peaks, problems)]
    for pr, denom, (j, g) in zip(probs, denoms, problems):
        o = jnp.dot(vt_ref[g * HEAD_DIM:(g + 1) * HEAD_DIM, band(j)], pr.astype(BF16),
                    preferred_element_type=F32)
        o = o / denom
        for hh in range(Q_GROUP):
            head = g * Q_GROUP + hh
            attn_t_ref[head * HEAD_DIM:(head + 1) * HEAD_DIM, rows(j)] = (
                o[:, hh * ATTN_BLOCK:(hh + 1) * ATTN_BLOCK])

    cu = gate_c * u
    cbuf_ref[SUBLANES:SUBLANES + ts, :] = cu
    conv = cw_ref[CONV_K_B - 1:CONV_K_B, :] * cu
    for kk in range(CONV_K_B - 1):
        back = CONV_K_B - 1 - kk
        conv = conv + cw_ref[kk:kk + 1, :] * cbuf_ref[SUBLANES - back:SUBLANES - back + ts, :]
    conv = gate_b * conv

    mix = jnp.concatenate([attn_t_ref[...].T, conv], axis=1).astype(BF16)
    o_ref[...] = x + jnp.dot(mix, wout_ref[...], preferred_element_type=F32)

    kd_ref[:, 0:ATTN_BLOCK, :] = kd_ref[:, ts:ts + ATTN_BLOCK, :]
    vt_ref[:, 0:ATTN_BLOCK] = vt_ref[:, ts:ts + ATTN_BLOCK]
    cbuf_ref[0:SUBLANES, :] = cbuf_ref[ts:ts + SUBLANES, :]


def _even_mixer(x, g, w_in, q_gain, k_gain, sinks, conv_w, w_out):
    b, s, d = x.shape
    ts = EVEN_TILE
    n_in = w_in.shape[1]
    cw_dim = conv_w.shape[1]
    mix_dim = w_out.shape[0]
    qg = jnp.tile(q_gain, N_Q_HEADS).reshape(1, N_Q_HEADS * HEAD_DIM)
    kg = jnp.tile(k_gain, N_KV_HEADS).reshape(1, N_KV_HEADS * HEAD_DIM)
    const = lambda bi, si: (0, 0)
    return pl.pallas_call(
        _even_mixer_kernel,
        out_shape=jax.ShapeDtypeStruct((b, s, d), F32),
        grid=(b, s // ts),
        in_specs=[
            pl.BlockSpec(memory_space=pltpu.SMEM),
            pl.BlockSpec((None, ts, d), lambda bi, si: (bi, si, 0)),
            _resident((1, d), const),
            _resident((d, n_in), const),
            _resident((1, N_Q_HEADS * HEAD_DIM), const),
            _resident((1, N_KV_HEADS * HEAD_DIM), const),
            _resident((CONV_K_B, cw_dim), const),
            _resident((mix_dim, d), const),
        ],
        out_specs=pl.BlockSpec((None, ts, d), lambda bi, si: (bi, si, 0)),
        scratch_shapes=[
            pltpu.VMEM((N_KV_HEADS, ATTN_BLOCK + ts, LANES), BF16),
            pltpu.VMEM((N_KV_HEADS * HEAD_DIM, ATTN_BLOCK + ts), BF16),
            pltpu.VMEM((SUBLANES + ts, cw_dim), F32),
            pltpu.VMEM((N_Q_HEADS * HEAD_DIM, ts), F32),
        ],
        compiler_params=pltpu.CompilerParams(
            dimension_semantics=("arbitrary", "arbitrary"), vmem_limit_bytes=VMEM_LIMIT_BYTES),
        name="even_mixer",
    )(sinks.reshape(1, N_Q_HEADS), x, g.reshape(1, d), w_in.astype(BF16), qg, kg, conv_w,
      w_out.astype(BF16))


def _swiglu_rows(hb, wg_ref, wu_ref, wd_ref):
    d_ff = wg_ref.shape[1]
    acc = None
    for c in range(d_ff // FF_CHUNK):
        cols = slice(c * FF_CHUNK, (c + 1) * FF_CHUNK)
        gate = jnp.dot(hb, wg_ref[:, cols], preferred_element_type=F32)
        up = jnp.dot(hb, wu_ref[:, cols], preferred_element_type=F32)
        act = (gate * (1.0 / (1.0 + jnp.exp(-gate))) * up).astype(BF16)
        part = jnp.dot(act, wd_ref[cols, :], preferred_element_type=F32)
        acc = part if acc is None else acc + part
    return acc


def _swiglu_rows_fetching(hb, w_hbm, w_ref, stage, sem):
    d_ff = w_ref[0].shape[1]
    n = d_ff // WEIGHT_COLS

    def copies(c, s):
        cols = pl.ds(c * WEIGHT_COLS, WEIGHT_COLS)
        srcs = (w_hbm[0].at[:, cols], w_hbm[1].at[:, cols], w_hbm[2].at[cols, :])
        return [pltpu.make_async_copy(src, stage[m].at[s], sem.at[s, m]) for m, src in enumerate(srcs)]

    for cp in copies(0, 0):
        cp.start()
    acc = None
    for c in range(n):
        s = c % 2
        if c + 1 < n:
            for cp in copies(c + 1, 1 - s):
                cp.start()
        for cp in copies(c, s):
            cp.wait()
        cols = slice(c * WEIGHT_COLS, (c + 1) * WEIGHT_COLS)
        w_ref[0][:, cols] = stage[0][s].astype(BF16)
        w_ref[1][:, cols] = stage[1][s].astype(BF16)
        w_ref[2][cols, :] = stage[2][s].astype(BF16)
        gate = jnp.dot(hb, w_ref[0][:, cols], preferred_element_type=F32)
        up = jnp.dot(hb, w_ref[1][:, cols], preferred_element_type=F32)
        act = (gate * (1.0 / (1.0 + jnp.exp(-gate))) * up).astype(BF16)
        part = jnp.dot(act, w_ref[2][cols, :], preferred_element_type=F32)
        acc = part if acc is None else acc + part
    return acc


def _swiglu_weight_scratch(d, d_ff):
    return [
        pltpu.VMEM((d, d_ff), BF16),
        pltpu.VMEM((d, d_ff), BF16),
        pltpu.VMEM((d_ff, d), BF16),
        pltpu.VMEM((2, d, WEIGHT_COLS), F32),
        pltpu.VMEM((2, d, WEIGHT_COLS), F32),
        pltpu.VMEM((2, WEIGHT_COLS, d), F32),
        pltpu.SemaphoreType.DMA((2, 3)),
    ]


def _ffn_kernel(j, x_ref, g_ref, wg_hbm, wu_hbm, wd_hbm, o_ref,
                wg_ref, wu_ref, wd_ref, stage_g, stage_u, stage_d, sem):
    x = x_ref[...]
    hb = _rms(x, g_ref[...]).astype(BF16)
    w_ref = (wg_ref, wu_ref, wd_ref)

    @pl.when(pl.program_id(0) == 0)
    def _():
        o_ref[...] = x + _swiglu_rows_fetching(hb, (wg_hbm.at[j], wu_hbm.at[j], wd_hbm.at[j]), w_ref,
                                               (stage_g, stage_u, stage_d), sem)

    @pl.when(pl.program_id(0) > 0)
    def _():
        o_ref[...] = x + _swiglu_rows(hb, *w_ref)


def _ffn(x, g, j, w_gate, w_up, w_down):
    t, d = x.shape
    d_ff = w_gate.shape[2]
    tm = FFN_TILE
    return pl.pallas_call(
        functools.partial(_ffn_kernel, j),
        out_shape=jax.ShapeDtypeStruct((t, d), F32),
        grid=(t // tm,),
        in_specs=[
            pl.BlockSpec((tm, d), lambda i: (i, 0)),
            _resident((1, d), lambda i: (0, 0)),
            pl.BlockSpec(memory_space=pl.ANY),
            pl.BlockSpec(memory_space=pl.ANY),
            pl.BlockSpec(memory_space=pl.ANY),
        ],
        out_specs=pl.BlockSpec((tm, d), lambda i: (i, 0)),
        scratch_shapes=_swiglu_weight_scratch(d, d_ff),
        compiler_params=pltpu.CompilerParams(
            dimension_semantics=("arbitrary",), vmem_limit_bytes=VMEM_LIMIT_BYTES),
        name="dense_ffn",
    )(x, g.reshape(1, d), w_gate, w_up, w_down)


def _odd_mixer_kernel(x_ref, g_ref, win_ref, cw_ref, cb_ref, gw_ref, gab_ref, gxb_ref, lam_ref, wout_ref,
                      g2_ref, wr_ref, tri_ref, o_ref, h2_ref, route_ref, rt_ref, cnt_out_ref,
                      xbuf_ref, a_ref, b_ref, hs_ref, hcar_ref, cnt_ref):
    nb, tl, d = x_ref.shape
    ts = nb * tl
    w = cw_ref.shape[1]
    hd = w // LRU_HEADS
    st = pl.program_id(0)

    n_slabs = w // LANES
    past = (CONV_K_C - 1) * nb

    @pl.when(st == 0)
    def _():
        xbuf_ref[:, 0:past, :] = jnp.zeros((n_slabs, past, LANES), F32)
        hcar_ref[...] = jnp.zeros(hcar_ref.shape, F32)
        cnt_ref[...] = jnp.zeros(cnt_ref.shape, F32)

    x = x_ref[...].reshape(ts, d)
    h = _rms(x, g_ref[...]).astype(BF16)
    proj = jnp.dot(h, win_ref[...], preferred_element_type=F32)
    y = proj[:, 0:w]
    half_y = 0.5 * y
    y = half_y + half_y * jnp.tanh(y * (GELU_C0 + (GELU_C0 * GELU_C1) * (y * y)))
    xb = proj[:, w:2 * w]

    xc_slabs = []
    for c in range(n_slabs):
        cols = slice(c * LANES, (c + 1) * LANES)
        for bi in range(nb):
            xbuf_ref[c, pl.ds(past + bi, tl, stride=nb), :] = xb[bi * tl:(bi + 1) * tl, cols]
        acc = cb_ref[:, cols]
        for kk in range(CONV_K_C):
            back = (CONV_K_C - 1 - kk) * nb
            acc = acc + cw_ref[kk:kk + 1, cols] * xbuf_ref[c, past - back:past - back + ts, :]
        xc_slabs.append(acc)
    xc = jnp.concatenate(xc_slabs, axis=1)

    r_cols = []
    i_cols = []
    for hh in range(LRU_HEADS):
        ri = jnp.dot(xc[:, hh * hd:(hh + 1) * hd].astype(BF16), gw_ref[hh], preferred_element_type=F32)
        r_cols.append(ri[:, 0:hd])
        i_cols.append(ri[:, hd:2 * hd])
    tanh_r = jnp.tanh(jnp.concatenate(r_cols, axis=1) + gab_ref[...])
    ig = 0.5 + 0.5 * jnp.tanh(jnp.concatenate(i_cols, axis=1) + gxb_ref[...])
    neg_lam = -lam_ref[...]
    softplus = jnp.maximum(neg_lam, 0.0) + jnp.log1p(jnp.exp(-jnp.abs(neg_lam)))
    half_rate = (-0.5 * LRU_C) * softplus
    log_a = half_rate + half_rate * tanh_r
    a = jnp.exp(log_a)
    m2 = 1.0 - a * a
    mult = jnp.where(m2 > 0.0, m2 * lax.rsqrt(m2), 0.0)
    gated = ig * xc
    bb = mult * gated

    for c in range(n_slabs):
        cols = slice(c * LANES, (c + 1) * LANES)
        a_ref[c] = a[:, cols]
        b_ref[c] = bb[:, cols]

    @pl.when(st == 0)
    def _():
        for c in range(n_slabs):
            b_ref[c, 0:nb, :] = gated[0:nb, c * LANES:(c + 1) * LANES]

    def step(t, hprev):
        off = pl.multiple_of(t * nb, nb)
        hnew = []
        for c in range(w // LANES):
            hc = a_ref[c, pl.ds(off, nb), :] * hprev[c] + b_ref[c, pl.ds(off, nb), :]
            hs_ref[c, pl.ds(off, nb), :] = hc
            hnew.append(hc)
        return tuple(hnew)

    hlast = lax.fori_loop(0, tl, step, tuple(hcar_ref[c] for c in range(w // LANES)), unroll=4)
    for c in range(w // LANES):
        hcar_ref[c] = hlast[c]
    hs = jnp.concatenate(
        [jnp.concatenate([hs_ref[c, pl.ds(bi, tl, stride=nb), :] for c in range(w // LANES)], axis=1)
         for bi in range(nb)], axis=0)

    out = x + jnp.dot((y * hs).astype(BF16), wout_ref[...], preferred_element_type=F32)
    o_ref[...] = out.reshape(nb, tl, d)
    xbuf_ref[:, 0:past, :] = xbuf_ref[:, ts:ts + past, :]

    h2 = _rms(out, g2_ref[...])
    for bi in range(nb):
        _store_row_tiles(h2_ref.at[bi], 0, h2[bi * tl:(bi + 1) * tl, :])
    logits = lax.dot_general(wr_ref[...], h2.astype(BF16), (((1,), (1,)), ((), ())),
                             preferred_element_type=F32)
    erow = lax.broadcasted_iota(jnp.int32, (ROUTE_ROWS, ts), 0)
    lg = jnp.where(erow < N_EXPERTS, logits, -jnp.inf)
    m1 = jnp.max(lg, axis=0, keepdims=True)
    i1 = jnp.min(jnp.where(lg == m1, erow, ROUTE_ROWS), axis=0, keepdims=True)
    lg2 = jnp.where(erow == i1, -jnp.inf, lg)
    m2 = jnp.max(lg2, axis=0, keepdims=True)
    i2 = jnp.min(jnp.where(lg2 == m2, erow, ROUTE_ROWS), axis=0, keepdims=True)
    e2 = jnp.exp(m2 - m1)
    gate1 = 1.0 / (1.0 + e2)
    gate2 = e2 / (1.0 + e2)

    first = erow == i1
    second = erow == i2
    ind = jnp.where(first | second, 1.0, 0.0)
    cums = jnp.dot(ind.astype(BF16), tri_ref[...], preferred_element_type=F32)
    rank = cnt_ref[...] + cums - ind
    cnt_ref[...] = cnt_ref[...] + jnp.broadcast_to(cums[:, ts - 1:ts], cums.shape)
    rank1 = jnp.sum(jnp.where(first, rank, 0.0), axis=0, keepdims=True)
    rank2 = jnp.sum(jnp.where(second, rank, 0.0), axis=0, keepdims=True)
    cnt_out_ref[...] = cnt_ref[:, 0:ROUTE_LANES].astype(jnp.int32)

    fields = [None] * SUBLANES
    fields[ROUTE_E0], fields[ROUTE_E0 + 1] = i1.astype(F32), i2.astype(F32)
    fields[ROUTE_GATE0], fields[ROUTE_GATE0 + 1] = gate1, gate2
    fields[ROUTE_RANK0], fields[ROUTE_RANK0 + 1] = rank1, rank2
    zero_row = jnp.zeros((1, ts), F32)
    record_t = jnp.concatenate([zero_row if f is None else f for f in fields], axis=0)
    rt_ref[...] = record_t.astype(jnp.int32)
    route_ref[...] = jnp.concatenate(
        [record_t, jnp.zeros((ROUTE_LANES - SUBLANES, ts), F32)], axis=0).T.reshape(nb, tl, ROUTE_LANES)


def _odd_mixer(x, g, w_in, conv_w, conv_b, ga_w, ga_b, gx_w, gx_b, lam, w_out, g2, w_router):
    b, s, d = x.shape
    assert b == SUBLANES, "one timestep of all sequences must fill one sublane group"
    tl = SEQ_TILE // b
    ts = b * tl
    w = conv_w.shape[1]
    hd = w // LRU_HEADS
    gw = (0.5 * jnp.concatenate([ga_w, gx_w], axis=-1)).astype(BF16)
    ga_b, gx_b = 0.5 * ga_b, 0.5 * gx_b
    wr = jnp.pad(w_router.T, ((0, ROUTE_ROWS - N_EXPERTS), (0, 0))).astype(BF16)
    tri = jnp.triu(jnp.ones((ts, ts), BF16))
    nst = s // tl
    const = lambda si: (0, 0)
    row = lambda a: a.reshape(1, -1)
    tile = lambda si: (0, si, 0)
    out, h2, route, rt, counts = pl.pallas_call(
        _odd_mixer_kernel,
        out_shape=(jax.ShapeDtypeStruct((b, s, d), F32),
                   jax.ShapeDtypeStruct((b, s * SUBLANES, LANES), F32),
                   jax.ShapeDtypeStruct((b, s, ROUTE_LANES), F32),
                   jax.ShapeDtypeStruct((nst, SUBLANES, ts), jnp.int32),
                   jax.ShapeDtypeStruct((ROUTE_ROWS, ROUTE_LANES), jnp.int32)),
        grid=(nst,),
        in_specs=[
            pl.BlockSpec((b, tl, d), tile),
            _resident((1, d), const),
            _resident((d, 2 * w), const),
            _resident((CONV_K_C, w), const),
            _resident((1, w), const),
            _resident((LRU_HEADS, hd, 2 * hd), lambda si: (0, 0, 0)),
            _resident((1, w), const),
            _resident((1, w), const),
            _resident((1, w), const),
            _resident((w, d), const),
            _resident((1, d), const),
            _resident((ROUTE_ROWS, d), const),
            _resident((ts, ts), const),
        ],
        out_specs=(pl.BlockSpec((b, tl, d), tile),
                   pl.BlockSpec((b, tl * SUBLANES, LANES), tile),
                   pl.BlockSpec((b, tl, ROUTE_LANES), tile),
                   pl.BlockSpec((None, SUBLANES, ts), lambda si: (si, 0, 0)),
                   pl.BlockSpec((ROUTE_ROWS, ROUTE_LANES), const)),
        scratch_shapes=[
            pltpu.VMEM((w // LANES, (CONV_K_C - 1) * b + ts, LANES), F32),
            pltpu.VMEM((w // LANES, ts, LANES), F32),
            pltpu.VMEM((w // LANES, ts, LANES), F32),
            pltpu.VMEM((w // LANES, ts, LANES), F32),
            pltpu.VMEM((w // LANES, b, LANES), F32),
            pltpu.VMEM((ROUTE_ROWS, ts), F32),
        ],
        compiler_params=pltpu.CompilerParams(
            dimension_semantics=("arbitrary",), vmem_limit_bytes=VMEM_LIMIT_BYTES),
        name="odd_mixer",
    )(x, row(g), w_in.astype(BF16), conv_w, row(conv_b), gw, row(ga_b), row(gx_b), row(lam),
      w_out.astype(BF16), row(g2), wr, tri)
    rt = rt.reshape(nst, SUBLANES, b, tl).transpose(1, 2, 0, 3).reshape(SUBLANES, b * s)
    return out, h2.reshape(b * s * SUBLANES, LANES), route, rt, counts


def _moe_tables(rt, counts, t):
    r = MOE_ROWS
    tm = MOVE_TILE
    n_blocks = TOP_K * t // r + N_EXPERTS
    counts = counts[0:N_EXPERTS, 0]
    padded = (counts + r - 1) // r * r
    pad_end = jnp.cumsum(padded)
    pad_start = pad_end - padded
    e = rt[ROUTE_E0:ROUTE_E0 + TOP_K, :]
    rank = rt[ROUTE_RANK0:ROUTE_RANK0 + TOP_K, :]
    start = jnp.zeros_like(e)
    for ex in range(N_EXPERTS):
        start = jnp.where(e == ex, pad_start[ex], start)
    dest = start + rank
    blk = jnp.arange(n_blocks, dtype=jnp.int32)
    blk_expert = jnp.minimum(jnp.sum(blk[:, None] * r >= pad_end[None, :], axis=1), N_EXPERTS - 1)
    n_used = pad_end[N_EXPERTS - 1] // r
    last_blk = jnp.where(padded > 0, pad_end // r - 1, -1)
    has_pad = jnp.any(blk[:, None] == last_blk[None, :], axis=1) | (blk >= n_used)
    dest = dest.astype(jnp.int32).reshape(TOP_K, t // tm, tm).transpose(1, 0, 2).reshape(t // tm, TOP_K * tm)
    return dest, blk_expert.astype(jnp.int32), has_pad.astype(jnp.int32), n_used.astype(jnp.int32).reshape(1)


def _row_copy(src, src_row, dst, dst_row, sem):
    return pltpu.make_async_copy(src.at[pl.ds(src_row * SUBLANES, SUBLANES), :],
                                 dst.at[pl.ds(dst_row * SUBLANES, SUBLANES), :], sem)


def _rows_wait(src, dst, n_rows, sem):
    pltpu.make_async_copy(src.at[pl.ds(0, n_rows * SUBLANES), :],
                          dst.at[pl.ds(0, n_rows * SUBLANES), :], sem).wait()


def _dispatch_kernel(pad_ref, dest_hbm, h_ref, xg_hbm, dsm, stage, zbuf, sem_idx, sem_row, sem_zero):
    i = pl.program_id(0)
    n = pl.num_programs(0)
    tm = h_ref.shape[0] // SUBLANES
    r = zbuf.shape[0] // SUBLANES
    n_blocks = pad_ref.shape[0]
    slot = lax.rem(i, 2)

    def idx_copy(step, s):
        return pltpu.make_async_copy(dest_hbm.at[step], dsm.at[pl.ds(s * TOP_K * tm, TOP_K * tm)],
                                     sem_idx.at[s])

    def zero_copy(blk):
        return pltpu.make_async_copy(zbuf, xg_hbm.at[pl.ds(blk * r * SUBLANES, r * SUBLANES), :], sem_zero)

    @pl.when(i == 0)
    def _():
        idx_copy(0, 0).start()
        zbuf[...] = jnp.zeros(zbuf.shape, zbuf.dtype)

        def zero_start(blk, c):
            @pl.when(pad_ref[blk] == 1)
            def _():
                zero_copy(blk).start()
            return c

        def zero_wait(blk, c):
            @pl.when(pad_ref[blk] == 1)
            def _():
                zero_copy(blk).wait()
            return c

        lax.fori_loop(0, n_blocks, zero_start, 0)
        lax.fori_loop(0, n_blocks, zero_wait, 0)

    @pl.when(i + 1 < n)
    def _():
        idx_copy(i + 1, 1 - slot).start()

    idx_copy(i, slot).wait()

    stage_row0 = slot * tm
    stage[pl.ds(stage_row0 * SUBLANES, tm * SUBLANES), :] = h_ref[...]

    def issue(row, c):
        for k in range(TOP_K):
            _row_copy(stage, stage_row0 + row, xg_hbm, dsm[(slot * TOP_K + k) * tm + row],
                      sem_row.at[slot]).start(priority=k % DMA_THREADS)
        return c

    lax.fori_loop(0, tm, issue, 0, unroll=ROW_DMA_UNROLL)

    def wait_slot(s):
        for _ in range(TOP_K):
            _rows_wait(stage, xg_hbm, tm, sem_row.at[s])

    @pl.when(i > 0)
    def _():
        wait_slot(1 - slot)

    @pl.when(i == n - 1)
    def _():
        wait_slot(slot)


def _dispatch(h2, dest, has_pad):
    tm = dest.shape[1] // TOP_K
    t = h2.shape[0] // SUBLANES
    r = MOE_ROWS
    n_rows = TOP_K * t + N_EXPERTS * r
    grid_spec = pltpu.PrefetchScalarGridSpec(
        num_scalar_prefetch=1,
        grid=(t // tm,),
        in_specs=[pl.BlockSpec(memory_space=pl.ANY),
                  pl.BlockSpec((tm * SUBLANES, LANES), lambda i, pad: (i, 0))],
        out_specs=pl.BlockSpec(memory_space=pl.ANY),
        scratch_shapes=[
            pltpu.SMEM((2 * TOP_K * tm,), jnp.int32),
            pltpu.VMEM((2 * tm * SUBLANES, LANES), F32),
            pltpu.VMEM((r * SUBLANES, LANES), F32),
            pltpu.SemaphoreType.DMA((2,)),
            pltpu.SemaphoreType.DMA((2,)),
            pltpu.SemaphoreType.DMA,
        ],
    )
    return pl.pallas_call(
        _dispatch_kernel,
        out_shape=jax.ShapeDtypeStruct((n_rows * SUBLANES, LANES), F32),
        grid_spec=grid_spec,
        compiler_params=pltpu.CompilerParams(dimension_semantics=("arbitrary",)),
        name="moe_dispatch",
    )(has_pad, dest, h2)


def _expert_kernel(j, be_ref, nu_ref, x_ref, wg_hbm, wu_hbm, wd_hbm, y_ref,
                   wg_ref, wu_ref, wd_ref, stage_g, stage_u, stage_d, sem):
    i = pl.program_id(0)
    r = x_ref.shape[0] // SUBLANES
    e = be_ref[i]
    used = i < nu_ref[0]
    new_expert = (i == 0) | (e != be_ref[jnp.maximum(i - 1, 0)])
    w_ref = (wg_ref, wu_ref, wd_ref)

    @pl.when(used & new_expert)
    def _():
        xb = _load_row_tiles(x_ref, 0, r).astype(BF16)
        y = _swiglu_rows_fetching(xb, (wg_hbm.at[j, e], wu_hbm.at[j, e], wd_hbm.at[j, e]), w_ref,
                                  (stage_g, stage_u, stage_d), sem)
        _store_row_tiles(y_ref, 0, y)

    @pl.when(used & jnp.logical_not(new_expert))
    def _():
        xb = _load_row_tiles(x_ref, 0, r).astype(BF16)
        _store_row_tiles(y_ref, 0, _swiglu_rows(xb, *w_ref))

    @pl.when(jnp.logical_not(used))
    def _():
        y_ref[...] = jnp.zeros(y_ref.shape, y_ref.dtype)


def _experts(xg, blk_expert, n_used, j, w_gate, w_up, w_down):
    d, d_ff = w_gate.shape[2], w_gate.shape[3]
    r = MOE_ROWS
    rows = lambda i, be, nu: (i, 0)
    grid_spec = pltpu.PrefetchScalarGridSpec(
        num_scalar_prefetch=2,
        grid=(xg.shape[0] // (r * SUBLANES),),
        in_specs=[
            pl.BlockSpec((r * SUBLANES, LANES), rows),
            pl.BlockSpec(memory_space=pl.ANY),
            pl.BlockSpec(memory_space=pl.ANY),
            pl.BlockSpec(memory_space=pl.ANY),
        ],
        out_specs=pl.BlockSpec((r * SUBLANES, LANES), rows),
        scratch_shapes=_swiglu_weight_scratch(d, d_ff),
    )
    return pl.pallas_call(
        functools.partial(_expert_kernel, j),
        out_shape=jax.ShapeDtypeStruct(xg.shape, F32),
        grid_spec=grid_spec,
        compiler_params=pltpu.CompilerParams(
            dimension_semantics=("arbitrary",), vmem_limit_bytes=VMEM_LIMIT_BYTES),
        name="moe_experts",
    )(blk_expert, n_used, xg, w_gate, w_up, w_down)


def _combine_kernel(dest_hbm, y_hbm, x_ref, route_ref, o_ref, dsm, ybuf, sem_idx, sem_row):
    i = pl.program_id(0)
    n = pl.num_programs(0)
    tm = x_ref.shape[0]
    slot = lax.rem(i, 2)

    def idx_copy(step):
        s3 = lax.rem(step, 3)
        return pltpu.make_async_copy(dest_hbm.at[step], dsm.at[pl.ds(s3 * TOP_K * tm, TOP_K * tm)],
                                     sem_idx.at[s3])

    def request_rows(step):
        s2 = lax.rem(step, 2)
        table = lax.rem(step, 3) * TOP_K * tm

        def issue(row, c):
            for k in range(TOP_K):
                _row_copy(y_hbm, dsm[table + k * tm + row], ybuf, (s2 * TOP_K + k) * tm + row,
                          sem_row.at[s2]).start(priority=k % DMA_THREADS)
            return c

        lax.fori_loop(0, tm, issue, 0, unroll=ROW_DMA_UNROLL)

    @pl.when(i == 0)
    def _():
        idx_copy(0).start()
        if_next = n > 1

        @pl.when(if_next)
        def _():
            idx_copy(1).start()

        idx_copy(0).wait()
        request_rows(0)

    @pl.when(i + 2 < n)
    def _():
        idx_copy(i + 2).start()

    @pl.when(i + 1 < n)
    def _():
        idx_copy(i + 1).wait()
        request_rows(i + 1)

    for _ in range(TOP_K):
        _rows_wait(y_hbm, ybuf, tm, sem_row.at[slot])
    rt = route_ref[...]
    moe = None
    for k in range(TOP_K):
        yk = _load_row_tiles(ybuf, (slot * TOP_K + k) * tm * SUBLANES, tm)
        term = yk * rt[:, ROUTE_GATE0 + k:ROUTE_GATE0 + k + 1]
        moe = term if moe is None else moe + term
    o_ref[...] = x_ref[...] + moe


def _combine(x, y, route, dest):
    t, d = x.shape
    tm = dest.shape[1] // TOP_K
    return pl.pallas_call(
        _combine_kernel,
        out_shape=jax.ShapeDtypeStruct((t, d), F32),
        grid=(t // tm,),
        in_specs=[
            pl.BlockSpec(memory_space=pl.ANY),
            pl.BlockSpec(memory_space=pl.ANY),
            pl.BlockSpec((tm, d), lambda i: (i, 0)),
            pl.BlockSpec((tm, ROUTE_LANES), lambda i: (i, 0)),
        ],
        out_specs=pl.BlockSpec((tm, d), lambda i: (i, 0)),
        scratch_shapes=[
            pltpu.SMEM((3 * TOP_K * tm,), jnp.int32),
            pltpu.VMEM((2 * TOP_K * tm * SUBLANES, LANES), F32),
            pltpu.SemaphoreType.DMA((3,)),
            pltpu.SemaphoreType.DMA((2,)),
        ],
        compiler_params=pltpu.CompilerParams(
            dimension_semantics=("arbitrary",), vmem_limit_bytes=VMEM_LIMIT_BYTES),
        name="moe_combine",
    )(dest, y, x, route)


def kernel(x, norm_mix, norm_ffn, hy_w_in, hy_q_gain, hy_k_gain, hy_sinks, hy_conv_w, hy_w_out, rg_w_in, rg_conv_w, rg_conv_b, rg_gate_a_w, rg_gate_a_b, rg_gate_x_w, rg_gate_x_b, rg_lambda, rg_w_out, ffn_w_gate, ffn_w_up, ffn_w_down, moe_router, moe_w_gate, moe_w_up, moe_w_down):
    b, s, d = x.shape
    t = b * s
    depth = norm_mix.shape[0]
    for layer in range(depth):
        j = layer // 2
        if layer % 2 == 0:
            x = _even_mixer(x, norm_mix[layer], hy_w_in[j], hy_q_gain[j], hy_k_gain[j], hy_sinks[j],
                            hy_conv_w[j], hy_w_out[j])
            x = _ffn(x.reshape(t, d), norm_ffn[layer], j, ffn_w_gate, ffn_w_up, ffn_w_down)
            x = x.reshape(b, s, d)
        else:
            x, h2, route, rt, counts = _odd_mixer(
                x, norm_mix[layer], rg_w_in[j], rg_conv_w[j], rg_conv_b[j], rg_gate_a_w[j], rg_gate_a_b[j],
                rg_gate_x_w[j], rg_gate_x_b[j], rg_lambda[j], rg_w_out[j], norm_ffn[layer], moe_router[j])
            dest, blk_expert, has_pad, n_used = _moe_tables(rt, counts, t)
            xg = _dispatch(h2, dest, has_pad)
            y = _experts(xg, blk_expert, n_used, j, moe_w_gate, moe_w_up, moe_w_down)
            x = _combine(x.reshape(t, d), y, route.reshape(t, ROUTE_LANES), dest).reshape(b, s, d)
    return x
```

```python
import functools

import jax
import jax.numpy as jnp
from jax import lax
from jax.experimental import pallas as pl
from jax.experimental.pallas import tpu as pltpu

F32 = jnp.float32
BF16 = jnp.bfloat16

LANES = 128
SUBLANES = 8
VMEM_LIMIT_BYTES = 56 * 1024 * 1024
DMA_THREADS = 2
ROW_DMA_UNROLL = 32

N_Q_HEADS = 8
N_KV_HEADS = 2
HEAD_DIM = 64
Q_GROUP = N_Q_HEADS // N_KV_HEADS
WINDOW = 128
ATTN_BLOCK = 128
ALIBI_MAX = 8.0
CONV_K_B = 3
CONV_K_C = 4
LRU_HEADS = 8
LRU_C = 8.0
N_EXPERTS = 8
TOP_K = 2
NORM_EPS = 1e-6
NEG_INF = -1e30

SEQ_TILE = 512
MOVE_TILE = 512
EVEN_TILE = 1024
FFN_TILE = 512
MOE_ROWS = 512
FF_CHUNK = 1792
WEIGHT_COLS = 512
ROUTE_ROWS = 16
ROUTE_LANES = LANES
ROUTE_E0 = 0
ROUTE_GATE0 = 2
ROUTE_RANK0 = 4
GELU_C0 = 0.7978845608028654
GELU_C1 = 0.044715


def _rms(x, g):
    return x * lax.rsqrt(jnp.mean(x * x, axis=-1, keepdims=True) + NORM_EPS) * g


def _split_bf16(x):
    hi = x.astype(BF16)
    lo = (x - hi.astype(F32)).astype(BF16)
    return hi, lo


def _store_row_tiles(ref, start, x):
    n = x.shape[0]
    for c in range(x.shape[1] // LANES):
        ref[pl.ds(start + c, n, stride=SUBLANES), :] = x[:, c * LANES:(c + 1) * LANES]


def _load_row_tiles(ref, start, n):
    return jnp.concatenate([ref[pl.ds(start + c, n, stride=SUBLANES), :] for c in range(SUBLANES)], axis=1)


def _resident(shape, index_map):
    return pl.BlockSpec(shape, index_map, pipeline_mode=pl.Buffered(1))


def _even_mixer_kernel(sinks_ref, x_ref, g_ref, win_ref, qg_ref, kg_ref, cw_ref, wout_ref, o_ref,
                       kd_ref, vt_ref, cbuf_ref, attn_t_ref):
    ts = x_ref.shape[0]
    q_dim = N_Q_HEADS * HEAD_DIM
    kv_dim = N_KV_HEADS * HEAD_DIM
    cw_dim = cw_ref.shape[1]
    st = pl.program_id(1)

    @pl.when(st == 0)
    def _():
        kd_ref[:, 0:ATTN_BLOCK, :] = jnp.zeros((N_KV_HEADS, ATTN_BLOCK, LANES), BF16)
        vt_ref[:, 0:ATTN_BLOCK] = jnp.zeros((kv_dim, ATTN_BLOCK), BF16)
        cbuf_ref[0:SUBLANES, :] = jnp.zeros((SUBLANES, cw_dim), F32)

    x = x_ref[...]
    h = _rms(x, g_ref[...]).astype(BF16)
    proj = jnp.dot(h, win_ref[...], preferred_element_type=F32)
    q = proj[:, 0:q_dim]
    k = proj[:, q_dim:q_dim + kv_dim]
    v = proj[:, q_dim + kv_dim:q_dim + 2 * kv_dim]
    o0 = q_dim + 2 * kv_dim
    gate_b = proj[:, o0:o0 + cw_dim]
    gate_c = proj[:, o0 + cw_dim:o0 + 2 * cw_dim]
    u = proj[:, o0 + 2 * cw_dim:o0 + 3 * cw_dim]

    ri = lax.broadcasted_iota(jnp.int32, (LANES, LANES), 0)
    ci = lax.broadcasted_iota(jnp.int32, (LANES, LANES), 1)
    half_mean = jnp.where((ri < HEAD_DIM) == (ci < HEAD_DIM), 1.0 / HEAD_DIM, 0.0).astype(BF16)

    def half_mean_square(z):
        hi, lo = _split_bf16(z * z)
        cols = []
        for c in range(z.shape[1] // LANES):
            sl = slice(c * LANES, (c + 1) * LANES)
            cols.append(jnp.dot(hi[:, sl], half_mean, preferred_element_type=F32)
                        + jnp.dot(lo[:, sl], half_mean, preferred_element_type=F32))
        return cols[0] if len(cols) == 1 else jnp.concatenate(cols, axis=1)

    qn = q * lax.rsqrt(half_mean_square(q) + NORM_EPS) * qg_ref[...] * (HEAD_DIM ** -0.5)
    kn = k * lax.rsqrt(half_mean_square(k) + NORM_EPS) * kg_ref[...]

    low_half_t = lax.broadcasted_iota(jnp.int32, (ts, LANES), 1) < HEAD_DIM
    k_sw = pltpu.roll(kn, HEAD_DIM, axis=1)
    kd_ref[0, ATTN_BLOCK:ATTN_BLOCK + ts, :] = jnp.where(low_half_t, kn, k_sw).astype(BF16)
    kd_ref[1, ATTN_BLOCK:ATTN_BLOCK + ts, :] = jnp.where(low_half_t, k_sw, kn).astype(BF16)
    vt_ref[:, ATTN_BLOCK:ATTN_BLOCK + ts] = v.T.astype(BF16)

    low_half = lax.broadcasted_iota(jnp.int32, (ATTN_BLOCK, LANES), 1) < HEAD_DIM
    sj = lax.broadcasted_iota(jnp.int32, (2 * ATTN_BLOCK, ATTN_BLOCK), 0)
    qi = lax.broadcasted_iota(jnp.int32, (2 * ATTN_BLOCK, ATTN_BLOCK), 1)
    diff = ATTN_BLOCK + qi - sj
    in_window = (diff >= 0) & (diff < WINDOW)
    diff_f = diff.astype(F32)
    first_key = jnp.where(st == 0, ATTN_BLOCK, 0)
    in_window_first = in_window & (sj >= first_key)

    def penalty(valid, g):
        cols = []
        for hh in range(Q_GROUP):
            slope = 2.0 ** (-ALIBI_MAX * (g * Q_GROUP + hh + 1) / N_Q_HEADS)
            cols.append(jnp.where(valid, slope * diff_f, -NEG_INF))
        return jnp.concatenate(cols, axis=1)

    pen = [penalty(in_window, g) for g in range(N_KV_HEADS)]
    pen_first = [penalty(in_window_first, g) for g in range(N_KV_HEADS)]
    sink = [jnp.concatenate([jnp.full((1, ATTN_BLOCK), sinks_ref[0, g * Q_GROUP + hh], F32)
                             for hh in range(Q_GROUP)], axis=1) for g in range(N_KV_HEADS)]

    problems = [(j, g) for j in range(ts // ATTN_BLOCK) for g in range(N_KV_HEADS)]
    rows = lambda j: slice(j * ATTN_BLOCK, (j + 1) * ATTN_BLOCK)
    band = lambda j: slice(j * ATTN_BLOCK, (j + 2) * ATTN_BLOCK)

    scores = []
    for j, g in problems:
        q_rows = []
        for p in range(Q_GROUP // 2):
            c = g * (Q_GROUP // 2) + p
            qp = qn[rows(j), c * LANES:(c + 1) * LANES]
            q_rows.append(jnp.where(low_half, qp, 0.0))
            q_rows.append(jnp.where(low_half, 0.0, qp))
        qs = jnp.concatenate(q_rows, axis=0).astype(BF16)
        s = lax.dot_general(kd_ref[g, band(j), :], qs, (((1,), (1,)), ((), ())),
                            preferred_element_type=F32)
        scores.append(s - (pen_first[g] if j == 0 else pen[g]))
    peaks = [jnp.maximum(jnp.max(sh, axis=0, keepdims=True), sink[g]) for sh, (j, g) in zip(scores, problems)]
    probs = [jnp.exp(sh - m) for sh, m in zip(scores, peaks)]
    denoms = [jnp.sum(pr, axis=0, keepdims=True) + jnp.exp(sink[g] - m)
              for pr, m, (j, g) in zip(probs, peaks, problems)]
    for pr, denom, (j, g) in zip(probs, denoms, problems):
        o = jnp.dot(vt_ref[g * HEAD_DIM:(g + 1) * HEAD_DIM, band(j)], pr.astype(BF16),
                    preferred_element_type=F32)
        o = o / denom
        for hh in range(Q_GROUP):
            head = g * Q_GROUP + hh
            attn_t_ref[head * HEAD_DIM:(head + 1) * HEAD_DIM, rows(j)] = (
                o[:, hh * ATTN_BLOCK:(hh + 1) * ATTN_BLOCK])

    cu = gate_c * u
    cbuf_ref[SUBLANES:SUBLANES + ts, :] = cu
    conv = cw_ref[CONV_K_B - 1:CONV_K_B, :] * cu
    for kk in range(CONV_K_B - 1):
        back = CONV_K_B - 1 - kk
        conv = conv + cw_ref[kk:kk + 1, :] * cbuf_ref[SUBLANES - back:SUBLANES - back + ts, :]
    conv = gate_b * conv

    mix = jnp.concatenate([attn_t_ref[...].T, conv], axis=1).astype(BF16)
    o_ref[...] = x + jnp.dot(mix, wout_ref[...], preferred_element_type=F32)

    kd_ref[:, 0:ATTN_BLOCK, :] = kd_ref[:, ts:ts + ATTN_BLOCK, :]
    vt_ref[:, 0:ATTN_BLOCK] = vt_ref[:, ts:ts + ATTN_BLOCK]
    cbuf_ref[0:SUBLANES, :] = cbuf_ref[ts:ts + SUBLANES, :]


def _even_mixer(x, g, w_in, q_gain, k_gain, sinks, conv_w, w_out):
    b, s, d = x.shape
    ts = EVEN_TILE
    n_in = w_in.shape[1]
    cw_dim = conv_w.shape[1]
    mix_dim = w_out.shape[0]
    qg = jnp.tile(q_gain, N_Q_HEADS).reshape(1, N_Q_HEADS * HEAD_DIM)
    kg = jnp.tile(k_gain, N_KV_HEADS).reshape(1, N_KV_HEADS * HEAD_DIM)
    const = lambda bi, si: (0, 0)
    return pl.pallas_call(
        _even_mixer_kernel,
        out_shape=jax.ShapeDtypeStruct((b, s, d), F32),
        grid=(b, s // ts),
        in_specs=[
            pl.BlockSpec(memory_space=pltpu.SMEM),
            pl.BlockSpec((None, ts, d), lambda bi, si: (bi, si, 0)),
            _resident((1, d), const),
            _resident((d, n_in), const),
            _resident((1, N_Q_HEADS * HEAD_DIM), const),
            _resident((1, N_KV_HEADS * HEAD_DIM), const),
            _resident((CONV_K_B, cw_dim), const),
            _resident((mix_dim, d), const),
        ],
        out_specs=pl.BlockSpec((None, ts, d), lambda bi, si: (bi, si, 0)),
        scratch_shapes=[
            pltpu.VMEM((N_KV_HEADS, ATTN_BLOCK + ts, LANES), BF16),
            pltpu.VMEM((N_KV_HEADS * HEAD_DIM, ATTN_BLOCK + ts), BF16),
            pltpu.VMEM((SUBLANES + ts, cw_dim), F32),
            pltpu.VMEM((N_Q_HEADS * HEAD_DIM, ts), F32),
        ],
        compiler_params=pltpu.CompilerParams(
            dimension_semantics=("arbitrary", "arbitrary"), vmem_limit_bytes=VMEM_LIMIT_BYTES),
        name="even_mixer",
    )(sinks.reshape(1, N_Q_HEADS), x, g.reshape(1, d), w_in.astype(BF16), qg, kg, conv_w,
      w_out.astype(BF16))


def _swiglu_rows(hb, wg_ref, wu_ref, wd_ref):
    d_ff = wg_ref.shape[1]
    acc = None
    for c in range(d_ff // FF_CHUNK):
        cols = slice(c * FF_CHUNK, (c + 1) * FF_CHUNK)
        gate = jnp.dot(hb, wg_ref[:, cols], preferred_element_type=F32)
        up = jnp.dot(hb, wu_ref[:, cols], preferred_element_type=F32)
        act = (gate * (1.0 / (1.0 + jnp.exp(-gate))) * up).astype(BF16)
        part = jnp.dot(act, wd_ref[cols, :], preferred_element_type=F32)
        acc = part if acc is None else acc + part
    return acc


def _swiglu_rows_fetching(hb, w_hbm, w_ref, stage, sem):
    d_ff = w_ref[0].shape[1]
    n = d_ff // WEIGHT_COLS

    def copies(c, s):
        cols = pl.ds(c * WEIGHT_COLS, WEIGHT_COLS)
        srcs = (w_hbm[0].at[:, cols], w_hbm[1].at[:, cols], w_hbm[2].at[cols, :])
        return [pltpu.make_async_copy(src, stage[m].at[s], sem.at[s, m]) for m, src in enumerate(srcs)]

    for cp in copies(0, 0):
        cp.start()
    acc = None
    for c in range(n):
        s = c % 2
        if c + 1 < n:
            for cp in copies(c + 1, 1 - s):
                cp.start()
        for cp in copies(c, s):
            cp.wait()
        cols = slice(c * WEIGHT_COLS, (c + 1) * WEIGHT_COLS)
        w_ref[0][:, cols] = stage[0][s].astype(BF16)
        w_ref[1][:, cols] = stage[1][s].astype(BF16)
        w_ref[2][cols, :] = stage[2][s].astype(BF16)
        gate = jnp.dot(hb, w_ref[0][:, cols], preferred_element_type=F32)
        up = jnp.dot(hb, w_ref[1][:, cols], preferred_element_type=F32)
        act = (gate * (1.0 / (1.0 + jnp.exp(-gate))) * up).astype(BF16)
        part = jnp.dot(act, w_ref[2][cols, :], preferred_element_type=F32)
        acc = part if acc is None else acc + part
    return acc


def _swiglu_weight_scratch(d, d_ff):
    return [
        pltpu.VMEM((d, d_ff), BF16),
        pltpu.VMEM((d, d_ff), BF16),
        pltpu.VMEM((d_ff, d), BF16),
        pltpu.VMEM((2, d, WEIGHT_COLS), F32),
        pltpu.VMEM((2, d, WEIGHT_COLS), F32),
        pltpu.VMEM((2, WEIGHT_COLS, d), F32),
        pltpu.SemaphoreType.DMA((2, 3)),
    ]


def _ffn_kernel(j, x_ref, g_ref, wg_hbm, wu_hbm, wd_hbm, o_ref,
                wg_ref, wu_ref, wd_ref, stage_g, stage_u, stage_d, sem):
    x = x_ref[...]
    hb = _rms(x, g_ref[...]).astype(BF16)
    w_ref = (wg_ref, wu_ref, wd_ref)

    @pl.when(pl.program_id(0) == 0)
    def _():
        o_ref[...] = x + _swiglu_rows_fetching(hb, (wg_hbm.at[j], wu_hbm.at[j], wd_hbm.at[j]), w_ref,
                                               (stage_g, stage_u, stage_d), sem)

    @pl.when(pl.program_id(0) > 0)
    def _():
        o_ref[...] = x + _swiglu_rows(hb, *w_ref)


def _ffn(x, g, j, w_gate, w_up, w_down):
    t, d = x.shape
    d_ff = w_gate.shape[2]
    tm = FFN_TILE
    return pl.pallas_call(
        functools.partial(_ffn_kernel, j),
        out_shape=jax.ShapeDtypeStruct((t, d), F32),
        grid=(t // tm,),
        in_specs=[
            pl.BlockSpec((tm, d), lambda i: (i, 0)),
            _resident((1, d), lambda i: (0, 0)),
            pl.BlockSpec(memory_space=pl.ANY),
            pl.BlockSpec(memory_space=pl.ANY),
            pl.BlockSpec(memory_space=pl.ANY),
        ],
        out_specs=pl.BlockSpec((tm, d), lambda i: (i, 0)),
        scratch_shapes=_swiglu_weight_scratch(d, d_ff),
        compiler_params=pltpu.CompilerParams(
            dimension_semantics=("arbitrary",), vmem_limit_bytes=VMEM_LIMIT_BYTES),
        name="dense_ffn",
    )(x, g.reshape(1, d), w_gate, w_up, w_down)


def _odd_mixer_kernel(x_ref, g_ref, win_ref, cw_ref, cb_ref, gw_ref, gab_ref, gxb_ref, lam_ref, wout_ref,
                      g2_ref, wr_ref, tri_ref, o_ref, h2_ref, route_ref, rt_ref, cnt_out_ref,
                      xbuf_ref, a_ref, b_ref, hs_ref, hcar_ref, cnt_ref):
    nb, tl, d = x_ref.shape
    ts = nb * tl
    w = cw_ref.shape[1]
    hd = w // LRU_HEADS
    st = pl.program_id(0)

    n_slabs = w // LANES
    past = (CONV_K_C - 1) * nb

    @pl.when(st == 0)
    def _():
        xbuf_ref[:, 0:past, :] = jnp.zeros((n_slabs, past, LANES), F32)
        hcar_ref[...] = jnp.zeros(hcar_ref.shape, F32)
        cnt_ref[...] = jnp.zeros(cnt_ref.shape, F32)

    x = x_ref[...].reshape(ts, d)
    h = _rms(x, g_ref[...]).astype(BF16)
    proj = jnp.dot(h, win_ref[...], preferred_element_type=F32)
    y = proj[:, 0:w]
    half_y = 0.5 * y
    y = half_y + half_y * jnp.tanh(y * (GELU_C0 + (GELU_C0 * GELU_C1) * (y * y)))
    xb = proj[:, w:2 * w]

    neg_lam = -lam_ref[...]
    softplus = jnp.maximum(neg_lam, 0.0) + jnp.log1p(jnp.exp(-jnp.abs(neg_lam)))
    half_rate = (-0.5 * LRU_C) * softplus
    first_step = st == 0
    for c in range(n_slabs):
        cols = slice(c * LANES, (c + 1) * LANES)
        for bi in range(nb):
            xbuf_ref[c, pl.ds(past + bi, tl, stride=nb), :] = xb[bi * tl:(bi + 1) * tl, cols]
        xc = cb_ref[:, cols]
        for kk in range(CONV_K_C):
            back = (CONV_K_C - 1 - kk) * nb
            xc = xc + cw_ref[kk:kk + 1, cols] * xbuf_ref[c, past - back:past - back + ts, :]
        ri = jnp.dot(xc.astype(BF16), gw_ref[c], preferred_element_type=F32)
        tanh_r = jnp.tanh(ri[:, 0:hd] + gab_ref[:, cols])
        ig = 0.5 + 0.5 * jnp.tanh(ri[:, hd:2 * hd] + gxb_ref[:, cols])
        a = jnp.exp(half_rate[:, cols] + half_rate[:, cols] * tanh_r)
        m2 = 1.0 - a * a
        mult = jnp.where(m2 > 0.0, m2 * lax.rsqrt(m2), 0.0)
        gated = ig * xc
        a_ref[c] = a
        b_ref[c] = mult * gated
        b_ref[c, 0:nb, :] = jnp.where(first_step, gated[0:nb, :], b_ref[c, 0:nb, :])

    def step(t, hprev):
        off = pl.multiple_of(t * nb, nb)
        hnew = []
        for c in range(w // LANES):
            hc = a_ref[c, pl.ds(off, nb), :] * hprev[c] + b_ref[c, pl.ds(off, nb), :]
            hs_ref[c, pl.ds(off, nb), :] = hc
            hnew.append(hc)
        return tuple(hnew)

    hlast = lax.fori_loop(0, tl, step, tuple(hcar_ref[c] for c in range(w // LANES)), unroll=4)
    for c in range(w // LANES):
        hcar_ref[c] = hlast[c]
    hs = jnp.concatenate(
        [jnp.concatenate([hs_ref[c, pl.ds(bi, tl, stride=nb), :] for c in range(w // LANES)], axis=1)
         for bi in range(nb)], axis=0)

    out = x + jnp.dot((y * hs).astype(BF16), wout_ref[...], preferred_element_type=F32)
    o_ref[...] = out.reshape(nb, tl, d)
    xbuf_ref[:, 0:past, :] = xbuf_ref[:, ts:ts + past, :]

    h2 = _rms(out, g2_ref[...])
    for bi in range(nb):
        _store_row_tiles(h2_ref.at[bi], 0, h2[bi * tl:(bi + 1) * tl, :])
    logits = lax.dot_general(wr_ref[...], h2.astype(BF16), (((1,), (1,)), ((), ())),
                             preferred_element_type=F32)
    erow = lax.broadcasted_iota(jnp.int32, (ROUTE_ROWS, ts), 0)
    lg = jnp.where(erow < N_EXPERTS, logits, -jnp.inf)
    m1 = jnp.max(lg, axis=0, keepdims=True)
    i1 = jnp.min(jnp.where(lg == m1, erow, ROUTE_ROWS), axis=0, keepdims=True)
    lg2 = jnp.where(erow == i1, -jnp.inf, lg)
    m2 = jnp.max(lg2, axis=0, keepdims=True)
    i2 = jnp.min(jnp.where(lg2 == m2, erow, ROUTE_ROWS), axis=0, keepdims=True)
    e2 = jnp.exp(m2 - m1)
    gate1 = 1.0 / (1.0 + e2)
    gate2 = e2 / (1.0 + e2)

    first = erow == i1
    second = erow == i2
    ind = jnp.where(first | second, 1.0, 0.0)
    cums = jnp.dot(ind.astype(BF16), tri_ref[...], preferred_element_type=F32)
    rank = cnt_ref[...] + cums - ind
    cnt_ref[...] = cnt_ref[...] + jnp.broadcast_to(cums[:, ts - 1:ts], cums.shape)
    rank1 = jnp.sum(jnp.where(first, rank, 0.0), axis=0, keepdims=True)
    rank2 = jnp.sum(jnp.where(second, rank, 0.0), axis=0, keepdims=True)
    cnt_out_ref[...] = cnt_ref[:, 0:ROUTE_LANES].astype(jnp.int32)

    fields = [None] * SUBLANES
    fields[ROUTE_E0], fields[ROUTE_E0 + 1] = i1.astype(F32), i2.astype(F32)
    fields[ROUTE_GATE0], fields[ROUTE_GATE0 + 1] = gate1, gate2
    fields[ROUTE_RANK0], fields[ROUTE_RANK0 + 1] = rank1, rank2
    zero_row = jnp.zeros((1, ts), F32)
    record_t = jnp.concatenate([zero_row if f is None else f for f in fields], axis=0)
    rt_ref[...] = record_t.astype(jnp.int32)
    route_ref[...] = jnp.concatenate(
        [record_t, jnp.zeros((ROUTE_LANES - SUBLANES, ts), F32)], axis=0).T.reshape(nb, tl, ROUTE_LANES)


def _odd_mixer(x, g, w_in, conv_w, conv_b, ga_w, ga_b, gx_w, gx_b, lam, w_out, g2, w_router):
    b, s, d = x.shape
    assert b == SUBLANES, "one timestep of all sequences must fill one sublane group"
    assert conv_w.shape[1] // LRU_HEADS == LANES, "one recurrence head must be one 128-lane slab"
    tl = SEQ_TILE // b
    ts = b * tl
    w = conv_w.shape[1]
    hd = w // LRU_HEADS
    gw = (0.5 * jnp.concatenate([ga_w, gx_w], axis=-1)).astype(BF16)
    ga_b, gx_b = 0.5 * ga_b, 0.5 * gx_b
    wr = jnp.pad(w_router.T, ((0, ROUTE_ROWS - N_EXPERTS), (0, 0))).astype(BF16)
    tri = jnp.triu(jnp.ones((ts, ts), BF16))
    nst = s // tl
    const = lambda si: (0, 0)
    row = lambda a: a.reshape(1, -1)
    tile = lambda si: (0, si, 0)
    out, h2, route, rt, counts = pl.pallas_call(
        _odd_mixer_kernel,
        out_shape=(jax.ShapeDtypeStruct((b, s, d), F32),
                   jax.ShapeDtypeStruct((b, s * SUBLANES, LANES), F32),
                   jax.ShapeDtypeStruct((b, s, ROUTE_LANES), F32),
                   jax.ShapeDtypeStruct((nst, SUBLANES, ts), jnp.int32),
                   jax.ShapeDtypeStruct((ROUTE_ROWS, ROUTE_LANES), jnp.int32)),
        grid=(nst,),
        in_specs=[
            pl.BlockSpec((b, tl, d), tile),
            _resident((1, d), const),
            _resident((d, 2 * w), const),
            _resident((CONV_K_C, w), const),
            _resident((1, w), const),
            _resident((LRU_HEADS, hd, 2 * hd), lambda si: (0, 0, 0)),
            _resident((1, w), const),
            _resident((1, w), const),
            _resident((1, w), const),
            _resident((w, d), const),
            _resident((1, d), const),
            _resident((ROUTE_ROWS, d), const),
            _resident((ts, ts), const),
        ],
        out_specs=(pl.BlockSpec((b, tl, d), tile),
                   pl.BlockSpec((b, tl * SUBLANES, LANES), tile),
                   pl.BlockSpec((b, tl, ROUTE_LANES), tile),
                   pl.BlockSpec((None, SUBLANES, ts), lambda si: (si, 0, 0)),
                   pl.BlockSpec((ROUTE_ROWS, ROUTE_LANES), const)),
        scratch_shapes=[
            pltpu.VMEM((w // LANES, (CONV_K_C - 1) * b + ts, LANES), F32),
            pltpu.VMEM((w // LANES, ts, LANES), F32),
            pltpu.VMEM((w // LANES, ts, LANES), F32),
            pltpu.VMEM((w // LANES, ts, LANES), F32),
            pltpu.VMEM((w // LANES, b, LANES), F32),
            pltpu.VMEM((ROUTE_ROWS, ts), F32),
        ],
        compiler_params=pltpu.CompilerParams(
            dimension_semantics=("arbitrary",), vmem_limit_bytes=VMEM_LIMIT_BYTES),
        name="odd_mixer",
    )(x, row(g), w_in.astype(BF16), conv_w, row(conv_b), gw, row(ga_b), row(gx_b), row(lam),
      w_out.astype(BF16), row(g2), wr, tri)
    rt = rt.reshape(nst, SUBLANES, b, tl).transpose(1, 2, 0, 3).reshape(SUBLANES, b * s)
    return out, h2.reshape(b * s * SUBLANES, LANES), route, rt, counts


def _moe_tables(rt, counts, t):
    r = MOE_ROWS
    tm = MOVE_TILE
    n_blocks = TOP_K * t // r + N_EXPERTS
    counts = counts[0:N_EXPERTS, 0]
    padded = (counts + r - 1) // r * r
    pad_end = jnp.cumsum(padded)
    pad_start = pad_end - padded
    e = rt[ROUTE_E0:ROUTE_E0 + TOP_K, :]
    rank = rt[ROUTE_RANK0:ROUTE_RANK0 + TOP_K, :]
    start = jnp.zeros_like(e)
    for ex in range(N_EXPERTS):
        start = jnp.where(e == ex, pad_start[ex], start)
    dest = start + rank
    blk = jnp.arange(n_blocks, dtype=jnp.int32)
    blk_expert = jnp.minimum(jnp.sum(blk[:, None] * r >= pad_end[None, :], axis=1), N_EXPERTS - 1)
    n_used = pad_end[N_EXPERTS - 1] // r
    last_blk = jnp.where(padded > 0, pad_end // r - 1, -1)
    has_pad = jnp.any(blk[:, None] == last_blk[None, :], axis=1) | (blk >= n_used)
    dest = dest.astype(jnp.int32).reshape(TOP_K, t // tm, tm).transpose(1, 0, 2).reshape(t // tm, TOP_K * tm)
    return dest, blk_expert.astype(jnp.int32), has_pad.astype(jnp.int32), n_used.astype(jnp.int32).reshape(1)


def _row_copy(src, src_row, dst, dst_row, sem):
    return pltpu.make_async_copy(src.at[pl.ds(src_row * SUBLANES, SUBLANES), :],
                                 dst.at[pl.ds(dst_row * SUBLANES, SUBLANES), :], sem)


def _rows_wait(src, dst, n_rows, sem):
    pltpu.make_async_copy(src.at[pl.ds(0, n_rows * SUBLANES), :],
                          dst.at[pl.ds(0, n_rows * SUBLANES), :], sem).wait()


def _dispatch_kernel(pad_ref, dest_hbm, h_ref, xg_hbm, dsm, stage, zbuf, sem_idx, sem_row, sem_zero):
    i = pl.program_id(0)
    n = pl.num_programs(0)
    tm = h_ref.shape[0] // SUBLANES
    r = zbuf.shape[0] // SUBLANES
    n_blocks = pad_ref.shape[0]
    slot = lax.rem(i, 2)

    def idx_copy(step, s):
        return pltpu.make_async_copy(dest_hbm.at[step], dsm.at[pl.ds(s * TOP_K * tm, TOP_K * tm)],
                                     sem_idx.at[s])

    def zero_copy(blk):
        return pltpu.make_async_copy(zbuf, xg_hbm.at[pl.ds(blk * r * SUBLANES, r * SUBLANES), :], sem_zero)

    @pl.when(i == 0)
    def _():
        idx_copy(0, 0).start()
        zbuf[...] = jnp.zeros(zbuf.shape, zbuf.dtype)

        def zero_start(blk, c):
            @pl.when(pad_ref[blk] == 1)
            def _():
                zero_copy(blk).start()
            return c

        def zero_wait(blk, c):
            @pl.when(pad_ref[blk] == 1)
            def _():
                zero_copy(blk).wait()
            return c

        lax.fori_loop(0, n_blocks, zero_start, 0)
        lax.fori_loop(0, n_blocks, zero_wait, 0)

    @pl.when(i + 1 < n)
    def _():
        idx_copy(i + 1, 1 - slot).start()

    idx_copy(i, slot).wait()

    stage_row0 = slot * tm
    stage[pl.ds(stage_row0 * SUBLANES, tm * SUBLANES), :] = h_ref[...]

    def issue(row, c):
        for k in range(TOP_K):
            _row_copy(stage, stage_row0 + row, xg_hbm, dsm[(slot * TOP_K + k) * tm + row],
                      sem_row.at[slot]).start(priority=k % DMA_THREADS)
        return c

    lax.fori_loop(0, tm, issue, 0, unroll=ROW_DMA_UNROLL)

    def wait_slot(s):
        for _ in range(TOP_K):
            _rows_wait(stage, xg_hbm, tm, sem_row.at[s])

    @pl.when(i > 0)
    def _():
        wait_slot(1 - slot)

    @pl.when(i == n - 1)
    def _():
        wait_slot(slot)


def _dispatch(h2, dest, has_pad):
    tm = dest.shape[1] // TOP_K
    t = h2.shape[0] // SUBLANES
    r = MOE_ROWS
    n_rows = TOP_K * t + N_EXPERTS * r
    grid_spec = pltpu.PrefetchScalarGridSpec(
        num_scalar_prefetch=1,
        grid=(t // tm,),
        in_specs=[pl.BlockSpec(memory_space=pl.ANY),
                  pl.BlockSpec((tm * SUBLANES, LANES), lambda i, pad: (i, 0))],
        out_specs=pl.BlockSpec(memory_space=pl.ANY),
        scratch_shapes=[
            pltpu.SMEM((2 * TOP_K * tm,), jnp.int32),
            pltpu.VMEM((2 * tm * SUBLANES, LANES), F32),
            pltpu.VMEM((r * SUBLANES, LANES), F32),
            pltpu.SemaphoreType.DMA((2,)),
            pltpu.SemaphoreType.DMA((2,)),
            pltpu.SemaphoreType.DMA,
        ],
    )
    return pl.pallas_call(
        _dispatch_kernel,
        out_shape=jax.ShapeDtypeStruct((n_rows * SUBLANES, LANES), F32),
        grid_spec=grid_spec,
        compiler_params=pltpu.CompilerParams(dimension_semantics=("arbitrary",)),
        name="moe_dispatch",
    )(has_pad, dest, h2)


def _expert_kernel(j, be_ref, nu_ref, x_ref, wg_hbm, wu_hbm, wd_hbm, y_ref,
                   wg_ref, wu_ref, wd_ref, stage_g, stage_u, stage_d, sem):
    i = pl.program_id(0)
    r = x_ref.shape[0] // SUBLANES
    e = be_ref[i]
    used = i < nu_ref[0]
    new_expert = (i == 0) | (e != be_ref[jnp.maximum(i - 1, 0)])
    w_ref = (wg_ref, wu_ref, wd_ref)

    @pl.when(used & new_expert)
    def _():
        xb = _load_row_tiles(x_ref, 0, r).astype(BF16)
        y = _swiglu_rows_fetching(xb, (wg_hbm.at[j, e], wu_hbm.at[j, e], wd_hbm.at[j, e]), w_ref,
                                  (stage_g, stage_u, stage_d), sem)
        _store_row_tiles(y_ref, 0, y)

    @pl.when(used & jnp.logical_not(new_expert))
    def _():
        xb = _load_row_tiles(x_ref, 0, r).astype(BF16)
        _store_row_tiles(y_ref, 0, _swiglu_rows(xb, *w_ref))

    @pl.when(jnp.logical_not(used))
    def _():
        y_ref[...] = jnp.zeros(y_ref.shape, y_ref.dtype)


def _experts(xg, blk_expert, n_used, j, w_gate, w_up, w_down):
    d, d_ff = w_gate.shape[2], w_gate.shape[3]
    r = MOE_ROWS
    rows = lambda i, be, nu: (i, 0)
    grid_spec = pltpu.PrefetchScalarGridSpec(
        num_scalar_prefetch=2,
        grid=(xg.shape[0] // (r * SUBLANES),),
        in_specs=[
            pl.BlockSpec((r * SUBLANES, LANES), rows),
            pl.BlockSpec(memory_space=pl.ANY),
            pl.BlockSpec(memory_space=pl.ANY),
            pl.BlockSpec(memory_space=pl.ANY),
        ],
        out_specs=pl.BlockSpec((r * SUBLANES, LANES), rows),
        scratch_shapes=_swiglu_weight_scratch(d, d_ff),
    )
    return pl.pallas_call(
        functools.partial(_expert_kernel, j),
        out_shape=jax.ShapeDtypeStruct(xg.shape, F32),
        grid_spec=grid_spec,
        compiler_params=pltpu.CompilerParams(
            dimension_semantics=("arbitrary",), vmem_limit_bytes=VMEM_LIMIT_BYTES),
        name="moe_experts",
    )(blk_expert, n_used, xg, w_gate, w_up, w_down)


def _combine_kernel(dest_hbm, y_hbm, x_ref, route_ref, o_ref, dsm, ybuf, sem_idx, sem_row):
    i = pl.program_id(0)
    n = pl.num_programs(0)
    tm = x_ref.shape[0]
    slot = lax.rem(i, 2)

    def idx_copy(step):
        s3 = lax.rem(step, 3)
        return pltpu.make_async_copy(dest_hbm.at[step], dsm.at[pl.ds(s3 * TOP_K * tm, TOP_K * tm)],
                                     sem_idx.at[s3])

    def request_rows(step):
        s2 = lax.rem(step, 2)
        table = lax.rem(step, 3) * TOP_K * tm

        def issue(row, c):
            for k in range(TOP_K):
                _row_copy(y_hbm, dsm[table + k * tm + row], ybuf, (s2 * TOP_K + k) * tm + row,
                          sem_row.at[s2]).start(priority=k % DMA_THREADS)
            return c

        lax.fori_loop(0, tm, issue, 0, unroll=ROW_DMA_UNROLL)

    @pl.when(i == 0)
    def _():
        idx_copy(0).start()
        if_next = n > 1

        @pl.when(if_next)
        def _():
            idx_copy(1).start()

        idx_copy(0).wait()
        request_rows(0)

    @pl.when(i + 2 < n)
    def _():
        idx_copy(i + 2).start()

    @pl.when(i + 1 < n)
    def _():
        idx_copy(i + 1).wait()
        request_rows(i + 1)

    for _ in range(TOP_K):
        _rows_wait(y_hbm, ybuf, tm, sem_row.at[slot])
    rt = route_ref[...]
    moe = None
    for k in range(TOP_K):
        yk = _load_row_tiles(ybuf, (slot * TOP_K + k) * tm * SUBLANES, tm)
        term = yk * rt[:, ROUTE_GATE0 + k:ROUTE_GATE0 + k + 1]
        moe = term if moe is None else moe + term
    o_ref[...] = x_ref[...] + moe


def _combine(x, y, route, dest):
    t, d = x.shape
    tm = dest.shape[1] // TOP_K
    return pl.pallas_call(
        _combine_kernel,
        out_shape=jax.ShapeDtypeStruct((t, d), F32),
        grid=(t // tm,),
        in_specs=[
            pl.BlockSpec(memory_space=pl.ANY),
            pl.BlockSpec(memory_space=pl.ANY),
            pl.BlockSpec((tm, d), lambda i: (i, 0)),
            pl.BlockSpec((tm, ROUTE_LANES), lambda i: (i, 0)),
        ],
        out_specs=pl.BlockSpec((tm, d), lambda i: (i, 0)),
        scratch_shapes=[
            pltpu.SMEM((3 * TOP_K * tm,), jnp.int32),
            pltpu.VMEM((2 * TOP_K * tm * SUBLANES, LANES), F32),
            pltpu.SemaphoreType.DMA((3,)),
            pltpu.SemaphoreType.DMA((2,)),
        ],
        compiler_params=pltpu.CompilerParams(
            dimension_semantics=("arbitrary",), vmem_limit_bytes=VMEM_LIMIT_BYTES),
        name="moe_combine",
    )(dest, y, x, route)


def kernel(x, norm_mix, norm_ffn, hy_w_in, hy_q_gain, hy_k_gain, hy_sinks, hy_conv_w, hy_w_out, rg_w_in, rg_conv_w, rg_conv_b, rg_gate_a_w, rg_gate_a_b, rg_gate_x_w, rg_gate_x_b, rg_lambda, rg_w_out, ffn_w_gate, ffn_w_up, ffn_w_down, moe_router, moe_w_gate, moe_w_up, moe_w_down):
    b, s, d = x.shape
    t = b * s
    depth = norm_mix.shape[0]
    for layer in range(depth):
        j = layer // 2
        if layer % 2 == 0:
            x = _even_mixer(x, norm_mix[layer], hy_w_in[j], hy_q_gain[j], hy_k_gain[j], hy_sinks[j],
                            hy_conv_w[j], hy_w_out[j])
            x = _ffn(x.reshape(t, d), norm_ffn[layer], j, ffn_w_gate, ffn_w_up, ffn_w_down)
            x = x.reshape(b, s, d)
        else:
            x, h2, route, rt, counts = _odd_mixer(
                x, norm_mix[layer], rg_w_in[j], rg_conv_w[j], rg_conv_b[j], rg_gate_a_w[j], rg_gate_a_b[j],
                rg_gate_x_w[j], rg_gate_x_b[j], rg_lambda[j], rg_w_out[j], norm_ffn[layer], moe_router[j])
            dest, blk_expert, has_pad, n_used = _moe_tables(rt, counts, t)
            xg = _dispatch(h2, dest, has_pad)
            y = _experts(xg, blk_expert, n_used, j, moe_w_gate, moe_w_up, moe_w_down)
            x = _combine(x.reshape(t, d), y, route.reshape(t, ROUTE_LANES), dest).reshape(b, s, d)
    return x
```
